```python
import jax, jax.numpy as jnp
from jax import lax
import numpy as np

D_MODEL = 1024
BATCH = 8
SEQ = 4096
DEPTH = 1

ATTN_GROUPS = ((128, 1), (512, 4), (2048, 16))
N_GROUPS = 3
ATTN_HEADS = 4
ATTN_HEAD_DIM = 128
ATTN_GROUP_WIDTH = ATTN_HEADS * ATTN_HEAD_DIM
ATTN_QKV_WIDTH = N_GROUPS * ATTN_GROUP_WIDTH
ATTN_OUT_WIDTH = ATTN_GROUP_WIDTH
RET_HEADS = 8
RET_KEY_DIM = 64
RET_VALUE_DIM = 128
RET_QK_WIDTH = RET_HEADS * RET_KEY_DIM
RET_V_WIDTH = RET_HEADS * RET_VALUE_DIM
RET_CHUNK = 128
ROPE_BASE = 10000.0
N_BRANCHES = 2
NORM_EPS = 1e-6
IN_SIZES = (ATTN_QKV_WIDTH, ATTN_QKV_WIDTH, ATTN_QKV_WIDTH, ATTN_OUT_WIDTH,
            RET_QK_WIDTH, RET_QK_WIDTH, RET_V_WIDTH, RET_V_WIDTH,
            N_BRANCHES * D_MODEL)
IN_WIDTH = sum(IN_SIZES)
SPLIT_POINTS = tuple(int(p) for p in np.cumsum(IN_SIZES)[:-1])

kernel_name = "hybrid_dilated_attn_retention_gated_block"


def rmsnorm(x, w):
    xf = x.astype(jnp.float32)
    y = xf * lax.rsqrt(jnp.mean(xf * xf, axis=-1, keepdims=True) + NORM_EPS)
    return (y * w.astype(jnp.float32)).astype(x.dtype)


def dilated_group_attention(q, k, v, window, dilation):
    B, S, H, E = q.shape
    d = dilation
    W = window // dilation
    L = S // d
    nb = -(-L // W)
    Lp = nb * W

    def to_sub(t):
        t = t.reshape(B, L, d, H, E).transpose(0, 2, 1, 3, 4)
        return jnp.pad(t, ((0, 0), (0, 0), (0, Lp - L), (0, 0), (0, 0)))

    def kv_blocks(t):
        t = jnp.pad(to_sub(t), ((0, 0), (0, 0), (W, 0), (0, 0), (0, 0)))
        t = t.reshape(B, d, nb + 1, W, H, E)
        return jnp.concatenate([t[:, :, :-1], t[:, :, 1:]], axis=3)

    def from_sub(t):
        rest = t.shape[4:]
        t = t.reshape((B, d, Lp) + rest)[:, :, :L]
        return jnp.swapaxes(t, 1, 2).reshape((B, S) + rest)

    qs = to_sub(q).reshape(B, d, nb, W, H, E)
    ks = kv_blocks(k)
    vs = kv_blocks(v)
    s = jnp.einsum('bdnqhe,bdnkhe->bdnhqk', qs, ks).astype(jnp.float32)
    a = jnp.arange(W)[None, :, None]
    c = jnp.arange(2 * W)[None, None, :]
    blk = jnp.arange(nb)[:, None, None]
    key_pos = (blk - 1) * W + c
    valid = (c >= a) & (c <= a + W) & (key_pos >= 0)
    s = jnp.where(valid[:, None], s, -jnp.inf)
    m = jnp.max(s, axis=-1, keepdims=True)
    p = jnp.exp(s - m)
    den = jnp.sum(p, axis=-1)
    o = jnp.einsum('bdnhqk,bdnkhe->bdnqhe', p, vs.astype(jnp.float32))
    den_t = jnp.swapaxes(den, -1, -2)
    o = o / den_t[..., None]
    lse = jnp.swapaxes(m[..., 0], -1, -2) + jnp.log(den_t)
    return from_sub(o), from_sub(lse)


def rotate_pairs(t, cos, sin):
    t1 = t[..., 0::2]
    t2 = t[..., 1::2]
    return jnp.stack([t1 * cos - t2 * sin, t1 * sin + t2 * cos], axis=-1).reshape(t.shape)


def retention(q, k, v, gn_w):
    B, S, H, dk = q.shape
    dv = v.shape[-1]
    q = q.astype(jnp.float32)
    k = k.astype(jnp.float32)
    v = v.astype(jnp.float32)
    pos = jnp.arange(S, dtype=jnp.float32)
    inv_freq = ROPE_BASE ** (-jnp.linspace(0.0, 1.0, dk // 2, dtype=jnp.float32))
    ang = pos[:, None] * inv_freq[None, :]
    cos = jnp.cos(ang)[:, None, :]
    sin = jnp.sin(ang)[:, None, :]
    q = rotate_pairs(q, cos, sin)
    k = rotate_pairs(k, cos, sin) * (dk ** -0.5)
    log_gamma = jnp.log(1.0 - 2.0 ** (-5.0 - jnp.arange(H, dtype=jnp.float32)))

    C = RET_CHUNK
    N = S // C
    qc = q.reshape(B, N, C, H, dk)
    kc = k.reshape(B, N, C, H, dk)
    vc = v.reshape(B, N, C, H, dv)
    idx = jnp.arange(C, dtype=jnp.float32)
    diff = idx[:, None] - idx[None, :]
    decay_intra = jnp.where(diff[None] >= 0,
                            jnp.exp(jnp.maximum(diff, 0.0)[None] * log_gamma[:, None, None]),
                            0.0)
    s = jnp.einsum('bnqhd,bnkhd->bnhqk', qc, kc) * decay_intra
    o_intra = jnp.einsum('bnhqk,bnkhe->bnqhe', s, vc)

    key_decay = jnp.exp((C - 1 - idx)[None, :] * log_gamma[:, None])
    kv = jnp.einsum('bnkhd,hk,bnkhe->bnhde', kc, key_decay, vc)
    chunk_decay = jnp.exp(C * log_gamma)[None, :, None, None]

    def step(R, kv_n):
        return chunk_decay * R + kv_n, R

    _, R_prev = lax.scan(step, jnp.zeros((B, H, dk, dv), jnp.float32),
                         jnp.moveaxis(kv, 1, 0))
    R_prev = jnp.moveaxis(R_prev, 0, 1)
    query_decay = jnp.exp((idx + 1.0)[None, :] * log_gamma[:, None])
    o_cross = jnp.einsum('bnqhd,bnhde,hq->bnqhe', qc, R_prev, query_decay)
    o = (o_intra + o_cross).reshape(B, S, H, dv)
    o = o * lax.rsqrt(jnp.mean(o * o, axis=-1, keepdims=True) + NORM_EPS)
    return o.reshape(B, S, H * dv) * gn_w.astype(jnp.float32)


def setup_inputs(seed: int = 0) -> dict:
    key = jax.random.key(seed)
    ks = jax.random.split(key, 10)
    f32 = jnp.float32
    x = jax.random.normal(ks[0], (BATCH, SEQ, D_MODEL), f32)
    ln1_w = 1.0 + 0.01 * jax.random.normal(ks[1], (DEPTH, D_MODEL), f32)
    w_in = jax.random.normal(ks[2], (DEPTH, D_MODEL, IN_WIDTH), f32) * D_MODEL ** -0.5
    b_gate = 0.01 * jax.random.normal(ks[3], (DEPTH, N_BRANCHES * D_MODEL), f32)
    attn_proj = jax.random.normal(ks[4], (DEPTH, ATTN_OUT_WIDTH, D_MODEL), f32) * ATTN_OUT_WIDTH ** -0.5
    ret_proj = jax.random.normal(ks[5], (DEPTH, RET_V_WIDTH, D_MODEL), f32) * RET_V_WIDTH ** -0.5
    ret_gn_w = 1.0 + 0.01 * jax.random.normal(ks[6], (DEPTH, RET_V_WIDTH), f32)
    w_out = jax.random.normal(ks[7], (DEPTH, D_MODEL, D_MODEL), f32) * D_MODEL ** -0.5
    lnf_w = 1.0 + 0.01 * jax.random.normal(ks[8], (D_MODEL,), f32)
    return {"x": x, "ln1_w": ln1_w, "w_in": w_in, "b_gate": b_gate,
            "attn_proj": attn_proj, "ret_proj": ret_proj, "ret_gn_w": ret_gn_w,
            "w_out": w_out, "lnf_w": lnf_w}


def reference(x, ln1_w, w_in, b_gate, attn_proj, ret_proj, ret_gn_w, w_out, lnf_w):
    B, S, _ = x.shape
    h = x
    for l in range(DEPTH):
        xn = rmsnorm(h, ln1_w[l])
        z = xn @ w_in[l]
        aq, ak, av, ag, rq, rk, rv, rg, mg = jnp.split(z, SPLIT_POINTS, axis=-1)

        aq = aq.reshape(B, S, N_GROUPS, ATTN_HEADS, ATTN_HEAD_DIM) * (ATTN_HEAD_DIM ** -0.5)
        ak = ak.reshape(B, S, N_GROUPS, ATTN_HEADS, ATTN_HEAD_DIM)
        av = av.reshape(B, S, N_GROUPS, ATTN_HEADS, ATTN_HEAD_DIM)
        outs = []
        lses = []
        for g, (win, dil) in enumerate(ATTN_GROUPS):
            o_g, lse_g = dilated_group_attention(aq[:, :, g], ak[:, :, g], av[:, :, g], win, dil)
            outs.append(o_g)
            lses.append(lse_g)
        mix = jax.nn.softmax(jnp.stack(lses, axis=0), axis=0)
        attn = jnp.einsum('gbsh,gbshe->bshe', mix, jnp.stack(outs, axis=0))
        attn = attn.reshape(B, S, ATTN_OUT_WIDTH).astype(x.dtype) * jax.nn.silu(ag)
        y_attn = attn @ attn_proj[l]

        ret = retention(rq.reshape(B, S, RET_HEADS, RET_KEY_DIM),
                        rk.reshape(B, S, RET_HEADS, RET_KEY_DIM),
                        rv.reshape(B, S, RET_HEADS, RET_VALUE_DIM),
                        ret_gn_w[l])
        y_ret = (ret.astype(x.dtype) * jax.nn.silu(rg)) @ ret_proj[l]

        gates = jax.nn.sigmoid(mg + b_gate[l]).reshape(B, S, N_BRANCHES, D_MODEL)
        merged = gates[:, :, 0] * y_attn + gates[:, :, 1] * y_ret
        h = h + merged @ w_out[l]
    return rmsnorm(h, lnf_w)
```

```python
import functools

import jax
import jax.numpy as jnp
from jax import lax
from jax.experimental import pallas as pl
from jax.experimental.pallas import tpu as pltpu

D_MODEL = 1024
ATTN_GROUPS = ((128, 1), (512, 4), (2048, 16))
N_GROUPS = 3
ATTN_HEADS = 4
HEAD_DIM = 128
GROUP_WIDTH = ATTN_HEADS * HEAD_DIM
QKV_WIDTH = N_GROUPS * GROUP_WIDTH
BAND = 128
RET_HEADS = 8
RET_KEY_DIM = 64
RET_VALUE_DIM = 128
RET_QK_WIDTH = RET_HEADS * RET_KEY_DIM
RET_V_WIDTH = RET_HEADS * RET_VALUE_DIM
RET_CHUNK = 128
ROPE_BASE = 10000.0
NORM_EPS = 1e-6
NEG_BIG = -1e30
LANES = 128

PROJ_ROWS = 512
ATTN_TILE = 2048
OUT_ROWS = 512
VMEM_LIMIT = 56 * 1024 * 1024

BF16 = jnp.bfloat16
F32 = jnp.float32


def _rmsnorm_rows(x, w):
    return x * lax.rsqrt(jnp.mean(x * x, axis=-1, keepdims=True) + NORM_EPS) * w


def _dot(a, b):
    return jnp.dot(a, b, preferred_element_type=F32)


def _qkv_proj_kernel(x_ref, lnw_ref, w1_ref, w2_ref, w3_ref,
                     o1_ref, o2_ref, o3_ref, xn_scr):
    xn = _rmsnorm_rows(x_ref[...], lnw_ref[...])
    n_lane_tiles = D_MODEL // LANES
    for ct in range(n_lane_tiles):
        xn_scr[ct] = xn[:, ct * LANES:(ct + 1) * LANES]
    q_scale = HEAD_DIM ** -0.5

    def strided_rows(r, rows, dil):
        return jnp.concatenate(
            [xn_scr[ct, pl.ds(r, rows, stride=dil), :] for ct in range(n_lane_tiles)], axis=1)

    def project(lhs, w_ref):
        r = _dot(lhs, w_ref[...])
        q = r[:, :GROUP_WIDTH] * q_scale
        return jnp.concatenate([q, r[:, GROUP_WIDTH:]], axis=1).astype(BF16)

    o1_ref[...] = project(xn.astype(BF16), w1_ref)

    for o_ref, w_ref, (_, dil) in ((o2_ref, w2_ref, ATTN_GROUPS[1]),
                                   (o3_ref, w3_ref, ATTN_GROUPS[2])):
        rows = PROJ_ROWS // dil
        lhs = jnp.concatenate(
            [strided_rows(r, rows, dil).astype(BF16) for r in range(dil)], axis=0)
        res = project(lhs, w_ref)
        for r in range(dil):
            o_ref[r] = res[r * rows:(r + 1) * rows]


def _qkv_proj(x, lnw, w1, w2, w3):
    B, S, _ = x.shape
    d2, d3 = ATTN_GROUPS[1][1], ATTN_GROUPS[2][1]
    nt = S // PROJ_ROWS
    wspec = pl.BlockSpec((D_MODEL, QKV_WIDTH), lambda b, t: (0, 0))
    return pl.pallas_call(
        _qkv_proj_kernel,
        grid=(B, nt),
        in_specs=[
            pl.BlockSpec((None, PROJ_ROWS, D_MODEL), lambda b, t: (b, t, 0)),
            pl.BlockSpec((1, D_MODEL), lambda b, t: (0, 0)),
            wspec, wspec, wspec,
        ],
        out_specs=[
            pl.BlockSpec((None, PROJ_ROWS, QKV_WIDTH), lambda b, t: (b, t, 0)),
            pl.BlockSpec((None, d2, PROJ_ROWS // d2, QKV_WIDTH), lambda b, t: (b, 0, t, 0)),
            pl.BlockSpec((None, d3, PROJ_ROWS // d3, QKV_WIDTH), lambda b, t: (b, 0, t, 0)),
        ],
        out_shape=[
            jax.ShapeDtypeStruct((B, S, QKV_WIDTH), BF16),
            jax.ShapeDtypeStruct((B, d2, S // d2, QKV_WIDTH), BF16),
            jax.ShapeDtypeStruct((B, d3, S // d3, QKV_WIDTH), BF16),
        ],
        scratch_shapes=[pltpu.VMEM((D_MODEL // LANES, PROJ_ROWS, LANES), F32)],
        compiler_params=pltpu.CompilerParams(
            dimension_semantics=("arbitrary", "arbitrary"), vmem_limit_bytes=VMEM_LIMIT),
        name="qkv_proj",
    )(x, lnw, w1, w2, w3)


def _swap_pair_halves(t):
    n = t.shape[-1]
    lane = lax.broadcasted_iota(jnp.int32, t.shape, 1)
    up = pltpu.roll(t, n - RET_KEY_DIM // 2, axis=1)
    down = pltpu.roll(t, RET_KEY_DIM // 2, axis=1)
    return jnp.where(lane % RET_KEY_DIM < RET_KEY_DIM // 2, up, down)


def _nat_proj_kernel(x_ref, lnw_ref, wg_ref, wrg_ref, wmg_ref, wrv_ref, wqk_ref,
                     cos_ref, sin_ref,
                     ag_ref, rg_ref, mg_ref, rv_ref, rqk_ref):
    xb = _rmsnorm_rows(x_ref[...], lnw_ref[...]).astype(BF16)
    ag_ref[...] = _dot(xb, wg_ref[...]).astype(BF16)
    rg_ref[...] = _dot(xb, wrg_ref[...]).astype(BF16)
    mg_ref[...] = _dot(xb, wmg_ref[...]).astype(BF16)
    rv_ref[...] = _dot(xb, wrv_ref[...]).astype(BF16)
    qk = _dot(xb, wqk_ref[...])
    cos = cos_ref[...]
    sin = sin_ref[...]
    k_scale = RET_KEY_DIM ** -0.5
    q = qk[:, :RET_QK_WIDTH]
    k = qk[:, RET_QK_WIDTH:]
    q = q * cos + _swap_pair_halves(q) * sin
    k = (k * cos + _swap_pair_halves(k) * sin) * k_scale
    rqk_ref[...] = jnp.concatenate([q, k], axis=1).astype(BF16)


def _nat_proj(x, lnw, wg, wrg, wmg, wrv, wqk, cos_t, sin_t):
    B, S, _ = x.shape
    nt = S // PROJ_ROWS

    def wspec(n):
        return pl.BlockSpec((D_MODEL, n), lambda b, t: (0, 0))

    def ospec(n):
        return pl.BlockSpec((None, PROJ_ROWS, n), lambda b, t: (b, t, 0))

    widths = (GROUP_WIDTH, RET_V_WIDTH, 2 * D_MODEL, RET_V_WIDTH, 2 * RET_QK_WIDTH)
    tspec = pl.BlockSpec((PROJ_ROWS, RET_QK_WIDTH), lambda b, t: (t, 0))
    return pl.pallas_call(
        _nat_proj_kernel,
        grid=(B, nt),
        in_specs=[
            pl.BlockSpec((None, PROJ_ROWS, D_MODEL), lambda b, t: (b, t, 0)),
            pl.BlockSpec((1, D_MODEL), lambda b, t: (0, 0)),
            *[wspec(n) for n in widths],
            tspec, tspec,
        ],
        out_specs=[ospec(n) for n in widths],
        out_shape=[jax.ShapeDtypeStruct((B, S, n), BF16) for n in widths],
        compiler_params=pltpu.CompilerParams(
            dimension_semantics=("arbitrary", "arbitrary"), vmem_limit_bytes=VMEM_LIMIT),
        name="nat_proj",
    )(x, lnw, wg, wrg, wmg, wrv, wqk, cos_t, sin_t)


def _dil_attn_kernel(*refs):
    in_refs = refs[:5 * N_GROUPS]
    ag_ref = refs[5 * N_GROUPS]
    out_ref = refs[5 * N_GROUPS + 1]
    o_scr, l_scr = refs[5 * N_GROUPS + 2:]
    tile = pl.program_id(1)

    a = lax.broadcasted_iota(jnp.int32, (BAND, 2 * BAND), 0)
    c = lax.broadcasted_iota(jnp.int32, (BAND, 2 * BAND), 1)
    band_ok = (c >= a) & (c <= a + BAND)
    bias = jnp.where(band_ok, 0.0, NEG_BIG).astype(F32)
    bias_first = jnp.where(band_ok & (c >= BAND), 0.0, NEG_BIG).astype(F32)
    bias_tile_start = jnp.where(tile == 0, bias_first, bias)

    for g, (_, dil) in enumerate(ATTN_GROUPS):
        q_ref, kc_ref, kp_ref, vc_ref, vp_ref = in_refs[5 * g:5 * g + 5]
        nblk = ATTN_TILE // dil // BAND
        for r in range(dil):
            for n in range(nblk):
                q = q_ref[r, n * BAND:(n + 1) * BAND, :]
                if n == 0:
                    k = jnp.concatenate([kp_ref[r], kc_ref[r, 0:BAND, :]], axis=0)
                    v = jnp.concatenate([vp_ref[r], vc_ref[r, 0:BAND, :]], axis=0)
                    b = bias_tile_start
                else:
                    k = kc_ref[r, (n - 1) * BAND:(n + 1) * BAND, :]
                    v = vc_ref[r, (n - 1) * BAND:(n + 1) * BAND, :]
                    b = bias
                s = lax.dot_general(q, k, (((1,), (1,)), ((), ())),
                                    preferred_element_type=F32) + b
                m = jnp.max(s, axis=-1, keepdims=True)
                p = jnp.exp(s - m)
                den = jnp.sum(p, axis=-1, keepdims=True)
                o = _dot(p.astype(BF16), v) / den
                lse = m + jnp.log(den)
                rows = pl.ds(n * BAND * dil + r, BAND, stride=dil)
                o_scr[g, rows, :] = o
                l_scr[g, rows, :] = jnp.broadcast_to(lse, (BAND, HEAD_DIM))

    l0, l1, l2 = l_scr[0], l_scr[1], l_scr[2]
    mx = jnp.maximum(jnp.maximum(l0, l1), l2)
    e0 = jnp.exp(l0 - mx)
    e1 = jnp.exp(l1 - mx)
    e2 = jnp.exp(l2 - mx)
    attn = (e0 * o_scr[0] + e1 * o_scr[1] + e2 * o_scr[2]) / (e0 + e1 + e2)
    gate = ag_ref[...].astype(F32)
    out_ref[...] = (attn * (gate * jax.nn.sigmoid(gate))).astype(BF16)


def _dil_attn(qkv_groups, ag):
    B, S, _ = ag.shape
    nt = S // ATTN_TILE
    in_specs = []
    args = []
    for (_, dil), qkv in zip(ATTN_GROUPS, qkv_groups):
        rows = ATTN_TILE // dil
        ratio = rows // BAND

        def cur(col0, rows=rows, dil=dil):
            return pl.BlockSpec((None, dil, rows, HEAD_DIM),
                                lambda b, t, h: (b, 0, t, col0 + h))

        def prev(col0, ratio=ratio, dil=dil):
            return pl.BlockSpec((None, dil, BAND, HEAD_DIM),
                                lambda b, t, h: (b, 0, jnp.maximum(t * ratio - 1, 0), col0 + h))

        in_specs += [cur(0), cur(ATTN_HEADS), prev(ATTN_HEADS),
                     cur(2 * ATTN_HEADS), prev(2 * ATTN_HEADS)]
        args += [qkv] * 5
    in_specs.append(pl.BlockSpec((None, ATTN_TILE, HEAD_DIM), lambda b, t, h: (b, t, h)))
    return pl.pallas_call(
        _dil_attn_kernel,
        grid=(B, nt, ATTN_HEADS),
        in_specs=in_specs,
        out_specs=pl.BlockSpec((None, ATTN_TILE, HEAD_DIM), lambda b, t, h: (b, t, h)),
        out_shape=jax.ShapeDtypeStruct((B, S, GROUP_WIDTH), BF16),
        scratch_shapes=[pltpu.VMEM((N_GROUPS, ATTN_TILE, HEAD_DIM), F32),
                        pltpu.VMEM((N_GROUPS, ATTN_TILE, HEAD_DIM), F32)],
        compiler_params=pltpu.CompilerParams(
            dimension_semantics=("arbitrary", "arbitrary", "arbitrary"),
            vmem_limit_bytes=VMEM_LIMIT),
        name="dil_attn",
    )(*args, ag)


def _retention_kernel(qk_ref, v_ref, rg_ref, gnw_ref, dmat_ref, qd_ref, kd_ref, cd_ref,
                      out_ref, state):
    @pl.when(pl.program_id(1) == 0)
    def _():
        state[...] = jnp.zeros_like(state)

    lane = lax.broadcasted_iota(jnp.int32, (RET_CHUNK, 2 * RET_KEY_DIM), 1)
    row = lax.broadcasted_iota(jnp.int32, (2 * RET_KEY_DIM, RET_VALUE_DIM), 0)
    for j in range(RET_HEADS // 2):
        cols = slice(j * 2 * RET_KEY_DIM, (j + 1) * 2 * RET_KEY_DIM)
        qp = qk_ref[:, cols]
        kp = qk_ref[:, RET_QK_WIDTH + cols.start:RET_QK_WIDTH + cols.stop]
        qx = (qp.astype(F32) * qd_ref[j]).astype(BF16)
        kx_t = (kp.astype(F32) * kd_ref[j]).T.astype(BF16)
        r_prev = state[j]
        r_prev_b = r_prev.astype(BF16)
        zero = jnp.zeros_like(qp)
        for hh in range(2):
            h = 2 * j + hh
            sel = (lane < RET_KEY_DIM) if hh == 0 else (lane >= RET_KEY_DIM)
            s = lax.dot_general(jnp.where(sel, qp, zero), kp, (((1,), (1,)), ((), ())),
                                preferred_element_type=F32)
            p = (s * dmat_ref[h]).astype(BF16)
            lhs = jnp.concatenate([p, jnp.where(sel, qx, zero)], axis=1)
            vh = v_ref[:, h * RET_VALUE_DIM:(h + 1) * RET_VALUE_DIM]
            rhs = jnp.concatenate([vh, r_prev_b], axis=0)
            o = _dot(lhs, rhs)
            o = o * lax.rsqrt(jnp.mean(o * o, axis=-1, keepdims=True) + NORM_EPS)
            hs = slice(h * RET_VALUE_DIM, (h + 1) * RET_VALUE_DIM)
            gate = rg_ref[:, hs].astype(F32)
            out_ref[:, hs] = (o * gnw_ref[:, hs] * (gate * jax.nn.sigmoid(gate))).astype(BF16)
        v_pair = v_ref[:, 2 * j * RET_VALUE_DIM:(2 * j + 2) * RET_VALUE_DIM]
        kv = _dot(kx_t, v_pair)
        kv = jnp.where(row < RET_KEY_DIM, kv[:, :RET_VALUE_DIM], kv[:, RET_VALUE_DIM:])
        state[j] = cd_ref[j] * r_prev + kv


def _retention(rqk, rv, rg, gnw, dmat, qd, kd, cd):
    B, S, _ = rv.shape
    nc = S // RET_CHUNK
    npair = RET_HEADS // 2

    def const(shape):
        return pl.BlockSpec(shape, lambda b, n: (0,) * len(shape))

    return pl.pallas_call(
        _retention_kernel,
        grid=(B, nc),
        in_specs=[
            pl.BlockSpec((None, RET_CHUNK, 2 * RET_QK_WIDTH), lambda b, n: (b, n, 0)),
            pl.BlockSpec((None, RET_CHUNK, RET_V_WIDTH), lambda b, n: (b, n, 0)),
            pl.BlockSpec((None, RET_CHUNK, RET_V_WIDTH), lambda b, n: (b, n, 0)),
            const((1, RET_V_WIDTH)),
            const((RET_HEADS, RET_CHUNK, RET_CHUNK)),
            const((npair, RET_CHUNK, 2 * RET_KEY_DIM)),
            const((npair, RET_CHUNK, 2 * RET_KEY_DIM)),
            const((npair, 2 * RET_KEY_DIM, RET_VALUE_DIM)),
        ],
        out_specs=pl.BlockSpec((None, RET_CHUNK, RET_V_WIDTH), lambda b, n: (b, n, 0)),
        out_shape=jax.ShapeDtypeStruct((B, S, RET_V_WIDTH), BF16),
        scratch_shapes=[pltpu.VMEM((npair, 2 * RET_KEY_DIM, RET_VALUE_DIM), F32)],
        compiler_params=pltpu.CompilerParams(
            dimension_semantics=("arbitrary", "arbitrary"), vmem_limit_bytes=VMEM_LIMIT),
        name="retention",
    )(rqk, rv, rg, gnw, dmat, qd, kd, cd)


def _merge_out_kernel(a_ref, r_ref, mg_ref, x_ref, wa_ref, wr_ref, wo_ref, bg_ref, lnf_ref,
                      out_ref):
    y_attn = _dot(a_ref[...], wa_ref[...])
    y_ret = _dot(r_ref[...], wr_ref[...])
    gates = jax.nn.sigmoid(mg_ref[...].astype(F32) + bg_ref[...])
    merged = gates[:, :D_MODEL] * y_attn + gates[:, D_MODEL:] * y_ret
    h = x_ref[...] + _dot(merged.astype(BF16), wo_ref[...])
    out_ref[...] = _rmsnorm_rows(h, lnf_ref[...])


def _merge_out(attn_g, ret_g, mg, x, wa, wr, wo, bg, lnf):
    B, S, _ = x.shape
    nt = S // OUT_ROWS

    def rows(n):
        return pl.BlockSpec((None, OUT_ROWS, n), lambda b, t: (b, t, 0))

    def const(shape):
        return pl.BlockSpec(shape, lambda b, t: (0,) * len(shape))

    return pl.pallas_call(
        _merge_out_kernel,
        grid=(B, nt),
        in_specs=[rows(GROUP_WIDTH), rows(RET_V_WIDTH), rows(2 * D_MODEL), rows(D_MODEL),
                  const((GROUP_WIDTH, D_MODEL)), const((RET_V_WIDTH, D_MODEL)),
                  const((D_MODEL, D_MODEL)), const((1, 2 * D_MODEL)), const((1, D_MODEL))],
        out_specs=rows(D_MODEL),
        out_shape=jax.ShapeDtypeStruct((B, S, D_MODEL), F32),
        compiler_params=pltpu.CompilerParams(
            dimension_semantics=("arbitrary", "arbitrary"), vmem_limit_bytes=VMEM_LIMIT),
        name="merge_out",
    )(attn_g, ret_g, mg, x, wa, wr, wo, bg, lnf)


def _rope_tables(S):
    half = RET_KEY_DIM // 2
    pos = jnp.arange(S, dtype=F32)
    inv_freq = ROPE_BASE ** (-jnp.linspace(0.0, 1.0, half, dtype=F32))
    ang = pos[:, None] * inv_freq[None, :]
    cos = jnp.cos(ang)
    sin = jnp.sin(ang)
    cos_h = jnp.concatenate([cos, cos], axis=1)
    sin_h = jnp.concatenate([-sin, sin], axis=1)
    return jnp.tile(cos_h, (1, RET_HEADS)), jnp.tile(sin_h, (1, RET_HEADS))


def _retention_constants():
    H, C, dk, dv = RET_HEADS, RET_CHUNK, RET_KEY_DIM, RET_VALUE_DIM
    log_gamma = jnp.log(1.0 - 2.0 ** (-5.0 - jnp.arange(H, dtype=F32)))
    idx = jnp.arange(C, dtype=F32)
    diff = idx[:, None] - idx[None, :]
    dmat = jnp.where(diff[None] >= 0,
                     jnp.exp(jnp.maximum(diff, 0.0)[None] * log_gamma[:, None, None]), 0.0)
    key_decay = jnp.exp((C - 1 - idx)[None, :] * log_gamma[:, None])
    query_decay = jnp.exp((idx + 1.0)[None, :] * log_gamma[:, None])
    chunk_decay = jnp.exp(C * log_gamma)

    def pair_lanes(t):
        t = jnp.broadcast_to(t[:, :, None], (H, C, dk)).reshape(H // 2, 2, C, dk)
        return jnp.concatenate([t[:, 0], t[:, 1]], axis=-1)

    cd = jnp.broadcast_to(chunk_decay[:, None, None], (H, dk, dv)).reshape(H // 2, 2 * dk, dv)
    return dmat, pair_lanes(query_decay), pair_lanes(key_decay), cd


def kernel(x, ln1_w, w_in, b_gate, attn_proj, ret_proj, ret_gn_w, w_out, lnf_w):
    B, S, _ = x.shape
    assert w_in.shape[0] == 1, "single layer"
    w = w_in[0]
    o_q, o_k, o_v = 0, QKV_WIDTH, 2 * QKV_WIDTH
    o_ag = 3 * QKV_WIDTH
    o_rq = o_ag + GROUP_WIDTH
    o_rk = o_rq + RET_QK_WIDTH
    o_rv = o_rk + RET_QK_WIDTH
    o_rg = o_rv + RET_V_WIDTH
    o_mg = o_rg + RET_V_WIDTH

    def cols(start, width):
        return w[:, start:start + width]

    def group_w(g):
        off = g * GROUP_WIDTH
        return jnp.concatenate([cols(o_q + off, GROUP_WIDTH), cols(o_k + off, GROUP_WIDTH),
                                cols(o_v + off, GROUP_WIDTH)], axis=1).astype(BF16)

    e = jnp.arange(RET_KEY_DIM // 2)
    head_perm = jnp.concatenate([2 * e, 2 * e + 1])
    perm = (jnp.arange(RET_HEADS)[:, None] * RET_KEY_DIM + head_perm[None, :]).reshape(-1)
    wqk = jnp.concatenate([cols(o_rq, RET_QK_WIDTH)[:, perm],
                           cols(o_rk, RET_QK_WIDTH)[:, perm]], axis=1).astype(BF16)

    lnw = ln1_w[0].reshape(1, D_MODEL)
    q1, q2, q3 = _qkv_proj(x, lnw, group_w(0), group_w(1), group_w(2))
    cos_t, sin_t = _rope_tables(S)
    ag, rg, mg, rv, rqk = _nat_proj(
        x, lnw, cols(o_ag, GROUP_WIDTH).astype(BF16), cols(o_rg, RET_V_WIDTH).astype(BF16),
        cols(o_mg, 2 * D_MODEL).astype(BF16), cols(o_rv, RET_V_WIDTH).astype(BF16), wqk,
        cos_t, sin_t)

    attn_g = _dil_attn((q1.reshape(B, 1, S, QKV_WIDTH), q2, q3), ag)

    dmat, qd, kd, cd = _retention_constants()
    ret_g = _retention(rqk, rv, rg, ret_gn_w[0].reshape(1, RET_V_WIDTH), dmat, qd, kd, cd)

    return _merge_out(attn_g, ret_g, mg, x,
                      attn_proj[0].astype(BF16), ret_proj[0].astype(BF16),
                      w_out[0].astype(BF16), b_gate[0].reshape(1, 2 * D_MODEL),
                      lnf_w.reshape(1, D_MODEL))
```

```python
import jax
import jax.numpy as jnp
from jax import lax
from jax.experimental import pallas as pl
from jax.experimental.pallas import tpu as pltpu

D_MODEL = 1024
ATTN_GROUPS = ((128, 1), (512, 4), (2048, 16))
N_GROUPS = 3
ATTN_HEADS = 4
HEAD_DIM = 128
GROUP_WIDTH = ATTN_HEADS * HEAD_DIM
QKV_WIDTH = N_GROUPS * GROUP_WIDTH
BAND = 128
RET_HEADS = 8
RET_KEY_DIM = 64
RET_VALUE_DIM = 128
RET_QK_WIDTH = RET_HEADS * RET_KEY_DIM
RET_V_WIDTH = RET_HEADS * RET_VALUE_DIM
RET_CHUNK = 128
ROPE_BASE = 10000.0
NORM_EPS = 1e-6
NEG_BIG = -1e30
LANES = 128

PROJ_ROWS = 512
ATTN_TILE = 2048
RET_BATCH = 4
OUT_ROWS = 512
VMEM_LIMIT = 56 * 1024 * 1024

BF16 = jnp.bfloat16
F32 = jnp.float32


def _rmsnorm_rows(x, w):
    return x * lax.rsqrt(jnp.mean(x * x, axis=-1, keepdims=True) + NORM_EPS) * w


def _dot(a, b):
    return jnp.dot(a, b, preferred_element_type=F32)


def _silu(t):
    return t * jax.nn.sigmoid(t)


def _qkv_proj_kernel(x_ref, lnw_ref, w1_ref, w2_ref, w3_ref,
                     o1_ref, o2_ref, o3_ref, xn_scr):
    xn = _rmsnorm_rows(x_ref[...], lnw_ref[...])
    n_lane_tiles = D_MODEL // LANES
    for ct in range(n_lane_tiles):
        xn_scr[ct] = xn[:, ct * LANES:(ct + 1) * LANES]
    q_scale = HEAD_DIM ** -0.5

    def strided_rows(r, rows, dil):
        return jnp.concatenate(
            [xn_scr[ct, pl.ds(r, rows, stride=dil), :] for ct in range(n_lane_tiles)], axis=1)

    def project(lhs, w_ref):
        r = _dot(lhs, w_ref[...])
        q = r[:, :GROUP_WIDTH] * q_scale
        return jnp.concatenate([q, r[:, GROUP_WIDTH:]], axis=1).astype(BF16)

    o1_ref[...] = project(xn.astype(BF16), w1_ref)

    for o_ref, w_ref, (_, dil) in ((o2_ref, w2_ref, ATTN_GROUPS[1]),
                                   (o3_ref, w3_ref, ATTN_GROUPS[2])):
        rows = PROJ_ROWS // dil
        lhs = jnp.concatenate(
            [strided_rows(r, rows, dil).astype(BF16) for r in range(dil)], axis=0)
        res = project(lhs, w_ref)
        for r in range(dil):
            o_ref[r] = res[r * rows:(r + 1) * rows]


def _qkv_proj(x, lnw, w1, w2, w3):
    B, S, _ = x.shape
    d2, d3 = ATTN_GROUPS[1][1], ATTN_GROUPS[2][1]
    nt = S // PROJ_ROWS
    wspec = pl.BlockSpec((D_MODEL, QKV_WIDTH), lambda b, t: (0, 0))
    return pl.pallas_call(
        _qkv_proj_kernel,
        grid=(B, nt),
        in_specs=[
            pl.BlockSpec((None, PROJ_ROWS, D_MODEL), lambda b, t: (b, t, 0)),
            pl.BlockSpec((1, D_MODEL), lambda b, t: (0, 0)),
            wspec, wspec, wspec,
        ],
        out_specs=[
            pl.BlockSpec((None, PROJ_ROWS, QKV_WIDTH), lambda b, t: (b, t, 0)),
            pl.BlockSpec((None, d2, PROJ_ROWS // d2, QKV_WIDTH), lambda b, t: (b, 0, t, 0)),
            pl.BlockSpec((None, d3, PROJ_ROWS // d3, QKV_WIDTH), lambda b, t: (b, 0, t, 0)),
        ],
        out_shape=[
            jax.ShapeDtypeStruct((B, S, QKV_WIDTH), BF16),
            jax.ShapeDtypeStruct((B, d2, S // d2, QKV_WIDTH), BF16),
            jax.ShapeDtypeStruct((B, d3, S // d3, QKV_WIDTH), BF16),
        ],
        scratch_shapes=[pltpu.VMEM((D_MODEL // LANES, PROJ_ROWS, LANES), F32)],
        compiler_params=pltpu.CompilerParams(
            dimension_semantics=("arbitrary", "arbitrary"), vmem_limit_bytes=VMEM_LIMIT),
        name="qkv_proj",
    )(x, lnw, w1, w2, w3)


def _swap_pair_halves(t):
    n = t.shape[-1]
    lane = lax.broadcasted_iota(jnp.int32, t.shape, 1)
    up = pltpu.roll(t, n - RET_KEY_DIM // 2, axis=1)
    down = pltpu.roll(t, RET_KEY_DIM // 2, axis=1)
    return jnp.where(lane % RET_KEY_DIM < RET_KEY_DIM // 2, up, down)


def _nat_proj_kernel(x_ref, lnw_ref, wqk_ref, wg_ref, wrg_ref, wmg_ref, wrv_ref,
                     cos_ref, sin_ref, kdec_ref, bg_ref,
                     rq_ref, rkt_ref, ag_ref, rg_ref, mg_ref, rv_ref):
    xb = _rmsnorm_rows(x_ref[...], lnw_ref[...]).astype(BF16)
    qk = _dot(xb, wqk_ref[...])
    cos = cos_ref[...]
    sin = sin_ref[...]
    q = qk[:, :RET_QK_WIDTH]
    k = qk[:, RET_QK_WIDTH:]
    rq_ref[...] = (q * cos + _swap_pair_halves(q) * sin).astype(BF16)
    k = (k * cos + _swap_pair_halves(k) * sin) * kdec_ref[...]
    for c in range(PROJ_ROWS // RET_CHUNK):
        rkt_ref[c] = k[c * RET_CHUNK:(c + 1) * RET_CHUNK, :].T.astype(BF16)
    ag_ref[...] = _silu(_dot(xb, wg_ref[...])).astype(BF16)
    rg_ref[...] = _silu(_dot(xb, wrg_ref[...])).astype(BF16)
    mg_ref[...] = jax.nn.sigmoid(_dot(xb, wmg_ref[...]) + bg_ref[...]).astype(BF16)
    rv_ref[...] = _dot(xb, wrv_ref[...]).astype(BF16)


def _nat_proj(x, lnw, wqk, wg, wrg, wmg, wrv, cos_t, sin_t, kdec, bg):
    B, S, _ = x.shape
    nt = S // PROJ_ROWS
    cpt = PROJ_ROWS // RET_CHUNK

    def wspec(n):
        return pl.BlockSpec((D_MODEL, n), lambda t, b: (0, 0))

    def ospec(n):
        return pl.BlockSpec((None, PROJ_ROWS, n), lambda t, b: (b, t, 0))

    widths = (GROUP_WIDTH, RET_V_WIDTH, 2 * D_MODEL, RET_V_WIDTH)
    tspec = pl.BlockSpec((PROJ_ROWS, RET_QK_WIDTH), lambda t, b: (t, 0))
    return pl.pallas_call(
        _nat_proj_kernel,
        grid=(nt, B),
        in_specs=[
            pl.BlockSpec((None, PROJ_ROWS, D_MODEL), lambda t, b: (b, t, 0)),
            pl.BlockSpec((1, D_MODEL), lambda t, b: (0, 0)),
            wspec(2 * RET_QK_WIDTH),
            *[wspec(n) for n in widths],
            tspec, tspec,
            pl.BlockSpec((PROJ_ROWS, RET_QK_WIDTH), lambda t, b: (0, 0)),
            pl.BlockSpec((1, 2 * D_MODEL), lambda t, b: (0, 0)),
        ],
        out_specs=[
            ospec(RET_QK_WIDTH),
            pl.BlockSpec((None, cpt, RET_QK_WIDTH, RET_CHUNK), lambda t, b: (b, t, 0, 0)),
            *[ospec(n) for n in widths],
        ],
        out_shape=[
            jax.ShapeDtypeStruct((B, S, RET_QK_WIDTH), BF16),
            jax.ShapeDtypeStruct((B, S // RET_CHUNK, RET_QK_WIDTH, RET_CHUNK), BF16),
            *[jax.ShapeDtypeStruct((B, S, n), BF16) for n in widths],
        ],
        compiler_params=pltpu.CompilerParams(
            dimension_semantics=("arbitrary", "arbitrary"), vmem_limit_bytes=VMEM_LIMIT),
        name="nat_proj",
    )(x, lnw, wqk, wg, wrg, wmg, wrv, cos_t, sin_t, kdec, bg)


def _dil_attn_kernel(*refs):
    in_refs = refs[:5 * N_GROUPS]
    ag_ref = refs[5 * N_GROUPS]
    out_ref = refs[5 * N_GROUPS + 1]
    o_scr, l_scr = refs[5 * N_GROUPS + 2:]
    tile = pl.program_id(1)

    a = lax.broadcasted_iota(jnp.int32, (BAND, 2 * BAND), 0)
    c = lax.broadcasted_iota(jnp.int32, (BAND, 2 * BAND), 1)
    band_ok = (c >= a) & (c <= a + BAND)
    bias = jnp.where(band_ok, 0.0, NEG_BIG).astype(F32)
    bias_first = jnp.where(band_ok & (c >= BAND), 0.0, NEG_BIG).astype(F32)
    bias_tile_start = jnp.where(tile == 0, bias_first, bias)

    for g, (_, dil) in enumerate(ATTN_GROUPS):
        q_ref, kc_ref, kp_ref, vc_ref, vp_ref = in_refs[5 * g:5 * g + 5]
        nblk = ATTN_TILE // dil // BAND
        for r in range(dil):
            for n in range(nblk):
                q = q_ref[r, n * BAND:(n + 1) * BAND, :]
                if n == 0:
                    k = jnp.concatenate([kp_ref[r], kc_ref[r, 0:BAND, :]], axis=0)
                    v = jnp.concatenate([vp_ref[r], vc_ref[r, 0:BAND, :]], axis=0)
                    b = bias_tile_start
                else:
                    k = kc_ref[r, (n - 1) * BAND:(n + 1) * BAND, :]
                    v = vc_ref[r, (n - 1) * BAND:(n + 1) * BAND, :]
                    b = bias
                s = lax.dot_general(q, k, (((1,), (1,)), ((), ())),
                                    preferred_element_type=F32) + b
                m = jnp.max(s, axis=-1, keepdims=True)
                p = jnp.exp(s - m)
                den = jnp.sum(p, axis=-1, keepdims=True)
                o = _dot(p.astype(BF16), v) / den
                lse = m + jnp.log(den)
                rows = pl.ds(n * BAND * dil + r, BAND, stride=dil)
                o_scr[g, rows, :] = o
                l_scr[g, rows, :] = jnp.broadcast_to(lse, (BAND, HEAD_DIM))

    l0, l1, l2 = l_scr[0], l_scr[1], l_scr[2]
    mx = jnp.maximum(jnp.maximum(l0, l1), l2)
    e0 = jnp.exp(l0 - mx)
    e1 = jnp.exp(l1 - mx)
    e2 = jnp.exp(l2 - mx)
    attn = (e0 * o_scr[0] + e1 * o_scr[1] + e2 * o_scr[2]) / (e0 + e1 + e2)
    out_ref[...] = (attn * ag_ref[...].astype(F32)).astype(BF16)


def _dil_attn(qkv_groups, ag):
    B, S, _ = ag.shape
    nt = S // ATTN_TILE
    in_specs = []
    args = []
    for (_, dil), qkv in zip(ATTN_GROUPS, qkv_groups):
        rows = ATTN_TILE // dil
        ratio = rows // BAND

        def cur(col0, rows=rows, dil=dil):
            return pl.BlockSpec((None, dil, rows, HEAD_DIM),
                                lambda b, t, h: (b, 0, t, col0 + h))

        def prev(col0, ratio=ratio, dil=dil):
            return pl.BlockSpec((None, dil, BAND, HEAD_DIM),
                                lambda b, t, h: (b, 0, jnp.maximum(t * ratio - 1, 0), col0 + h))

        in_specs += [cur(0), cur(ATTN_HEADS), prev(ATTN_HEADS),
                     cur(2 * ATTN_HEADS), prev(2 * ATTN_HEADS)]
        args += [qkv] * 5
    in_specs.append(pl.BlockSpec((None, ATTN_TILE, HEAD_DIM), lambda b, t, h: (b, t, h)))
    return pl.pallas_call(
        _dil_attn_kernel,
        grid=(B, nt, ATTN_HEADS),
        in_specs=in_specs,
        out_specs=pl.BlockSpec((None, ATTN_TILE, HEAD_DIM), lambda b, t, h: (b, t, h)),
        out_shape=jax.ShapeDtypeStruct((B, S, GROUP_WIDTH), BF16),
        scratch_shapes=[pltpu.VMEM((N_GROUPS, ATTN_TILE, HEAD_DIM), F32),
                        pltpu.VMEM((N_GROUPS, ATTN_TILE, HEAD_DIM), F32)],
        compiler_params=pltpu.CompilerParams(
            dimension_semantics=("arbitrary", "arbitrary", "arbitrary"),
            vmem_limit_bytes=VMEM_LIMIT),
        name="dil_attn",
    )(*args, ag)


def _retention_kernel(q_ref, kt_ref, v_ref, gate_ref, gnw_ref, eps_ref, cd_ref,
                      out_ref, state, p_scr):
    @pl.when(pl.program_id(1) == 0)
    def _():
        state[...] = jnp.zeros_like(state)

    dk, dv, C = RET_KEY_DIM, RET_VALUE_DIM, RET_CHUNK
    row_k = lax.broadcasted_iota(jnp.int32, (2 * dk, C), 0)
    row = lax.broadcasted_iota(jnp.int32, (2 * dk, 2 * dv), 0)
    col = lax.broadcasted_iota(jnp.int32, (C, 2 * dv), 1)
    diag_blk = (row < dk) == (col < dv)
    causal = (col % C) <= lax.broadcasted_iota(jnp.int32, (C, 2 * C), 0)
    units = [(bi, j) for bi in range(RET_BATCH) for j in range(RET_HEADS // 2)]

    def q_pair(bi, j):
        return q_ref[bi, :, j * 2 * dk:(j + 1) * 2 * dk]

    def kt_pair(bi, j):
        return kt_ref[bi, j * 2 * dk:(j + 1) * 2 * dk, :]

    def v_pair(bi, j):
        return v_ref[bi, :, j * 2 * dv:(j + 1) * 2 * dv]

    for bi, j in units:
        kt = kt_pair(bi, j)
        zero = jnp.zeros_like(kt)
        kt_blk = jnp.concatenate([jnp.where(row_k < dk, kt, zero),
                                  jnp.where(row_k >= dk, kt, zero)], axis=1)
        p_scr[bi, j] = jnp.where(causal, _dot(q_pair(bi, j), kt_blk), 0.0).astype(BF16)

    for bi, j in units:
        v = v_pair(bi, j)
        zero = jnp.zeros_like(v)
        v_blk = jnp.concatenate([jnp.where(col < dv, v, zero),
                                 jnp.where(col >= dv, v, zero)], axis=0)
        u = _dot(jnp.concatenate([p_scr[bi, j], q_pair(bi, j)], axis=1),
                 jnp.concatenate([v_blk, state[bi, j].astype(BF16)], axis=0))
        for hh in range(2):
            h = 2 * j + hh
            hs = slice(h * dv, (h + 1) * dv)
            uh = u[:, hh * dv:(hh + 1) * dv]
            uh = uh * lax.rsqrt(jnp.mean(uh * uh, axis=-1, keepdims=True) + eps_ref[h])
            out_ref[bi, :, hs] = (uh * gnw_ref[:, hs]
                                  * gate_ref[bi, :, hs].astype(F32)).astype(BF16)

    for bi, j in units:
        kv = _dot(kt_pair(bi, j), v_pair(bi, j))
        state[bi, j] = cd_ref[j] * (state[bi, j] + jnp.where(diag_blk, kv, 0.0))


def _retention(rq, rkt, rv, gate, gnw, eps_t, cd):
    B, S, _ = rv.shape
    nc = S // RET_CHUNK
    npair = RET_HEADS // 2
    pair_shape = (RET_BATCH, npair, 2 * RET_KEY_DIM, 2 * RET_VALUE_DIM)

    def const(shape):
        return pl.BlockSpec(shape, lambda b, n: (0,) * len(shape))

    def rows(width):
        return pl.BlockSpec((RET_BATCH, RET_CHUNK, width), lambda b, n: (b, n, 0))

    return pl.pallas_call(
        _retention_kernel,
        grid=(B // RET_BATCH, nc),
        in_specs=[
            rows(RET_QK_WIDTH),
            pl.BlockSpec((RET_BATCH, None, RET_QK_WIDTH, RET_CHUNK), lambda b, n: (b, n, 0, 0)),
            rows(RET_V_WIDTH),
            rows(RET_V_WIDTH),
            const((1, RET_V_WIDTH)),
            const((RET_HEADS, RET_CHUNK, RET_VALUE_DIM)),
            const(pair_shape[1:]),
        ],
        out_specs=rows(RET_V_WIDTH),
        out_shape=jax.ShapeDtypeStruct((B, S, RET_V_WIDTH), BF16),
        scratch_shapes=[pltpu.VMEM(pair_shape, F32),
                        pltpu.VMEM((RET_BATCH, npair, RET_CHUNK, 2 * RET_CHUNK), BF16)],
        compiler_params=pltpu.CompilerParams(
            dimension_semantics=("arbitrary", "arbitrary"), vmem_limit_bytes=VMEM_LIMIT),
        name="retention",
    )(rq, rkt, rv, gate, gnw, eps_t, cd)


def _merge_out_kernel(a_ref, r_ref, mg_ref, x_ref, wa_ref, wr_ref, wo_ref, lnf_ref, out_ref):
    y_attn = _dot(a_ref[...], wa_ref[...])
    y_ret = _dot(r_ref[...], wr_ref[...])
    merged = (mg_ref[:, :D_MODEL].astype(F32) * y_attn
              + mg_ref[:, D_MODEL:].astype(F32) * y_ret)
    h = x_ref[...] + _dot(merged.astype(BF16), wo_ref[...])
    out_ref[...] = _rmsnorm_rows(h, lnf_ref[...])


def _merge_out(attn_g, ret_g, mg, x, wa, wr, wo, lnf):
    B, S, _ = x.shape
    nt = S // OUT_ROWS

    def rows(n):
        return pl.BlockSpec((None, OUT_ROWS, n), lambda b, t: (b, t, 0))

    def const(shape):
        return pl.BlockSpec(shape, lambda b, t: (0,) * len(shape))

    return pl.pallas_call(
        _merge_out_kernel,
        grid=(B, nt),
        in_specs=[rows(GROUP_WIDTH), rows(RET_V_WIDTH), rows(2 * D_MODEL), rows(D_MODEL),
                  const((GROUP_WIDTH, D_MODEL)), const((RET_V_WIDTH, D_MODEL)),
                  const((D_MODEL, D_MODEL)), const((1, D_MODEL))],
        out_specs=rows(D_MODEL),
        out_shape=jax.ShapeDtypeStruct((B, S, D_MODEL), F32),
        compiler_params=pltpu.CompilerParams(
            dimension_semantics=("arbitrary", "arbitrary"), vmem_limit_bytes=VMEM_LIMIT),
        name="merge_out",
    )(attn_g, ret_g, mg, x, wa, wr, wo, lnf)


def _rope_tables(S):
    half = RET_KEY_DIM // 2
    pos = jnp.arange(S, dtype=F32)
    inv_freq = ROPE_BASE ** (-jnp.linspace(0.0, 1.0, half, dtype=F32))
    ang = pos[:, None] * inv_freq[None, :]
    cos = jnp.cos(ang)
    sin = jnp.sin(ang)
    cos_h = jnp.concatenate([cos, cos], axis=1)
    sin_h = jnp.concatenate([-sin, sin], axis=1)
    return jnp.tile(cos_h, (1, RET_HEADS)), jnp.tile(sin_h, (1, RET_HEADS))


def _retention_constants():
    H, C, dk, dv = RET_HEADS, RET_CHUNK, RET_KEY_DIM, RET_VALUE_DIM
    log_gamma = jnp.log(1.0 - 2.0 ** (-5.0 - jnp.arange(H, dtype=F32)))
    idx = jnp.arange(C, dtype=F32)
    inv_decay = jnp.exp(-(idx + 1.0)[None, :] * log_gamma[:, None])
    chunk_decay = jnp.exp(C * log_gamma)
    kdec = jnp.repeat(inv_decay.T, dk, axis=1) * dk ** -0.5
    kdec = jnp.tile(kdec, (PROJ_ROWS // C, 1))
    eps_t = jnp.broadcast_to((NORM_EPS * inv_decay * inv_decay)[:, :, None], (H, C, dv))
    cd = jnp.broadcast_to(chunk_decay[:, None, None], (H, dk, 2 * dv)).reshape(H // 2, 2 * dk, 2 * dv)
    return kdec, eps_t, cd


def kernel(x, ln1_w, w_in, b_gate, attn_proj, ret_proj, ret_gn_w, w_out, lnf_w):
    B, S, _ = x.shape
    assert w_in.shape[0] == 1, "single layer"
    w = w_in[0]
    o_q, o_k, o_v = 0, QKV_WIDTH, 2 * QKV_WIDTH
    o_ag = 3 * QKV_WIDTH
    o_rq = o_ag + GROUP_WIDTH
    o_rk = o_rq + RET_QK_WIDTH
    o_rv = o_rk + RET_QK_WIDTH
    o_rg = o_rv + RET_V_WIDTH
    o_mg = o_rg + RET_V_WIDTH

    def cols(start, width):
        return w[:, start:start + width]

    def group_w(g):
        off = g * GROUP_WIDTH
        return jnp.concatenate([cols(o_q + off, GROUP_WIDTH), cols(o_k + off, GROUP_WIDTH),
                                cols(o_v + off, GROUP_WIDTH)], axis=1).astype(BF16)

    e = jnp.arange(RET_KEY_DIM // 2)
    head_perm = jnp.concatenate([2 * e, 2 * e + 1])
    perm = (jnp.arange(RET_HEADS)[:, None] * RET_KEY_DIM + head_perm[None, :]).reshape(-1)
    wqk = jnp.concatenate([cols(o_rq, RET_QK_WIDTH)[:, perm],
                           cols(o_rk, RET_QK_WIDTH)[:, perm]], axis=1).astype(BF16)

    lnw = ln1_w[0].reshape(1, D_MODEL)
    q1, q2, q3 = _qkv_proj(x, lnw, group_w(0), group_w(1), group_w(2))
    cos_t, sin_t = _rope_tables(S)
    kdec, eps_t, cd = _retention_constants()
    rq, rkt, ag, rg, mg, rv = _nat_proj(
        x, lnw, wqk, cols(o_ag, GROUP_WIDTH).astype(BF16), cols(o_rg, RET_V_WIDTH).astype(BF16),
        cols(o_mg, 2 * D_MODEL).astype(BF16), cols(o_rv, RET_V_WIDTH).astype(BF16),
        cos_t, sin_t, kdec, b_gate[0].reshape(1, 2 * D_MODEL))

    attn_g = _dil_attn((q1.reshape(B, 1, S, QKV_WIDTH), q2, q3), ag)
    ret_g = _retention(rq, rkt, rv, rg, ret_gn_w[0].reshape(1, RET_V_WIDTH), eps_t, cd)

    return _merge_out(attn_g, ret_g, mg, x,
                      attn_proj[0].astype(BF16), ret_proj[0].astype(BF16),
                      w_out[0].astype(BF16), lnf_w.reshape(1, D_MODEL))
```

```python
import jax
import jax.numpy as jnp
from jax import lax
from jax.experimental import pallas as pl
from jax.experimental.pallas import tpu as pltpu

D_MODEL = 1024
ATTN_GROUPS = ((128, 1), (512, 4), (2048, 16))
N_GROUPS = 3
ATTN_HEADS = 4
HEAD_DIM = 128
GROUP_WIDTH = ATTN_HEADS * HEAD_DIM
QKV_WIDTH = N_GROUPS * GROUP_WIDTH
BAND = 128
RET_HEADS = 8
RET_KEY_DIM = 64
RET_VALUE_DIM = 128
RET_QK_WIDTH = RET_HEADS * RET_KEY_DIM
RET_V_WIDTH = RET_HEADS * RET_VALUE_DIM
RET_CHUNK = 128
ROPE_BASE = 10000.0
NORM_EPS = 1e-6
NEG_BIG = -1e30
LANES = 128
LOG2E = 1.4426950408889634
MIX_STRIDE = 4

PROJ_ROWS = 512
ATTN_TILE = 2048
RET_BATCH = 4
OUT_ROWS = 512
VMEM_LIMIT = 56 * 1024 * 1024

BF16 = jnp.bfloat16
F32 = jnp.float32


def _rmsnorm_rows(x, w):
    return x * lax.rsqrt(jnp.mean(x * x, axis=-1, keepdims=True) + NORM_EPS) * w


def _dot(a, b):
    return jnp.dot(a, b, preferred_element_type=F32)


def _silu(t):
    return t * jax.nn.sigmoid(t)


def _qkv_proj_kernel(x_ref, lnw_ref, w1_ref, w2_ref, w3_ref,
                     o1_ref, o2_ref, o3_ref, xn_scr):
    xn = _rmsnorm_rows(x_ref[...], lnw_ref[...])
    n_lane_tiles = D_MODEL // LANES
    for ct in range(n_lane_tiles):
        xn_scr[ct] = xn[:, ct * LANES:(ct + 1) * LANES]
    q_scale = HEAD_DIM ** -0.5 * LOG2E

    def strided_rows(r, rows, dil):
        return jnp.concatenate(
            [xn_scr[ct, pl.ds(r, rows, stride=dil), :] for ct in range(n_lane_tiles)], axis=1)

    def project(lhs, w_ref):
        r = _dot(lhs, w_ref[...])
        q = r[:, :GROUP_WIDTH] * q_scale
        return jnp.concatenate([q, r[:, GROUP_WIDTH:]], axis=1).astype(BF16)

    sub = BAND // MIX_STRIDE
    lhs_q = jnp.concatenate(
        [strided_rows(blk * BAND + r4, sub, MIX_STRIDE).astype(BF16)
         for blk in range(PROJ_ROWS // BAND) for r4 in range(MIX_STRIDE)], axis=0)
    o1_ref[:, :GROUP_WIDTH] = (_dot(lhs_q, w1_ref[:, :GROUP_WIDTH]) * q_scale).astype(BF16)
    o1_ref[:, GROUP_WIDTH:] = _dot(xn.astype(BF16), w1_ref[:, GROUP_WIDTH:]).astype(BF16)

    for o_ref, w_ref, (_, dil) in ((o2_ref, w2_ref, ATTN_GROUPS[1]),
                                   (o3_ref, w3_ref, ATTN_GROUPS[2])):
        rows = PROJ_ROWS // dil
        lhs = jnp.concatenate(
            [strided_rows(r, rows, dil).astype(BF16) for r in range(dil)], axis=0)
        res = project(lhs, w_ref)
        for r in range(dil):
            o_ref[r] = res[r * rows:(r + 1) * rows]


def _qkv_proj(x, lnw, w1, w2, w3):
    B, S, _ = x.shape
    d2, d3 = ATTN_GROUPS[1][1], ATTN_GROUPS[2][1]
    nt = S // PROJ_ROWS
    wspec = pl.BlockSpec((D_MODEL, QKV_WIDTH), lambda b, t: (0, 0))
    return pl.pallas_call(
        _qkv_proj_kernel,
        grid=(B, nt),
        in_specs=[
            pl.BlockSpec((None, PROJ_ROWS, D_MODEL), lambda b, t: (b, t, 0)),
            pl.BlockSpec((1, D_MODEL), lambda b, t: (0, 0)),
            wspec, wspec, wspec,
        ],
        out_specs=[
            pl.BlockSpec((None, PROJ_ROWS, QKV_WIDTH), lambda b, t: (b, t, 0)),
            pl.BlockSpec((None, d2, PROJ_ROWS // d2, QKV_WIDTH), lambda b, t: (b, 0, t, 0)),
            pl.BlockSpec((None, d3, PROJ_ROWS // d3, QKV_WIDTH), lambda b, t: (b, 0, t, 0)),
        ],
        out_shape=[
            jax.ShapeDtypeStruct((B, S, QKV_WIDTH), BF16),
            jax.ShapeDtypeStruct((B, d2, S // d2, QKV_WIDTH), BF16),
            jax.ShapeDtypeStruct((B, d3, S // d3, QKV_WIDTH), BF16),
        ],
        scratch_shapes=[pltpu.VMEM((D_MODEL // LANES, PROJ_ROWS, LANES), F32)],
        compiler_params=pltpu.CompilerParams(
            dimension_semantics=("arbitrary", "arbitrary"), vmem_limit_bytes=VMEM_LIMIT),
        name="qkv_proj",
    )(x, lnw, w1, w2, w3)


def _swap_pair_halves(t):
    n = t.shape[-1]
    lane = lax.broadcasted_iota(jnp.int32, t.shape, 1)
    up = pltpu.roll(t, n - RET_KEY_DIM // 2, axis=1)
    down = pltpu.roll(t, RET_KEY_DIM // 2, axis=1)
    return jnp.where(lane % RET_KEY_DIM < RET_KEY_DIM // 2, up, down)


def _nat_proj_kernel(x_ref, lnw_ref, wqk_ref, wg_ref, wrg_ref, wmg_ref, wrv_ref,
                     cos_ref, sin_ref, kdec_ref, bg_ref,
                     rq_ref, rkt_ref, ag_ref, rg_ref, mg_ref, rv_ref):
    xb = _rmsnorm_rows(x_ref[...], lnw_ref[...]).astype(BF16)
    qk = _dot(xb, wqk_ref[...])
    cos = cos_ref[...]
    sin = sin_ref[...]
    q = qk[:, :RET_QK_WIDTH]
    k = qk[:, RET_QK_WIDTH:]
    rq_ref[...] = (q * cos + _swap_pair_halves(q) * sin).astype(BF16)
    k = (k * cos + _swap_pair_halves(k) * sin) * kdec_ref[...]
    for c in range(PROJ_ROWS // RET_CHUNK):
        rkt_ref[c] = k[c * RET_CHUNK:(c + 1) * RET_CHUNK, :].T.astype(BF16)
    ag_ref[...] = _silu(_dot(xb, wg_ref[...])).astype(BF16)
    rg_ref[...] = _silu(_dot(xb, wrg_ref[...])).astype(BF16)
    mg_ref[...] = jax.nn.sigmoid(_dot(xb, wmg_ref[...]) + bg_ref[...]).astype(BF16)
    rv_ref[...] = _dot(xb, wrv_ref[...]).astype(BF16)


def _nat_proj(x, lnw, wqk, wg, wrg, wmg, wrv, cos_t, sin_t, kdec, bg):
    B, S, _ = x.shape
    nt = S // PROJ_ROWS
    cpt = PROJ_ROWS // RET_CHUNK

    def wspec(n):
        return pl.BlockSpec((D_MODEL, n), lambda t, b: (0, 0))

    def ospec(n):
        return pl.BlockSpec((None, PROJ_ROWS, n), lambda t, b: (b, t, 0))

    widths = (GROUP_WIDTH, RET_V_WIDTH, 2 * D_MODEL, RET_V_WIDTH)
    tspec = pl.BlockSpec((PROJ_ROWS, RET_QK_WIDTH), lambda t, b: (t, 0))
    return pl.pallas_call(
        _nat_proj_kernel,
        grid=(nt, B),
        in_specs=[
            pl.BlockSpec((None, PROJ_ROWS, D_MODEL), lambda t, b: (b, t, 0)),
            pl.BlockSpec((1, D_MODEL), lambda t, b: (0, 0)),
            wspec(2 * RET_QK_WIDTH),
            *[wspec(n) for n in widths],
            tspec, tspec,
            pl.BlockSpec((PROJ_ROWS, RET_QK_WIDTH), lambda t, b: (0, 0)),
            pl.BlockSpec((1, 2 * D_MODEL), lambda t, b: (0, 0)),
        ],
        out_specs=[
            ospec(RET_QK_WIDTH),
            pl.BlockSpec((None, cpt, RET_QK_WIDTH, RET_CHUNK), lambda t, b: (b, t, 0, 0)),
            *[ospec(n) for n in widths],
        ],
        out_shape=[
            jax.ShapeDtypeStruct((B, S, RET_QK_WIDTH), BF16),
            jax.ShapeDtypeStruct((B, S // RET_CHUNK, RET_QK_WIDTH, RET_CHUNK), BF16),
            *[jax.ShapeDtypeStruct((B, S, n), BF16) for n in widths],
        ],
        compiler_params=pltpu.CompilerParams(
            dimension_semantics=("arbitrary", "arbitrary"), vmem_limit_bytes=VMEM_LIMIT),
        name="nat_proj",
    )(x, lnw, wqk, wg, wrg, wmg, wrv, cos_t, sin_t, kdec, bg)


def _dil_attn_kernel(*refs):
    in_refs = refs[:5 * N_GROUPS]
    bias_ref, ag_ref, out_ref = refs[5 * N_GROUPS:5 * N_GROUPS + 3]
    acc_scr, max_scr, den_scr, nat_scr = refs[5 * N_GROUPS + 3:]
    first = (pl.program_id(1) == 0).astype(jnp.int32)
    ones = jnp.ones((2 * BAND, HEAD_DIM), BF16)
    quarter = ATTN_TILE // MIX_STRIDE

    def unit(g, r, n):
        dil = ATTN_GROUPS[g][1]
        q_ref, kc_ref, kp_ref, vc_ref, vp_ref = in_refs[5 * g:5 * g + 5]
        bias_base = 2 if dil == 1 else 0
        q = q_ref[r, n * BAND:(n + 1) * BAND, :]
        if n == 0:
            k = jnp.concatenate([kp_ref[r], kc_ref[r, 0:BAND, :]], axis=0)
            v = jnp.concatenate([vp_ref[r], vc_ref[r, 0:BAND, :]], axis=0)
            bias = bias_ref[bias_base + first]
        else:
            k = kc_ref[r, (n - 1) * BAND:(n + 1) * BAND, :]
            v = vc_ref[r, (n - 1) * BAND:(n + 1) * BAND, :]
            bias = bias_ref[bias_base]
        s = lax.dot_general(q, k, (((1,), (1,)), ((), ())), preferred_element_type=F32) + bias
        m = jnp.max(s, axis=-1, keepdims=True)
        p = jnp.exp2(s - m).astype(BF16)
        res = _dot(p, jnp.concatenate([v, ones], axis=1))
        acc, den = res[:, :HEAD_DIM], res[:, HEAD_DIM:]
        mb = jnp.broadcast_to(m, (BAND, HEAD_DIM))
        if dil == 1:
            sub = BAND // MIX_STRIDE
            for r4 in range(MIX_STRIDE):
                dst = pl.ds(r4 * quarter + n * sub, sub)
                src = slice(r4 * sub, (r4 + 1) * sub)
                acc_scr[g, dst, :] = acc[src]
                max_scr[g, dst, :] = mb[src]
                den_scr[g, dst, :] = den[src]
        else:
            if dil == MIX_STRIDE:
                dst = pl.ds(r * quarter + n * BAND, BAND)
            else:
                sub_stride = dil // MIX_STRIDE
                dst = pl.ds((r % MIX_STRIDE) * quarter + r // MIX_STRIDE
                            + n * BAND * sub_stride, BAND, stride=sub_stride)
            acc_scr[g, dst, :] = acc
            max_scr[g, dst, :] = mb
            den_scr[g, dst, :] = den

    def mix_quarter(r4):
        for ch in range(quarter // BAND):
            rows = pl.ds(r4 * quarter + ch * BAND, BAND)
            m0, m1, m2 = max_scr[0, rows, :], max_scr[1, rows, :], max_scr[2, rows, :]
            mx = jnp.maximum(jnp.maximum(m0, m1), m2)
            w0 = jnp.exp2(m0 - mx)
            w1 = jnp.exp2(m1 - mx)
            w2 = jnp.exp2(m2 - mx)
            num = w0 * acc_scr[0, rows, :] + w1 * acc_scr[1, rows, :] + w2 * acc_scr[2, rows, :]
            den = w0 * den_scr[0, rows, :] + w1 * den_scr[1, rows, :] + w2 * den_scr[2, rows, :]
            nat_scr[pl.ds(r4 + ch * BAND * MIX_STRIDE, BAND, stride=MIX_STRIDE), :] = num / den

    for n in range(ATTN_TILE // BAND):
        unit(0, 0, n)
    for r4 in range(MIX_STRIDE):
        for g in range(1, N_GROUPS):
            dil = ATTN_GROUPS[g][1]
            for r in range(r4, dil, MIX_STRIDE):
                for n in range(ATTN_TILE // dil // BAND):
                    unit(g, r, n)
        mix_quarter(r4)
    out_ref[...] = (nat_scr[...] * ag_ref[...].astype(F32)).astype(BF16)


def _attn_bias_tables():
    rho = jnp.arange(BAND)[:, None]
    c = jnp.arange(2 * BAND)[None, :]
    sub = BAND // MIX_STRIDE
    tables = []
    for a in (rho, MIX_STRIDE * (rho % sub) + rho // sub):
        ok = (c >= a) & (c <= a + BAND)
        tables += [ok, ok & (c >= BAND)]
    return jnp.where(jnp.stack(tables), 0.0, NEG_BIG).astype(F32)


def _dil_attn(qkv_groups, ag):
    B, S, _ = ag.shape
    nt = S // ATTN_TILE
    in_specs = []
    args = []
    for (_, dil), qkv in zip(ATTN_GROUPS, qkv_groups):
        rows = ATTN_TILE // dil
        ratio = rows // BAND

        def cur(col0, rows=rows, dil=dil):
            return pl.BlockSpec((None, dil, rows, HEAD_DIM),
                                lambda b, t, h: (b, 0, t, col0 + h))

        def prev(col0, ratio=ratio, dil=dil):
            return pl.BlockSpec((None, dil, BAND, HEAD_DIM),
                                lambda b, t, h: (b, 0, jnp.maximum(t * ratio - 1, 0), col0 + h))

        in_specs += [cur(0), cur(ATTN_HEADS), prev(ATTN_HEADS),
                     cur(2 * ATTN_HEADS), prev(2 * ATTN_HEADS)]
        args += [qkv] * 5
    bias = _attn_bias_tables()
    in_specs.append(pl.BlockSpec(bias.shape, lambda b, t, h: (0, 0, 0)))
    in_specs.append(pl.BlockSpec((None, ATTN_TILE, HEAD_DIM), lambda b, t, h: (b, t, h)))
    per_group = pltpu.VMEM((N_GROUPS, ATTN_TILE, HEAD_DIM), F32)
    return pl.pallas_call(
        _dil_attn_kernel,
        grid=(B, nt, ATTN_HEADS),
        in_specs=in_specs,
        out_specs=pl.BlockSpec((None, ATTN_TILE, HEAD_DIM), lambda b, t, h: (b, t, h)),
        out_shape=jax.ShapeDtypeStruct((B, S, GROUP_WIDTH), BF16),
        scratch_shapes=[per_group, per_group, per_group,
                        pltpu.VMEM((ATTN_TILE, HEAD_DIM), F32)],
        compiler_params=pltpu.CompilerParams(
            dimension_semantics=("arbitrary", "arbitrary", "arbitrary"),
            vmem_limit_bytes=VMEM_LIMIT),
        name="dil_attn",
    )(*args, bias, ag)


def _retention_kernel(q_ref, kt_ref, v_ref, gate_ref, gnw_ref, eps_ref, cd_ref,
                      out_ref, state, p_scr):
    @pl.when(pl.program_id(1) == 0)
    def _():
        state[...] = jnp.zeros_like(state)

    dk, dv, C = RET_KEY_DIM, RET_VALUE_DIM, RET_CHUNK
    row_k = lax.broadcasted_iota(jnp.int32, (2 * dk, C), 0)
    row = lax.broadcasted_iota(jnp.int32, (2 * dk, 2 * dv), 0)
    col = lax.broadcasted_iota(jnp.int32, (C, 2 * dv), 1)
    diag_blk = (row < dk) == (col < dv)
    causal = (col % C) <= lax.broadcasted_iota(jnp.int32, (C, 2 * C), 0)
    units = [(bi, j) for bi in range(RET_BATCH) for j in range(RET_HEADS // 2)]

    def q_pair(bi, j):
        return q_ref[bi, :, j * 2 * dk:(j + 1) * 2 * dk]

    def kt_pair(bi, j):
        return kt_ref[bi, j * 2 * dk:(j + 1) * 2 * dk, :]

    def v_pair(bi, j):
        return v_ref[bi, :, j * 2 * dv:(j + 1) * 2 * dv]

    for bi, j in units:
        kt = kt_pair(bi, j)
        zero = jnp.zeros_like(kt)
        kt_blk = jnp.concatenate([jnp.where(row_k < dk, kt, zero),
                                  jnp.where(row_k >= dk, kt, zero)], axis=1)
        p_scr[bi, j] = jnp.where(causal, _dot(q_pair(bi, j), kt_blk), 0.0).astype(BF16)

    for bi, j in units:
        v = v_pair(bi, j)
        zero = jnp.zeros_like(v)
        v_blk = jnp.concatenate([jnp.where(col < dv, v, zero),
                                 jnp.where(col >= dv, v, zero)], axis=0)
        u = _dot(jnp.concatenate([p_scr[bi, j], q_pair(bi, j)], axis=1),
                 jnp.concatenate([v_blk, state[bi, j].astype(BF16)], axis=0))
        for hh in range(2):
            h = 2 * j + hh
            hs = slice(h * dv, (h + 1) * dv)
            uh = u[:, hh * dv:(hh + 1) * dv]
            uh = uh * lax.rsqrt(jnp.mean(uh * uh, axis=-1, keepdims=True) + eps_ref[h])
            out_ref[bi, :, hs] = (uh * gnw_ref[:, hs]
                                  * gate_ref[bi, :, hs].astype(F32)).astype(BF16)

    for bi, j in units:
        kv = _dot(kt_pair(bi, j), v_pair(bi, j))
        state[bi, j] = cd_ref[j] * (state[bi, j] + jnp.where(diag_blk, kv, 0.0))


def _retention(rq, rkt, rv, gate, gnw, eps_t, cd):
    B, S, _ = rv.shape
    nc = S // RET_CHUNK
    npair = RET_HEADS // 2
    pair_shape = (RET_BATCH, npair, 2 * RET_KEY_DIM, 2 * RET_VALUE_DIM)

    def const(shape):
        return pl.BlockSpec(shape, lambda b, n: (0,) * len(shape))

    def rows(width):
        return pl.BlockSpec((RET_BATCH, RET_CHUNK, width), lambda b, n: (b, n, 0))

    return pl.pallas_call(
        _retention_kernel,
        grid=(B // RET_BATCH, nc),
        in_specs=[
            rows(RET_QK_WIDTH),
            pl.BlockSpec((RET_BATCH, None, RET_QK_WIDTH, RET_CHUNK), lambda b, n: (b, n, 0, 0)),
            rows(RET_V_WIDTH),
            rows(RET_V_WIDTH),
            const((1, RET_V_WIDTH)),
            const((RET_HEADS, RET_CHUNK, RET_VALUE_DIM)),
            const(pair_shape[1:]),
        ],
        out_specs=rows(RET_V_WIDTH),
        out_shape=jax.ShapeDtypeStruct((B, S, RET_V_WIDTH), BF16),
        scratch_shapes=[pltpu.VMEM(pair_shape, F32),
                        pltpu.VMEM((RET_BATCH, npair, RET_CHUNK, 2 * RET_CHUNK), BF16)],
        compiler_params=pltpu.CompilerParams(
            dimension_semantics=("arbitrary", "arbitrary"), vmem_limit_bytes=VMEM_LIMIT),
        name="retention",
    )(rq, rkt, rv, gate, gnw, eps_t, cd)


def _merge_out_kernel(a_ref, r_ref, mg_ref, x_ref, wa_ref, wr_ref, wo_ref, lnf_ref, out_ref):
    y_attn = _dot(a_ref[...], wa_ref[...])
    y_ret = _dot(r_ref[...], wr_ref[...])
    merged = (mg_ref[:, :D_MODEL].astype(F32) * y_attn
              + mg_ref[:, D_MODEL:].astype(F32) * y_ret)
    h = x_ref[...] + _dot(merged.astype(BF16), wo_ref[...])
    out_ref[...] = _rmsnorm_rows(h, lnf_ref[...])


def _merge_out(attn_g, ret_g, mg, x, wa, wr, wo, lnf):
    B, S, _ = x.shape
    nt = S // OUT_ROWS

    def rows(n):
        return pl.BlockSpec((None, OUT_ROWS, n), lambda b, t: (b, t, 0))

    def const(shape):
        return pl.BlockSpec(shape, lambda b, t: (0,) * len(shape))

    return pl.pallas_call(
        _merge_out_kernel,
        grid=(B, nt),
        in_specs=[rows(GROUP_WIDTH), rows(RET_V_WIDTH), rows(2 * D_MODEL), rows(D_MODEL),
                  const((GROUP_WIDTH, D_MODEL)), const((RET_V_WIDTH, D_MODEL)),
                  const((D_MODEL, D_MODEL)), const((1, D_MODEL))],
        out_specs=rows(D_MODEL),
        out_shape=jax.ShapeDtypeStruct((B, S, D_MODEL), F32),
        compiler_params=pltpu.CompilerParams(
            dimension_semantics=("arbitrary", "arbitrary"), vmem_limit_bytes=VMEM_LIMIT),
        name="merge_out",
    )(attn_g, ret_g, mg, x, wa, wr, wo, lnf)


def _rope_tables(S):
    half = RET_KEY_DIM // 2
    pos = jnp.arange(S, dtype=F32)
    inv_freq = ROPE_BASE ** (-jnp.linspace(0.0, 1.0, half, dtype=F32))
    ang = pos[:, None] * inv_freq[None, :]
    cos = jnp.cos(ang)
    sin = jnp.sin(ang)
    cos_h = jnp.concatenate([cos, cos], axis=1)
    sin_h = jnp.concatenate([-sin, sin], axis=1)
    return jnp.tile(cos_h, (1, RET_HEADS)), jnp.tile(sin_h, (1, RET_HEADS))


def _retention_constants():
    H, C, dk, dv = RET_HEADS, RET_CHUNK, RET_KEY_DIM, RET_VALUE_DIM
    log_gamma = jnp.log(1.0 - 2.0 ** (-5.0 - jnp.arange(H, dtype=F32)))
    idx = jnp.arange(C, dtype=F32)
    inv_decay = jnp.exp(-(idx + 1.0)[None, :] * log_gamma[:, None])
    chunk_decay = jnp.exp(C * log_gamma)
    kdec = jnp.repeat(inv_decay.T, dk, axis=1) * dk ** -0.5
    kdec = jnp.tile(kdec, (PROJ_ROWS // C, 1))
    eps_t = jnp.broadcast_to((NORM_EPS * inv_decay * inv_decay)[:, :, None], (H, C, dv))
    cd = jnp.broadcast_to(chunk_decay[:, None, None], (H, dk, 2 * dv)).reshape(H // 2, 2 * dk, 2 * dv)
    return kdec, eps_t, cd


def kernel(x, ln1_w, w_in, b_gate, attn_proj, ret_proj, ret_gn_w, w_out, lnf_w):
    B, S, _ = x.shape
    assert w_in.shape[0] == 1, "single layer"
    w = w_in[0]
    o_q, o_k, o_v = 0, QKV_WIDTH, 2 * QKV_WIDTH
    o_ag = 3 * QKV_WIDTH
    o_rq = o_ag + GROUP_WIDTH
    o_rk = o_rq + RET_QK_WIDTH
    o_rv = o_rk + RET_QK_WIDTH
    o_rg = o_rv + RET_V_WIDTH
    o_mg = o_rg + RET_V_WIDTH

    def cols(start, width):
        return w[:, start:start + width]

    def group_w(g):
        off = g * GROUP_WIDTH
        return jnp.concatenate([cols(o_q + off, GROUP_WIDTH), cols(o_k + off, GROUP_WIDTH),
                                cols(o_v + off, GROUP_WIDTH)], axis=1).astype(BF16)

    e = jnp.arange(RET_KEY_DIM // 2)
    head_perm = jnp.concatenate([2 * e, 2 * e + 1])
    perm = (jnp.arange(RET_HEADS)[:, None] * RET_KEY_DIM + head_perm[None, :]).reshape(-1)
    wqk = jnp.concatenate([cols(o_rq, RET_QK_WIDTH)[:, perm],
                           cols(o_rk, RET_QK_WIDTH)[:, perm]], axis=1).astype(BF16)

    lnw = ln1_w[0].reshape(1, D_MODEL)
    q1, q2, q3 = _qkv_proj(x, lnw, group_w(0), group_w(1), group_w(2))
    cos_t, sin_t = _rope_tables(S)
    kdec, eps_t, cd = _retention_constants()
    rq, rkt, ag, rg, mg, rv = _nat_proj(
        x, lnw, wqk, cols(o_ag, GROUP_WIDTH).astype(BF16), cols(o_rg, RET_V_WIDTH).astype(BF16),
        cols(o_mg, 2 * D_MODEL).astype(BF16), cols(o_rv, RET_V_WIDTH).astype(BF16),
        cos_t, sin_t, kdec, b_gate[0].reshape(1, 2 * D_MODEL))

    attn_g = _dil_attn((q1.reshape(B, 1, S, QKV_WIDTH), q2, q3), ag)
    ret_g = _retention(rq, rkt, rv, rg, ret_gn_w[0].reshape(1, RET_V_WIDTH), eps_t, cd)

    return _merge_out(attn_g, ret_g, mg, x,
                      attn_proj[0].astype(BF16), ret_proj[0].astype(BF16),
                      w_out[0].astype(BF16), lnf_w.reshape(1, D_MODEL))
```

```python
import jax
import jax.numpy as jnp
from jax import lax
from jax.experimental import pallas as pl
from jax.experimental.pallas import tpu as pltpu

D_MODEL = 1024
ATTN_GROUPS = ((128, 1), (512, 4), (2048, 16))
N_GROUPS = 3
ATTN_HEADS = 4
HEAD_DIM = 128
GROUP_WIDTH = ATTN_HEADS * HEAD_DIM
QKV_WIDTH = N_GROUPS * GROUP_WIDTH
BAND = 128
RET_HEADS = 8
RET_KEY_DIM = 64
RET_VALUE_DIM = 128
RET_QK_WIDTH = RET_HEADS * RET_KEY_DIM
RET_V_WIDTH = RET_HEADS * RET_VALUE_DIM
RET_CHUNK = 128
ROPE_BASE = 10000.0
COL_AG = 3 * QKV_WIDTH
COL_RQ = COL_AG + GROUP_WIDTH
COL_RV = COL_RQ + 2 * RET_QK_WIDTH
COL_RG = COL_RV + RET_V_WIDTH
COL_MG = COL_RG + RET_V_WIDTH
NORM_EPS = 1e-6
NEG_BIG = -1e30
LANES = 128
LOG2E = 1.4426950408889634
MIX_STRIDE = 4

PROJ_ROWS = 512
ATTN_TILE = 2048
RET_BATCH = 4
OUT_ROWS = 512
VMEM_LIMIT = 56 * 1024 * 1024

BF16 = jnp.bfloat16
F32 = jnp.float32


def _rmsnorm_rows(x, w):
    return x * lax.rsqrt(jnp.mean(x * x, axis=-1, keepdims=True) + NORM_EPS) * w


def _dot(a, b):
    return jnp.dot(a, b, preferred_element_type=F32)


def _silu(t):
    return t * jax.nn.sigmoid(t)


def _qkv_proj_kernel(x_ref, lnw_ref, *refs):
    w_refs = refs[:3 * N_GROUPS]
    out_refs = refs[3 * N_GROUPS:4 * N_GROUPS]
    xn_scr = refs[4 * N_GROUPS]
    xn = _rmsnorm_rows(x_ref[...], lnw_ref[...])
    n_lane_tiles = D_MODEL // LANES
    for ct in range(n_lane_tiles):
        xn_scr[ct] = xn[:, ct * LANES:(ct + 1) * LANES]
    q_scale = HEAD_DIM ** -0.5 * LOG2E
    q_cols = slice(0, GROUP_WIDTH)
    k_cols = slice(GROUP_WIDTH, 2 * GROUP_WIDTH)
    v_cols = slice(2 * GROUP_WIDTH, 3 * GROUP_WIDTH)

    def strided_rows(r, rows, dil):
        return jnp.concatenate(
            [xn_scr[ct, pl.ds(r, rows, stride=dil), :] for ct in range(n_lane_tiles)], axis=1)

    for g, (_, dil) in enumerate(ATTN_GROUPS):
        wq_ref, wk_ref, wv_ref = w_refs[3 * g:3 * g + 3]
        o_ref = out_refs[g]
        if dil == 1:
            sub = BAND // MIX_STRIDE
            lhs_q = jnp.concatenate(
                [strided_rows(blk * BAND + r4, sub, MIX_STRIDE).astype(BF16)
                 for blk in range(PROJ_ROWS // BAND) for r4 in range(MIX_STRIDE)], axis=0)
            lhs = xn.astype(BF16)
            o_ref[0, :, q_cols] = (_dot(lhs_q, wq_ref[...]) * q_scale).astype(BF16)
            o_ref[0, :, k_cols] = _dot(lhs, wk_ref[...]).astype(BF16)
            o_ref[0, :, v_cols] = _dot(lhs, wv_ref[...]).astype(BF16)
        else:
            rows = PROJ_ROWS // dil
            lhs = jnp.concatenate(
                [strided_rows(r, rows, dil).astype(BF16) for r in range(dil)], axis=0)
            q = (_dot(lhs, wq_ref[...]) * q_scale).astype(BF16)
            k = _dot(lhs, wk_ref[...]).astype(BF16)
            v = _dot(lhs, wv_ref[...]).astype(BF16)
            for r in range(dil):
                rs = slice(r * rows, (r + 1) * rows)
                o_ref[r, :, q_cols] = q[rs]
                o_ref[r, :, k_cols] = k[rs]
                o_ref[r, :, v_cols] = v[rs]


def _qkv_proj(x, lnw, w16):
    B, S, _ = x.shape
    nt = S // PROJ_ROWS
    w_specs = [pl.BlockSpec((D_MODEL, GROUP_WIDTH), lambda b, t, cb=part * N_GROUPS + g: (0, cb))
               for g in range(N_GROUPS) for part in range(3)]
    return pl.pallas_call(
        _qkv_proj_kernel,
        grid=(B, nt),
        in_specs=[
            pl.BlockSpec((None, PROJ_ROWS, D_MODEL), lambda b, t: (b, t, 0)),
            pl.BlockSpec((1, D_MODEL), lambda b, t: (0, 0)),
            *w_specs,
        ],
        out_specs=[pl.BlockSpec((None, dil, PROJ_ROWS // dil, QKV_WIDTH), lambda b, t: (b, 0, t, 0))
                   for _, dil in ATTN_GROUPS],
        out_shape=[jax.ShapeDtypeStruct((B, dil, S // dil, QKV_WIDTH), BF16)
                   for _, dil in ATTN_GROUPS],
        scratch_shapes=[pltpu.VMEM((D_MODEL // LANES, PROJ_ROWS, LANES), F32)],
        compiler_params=pltpu.CompilerParams(
            dimension_semantics=("arbitrary", "arbitrary"), vmem_limit_bytes=VMEM_LIMIT),
        name="qkv_proj",
    )(x, lnw, *([w16] * (3 * N_GROUPS)))


def _swap_pair_halves(t):
    n = t.shape[-1]
    lane = lax.broadcasted_iota(jnp.int32, t.shape, 1)
    up = pltpu.roll(t, n - RET_KEY_DIM // 2, axis=1)
    down = pltpu.roll(t, RET_KEY_DIM // 2, axis=1)
    return jnp.where(lane % RET_KEY_DIM < RET_KEY_DIM // 2, up, down)


def _nat_proj_kernel(x_ref, lnw_ref, wqk_ref, wg_ref, wrg_ref, wmg_ref, wrv_ref,
                     cos_ref, sin_ref, kdec_ref, bg_ref,
                     rq_ref, rkt_ref, ag_ref, rg_ref, mg_ref, rv_ref):
    xb = _rmsnorm_rows(x_ref[...], lnw_ref[...]).astype(BF16)
    qk = _dot(xb, wqk_ref[...])
    cos = jnp.concatenate([cos_ref[...]] * (RET_QK_WIDTH // LANES), axis=1)
    sin = jnp.concatenate([sin_ref[...]] * (RET_QK_WIDTH // LANES), axis=1)
    q = qk[:, :RET_QK_WIDTH]
    k = qk[:, RET_QK_WIDTH:]
    rq_ref[...] = (q * cos + _swap_pair_halves(q) * sin).astype(BF16)
    k = (k * cos + _swap_pair_halves(k) * sin) * kdec_ref[...]
    for c in range(PROJ_ROWS // RET_CHUNK):
        rkt_ref[c] = k[c * RET_CHUNK:(c + 1) * RET_CHUNK, :].T.astype(BF16)
    mg_ref[...] = jax.nn.sigmoid(_dot(xb, wmg_ref[...]) + bg_ref[...]).astype(BF16)
    rg_ref[...] = _silu(_dot(xb, wrg_ref[...])).astype(BF16)
    ag_ref[...] = _silu(_dot(xb, wg_ref[...])).astype(BF16)
    rv_ref[...] = _dot(xb, wrv_ref[...]).astype(BF16)


def _nat_proj(x, lnw, wqk, w16, cos_t, sin_t, kdec, bg):
    B, S, _ = x.shape
    nt = S // PROJ_ROWS
    cpt = PROJ_ROWS // RET_CHUNK

    def wspec(n, col0=0):
        return pl.BlockSpec((D_MODEL, n), lambda t, b: (0, col0 // n))

    def ospec(n):
        return pl.BlockSpec((None, PROJ_ROWS, n), lambda t, b: (b, t, 0))

    widths = (GROUP_WIDTH, RET_V_WIDTH, 2 * D_MODEL, RET_V_WIDTH)
    col0s = (COL_AG, COL_RG, COL_MG, COL_RV)
    tspec = pl.BlockSpec((PROJ_ROWS, LANES), lambda t, b: (t, 0))
    return pl.pallas_call(
        _nat_proj_kernel,
        grid=(nt, B),
        in_specs=[
            pl.BlockSpec((None, PROJ_ROWS, D_MODEL), lambda t, b: (b, t, 0)),
            pl.BlockSpec((1, D_MODEL), lambda t, b: (0, 0)),
            wspec(2 * RET_QK_WIDTH),
            *[wspec(n, c0) for n, c0 in zip(widths, col0s)],
            tspec, tspec,
            pl.BlockSpec((PROJ_ROWS, RET_QK_WIDTH), lambda t, b: (0, 0)),
            pl.BlockSpec((1, 2 * D_MODEL), lambda t, b: (0, 0)),
        ],
        out_specs=[
            ospec(RET_QK_WIDTH),
            pl.BlockSpec((None, cpt, RET_QK_WIDTH, RET_CHUNK), lambda t, b: (b, t, 0, 0)),
            *[ospec(n) for n in widths],
        ],
        out_shape=[
            jax.ShapeDtypeStruct((B, S, RET_QK_WIDTH), BF16),
            jax.ShapeDtypeStruct((B, S // RET_CHUNK, RET_QK_WIDTH, RET_CHUNK), BF16),
            *[jax.ShapeDtypeStruct((B, S, n), BF16) for n in widths],
        ],
        compiler_params=pltpu.CompilerParams(
            dimension_semantics=("arbitrary", "arbitrary"), vmem_limit_bytes=VMEM_LIMIT),
        name="nat_proj",
    )(x, lnw, wqk, w16, w16, w16, w16, cos_t, sin_t, kdec, bg)


def _dil_attn_kernel(*refs):
    in_refs = refs[:5 * N_GROUPS]
    bias_ref, ag_ref, out_ref = refs[5 * N_GROUPS:5 * N_GROUPS + 3]
    acc_scr, max_scr, den_scr, nat_scr = refs[5 * N_GROUPS + 3:]
    first = (pl.program_id(1) == 0).astype(jnp.int32)
    ones = jnp.ones((2 * BAND, HEAD_DIM), BF16)
    quarter = ATTN_TILE // MIX_STRIDE

    def unit(g, r, n):
        dil = ATTN_GROUPS[g][1]
        q_ref, kc_ref, kp_ref, vc_ref, vp_ref = in_refs[5 * g:5 * g + 5]
        bias_base = 2 if dil == 1 else 0
        q = q_ref[r, n * BAND:(n + 1) * BAND, :]
        if n == 0:
            k = jnp.concatenate([kp_ref[r], kc_ref[r, 0:BAND, :]], axis=0)
            v = jnp.concatenate([vp_ref[r], vc_ref[r, 0:BAND, :]], axis=0)
            bias = bias_ref[bias_base + first]
        else:
            k = kc_ref[r, (n - 1) * BAND:(n + 1) * BAND, :]
            v = vc_ref[r, (n - 1) * BAND:(n + 1) * BAND, :]
            bias = bias_ref[bias_base]
        s = lax.dot_general(q, k, (((1,), (1,)), ((), ())), preferred_element_type=F32) + bias
        m = jnp.max(s, axis=-1, keepdims=True)
        p = jnp.exp2(s - m).astype(BF16)
        res = _dot(p, jnp.concatenate([v, ones], axis=1))
        acc, den = res[:, :HEAD_DIM], res[:, HEAD_DIM:]
        mb = jnp.broadcast_to(m, (BAND, HEAD_DIM))
        if dil == 1:
            sub = BAND // MIX_STRIDE
            for r4 in range(MIX_STRIDE):
                dst = pl.ds(r4 * quarter + n * sub, sub)
                src = slice(r4 * sub, (r4 + 1) * sub)
                acc_scr[g, dst, :] = acc[src]
                max_scr[g, dst, :] = mb[src]
                den_scr[g, dst, :] = den[src]
        else:
            if dil == MIX_STRIDE:
                dst = pl.ds(r * quarter + n * BAND, BAND)
            else:
                sub_stride = dil // MIX_STRIDE
                dst = pl.ds((r % MIX_STRIDE) * quarter + r // MIX_STRIDE
                            + n * BAND * sub_stride, BAND, stride=sub_stride)
            acc_scr[g, dst, :] = acc
            max_scr[g, dst, :] = mb
            den_scr[g, dst, :] = den

    def mix_quarter(r4):
        for ch in range(quarter // BAND):
            rows = pl.ds(r4 * quarter + ch * BAND, BAND)
            m0, m1, m2 = max_scr[0, rows, :], max_scr[1, rows, :], max_scr[2, rows, :]
            mx = jnp.maximum(jnp.maximum(m0, m1), m2)
            w0 = jnp.exp2(m0 - mx)
            w1 = jnp.exp2(m1 - mx)
            w2 = jnp.exp2(m2 - mx)
            num = w0 * acc_scr[0, rows, :] + w1 * acc_scr[1, rows, :] + w2 * acc_scr[2, rows, :]
            den = w0 * den_scr[0, rows, :] + w1 * den_scr[1, rows, :] + w2 * den_scr[2, rows, :]
            nat_scr[pl.ds(r4 + ch * BAND * MIX_STRIDE, BAND, stride=MIX_STRIDE), :] = num / den

    for n in range(ATTN_TILE // BAND):
        unit(0, 0, n)
    for r4 in range(MIX_STRIDE):
        for g in range(1, N_GROUPS):
            dil = ATTN_GROUPS[g][1]
            for r in range(r4, dil, MIX_STRIDE):
                for n in range(ATTN_TILE // dil // BAND):
                    unit(g, r, n)
        mix_quarter(r4)
    out_ref[...] = (nat_scr[...] * ag_ref[...].astype(F32)).astype(BF16)


def _attn_bias_tables():
    rho = jnp.arange(BAND)[:, None]
    c = jnp.arange(2 * BAND)[None, :]
    sub = BAND // MIX_STRIDE
    tables = []
    for a in (rho, MIX_STRIDE * (rho % sub) + rho // sub):
        ok = (c >= a) & (c <= a + BAND)
        tables += [ok, ok & (c >= BAND)]
    return jnp.where(jnp.stack(tables), 0.0, NEG_BIG).astype(F32)


def _dil_attn(qkv_groups, ag):
    B, S, _ = ag.shape
    nt = S // ATTN_TILE
    in_specs = []
    args = []
    for (_, dil), qkv in zip(ATTN_GROUPS, qkv_groups):
        rows = ATTN_TILE // dil
        ratio = rows // BAND

        def cur(col0, rows=rows, dil=dil):
            return pl.BlockSpec((None, dil, rows, HEAD_DIM),
                                lambda b, t, h: (b, 0, t, col0 + h))

        def prev(col0, ratio=ratio, dil=dil):
            return pl.BlockSpec((None, dil, BAND, HEAD_DIM),
                                lambda b, t, h: (b, 0, jnp.maximum(t * ratio - 1, 0), col0 + h))

        in_specs += [cur(0), cur(ATTN_HEADS), prev(ATTN_HEADS),
                     cur(2 * ATTN_HEADS), prev(2 * ATTN_HEADS)]
        args += [qkv] * 5
    bias = _attn_bias_tables()
    in_specs.append(pl.BlockSpec(bias.shape, lambda b, t, h: (0, 0, 0)))
    in_specs.append(pl.BlockSpec((None, ATTN_TILE, HEAD_DIM), lambda b, t, h: (b, t, h)))
    per_group = pltpu.VMEM((N_GROUPS, ATTN_TILE, HEAD_DIM), F32)
    return pl.pallas_call(
        _dil_attn_kernel,
        grid=(B, nt, ATTN_HEADS),
        in_specs=in_specs,
        out_specs=pl.BlockSpec((None, ATTN_TILE, HEAD_DIM), lambda b, t, h: (b, t, h)),
        out_shape=jax.ShapeDtypeStruct((B, S, GROUP_WIDTH), BF16),
        scratch_shapes=[per_group, per_group, per_group,
                        pltpu.VMEM((ATTN_TILE, HEAD_DIM), F32)],
        compiler_params=pltpu.CompilerParams(
            dimension_semantics=("arbitrary", "arbitrary", "arbitrary"),
            vmem_limit_bytes=VMEM_LIMIT),
        name="dil_attn",
    )(*args, bias, ag)


def _retention_kernel(q_ref, kt_ref, v_ref, gate_ref, gnw_ref, eps_ref, cd_ref,
                      out_ref, state, p_scr):
    @pl.when(pl.program_id(1) == 0)
    def _():
        state[...] = jnp.zeros_like(state)

    dk, dv, C = RET_KEY_DIM, RET_VALUE_DIM, RET_CHUNK
    row_k = lax.broadcasted_iota(jnp.int32, (2 * dk, C), 0)
    row = lax.broadcasted_iota(jnp.int32, (2 * dk, 2 * dv), 0)
    col = lax.broadcasted_iota(jnp.int32, (C, 2 * dv), 1)
    diag_blk = (row < dk) == (col < dv)
    causal = (col % C) <= lax.broadcasted_iota(jnp.int32, (C, 2 * C), 0)
    units = [(bi, j) for bi in range(RET_BATCH) for j in range(RET_HEADS // 2)]

    def q_pair(bi, j):
        return q_ref[bi, :, j * 2 * dk:(j + 1) * 2 * dk]

    def kt_pair(bi, j):
        return kt_ref[bi, j * 2 * dk:(j + 1) * 2 * dk, :]

    def v_pair(bi, j):
        return v_ref[bi, :, j * 2 * dv:(j + 1) * 2 * dv]

    for bi, j in units:
        kt = kt_pair(bi, j)
        zero = jnp.zeros_like(kt)
        kt_blk = jnp.concatenate([jnp.where(row_k < dk, kt, zero),
                                  jnp.where(row_k >= dk, kt, zero)], axis=1)
        p_scr[bi, j] = jnp.where(causal, _dot(q_pair(bi, j), kt_blk), 0.0).astype(BF16)

    for bi, j in units:
        v = v_pair(bi, j)
        zero = jnp.zeros_like(v)
        v_blk = jnp.concatenate([jnp.where(col < dv, v, zero),
                                 jnp.where(col >= dv, v, zero)], axis=0)
        u = _dot(jnp.concatenate([p_scr[bi, j], q_pair(bi, j)], axis=1),
                 jnp.concatenate([v_blk, state[bi, j].astype(BF16)], axis=0))
        for hh in range(2):
            h = 2 * j + hh
            hs = slice(h * dv, (h + 1) * dv)
            uh = u[:, hh * dv:(hh + 1) * dv]
            uh = uh * lax.rsqrt(jnp.mean(uh * uh, axis=-1, keepdims=True) + eps_ref[h])
            out_ref[bi, :, hs] = (uh * gnw_ref[:, hs]
                                  * gate_ref[bi, :, hs].astype(F32)).astype(BF16)

    for bi, j in units:
        kv = _dot(kt_pair(bi, j), v_pair(bi, j))
        state[bi, j] = cd_ref[j] * (state[bi, j] + jnp.where(diag_blk, kv, 0.0))


def _retention(rq, rkt, rv, gate, gnw, eps_t, cd):
    B, S, _ = rv.shape
    nc = S // RET_CHUNK
    npair = RET_HEADS // 2
    pair_shape = (RET_BATCH, npair, 2 * RET_KEY_DIM, 2 * RET_VALUE_DIM)

    def const(shape):
        return pl.BlockSpec(shape, lambda b, n: (0,) * len(shape))

    def rows(width):
        return pl.BlockSpec((RET_BATCH, RET_CHUNK, width), lambda b, n: (b, n, 0))

    return pl.pallas_call(
        _retention_kernel,
        grid=(B // RET_BATCH, nc),
        in_specs=[
            rows(RET_QK_WIDTH),
            pl.BlockSpec((RET_BATCH, None, RET_QK_WIDTH, RET_CHUNK), lambda b, n: (b, n, 0, 0)),
            rows(RET_V_WIDTH),
            rows(RET_V_WIDTH),
            const((1, RET_V_WIDTH)),
            const((RET_HEADS, RET_CHUNK, RET_VALUE_DIM)),
            const(pair_shape[1:]),
        ],
        out_specs=rows(RET_V_WIDTH),
        out_shape=jax.ShapeDtypeStruct((B, S, RET_V_WIDTH), BF16),
        scratch_shapes=[pltpu.VMEM(pair_shape, F32),
                        pltpu.VMEM((RET_BATCH, npair, RET_CHUNK, 2 * RET_CHUNK), BF16)],
        compiler_params=pltpu.CompilerParams(
            dimension_semantics=("arbitrary", "arbitrary"), vmem_limit_bytes=VMEM_LIMIT),
        name="retention",
    )(rq, rkt, rv, gate, gnw, eps_t, cd)


def _merge_out_kernel(a_ref, r_ref, mg_ref, x_ref, wa_ref, wr_ref, wo_ref, lnf_ref, out_ref):
    y_attn = _dot(a_ref[...], wa_ref[...])
    y_ret = _dot(r_ref[...], wr_ref[...])
    merged = (mg_ref[:, :D_MODEL].astype(F32) * y_attn
              + mg_ref[:, D_MODEL:].astype(F32) * y_ret)
    h = x_ref[...] + _dot(merged.astype(BF16), wo_ref[...])
    out_ref[...] = _rmsnorm_rows(h, lnf_ref[...])


def _merge_out(attn_g, ret_g, mg, x, wa, wr, wo, lnf):
    B, S, _ = x.shape
    nt = S // OUT_ROWS

    def rows(n):
        return pl.BlockSpec((None, OUT_ROWS, n), lambda b, t: (b, t, 0))

    def const(shape):
        return pl.BlockSpec(shape, lambda b, t: (0,) * len(shape))

    return pl.pallas_call(
        _merge_out_kernel,
        grid=(B, nt),
        in_specs=[rows(GROUP_WIDTH), rows(RET_V_WIDTH), rows(2 * D_MODEL), rows(D_MODEL),
                  const((GROUP_WIDTH, D_MODEL)), const((RET_V_WIDTH, D_MODEL)),
                  const((D_MODEL, D_MODEL)), const((1, D_MODEL))],
        out_specs=rows(D_MODEL),
        out_shape=jax.ShapeDtypeStruct((B, S, D_MODEL), F32),
        compiler_params=pltpu.CompilerParams(
            dimension_semantics=("arbitrary", "arbitrary"), vmem_limit_bytes=VMEM_LIMIT),
        name="merge_out",
    )(attn_g, ret_g, mg, x, wa, wr, wo, lnf)


def _rope_tables(S):
    half = RET_KEY_DIM // 2
    pos = jnp.arange(S, dtype=F32)
    inv_freq = ROPE_BASE ** (-jnp.linspace(0.0, 1.0, half, dtype=F32))
    ang = pos[:, None] * inv_freq[None, :]
    cos = jnp.cos(ang)
    sin = jnp.sin(ang)
    return (jnp.concatenate([cos, cos, cos, cos], axis=1),
            jnp.concatenate([-sin, sin, -sin, sin], axis=1))


def _retention_constants():
    H, C, dk, dv = RET_HEADS, RET_CHUNK, RET_KEY_DIM, RET_VALUE_DIM
    log_gamma = jnp.log(1.0 - 2.0 ** (-5.0 - jnp.arange(H, dtype=F32)))
    idx = jnp.arange(C, dtype=F32)
    inv_decay = jnp.exp(-(idx + 1.0)[None, :] * log_gamma[:, None])
    chunk_decay = jnp.exp(C * log_gamma)
    kdec = jnp.repeat(inv_decay.T, dk, axis=1) * dk ** -0.5
    kdec = jnp.tile(kdec, (PROJ_ROWS // C, 1))
    eps_t = jnp.broadcast_to((NORM_EPS * inv_decay * inv_decay)[:, :, None], (H, C, dv))
    cd = jnp.broadcast_to(chunk_decay[:, None, None], (H, dk, 2 * dv)).reshape(H // 2, 2 * dk, 2 * dv)
    return kdec, eps_t, cd


def kernel(x, ln1_w, w_in, b_gate, attn_proj, ret_proj, ret_gn_w, w_out, lnf_w):
    B, S, _ = x.shape
    assert w_in.shape[0] == 1, "single layer"
    w16 = w_in[0].astype(BF16)

    e = jnp.arange(RET_KEY_DIM // 2)
    head_perm = jnp.concatenate([2 * e, 2 * e + 1])
    perm = (jnp.arange(2 * RET_HEADS)[:, None] * RET_KEY_DIM + head_perm[None, :]).reshape(-1)
    wqk = w16[:, COL_RQ:COL_RQ + 2 * RET_QK_WIDTH][:, perm]

    lnw = ln1_w[0].reshape(1, D_MODEL)
    q1, q2, q3 = _qkv_proj(x, lnw, w16)
    cos_t, sin_t = _rope_tables(S)
    kdec, eps_t, cd = _retention_constants()
    rq, rkt, ag, rg, mg, rv = _nat_proj(
        x, lnw, wqk, w16, cos_t, sin_t, kdec, b_gate[0].reshape(1, 2 * D_MODEL))

    attn_g = _dil_attn((q1, q2, q3), ag)
    ret_g = _retention(rq, rkt, rv, rg, ret_gn_w[0].reshape(1, RET_V_WIDTH), eps_t, cd)

    return _merge_out(attn_g, ret_g, mg, x,
                      attn_proj[0].astype(BF16), ret_proj[0].astype(BF16),
                      w_out[0].astype(BF16), lnf_w.reshape(1, D_MODEL))
```

```python
import jax
import jax.numpy as jnp
from jax import lax
from jax.experimental import pallas as pl
from jax.experimental.pallas import tpu as pltpu

D_MODEL = 1024
ATTN_GROUPS = ((128, 1), (512, 4), (2048, 16))
N_GROUPS = 3
ATTN_HEADS = 4
HEAD_DIM = 128
GROUP_WIDTH = ATTN_HEADS * HEAD_DIM
QKV_WIDTH = N_GROUPS * GROUP_WIDTH
BAND = 128
RET_HEADS = 8
RET_KEY_DIM = 64
RET_VALUE_DIM = 128
RET_QK_WIDTH = RET_HEADS * RET_KEY_DIM
RET_V_WIDTH = RET_HEADS * RET_VALUE_DIM
RET_CHUNK = 128
ROPE_BASE = 10000.0
COL_AG = 3 * QKV_WIDTH
COL_RQ = COL_AG + GROUP_WIDTH
COL_RV = COL_RQ + 2 * RET_QK_WIDTH
COL_RG = COL_RV + RET_V_WIDTH
COL_MG = COL_RG + RET_V_WIDTH
NORM_EPS = 1e-6
NEG_BIG = -1e30
LANES = 128
LOG2E = 1.4426950408889634
MIX_STRIDE = 4

PROJ_ROWS = 1024
ATTN_TILE = 2048
RET_BATCH = 4
OUT_ROWS = 1024
VMEM_LIMIT = 56 * 1024 * 1024
RESIDENT = pl.Buffered(1)

BF16 = jnp.bfloat16
F32 = jnp.float32


def _rmsnorm_rows(x, w):
    return x * lax.rsqrt(jnp.mean(x * x, axis=-1, keepdims=True) + NORM_EPS) * w


def _dot(a, b):
    return jnp.dot(a, b, preferred_element_type=F32)


def _silu(t):
    return t * jax.nn.sigmoid(t)


def _qkv_proj_kernel(x_ref, lnw_ref, *refs):
    w_refs = refs[:3 * N_GROUPS]
    out_refs = refs[3 * N_GROUPS:4 * N_GROUPS]
    xn_scr = refs[4 * N_GROUPS]
    xn = _rmsnorm_rows(x_ref[...], lnw_ref[...])
    n_lane_tiles = D_MODEL // LANES
    for ct in range(n_lane_tiles):
        xn_scr[ct] = xn[:, ct * LANES:(ct + 1) * LANES]
    q_scale = HEAD_DIM ** -0.5 * LOG2E
    q_cols = slice(0, GROUP_WIDTH)
    k_cols = slice(GROUP_WIDTH, 2 * GROUP_WIDTH)
    v_cols = slice(2 * GROUP_WIDTH, 3 * GROUP_WIDTH)

    def strided_rows(r, rows, dil):
        return jnp.concatenate(
            [xn_scr[ct, pl.ds(r, rows, stride=dil), :] for ct in range(n_lane_tiles)], axis=1)

    for g, (_, dil) in enumerate(ATTN_GROUPS):
        wq_ref, wk_ref, wv_ref = w_refs[3 * g:3 * g + 3]
        o_ref = out_refs[g]
        if dil == 1:
            sub = BAND // MIX_STRIDE
            lhs_q = jnp.concatenate(
                [strided_rows(blk * BAND + r4, sub, MIX_STRIDE).astype(BF16)
                 for blk in range(PROJ_ROWS // BAND) for r4 in range(MIX_STRIDE)], axis=0)
            lhs = xn.astype(BF16)
            o_ref[0, :, q_cols] = (_dot(lhs_q, wq_ref[...]) * q_scale).astype(BF16)
            o_ref[0, :, k_cols] = _dot(lhs, wk_ref[...]).astype(BF16)
            o_ref[0, :, v_cols] = _dot(lhs, wv_ref[...]).astype(BF16)
        else:
            rows = PROJ_ROWS // dil
            lhs = jnp.concatenate(
                [strided_rows(r, rows, dil).astype(BF16) for r in range(dil)], axis=0)
            q = (_dot(lhs, wq_ref[...]) * q_scale).astype(BF16)
            k = _dot(lhs, wk_ref[...]).astype(BF16)
            v = _dot(lhs, wv_ref[...]).astype(BF16)
            for r in range(dil):
                rs = slice(r * rows, (r + 1) * rows)
                o_ref[r, :, q_cols] = q[rs]
                o_ref[r, :, k_cols] = k[rs]
                o_ref[r, :, v_cols] = v[rs]


def _qkv_proj(x, lnw, w16):
    B, S, _ = x.shape
    nt = S // PROJ_ROWS
    w_specs = [pl.BlockSpec((D_MODEL, GROUP_WIDTH), lambda b, t, cb=part * N_GROUPS + g: (0, cb),
                            pipeline_mode=RESIDENT)
               for g in range(N_GROUPS) for part in range(3)]
    return pl.pallas_call(
        _qkv_proj_kernel,
        grid=(B, nt),
        in_specs=[
            pl.BlockSpec((None, PROJ_ROWS, D_MODEL), lambda b, t: (b, t, 0)),
            pl.BlockSpec((1, D_MODEL), lambda b, t: (0, 0)),
            *w_specs,
        ],
        out_specs=[pl.BlockSpec((None, dil, PROJ_ROWS // dil, QKV_WIDTH), lambda b, t: (b, 0, t, 0))
                   for _, dil in ATTN_GROUPS],
        out_shape=[jax.ShapeDtypeStruct((B, dil, S // dil, QKV_WIDTH), BF16)
                   for _, dil in ATTN_GROUPS],
        scratch_shapes=[pltpu.VMEM((D_MODEL // LANES, PROJ_ROWS, LANES), F32)],
        compiler_params=pltpu.CompilerParams(
            dimension_semantics=("arbitrary", "arbitrary"), vmem_limit_bytes=VMEM_LIMIT),
        name="qkv_proj",
    )(x, lnw, *([w16] * (3 * N_GROUPS)))


def _swap_pair_halves(t):
    n = t.shape[-1]
    lane = lax.broadcasted_iota(jnp.int32, t.shape, 1)
    up = pltpu.roll(t, n - RET_KEY_DIM // 2, axis=1)
    down = pltpu.roll(t, RET_KEY_DIM // 2, axis=1)
    return jnp.where(lane % RET_KEY_DIM < RET_KEY_DIM // 2, up, down)


def _nat_proj_kernel(x_ref, lnw_ref, wqk_ref, wg_ref, wrg_ref, wmg_ref, wrv_ref,
                     cos_ref, sin_ref, kdec_ref, bg_ref,
                     rq_ref, rkt_ref, ag_ref, rg_ref, mg_ref, rv_ref):
    xb = _rmsnorm_rows(x_ref[...], lnw_ref[...]).astype(BF16)
    qk = _dot(xb, wqk_ref[...])
    cos = jnp.concatenate([cos_ref[...]] * (RET_QK_WIDTH // LANES), axis=1)
    sin = jnp.concatenate([sin_ref[...]] * (RET_QK_WIDTH // LANES), axis=1)
    q = qk[:, :RET_QK_WIDTH]
    k = qk[:, RET_QK_WIDTH:]
    rq_ref[...] = (q * cos + _swap_pair_halves(q) * sin).astype(BF16)
    k = (k * cos + _swap_pair_halves(k) * sin) * kdec_ref[...]
    for c in range(PROJ_ROWS // RET_CHUNK):
        rkt_ref[c] = k[c * RET_CHUNK:(c + 1) * RET_CHUNK, :].T.astype(BF16)
    mg_ref[...] = jax.nn.sigmoid(_dot(xb, wmg_ref[...]) + bg_ref[...]).astype(BF16)
    rg_ref[...] = _silu(_dot(xb, wrg_ref[...])).astype(BF16)
    ag_ref[...] = _silu(_dot(xb, wg_ref[...])).astype(BF16)
    rv_ref[...] = _dot(xb, wrv_ref[...]).astype(BF16)


def _nat_proj(x, lnw, wqk, w16, cos_t, sin_t, kdec, bg):
    B, S, _ = x.shape
    nt = S // PROJ_ROWS
    cpt = PROJ_ROWS // RET_CHUNK

    def wspec(n, col0=0):
        return pl.BlockSpec((D_MODEL, n), lambda t, b: (0, col0 // n), pipeline_mode=RESIDENT)

    def ospec(n):
        return pl.BlockSpec((None, PROJ_ROWS, n), lambda t, b: (b, t, 0))

    widths = (GROUP_WIDTH, RET_V_WIDTH, 2 * D_MODEL, RET_V_WIDTH)
    col0s = (COL_AG, COL_RG, COL_MG, COL_RV)
    tspec = pl.BlockSpec((PROJ_ROWS, LANES), lambda t, b: (t, 0))
    return pl.pallas_call(
        _nat_proj_kernel,
        grid=(nt, B),
        in_specs=[
            pl.BlockSpec((None, PROJ_ROWS, D_MODEL), lambda t, b: (b, t, 0)),
            pl.BlockSpec((1, D_MODEL), lambda t, b: (0, 0)),
            wspec(2 * RET_QK_WIDTH),
            *[wspec(n, c0) for n, c0 in zip(widths, col0s)],
            tspec, tspec,
            pl.BlockSpec((PROJ_ROWS, RET_QK_WIDTH), lambda t, b: (0, 0), pipeline_mode=RESIDENT),
            pl.BlockSpec((1, 2 * D_MODEL), lambda t, b: (0, 0)),
        ],
        out_specs=[
            ospec(RET_QK_WIDTH),
            pl.BlockSpec((None, cpt, RET_QK_WIDTH, RET_CHUNK), lambda t, b: (b, t, 0, 0)),
            *[ospec(n) for n in widths],
        ],
        out_shape=[
            jax.ShapeDtypeStruct((B, S, RET_QK_WIDTH), BF16),
            jax.ShapeDtypeStruct((B, S // RET_CHUNK, RET_QK_WIDTH, RET_CHUNK), BF16),
            *[jax.ShapeDtypeStruct((B, S, n), BF16) for n in widths],
        ],
        compiler_params=pltpu.CompilerParams(
            dimension_semantics=("arbitrary", "arbitrary"), vmem_limit_bytes=VMEM_LIMIT),
        name="nat_proj",
    )(x, lnw, wqk, w16, w16, w16, w16, cos_t, sin_t, kdec, bg)


def _dil_attn_kernel(*refs):
    in_refs = refs[:5 * N_GROUPS]
    bias_ref, ag_ref, out_ref = refs[5 * N_GROUPS:5 * N_GROUPS + 3]
    acc_scr, max_scr, den_scr, nat_scr = refs[5 * N_GROUPS + 3:]
    first = (pl.program_id(1) == 0).astype(jnp.int32)
    ones = jnp.ones((2 * BAND, HEAD_DIM), BF16)
    quarter = ATTN_TILE // MIX_STRIDE

    def unit(g, r, n):
        dil = ATTN_GROUPS[g][1]
        q_ref, kc_ref, kp_ref, vc_ref, vp_ref = in_refs[5 * g:5 * g + 5]
        bias_base = 2 if dil == 1 else 0
        q = q_ref[r, n * BAND:(n + 1) * BAND, :]
        if n == 0:
            k = jnp.concatenate([kp_ref[r], kc_ref[r, 0:BAND, :]], axis=0)
            v = jnp.concatenate([vp_ref[r], vc_ref[r, 0:BAND, :]], axis=0)
            bias = bias_ref[bias_base + first]
        else:
            k = kc_ref[r, (n - 1) * BAND:(n + 1) * BAND, :]
            v = vc_ref[r, (n - 1) * BAND:(n + 1) * BAND, :]
            bias = bias_ref[bias_base]
        s = lax.dot_general(q, k, (((1,), (1,)), ((), ())), preferred_element_type=F32) + bias
        m = jnp.max(s, axis=-1, keepdims=True)
        p = jnp.exp2(s - m).astype(BF16)
        res = _dot(p, jnp.concatenate([v, ones], axis=1))
        acc, den = res[:, :HEAD_DIM], res[:, HEAD_DIM:]
        mb = jnp.broadcast_to(m, (BAND, HEAD_DIM))
        if dil == 1:
            sub = BAND // MIX_STRIDE
            for r4 in range(MIX_STRIDE):
                dst = pl.ds(r4 * quarter + n * sub, sub)
                src = slice(r4 * sub, (r4 + 1) * sub)
                acc_scr[g, dst, :] = acc[src]
                max_scr[g, dst, :] = mb[src]
                den_scr[g, dst, :] = den[src]
        else:
            if dil == MIX_STRIDE:
                dst = pl.ds(r * quarter + n * BAND, BAND)
            else:
                sub_stride = dil // MIX_STRIDE
                dst = pl.ds((r % MIX_STRIDE) * quarter + r // MIX_STRIDE
                            + n * BAND * sub_stride, BAND, stride=sub_stride)
            acc_scr[g, dst, :] = acc
            max_scr[g, dst, :] = mb
            den_scr[g, dst, :] = den

    def mix_quarter(r4):
        for ch in range(quarter // BAND):
            rows = pl.ds(r4 * quarter + ch * BAND, BAND)
            m0, m1, m2 = max_scr[0, rows, :], max_scr[1, rows, :], max_scr[2, rows, :]
            mx = jnp.maximum(jnp.maximum(m0, m1), m2)
            w0 = jnp.exp2(m0 - mx)
            w1 = jnp.exp2(m1 - mx)
            w2 = jnp.exp2(m2 - mx)
            num = w0 * acc_scr[0, rows, :] + w1 * acc_scr[1, rows, :] + w2 * acc_scr[2, rows, :]
            den = w0 * den_scr[0, rows, :] + w1 * den_scr[1, rows, :] + w2 * den_scr[2, rows, :]
            nat_scr[pl.ds(r4 + ch * BAND * MIX_STRIDE, BAND, stride=MIX_STRIDE), :] = num / den

    for n in range(ATTN_TILE // BAND):
        unit(0, 0, n)
    for r4 in range(MIX_STRIDE):
        for g in range(1, N_GROUPS):
            dil = ATTN_GROUPS[g][1]
            for r in range(r4, dil, MIX_STRIDE):
                for n in range(ATTN_TILE // dil // BAND):
                    unit(g, r, n)
        mix_quarter(r4)
    out_ref[...] = (nat_scr[...] * ag_ref[...].astype(F32)).astype(BF16)


def _attn_bias_tables():
    rho = jnp.arange(BAND)[:, None]
    c = jnp.arange(2 * BAND)[None, :]
    sub = BAND // MIX_STRIDE
    tables = []
    for a in (rho, MIX_STRIDE * (rho % sub) + rho // sub):
        ok = (c >= a) & (c <= a + BAND)
        tables += [ok, ok & (c >= BAND)]
    return jnp.where(jnp.stack(tables), 0.0, NEG_BIG).astype(F32)


def _dil_attn(qkv_groups, ag):
    B, S, _ = ag.shape
    nt = S // ATTN_TILE
    in_specs = []
    args = []
    for (_, dil), qkv in zip(ATTN_GROUPS, qkv_groups):
        rows = ATTN_TILE // dil
        ratio = rows // BAND

        def cur(col0, rows=rows, dil=dil):
            return pl.BlockSpec((None, dil, rows, HEAD_DIM),
                                lambda b, t, h: (b, 0, t, col0 + h))

        def prev(col0, ratio=ratio, dil=dil):
            return pl.BlockSpec((None, dil, BAND, HEAD_DIM),
                                lambda b, t, h: (b, 0, jnp.maximum(t * ratio - 1, 0), col0 + h))

        in_specs += [cur(0), cur(ATTN_HEADS), prev(ATTN_HEADS),
                     cur(2 * ATTN_HEADS), prev(2 * ATTN_HEADS)]
        args += [qkv] * 5
    bias = _attn_bias_tables()
    in_specs.append(pl.BlockSpec(bias.shape, lambda b, t, h: (0, 0, 0)))
    in_specs.append(pl.BlockSpec((None, ATTN_TILE, HEAD_DIM), lambda b, t, h: (b, t, h)))
    per_group = pltpu.VMEM((N_GROUPS, ATTN_TILE, HEAD_DIM), F32)
    return pl.pallas_call(
        _dil_attn_kernel,
        grid=(B, nt, ATTN_HEADS),
        in_specs=in_specs,
        out_specs=pl.BlockSpec((None, ATTN_TILE, HEAD_DIM), lambda b, t, h: (b, t, h)),
        out_shape=jax.ShapeDtypeStruct((B, S, GROUP_WIDTH), BF16),
        scratch_shapes=[per_group, per_group, per_group,
                        pltpu.VMEM((ATTN_TILE, HEAD_DIM), F32)],
        compiler_params=pltpu.CompilerParams(
            dimension_semantics=("arbitrary", "arbitrary", "arbitrary"),
            vmem_limit_bytes=VMEM_LIMIT),
        name="dil_attn",
    )(*args, bias, ag)


def _retention_kernel(q_ref, kt_ref, v_ref, gate_ref, gnw_ref, eps_ref, cd_ref,
                      out_ref, state, p_scr):
    @pl.when(pl.program_id(1) == 0)
    def _():
        state[...] = jnp.zeros_like(state)

    dk, dv, C = RET_KEY_DIM, RET_VALUE_DIM, RET_CHUNK
    row_k = lax.broadcasted_iota(jnp.int32, (2 * dk, C), 0)
    row = lax.broadcasted_iota(jnp.int32, (2 * dk, 2 * dv), 0)
    col = lax.broadcasted_iota(jnp.int32, (C, 2 * dv), 1)
    diag_blk = (row < dk) == (col < dv)
    causal = (col % C) <= lax.broadcasted_iota(jnp.int32, (C, 2 * C), 0)
    units = [(bi, j) for bi in range(RET_BATCH) for j in range(RET_HEADS // 2)]

    def q_pair(bi, j):
        return q_ref[bi, :, j * 2 * dk:(j + 1) * 2 * dk]

    def kt_pair(bi, j):
        return kt_ref[bi, j * 2 * dk:(j + 1) * 2 * dk, :]

    def v_pair(bi, j):
        return v_ref[bi, :, j * 2 * dv:(j + 1) * 2 * dv]

    for bi, j in units:
        kt = kt_pair(bi, j)
        zero = jnp.zeros_like(kt)
        kt_blk = jnp.concatenate([jnp.where(row_k < dk, kt, zero),
                                  jnp.where(row_k >= dk, kt, zero)], axis=1)
        p_scr[bi, j] = jnp.where(causal, _dot(q_pair(bi, j), kt_blk), 0.0).astype(BF16)

    for bi, j in units:
        v = v_pair(bi, j)
        zero = jnp.zeros_like(v)
        v_blk = jnp.concatenate([jnp.where(col < dv, v, zero),
                                 jnp.where(col >= dv, v, zero)], axis=0)
        u = _dot(jnp.concatenate([p_scr[bi, j], q_pair(bi, j)], axis=1),
                 jnp.concatenate([v_blk, state[bi, j].astype(BF16)], axis=0))
        for hh in range(2):
            h = 2 * j + hh
            hs = slice(h * dv, (h + 1) * dv)
            uh = u[:, hh * dv:(hh + 1) * dv]
            uh = uh * lax.rsqrt(jnp.mean(uh * uh, axis=-1, keepdims=True) + eps_ref[h])
            out_ref[bi, :, hs] = (uh * gnw_ref[:, hs]
                                  * gate_ref[bi, :, hs].astype(F32)).astype(BF16)

    for bi, j in units:
        kv = _dot(kt_pair(bi, j), v_pair(bi, j))
        state[bi, j] = cd_ref[j] * (state[bi, j] + jnp.where(diag_blk, kv, 0.0))


def _retention(rq, rkt, rv, gate, gnw, eps_t, cd):
    B, S, _ = rv.shape
    nc = S // RET_CHUNK
    npair = RET_HEADS // 2
    pair_shape = (RET_BATCH, npair, 2 * RET_KEY_DIM, 2 * RET_VALUE_DIM)

    def const(shape):
        return pl.BlockSpec(shape, lambda b, n: (0,) * len(shape))

    def rows(width):
        return pl.BlockSpec((RET_BATCH, RET_CHUNK, width), lambda b, n: (b, n, 0))

    return pl.pallas_call(
        _retention_kernel,
        grid=(B // RET_BATCH, nc),
        in_specs=[
            rows(RET_QK_WIDTH),
            pl.BlockSpec((RET_BATCH, None, RET_QK_WIDTH, RET_CHUNK), lambda b, n: (b, n, 0, 0)),
            rows(RET_V_WIDTH),
            rows(RET_V_WIDTH),
            const((1, RET_V_WIDTH)),
            const((RET_HEADS, RET_CHUNK, RET_VALUE_DIM)),
            const(pair_shape[1:]),
        ],
        out_specs=rows(RET_V_WIDTH),
        out_shape=jax.ShapeDtypeStruct((B, S, RET_V_WIDTH), BF16),
        scratch_shapes=[pltpu.VMEM(pair_shape, F32),
                        pltpu.VMEM((RET_BATCH, npair, RET_CHUNK, 2 * RET_CHUNK), BF16)],
        compiler_params=pltpu.CompilerParams(
            dimension_semantics=("arbitrary", "arbitrary"), vmem_limit_bytes=VMEM_LIMIT),
        name="retention",
    )(rq, rkt, rv, gate, gnw, eps_t, cd)


def _merge_out_kernel(a_ref, r_ref, mg_ref, x_ref, wa_ref, wr_ref, wo_ref, lnf_ref, out_ref):
    y_attn = _dot(a_ref[...], wa_ref[...])
    y_ret = _dot(r_ref[...], wr_ref[...])
    merged = (mg_ref[:, :D_MODEL].astype(F32) * y_attn
              + mg_ref[:, D_MODEL:].astype(F32) * y_ret)
    h = x_ref[...] + _dot(merged.astype(BF16), wo_ref[...])
    out_ref[...] = _rmsnorm_rows(h, lnf_ref[...])


def _merge_out(attn_g, ret_g, mg, x, wa, wr, wo, lnf):
    B, S, _ = x.shape
    nt = S // OUT_ROWS

    def rows(n):
        return pl.BlockSpec((None, OUT_ROWS, n), lambda b, t: (b, t, 0))

    def const(shape):
        return pl.BlockSpec(shape, lambda b, t: (0,) * len(shape), pipeline_mode=RESIDENT)

    return pl.pallas_call(
        _merge_out_kernel,
        grid=(B, nt),
        in_specs=[rows(GROUP_WIDTH), rows(RET_V_WIDTH), rows(2 * D_MODEL), rows(D_MODEL),
                  const((GROUP_WIDTH, D_MODEL)), const((RET_V_WIDTH, D_MODEL)),
                  const((D_MODEL, D_MODEL)), const((1, D_MODEL))],
        out_specs=rows(D_MODEL),
        out_shape=jax.ShapeDtypeStruct((B, S, D_MODEL), F32),
        compiler_params=pltpu.CompilerParams(
            dimension_semantics=("arbitrary", "arbitrary"), vmem_limit_bytes=VMEM_LIMIT),
        name="merge_out",
    )(attn_g, ret_g, mg, x, wa, wr, wo, lnf)


def _rope_tables(S):
    half = RET_KEY_DIM // 2
    pos = jnp.arange(S, dtype=F32)
    inv_freq = ROPE_BASE ** (-jnp.linspace(0.0, 1.0, half, dtype=F32))
    ang = pos[:, None] * inv_freq[None, :]
    cos = jnp.cos(ang)
    sin = jnp.sin(ang)
    return (jnp.concatenate([cos, cos, cos, cos], axis=1),
            jnp.concatenate([-sin, sin, -sin, sin], axis=1))


def _retention_constants():
    H, C, dk, dv = RET_HEADS, RET_CHUNK, RET_KEY_DIM, RET_VALUE_DIM
    log_gamma = jnp.log(1.0 - 2.0 ** (-5.0 - jnp.arange(H, dtype=F32)))
    idx = jnp.arange(C, dtype=F32)
    inv_decay = jnp.exp(-(idx + 1.0)[None, :] * log_gamma[:, None])
    chunk_decay = jnp.exp(C * log_gamma)
    kdec = jnp.repeat(inv_decay.T, dk, axis=1) * dk ** -0.5
    kdec = jnp.tile(kdec, (PROJ_ROWS // C, 1))
    eps_t = jnp.broadcast_to((NORM_EPS * inv_decay * inv_decay)[:, :, None], (H, C, dv))
    cd = jnp.broadcast_to(chunk_decay[:, None, None], (H, dk, 2 * dv)).reshape(H // 2, 2 * dk, 2 * dv)
    return kdec, eps_t, cd


def kernel(x, ln1_w, w_in, b_gate, attn_proj, ret_proj, ret_gn_w, w_out, lnf_w):
    B, S, _ = x.shape
    assert w_in.shape[0] == 1, "single layer"
    w16 = w_in[0].astype(BF16)

    e = jnp.arange(RET_KEY_DIM // 2)
    head_perm = jnp.concatenate([2 * e, 2 * e + 1])
    perm = (jnp.arange(2 * RET_HEADS)[:, None] * RET_KEY_DIM + head_perm[None, :]).reshape(-1)
    wqk = w16[:, COL_RQ:COL_RQ + 2 * RET_QK_WIDTH][:, perm]

    lnw = ln1_w[0].reshape(1, D_MODEL)
    q1, q2, q3 = _qkv_proj(x, lnw, w16)
    cos_t, sin_t = _rope_tables(S)
    kdec, eps_t, cd = _retention_constants()
    rq, rkt, ag, rg, mg, rv = _nat_proj(
        x, lnw, wqk, w16, cos_t, sin_t, kdec, b_gate[0].reshape(1, 2 * D_MODEL))

    attn_g = _dil_attn((q1, q2, q3), ag)
    ret_g = _retention(rq, rkt, rv, rg, ret_gn_w[0].reshape(1, RET_V_WIDTH), eps_t, cd)

    return _merge_out(attn_g, ret_g, mg, x,
                      attn_proj[0].astype(BF16), ret_proj[0].astype(BF16),
                      w_out[0].astype(BF16), lnf_w.reshape(1, D_MODEL))
```

```python
import jax
import jax.numpy as jnp
from jax import lax
from jax.experimental import pallas as pl
from jax.experimental.pallas import tpu as pltpu

D_MODEL = 1024
ATTN_GROUPS = ((128, 1), (512, 4), (2048, 16))
N_GROUPS = 3
ATTN_HEADS = 4
HEAD_DIM = 128
GROUP_WIDTH = ATTN_HEADS * HEAD_DIM
QKV_WIDTH = N_GROUPS * GROUP_WIDTH
BAND = 128
RET_HEADS = 8
RET_KEY_DIM = 64
RET_VALUE_DIM = 128
RET_QK_WIDTH = RET_HEADS * RET_KEY_DIM
RET_V_WIDTH = RET_HEADS * RET_VALUE_DIM
RET_CHUNK = 128
ROPE_BASE = 10000.0
COL_AG = 3 * QKV_WIDTH
COL_RQ = COL_AG + GROUP_WIDTH
COL_RV = COL_RQ + 2 * RET_QK_WIDTH
COL_RG = COL_RV + RET_V_WIDTH
COL_MG = COL_RG + RET_V_WIDTH
NORM_EPS = 1e-6
NEG_BIG = -1e30
LANES = 128
LOG2E = 1.4426950408889634
MIX_STRIDE = 4

PROJ_ROWS = 1024
ATTN_TILE = 2048
GATE_ROWS = ATTN_TILE // ATTN_HEADS
RET_BATCH = 4
OUT_ROWS = 1024
VMEM_LIMIT = 56 * 1024 * 1024
RESIDENT = pl.Buffered(1)

BF16 = jnp.bfloat16
F32 = jnp.float32


def _rmsnorm_rows(x, w):
    return x * lax.rsqrt(jnp.mean(x * x, axis=-1, keepdims=True) + NORM_EPS) * w


def _dot(a, b):
    return jnp.dot(a, b, preferred_element_type=F32)


def _silu(t):
    return t * jax.nn.sigmoid(t)


def _qkv_proj_kernel(x_ref, lnw_ref, *refs):
    w_refs = refs[:3 * N_GROUPS]
    wrv_ref = refs[3 * N_GROUPS]
    out_refs = refs[3 * N_GROUPS + 1:4 * N_GROUPS + 1]
    rv_ref, xn_scr = refs[4 * N_GROUPS + 1:]
    xn = _rmsnorm_rows(x_ref[...], lnw_ref[...])
    n_lane_tiles = D_MODEL // LANES
    for ct in range(n_lane_tiles):
        xn_scr[ct] = xn[:, ct * LANES:(ct + 1) * LANES]
    q_scale = HEAD_DIM ** -0.5 * LOG2E
    q_cols = slice(0, GROUP_WIDTH)
    k_cols = slice(GROUP_WIDTH, 2 * GROUP_WIDTH)
    v_cols = slice(2 * GROUP_WIDTH, 3 * GROUP_WIDTH)

    def strided_rows(r, rows, dil):
        return jnp.concatenate(
            [xn_scr[ct, pl.ds(r, rows, stride=dil), :] for ct in range(n_lane_tiles)], axis=1)

    for g, (_, dil) in enumerate(ATTN_GROUPS):
        wq_ref, wk_ref, wv_ref = w_refs[3 * g:3 * g + 3]
        o_ref = out_refs[g]
        if dil == 1:
            sub = BAND // MIX_STRIDE
            lhs_q = jnp.concatenate(
                [strided_rows(blk * BAND + r4, sub, MIX_STRIDE).astype(BF16)
                 for blk in range(PROJ_ROWS // BAND) for r4 in range(MIX_STRIDE)], axis=0)
            lhs = xn.astype(BF16)
            o_ref[0, :, q_cols] = (_dot(lhs_q, wq_ref[...]) * q_scale).astype(BF16)
            o_ref[0, :, k_cols] = _dot(lhs, wk_ref[...]).astype(BF16)
            o_ref[0, :, v_cols] = _dot(lhs, wv_ref[...]).astype(BF16)
            rv_ref[...] = _dot(lhs, wrv_ref[...]).astype(BF16)
        else:
            rows = PROJ_ROWS // dil
            lhs = jnp.concatenate(
                [strided_rows(r, rows, dil).astype(BF16) for r in range(dil)], axis=0)
            q = (_dot(lhs, wq_ref[...]) * q_scale).astype(BF16)
            k = _dot(lhs, wk_ref[...]).astype(BF16)
            v = _dot(lhs, wv_ref[...]).astype(BF16)
            for r in range(dil):
                rs = slice(r * rows, (r + 1) * rows)
                o_ref[r, :, q_cols] = q[rs]
                o_ref[r, :, k_cols] = k[rs]
                o_ref[r, :, v_cols] = v[rs]


def _qkv_proj(x, lnw, w16):
    B, S, _ = x.shape
    nt = S // PROJ_ROWS
    w_specs = [pl.BlockSpec((D_MODEL, GROUP_WIDTH), lambda b, t, cb=part * N_GROUPS + g: (0, cb),
                            pipeline_mode=RESIDENT)
               for g in range(N_GROUPS) for part in range(3)]
    w_specs.append(pl.BlockSpec((D_MODEL, RET_V_WIDTH), lambda b, t: (0, COL_RV // RET_V_WIDTH),
                                pipeline_mode=RESIDENT))
    return pl.pallas_call(
        _qkv_proj_kernel,
        grid=(B, nt),
        in_specs=[
            pl.BlockSpec((None, PROJ_ROWS, D_MODEL), lambda b, t: (b, t, 0)),
            pl.BlockSpec((1, D_MODEL), lambda b, t: (0, 0)),
            *w_specs,
        ],
        out_specs=[
            *[pl.BlockSpec((None, dil, PROJ_ROWS // dil, QKV_WIDTH), lambda b, t: (b, 0, t, 0))
              for _, dil in ATTN_GROUPS],
            pl.BlockSpec((None, PROJ_ROWS, RET_V_WIDTH), lambda b, t: (b, t, 0)),
        ],
        out_shape=[
            *[jax.ShapeDtypeStruct((B, dil, S // dil, QKV_WIDTH), BF16) for _, dil in ATTN_GROUPS],
            jax.ShapeDtypeStruct((B, S, RET_V_WIDTH), BF16),
        ],
        scratch_shapes=[pltpu.VMEM((D_MODEL // LANES, PROJ_ROWS, LANES), F32)],
        compiler_params=pltpu.CompilerParams(
            dimension_semantics=("arbitrary", "arbitrary"), vmem_limit_bytes=VMEM_LIMIT),
        name="qkv_proj",
    )(x, lnw, *([w16] * (3 * N_GROUPS + 1)))


def _swap_pair_halves(t):
    n = t.shape[-1]
    lane = lax.broadcasted_iota(jnp.int32, t.shape, 1)
    up = pltpu.roll(t, n - RET_KEY_DIM // 2, axis=1)
    down = pltpu.roll(t, RET_KEY_DIM // 2, axis=1)
    return jnp.where(lane % RET_KEY_DIM < RET_KEY_DIM // 2, up, down)


def _gate_proj_items(x_ref, lnw_ref, wqk_ref, wg_ref, wrg_ref, wmg_ref,
                     cos_ref, sin_ref, kdec_ref, bg_ref,
                     rq_ref, rkt_ref, ag_ref, rg_ref, mg_ref):
    cache = {}

    def lhs():
        if "xb" not in cache:
            cache["xb"] = _rmsnorm_rows(x_ref[...], lnw_ref[...]).astype(BF16)
        return cache["xb"]

    def rope(t):
        pairs = t.shape[1] // LANES
        cos = jnp.concatenate([cos_ref[...]] * pairs, axis=1)
        sin = jnp.concatenate([sin_ref[...]] * pairs, axis=1)
        return t * cos + _swap_pair_halves(t) * sin

    def retention_q():
        t = _dot(lhs(), wqk_ref[:, :RET_QK_WIDTH])
        yield
        rq_ref[...] = rope(t).astype(BF16)

    def retention_k():
        t = _dot(lhs(), wqk_ref[:, RET_QK_WIDTH:])
        yield
        k = rope(t) * kdec_ref[...]
        for c in range(GATE_ROWS // RET_CHUNK):
            rkt_ref[c] = k[c * RET_CHUNK:(c + 1) * RET_CHUNK, :].T.astype(BF16)

    def merge_gate(half):
        cols = slice(half * D_MODEL, (half + 1) * D_MODEL)
        t = _dot(lhs(), wmg_ref[:, cols])
        yield
        mg_ref[:, cols] = jax.nn.sigmoid(t + bg_ref[:, cols]).astype(BF16)

    def retention_gate():
        t = _dot(lhs(), wrg_ref[...])
        yield
        rg_ref[...] = _silu(t).astype(BF16)

    def attention_gate():
        t = _dot(lhs(), wg_ref[...])
        yield
        ag_ref[...] = _silu(t).astype(BF16)

    return [retention_q, retention_k, lambda: merge_gate(0), lambda: merge_gate(1),
            retention_gate, attention_gate]


def _attention_items(in_refs, bias_ref, out_ref, acc_scr, max_scr, den_scr, nat_scr, first):
    ones = jnp.ones((2 * BAND, HEAD_DIM), BF16)
    quarter = ATTN_TILE // MIX_STRIDE

    def unit(g, r, n):
        dil = ATTN_GROUPS[g][1]
        q_ref, kc_ref, kp_ref, vc_ref, vp_ref = in_refs[5 * g:5 * g + 5]
        bias_base = 2 if dil == 1 else 0
        q = q_ref[r, n * BAND:(n + 1) * BAND, :]
        if n == 0:
            k = jnp.concatenate([kp_ref[r], kc_ref[r, 0:BAND, :]], axis=0)
            v = jnp.concatenate([vp_ref[r], vc_ref[r, 0:BAND, :]], axis=0)
            bias = bias_ref[bias_base + first]
        else:
            k = kc_ref[r, (n - 1) * BAND:(n + 1) * BAND, :]
            v = vc_ref[r, (n - 1) * BAND:(n + 1) * BAND, :]
            bias = bias_ref[bias_base]
        s = lax.dot_general(q, k, (((1,), (1,)), ((), ())), preferred_element_type=F32) + bias
        yield
        m = jnp.max(s, axis=-1, keepdims=True)
        p = jnp.exp2(s - m).astype(BF16)
        yield
        res = _dot(p, jnp.concatenate([v, ones], axis=1))
        acc, den = res[:, :HEAD_DIM], res[:, HEAD_DIM:]
        mb = jnp.broadcast_to(m, (BAND, HEAD_DIM))
        if dil == 1:
            sub = BAND // MIX_STRIDE
            for r4 in range(MIX_STRIDE):
                dst = pl.ds(r4 * quarter + n * sub, sub)
                src = slice(r4 * sub, (r4 + 1) * sub)
                acc_scr[g, dst, :] = acc[src]
                max_scr[g, dst, :] = mb[src]
                den_scr[g, dst, :] = den[src]
        else:
            if dil == MIX_STRIDE:
                dst = pl.ds(r * quarter + n * BAND, BAND)
            else:
                sub_stride = dil // MIX_STRIDE
                dst = pl.ds((r % MIX_STRIDE) * quarter + r // MIX_STRIDE
                            + n * BAND * sub_stride, BAND, stride=sub_stride)
            acc_scr[g, dst, :] = acc
            max_scr[g, dst, :] = mb
            den_scr[g, dst, :] = den

    def mix(r4, ch):
        rows = pl.ds(r4 * quarter + ch * BAND, BAND)
        m0, m1, m2 = max_scr[0, rows, :], max_scr[1, rows, :], max_scr[2, rows, :]
        mx = jnp.maximum(jnp.maximum(m0, m1), m2)
        w0 = jnp.exp2(m0 - mx)
        w1 = jnp.exp2(m1 - mx)
        w2 = jnp.exp2(m2 - mx)
        num = w0 * acc_scr[0, rows, :] + w1 * acc_scr[1, rows, :] + w2 * acc_scr[2, rows, :]
        den = w0 * den_scr[0, rows, :] + w1 * den_scr[1, rows, :] + w2 * den_scr[2, rows, :]
        nat_scr[pl.ds(r4 + ch * BAND * MIX_STRIDE, BAND, stride=MIX_STRIDE), :] = num / den

    def write_out():
        out_ref[...] = nat_scr[...].astype(BF16)

    def mix_quarter(r4):
        for ch in range(quarter // BAND):
            mix(r4, ch)

    per_round = ATTN_TILE // BAND // 2
    rounds = [[lambda n=n: unit(0, 0, n) for n in range(h * per_round, (h + 1) * per_round)]
              for h in range(2)]
    for r4 in range(MIX_STRIDE):
        rounds.append([lambda g=g, r=r, n=n: unit(g, r, n)
                       for g in range(1, N_GROUPS)
                       for r in range(r4, ATTN_GROUPS[g][1], MIX_STRIDE)
                       for n in range(ATTN_TILE // ATTN_GROUPS[g][1] // BAND)])
    return rounds, mix_quarter, write_out


def _attn_gates_kernel(*refs):
    n_attn = 5 * N_GROUPS
    in_refs = refs[:n_attn]
    bias_ref = refs[n_attn]
    proj_in = refs[n_attn + 1:n_attn + 11]
    attn_out = refs[n_attn + 11]
    proj_out = refs[n_attn + 12:n_attn + 17]
    acc_scr, max_scr, den_scr, nat_scr = refs[n_attn + 17:]
    first = (pl.program_id(1) == 0).astype(jnp.int32)
    rounds, mix_quarter, write_out = _attention_items(
        in_refs, bias_ref, attn_out, acc_scr, max_scr, den_scr, nat_scr, first)
    proj = _gate_proj_items(*proj_in, *proj_out)
    assert len(proj) == len(rounds)

    def advance(gen):
        next(gen, None)

    prev_units, prev_proj = [], None
    for r in range(len(rounds) + 1):
        units = [make() for make in rounds[r]] if r < len(rounds) else []
        for j in range(max(len(units), len(prev_units))):
            if j < len(units):
                advance(units[j])
            if j < len(prev_units):
                advance(prev_units[j])
        if r >= 3:
            mix_quarter(r - 3)
        proj_item = proj[r]() if r < len(proj) else None
        if proj_item is not None:
            advance(proj_item)
        for u in units:
            advance(u)
        if prev_proj is not None:
            advance(prev_proj)
        prev_units, prev_proj = units, proj_item
    write_out()


def _attn_bias_tables():
    rho = jnp.arange(BAND)[:, None]
    c = jnp.arange(2 * BAND)[None, :]
    sub = BAND // MIX_STRIDE
    tables = []
    for a in (rho, MIX_STRIDE * (rho % sub) + rho // sub):
        ok = (c >= a) & (c <= a + BAND)
        tables += [ok, ok & (c >= BAND)]
    return jnp.where(jnp.stack(tables), 0.0, NEG_BIG).astype(F32)


def _attn_gates(qkv_groups, x, lnw, wqk, w16, cos_t, sin_t, kdec, bg):
    B, S, _ = x.shape
    nt = S // ATTN_TILE
    cpt = GATE_ROWS // RET_CHUNK
    in_specs = []
    args = []
    for (_, dil), qkv in zip(ATTN_GROUPS, qkv_groups):
        rows = ATTN_TILE // dil
        ratio = rows // BAND

        def cur(col0, rows=rows, dil=dil):
            return pl.BlockSpec((None, dil, rows, HEAD_DIM),
                                lambda b, t, h: (b, 0, t, col0 + h))

        def prev(col0, ratio=ratio, dil=dil):
            return pl.BlockSpec((None, dil, BAND, HEAD_DIM),
                                lambda b, t, h: (b, 0, jnp.maximum(t * ratio - 1, 0), col0 + h))

        in_specs += [cur(0), cur(ATTN_HEADS), prev(ATTN_HEADS),
                     cur(2 * ATTN_HEADS), prev(2 * ATTN_HEADS)]
        args += [qkv] * 5
    bias = _attn_bias_tables()
    in_specs.append(pl.BlockSpec(bias.shape, lambda b, t, h: (0, 0, 0), pipeline_mode=RESIDENT))

    def wspec(n, col0=0):
        return pl.BlockSpec((D_MODEL, n), lambda b, t, h: (0, col0 // n), pipeline_mode=RESIDENT)

    def row_spec(n):
        return pl.BlockSpec((None, GATE_ROWS, n), lambda b, t, h: (b, t * ATTN_HEADS + h, 0))

    tspec = pl.BlockSpec((GATE_ROWS, LANES), lambda b, t, h: (t * ATTN_HEADS + h, 0))
    in_specs += [
        row_spec(D_MODEL),
        pl.BlockSpec((1, D_MODEL), lambda b, t, h: (0, 0)),
        wspec(2 * RET_QK_WIDTH),
        wspec(GROUP_WIDTH, COL_AG), wspec(RET_V_WIDTH, COL_RG), wspec(2 * D_MODEL, COL_MG),
        tspec, tspec,
        pl.BlockSpec((GATE_ROWS, RET_QK_WIDTH), lambda b, t, h: (0, 0), pipeline_mode=RESIDENT),
        pl.BlockSpec((1, 2 * D_MODEL), lambda b, t, h: (0, 0)),
    ]
    out_widths = (GROUP_WIDTH, RET_V_WIDTH, 2 * D_MODEL)
    per_group = pltpu.VMEM((N_GROUPS, ATTN_TILE, HEAD_DIM), F32)
    return pl.pallas_call(
        _attn_gates_kernel,
        grid=(B, nt, ATTN_HEADS),
        in_specs=in_specs,
        out_specs=[
            pl.BlockSpec((None, ATTN_TILE, HEAD_DIM), lambda b, t, h: (b, t, h)),
            row_spec(RET_QK_WIDTH),
            pl.BlockSpec((None, cpt, RET_QK_WIDTH, RET_CHUNK),
                         lambda b, t, h: (b, t * ATTN_HEADS + h, 0, 0)),
            *[row_spec(n) for n in out_widths],
        ],
        out_shape=[
            jax.ShapeDtypeStruct((B, S, GROUP_WIDTH), BF16),
            jax.ShapeDtypeStruct((B, S, RET_QK_WIDTH), BF16),
            jax.ShapeDtypeStruct((B, S // RET_CHUNK, RET_QK_WIDTH, RET_CHUNK), BF16),
            *[jax.ShapeDtypeStruct((B, S, n), BF16) for n in out_widths],
        ],
        scratch_shapes=[per_group, per_group, per_group,
                        pltpu.VMEM((ATTN_TILE, HEAD_DIM), F32)],
        compiler_params=pltpu.CompilerParams(
            dimension_semantics=("arbitrary", "arbitrary", "arbitrary"),
            vmem_limit_bytes=VMEM_LIMIT),
        name="attn_gates",
    )(*args, bias, x, lnw, wqk, w16, w16, w16, cos_t, sin_t, kdec, bg)


def _retention_kernel(q_ref, kt_ref, v_ref, gate_ref, gnw_ref, eps_ref, cd_ref,
                      out_ref, state, p_scr):
    @pl.when(pl.program_id(1) == 0)
    def _():
        state[...] = jnp.zeros_like(state)

    dk, dv, C = RET_KEY_DIM, RET_VALUE_DIM, RET_CHUNK
    row_k = lax.broadcasted_iota(jnp.int32, (2 * dk, C), 0)
    row = lax.broadcasted_iota(jnp.int32, (2 * dk, 2 * dv), 0)
    col = lax.broadcasted_iota(jnp.int32, (C, 2 * dv), 1)
    diag_blk = (row < dk) == (col < dv)
    causal = (col % C) <= lax.broadcasted_iota(jnp.int32, (C, 2 * C), 0)
    units = [(bi, j) for bi in range(RET_BATCH) for j in range(RET_HEADS // 2)]

    def q_pair(bi, j):
        return q_ref[bi, :, j * 2 * dk:(j + 1) * 2 * dk]

    def kt_pair(bi, j):
        return kt_ref[bi, j * 2 * dk:(j + 1) * 2 * dk, :]

    def v_pair(bi, j):
        return v_ref[bi, :, j * 2 * dv:(j + 1) * 2 * dv]

    for bi, j in units:
        kt = kt_pair(bi, j)
        zero = jnp.zeros_like(kt)
        kt_blk = jnp.concatenate([jnp.where(row_k < dk, kt, zero),
                                  jnp.where(row_k >= dk, kt, zero)], axis=1)
        p_scr[bi, j] = jnp.where(causal, _dot(q_pair(bi, j), kt_blk), 0.0).astype(BF16)

    for bi, j in units:
        v = v_pair(bi, j)
        zero = jnp.zeros_like(v)
        v_blk = jnp.concatenate([jnp.where(col < dv, v, zero),
                                 jnp.where(col >= dv, v, zero)], axis=0)
        u = _dot(jnp.concatenate([p_scr[bi, j], q_pair(bi, j)], axis=1),
                 jnp.concatenate([v_blk, state[bi, j].astype(BF16)], axis=0))
        for hh in range(2):
            h = 2 * j + hh
            hs = slice(h * dv, (h + 1) * dv)
            uh = u[:, hh * dv:(hh + 1) * dv]
            uh = uh * lax.rsqrt(jnp.mean(uh * uh, axis=-1, keepdims=True) + eps_ref[h])
            out_ref[bi, :, hs] = (uh * gnw_ref[:, hs]
                                  * gate_ref[bi, :, hs].astype(F32)).astype(BF16)

    for bi, j in units:
        kv = _dot(kt_pair(bi, j), v_pair(bi, j))
        state[bi, j] = cd_ref[j] * (state[bi, j] + jnp.where(diag_blk, kv, 0.0))


def _retention(rq, rkt, rv, gate, gnw, eps_t, cd):
    B, S, _ = rv.shape
    nc = S // RET_CHUNK
    npair = RET_HEADS // 2
    pair_shape = (RET_BATCH, npair, 2 * RET_KEY_DIM, 2 * RET_VALUE_DIM)

    def const(shape):
        return pl.BlockSpec(shape, lambda b, n: (0,) * len(shape))

    def rows(width):
        return pl.BlockSpec((RET_BATCH, RET_CHUNK, width), lambda b, n: (b, n, 0))

    return pl.pallas_call(
        _retention_kernel,
        grid=(B // RET_BATCH, nc),
        in_specs=[
            rows(RET_QK_WIDTH),
            pl.BlockSpec((RET_BATCH, None, RET_QK_WIDTH, RET_CHUNK), lambda b, n: (b, n, 0, 0)),
            rows(RET_V_WIDTH),
            rows(RET_V_WIDTH),
            const((1, RET_V_WIDTH)),
            const((RET_HEADS, RET_CHUNK, RET_VALUE_DIM)),
            const(pair_shape[1:]),
        ],
        out_specs=rows(RET_V_WIDTH),
        out_shape=jax.ShapeDtypeStruct((B, S, RET_V_WIDTH), BF16),
        scratch_shapes=[pltpu.VMEM(pair_shape, F32),
                        pltpu.VMEM((RET_BATCH, npair, RET_CHUNK, 2 * RET_CHUNK), BF16)],
        compiler_params=pltpu.CompilerParams(
            dimension_semantics=("arbitrary", "arbitrary"), vmem_limit_bytes=VMEM_LIMIT),
        name="retention",
    )(rq, rkt, rv, gate, gnw, eps_t, cd)


def _merge_out_kernel(a_ref, ag_ref, r_ref, mg_ref, x_ref, wa_ref, wr_ref, wo_ref, lnf_ref,
                      out_ref):
    attn = (a_ref[...].astype(F32) * ag_ref[...].astype(F32)).astype(BF16)
    y_attn = _dot(attn, wa_ref[...])
    y_ret = _dot(r_ref[...], wr_ref[...])
    merged = (mg_ref[:, :D_MODEL].astype(F32) * y_attn
              + mg_ref[:, D_MODEL:].astype(F32) * y_ret)
    h = x_ref[...] + _dot(merged.astype(BF16), wo_ref[...])
    out_ref[...] = _rmsnorm_rows(h, lnf_ref[...])


def _merge_out(attn, ag, ret_g, mg, x, wa, wr, wo, lnf):
    B, S, _ = x.shape
    nt = S // OUT_ROWS

    def rows(n):
        return pl.BlockSpec((None, OUT_ROWS, n), lambda b, t: (b, t, 0))

    def const(shape):
        return pl.BlockSpec(shape, lambda b, t: (0,) * len(shape), pipeline_mode=RESIDENT)

    return pl.pallas_call(
        _merge_out_kernel,
        grid=(B, nt),
        in_specs=[rows(GROUP_WIDTH), rows(GROUP_WIDTH), rows(RET_V_WIDTH), rows(2 * D_MODEL),
                  rows(D_MODEL),
                  const((GROUP_WIDTH, D_MODEL)), const((RET_V_WIDTH, D_MODEL)),
                  const((D_MODEL, D_MODEL)), const((1, D_MODEL))],
        out_specs=rows(D_MODEL),
        out_shape=jax.ShapeDtypeStruct((B, S, D_MODEL), F32),
        compiler_params=pltpu.CompilerParams(
            dimension_semantics=("arbitrary", "arbitrary"), vmem_limit_bytes=VMEM_LIMIT),
        name="merge_out",
    )(attn, ag, ret_g, mg, x, wa, wr, wo, lnf)


def _rope_tables(S):
    half = RET_KEY_DIM // 2
    pos = jnp.arange(S, dtype=F32)
    inv_freq = ROPE_BASE ** (-jnp.linspace(0.0, 1.0, half, dtype=F32))
    ang = pos[:, None] * inv_freq[None, :]
    cos = jnp.cos(ang)
    sin = jnp.sin(ang)
    return (jnp.concatenate([cos, cos, cos, cos], axis=1),
            jnp.concatenate([-sin, sin, -sin, sin], axis=1))


def _retention_constants():
    H, C, dk, dv = RET_HEADS, RET_CHUNK, RET_KEY_DIM, RET_VALUE_DIM
    log_gamma = jnp.log(1.0 - 2.0 ** (-5.0 - jnp.arange(H, dtype=F32)))
    idx = jnp.arange(C, dtype=F32)
    inv_decay = jnp.exp(-(idx + 1.0)[None, :] * log_gamma[:, None])
    chunk_decay = jnp.exp(C * log_gamma)
    kdec = jnp.repeat(inv_decay.T, dk, axis=1) * dk ** -0.5
    kdec = jnp.tile(kdec, (GATE_ROWS // C, 1))
    eps_t = jnp.broadcast_to((NORM_EPS * inv_decay * inv_decay)[:, :, None], (H, C, dv))
    cd = jnp.broadcast_to(chunk_decay[:, None, None], (H, dk, 2 * dv)).reshape(H // 2, 2 * dk, 2 * dv)
    return kdec, eps_t, cd


def kernel(x, ln1_w, w_in, b_gate, attn_proj, ret_proj, ret_gn_w, w_out, lnf_w):
    B, S, _ = x.shape
    assert w_in.shape[0] == 1, "single layer"
    w16 = w_in[0].astype(BF16)

    e = jnp.arange(RET_KEY_DIM // 2)
    head_perm = jnp.concatenate([2 * e, 2 * e + 1])
    perm = (jnp.arange(2 * RET_HEADS)[:, None] * RET_KEY_DIM + head_perm[None, :]).reshape(-1)
    wqk = w16[:, COL_RQ:COL_RQ + 2 * RET_QK_WIDTH][:, perm]

    lnw = ln1_w[0].reshape(1, D_MODEL)
    q1, q2, q3, rv = _qkv_proj(x, lnw, w16)
    cos_t, sin_t = _rope_tables(S)
    kdec, eps_t, cd = _retention_constants()
    attn, rq, rkt, ag, rg, mg = _attn_gates(
        (q1, q2, q3), x, lnw, wqk, w16, cos_t, sin_t, kdec, b_gate[0].reshape(1, 2 * D_MODEL))
    ret_g = _retention(rq, rkt, rv, rg, ret_gn_w[0].reshape(1, RET_V_WIDTH), eps_t, cd)

    return _merge_out(attn, ag, ret_g, mg, x,
                      attn_proj[0].astype(BF16), ret_proj[0].astype(BF16),
                      w_out[0].astype(BF16), lnf_w.reshape(1, D_MODEL))
```

```python
import jax
import jax.numpy as jnp
from jax import lax
from jax.experimental import pallas as pl
from jax.experimental.pallas import tpu as pltpu

D_MODEL = 1024
ATTN_GROUPS = ((128, 1), (512, 4), (2048, 16))
N_GROUPS = 3
ATTN_HEADS = 4
HEAD_DIM = 128
GROUP_WIDTH = ATTN_HEADS * HEAD_DIM
QKV_WIDTH = N_GROUPS * GROUP_WIDTH
BAND = 128
RET_HEADS = 8
RET_KEY_DIM = 64
RET_VALUE_DIM = 128
RET_QK_WIDTH = RET_HEADS * RET_KEY_DIM
RET_V_WIDTH = RET_HEADS * RET_VALUE_DIM
RET_CHUNK = 128
ROPE_BASE = 10000.0
COL_AG = 3 * QKV_WIDTH
COL_RQ = COL_AG + GROUP_WIDTH
COL_RV = COL_RQ + 2 * RET_QK_WIDTH
COL_RG = COL_RV + RET_V_WIDTH
COL_MG = COL_RG + RET_V_WIDTH
NORM_EPS = 1e-6
NEG_BIG = -1e30
LANES = 128
LOG2E = 1.4426950408889634
MIX_STRIDE = 4

PROJ_ROWS = 1024
ATTN_TILE = 2048
GATE_ROWS = ATTN_TILE // ATTN_HEADS
RET_BATCH = 8
OUT_ROWS = 1024
VMEM_LIMIT = 56 * 1024 * 1024
RESIDENT = pl.Buffered(1)

BF16 = jnp.bfloat16
F32 = jnp.float32


def _rmsnorm_rows(x, w):
    return x * lax.rsqrt(jnp.mean(x * x, axis=-1, keepdims=True) + NORM_EPS) * w


def _dot(a, b):
    return jnp.dot(a, b, preferred_element_type=F32)


def _sigmoid(t):
    return 0.5 * jnp.tanh(0.5 * t) + 0.5


def _silu(t):
    return t * _sigmoid(t)


def _qkv_proj_kernel(x_ref, lnw_ref, *refs):
    w_refs = refs[:3 * N_GROUPS]
    wrv_ref = refs[3 * N_GROUPS]
    out_refs = refs[3 * N_GROUPS + 1:4 * N_GROUPS + 1]
    rv_ref, xn_scr = refs[4 * N_GROUPS + 1:]
    xn = _rmsnorm_rows(x_ref[...], lnw_ref[...])
    n_lane_tiles = D_MODEL // LANES
    for ct in range(n_lane_tiles):
        xn_scr[ct] = xn[:, ct * LANES:(ct + 1) * LANES]
    q_scale = HEAD_DIM ** -0.5 * LOG2E
    q_cols = slice(0, GROUP_WIDTH)
    k_cols = slice(GROUP_WIDTH, 2 * GROUP_WIDTH)
    v_cols = slice(2 * GROUP_WIDTH, 3 * GROUP_WIDTH)

    def strided_rows(r, rows, dil):
        return jnp.concatenate(
            [xn_scr[ct, pl.ds(r, rows, stride=dil), :] for ct in range(n_lane_tiles)], axis=1)

    for g, (_, dil) in enumerate(ATTN_GROUPS):
        wq_ref, wk_ref, wv_ref = w_refs[3 * g:3 * g + 3]
        o_ref = out_refs[g]
        if dil == 1:
            sub = BAND // MIX_STRIDE
            lhs_q = jnp.concatenate(
                [strided_rows(blk * BAND + r4, sub, MIX_STRIDE).astype(BF16)
                 for blk in range(PROJ_ROWS // BAND) for r4 in range(MIX_STRIDE)], axis=0)
            lhs = xn.astype(BF16)
            o_ref[0, :, q_cols] = (_dot(lhs_q, wq_ref[...]) * q_scale).astype(BF16)
            o_ref[0, :, k_cols] = _dot(lhs, wk_ref[...]).astype(BF16)
            o_ref[0, :, v_cols] = _dot(lhs, wv_ref[...]).astype(BF16)
            rv_ref[...] = _dot(lhs, wrv_ref[...]).astype(BF16)
        else:
            rows = PROJ_ROWS // dil
            lhs = jnp.concatenate(
                [strided_rows(r, rows, dil).astype(BF16) for r in range(dil)], axis=0)
            q = (_dot(lhs, wq_ref[...]) * q_scale).astype(BF16)
            k = _dot(lhs, wk_ref[...]).astype(BF16)
            v = _dot(lhs, wv_ref[...]).astype(BF16)
            for r in range(dil):
                rs = slice(r * rows, (r + 1) * rows)
                o_ref[r, :, q_cols] = q[rs]
                o_ref[r, :, k_cols] = k[rs]
                o_ref[r, :, v_cols] = v[rs]


def _qkv_proj(x, lnw, w16):
    B, S, _ = x.shape
    nt = S // PROJ_ROWS
    w_specs = [pl.BlockSpec((D_MODEL, GROUP_WIDTH), lambda b, t, cb=part * N_GROUPS + g: (0, cb),
                            pipeline_mode=RESIDENT)
               for g in range(N_GROUPS) for part in range(3)]
    w_specs.append(pl.BlockSpec((D_MODEL, RET_V_WIDTH), lambda b, t: (0, COL_RV // RET_V_WIDTH),
                                pipeline_mode=RESIDENT))
    return pl.pallas_call(
        _qkv_proj_kernel,
        grid=(B, nt),
        in_specs=[
            pl.BlockSpec((None, PROJ_ROWS, D_MODEL), lambda b, t: (b, t, 0)),
            pl.BlockSpec((1, D_MODEL), lambda b, t: (0, 0)),
            *w_specs,
        ],
        out_specs=[
            *[pl.BlockSpec((None, dil, PROJ_ROWS // dil, QKV_WIDTH), lambda b, t: (b, 0, t, 0))
              for _, dil in ATTN_GROUPS],
            pl.BlockSpec((None, PROJ_ROWS, RET_V_WIDTH), lambda b, t: (b, t, 0)),
        ],
        out_shape=[
            *[jax.ShapeDtypeStruct((B, dil, S // dil, QKV_WIDTH), BF16) for _, dil in ATTN_GROUPS],
            jax.ShapeDtypeStruct((B, S, RET_V_WIDTH), BF16),
        ],
        scratch_shapes=[pltpu.VMEM((D_MODEL // LANES, PROJ_ROWS, LANES), F32)],
        compiler_params=pltpu.CompilerParams(
            dimension_semantics=("arbitrary", "arbitrary"), vmem_limit_bytes=VMEM_LIMIT),
        name="qkv_proj",
    )(x, lnw, *([w16] * (3 * N_GROUPS + 1)))


def _swap_pair_halves(t):
    n = t.shape[-1]
    lane = lax.broadcasted_iota(jnp.int32, t.shape, 1)
    up = pltpu.roll(t, n - RET_KEY_DIM // 2, axis=1)
    down = pltpu.roll(t, RET_KEY_DIM // 2, axis=1)
    return jnp.where(lane % RET_KEY_DIM < RET_KEY_DIM // 2, up, down)


def _gate_proj_items(x_ref, lnw_ref, wqk_ref, wg_ref, wrg_ref, wmg_ref,
                     cos_ref, sin_ref, kdec_ref, bg_ref,
                     rq_ref, rkt_ref, ag_ref, rg_ref, mg_ref):
    cache = {}

    def lhs():
        if "xb" not in cache:
            cache["xb"] = _rmsnorm_rows(x_ref[...], lnw_ref[...]).astype(BF16)
        return cache["xb"]

    def rope(t):
        pairs = t.shape[1] // LANES
        cos = jnp.concatenate([cos_ref[...]] * pairs, axis=1)
        sin = jnp.concatenate([sin_ref[...]] * pairs, axis=1)
        return t * cos + _swap_pair_halves(t) * sin

    def retention_q():
        t = _dot(lhs(), wqk_ref[:, :RET_QK_WIDTH])
        yield
        rq_ref[...] = rope(t).astype(BF16)

    def retention_k():
        t = _dot(lhs(), wqk_ref[:, RET_QK_WIDTH:])
        yield
        k = rope(t) * kdec_ref[...]
        for c in range(GATE_ROWS // RET_CHUNK):
            rkt_ref[c] = k[c * RET_CHUNK:(c + 1) * RET_CHUNK, :].T.astype(BF16)

    def merge_gate(half):
        cols = slice(half * D_MODEL, (half + 1) * D_MODEL)
        t = _dot(lhs(), wmg_ref[:, cols])
        yield
        mg_ref[:, cols] = _sigmoid(t + bg_ref[:, cols]).astype(BF16)

    def retention_gate():
        t = _dot(lhs(), wrg_ref[...])
        yield
        rg_ref[...] = _silu(t).astype(BF16)

    def attention_gate():
        t = _dot(lhs(), wg_ref[...])
        yield
        ag_ref[...] = _silu(t).astype(BF16)

    return [retention_q, retention_k, lambda: merge_gate(0), lambda: merge_gate(1),
            retention_gate, attention_gate]


def _attention_items(in_refs, bias_ref, out_ref, acc_scr, max_scr, den_scr, nat_scr, first):
    ones = jnp.ones((2 * BAND, HEAD_DIM), BF16)
    quarter = ATTN_TILE // MIX_STRIDE

    def unit(g, r, n):
        dil = ATTN_GROUPS[g][1]
        q_ref, kc_ref, kp_ref, vc_ref, vp_ref = in_refs[5 * g:5 * g + 5]
        bias_base = 2 if dil == 1 else 0
        q = q_ref[r, n * BAND:(n + 1) * BAND, :]
        if n == 0:
            k = jnp.concatenate([kp_ref[r], kc_ref[r, 0:BAND, :]], axis=0)
            v = jnp.concatenate([vp_ref[r], vc_ref[r, 0:BAND, :]], axis=0)
            bias = bias_ref[bias_base + first]
        else:
            k = kc_ref[r, (n - 1) * BAND:(n + 1) * BAND, :]
            v = vc_ref[r, (n - 1) * BAND:(n + 1) * BAND, :]
            bias = bias_ref[bias_base]
        s = lax.dot_general(q, k, (((1,), (1,)), ((), ())), preferred_element_type=F32) + bias
        yield
        m = jnp.max(s, axis=-1, keepdims=True)
        p = jnp.exp2(s - m).astype(BF16)
        yield
        res = _dot(p, jnp.concatenate([v, ones], axis=1))
        acc, den = res[:, :HEAD_DIM], res[:, HEAD_DIM:]
        mb = jnp.broadcast_to(m, (BAND, HEAD_DIM))
        if dil == 1:
            sub = BAND // MIX_STRIDE
            for r4 in range(MIX_STRIDE):
                dst = pl.ds(r4 * quarter + n * sub, sub)
                src = slice(r4 * sub, (r4 + 1) * sub)
                acc_scr[g, dst, :] = acc[src]
                max_scr[g, dst, :] = mb[src]
                den_scr[g, dst, :] = den[src]
        else:
            if dil == MIX_STRIDE:
                dst = pl.ds(r * quarter + n * BAND, BAND)
            else:
                sub_stride = dil // MIX_STRIDE
                dst = pl.ds((r % MIX_STRIDE) * quarter + r // MIX_STRIDE
                            + n * BAND * sub_stride, BAND, stride=sub_stride)
            acc_scr[g, dst, :] = acc
            max_scr[g, dst, :] = mb
            den_scr[g, dst, :] = den

    def mix(r4, ch):
        rows = pl.ds(r4 * quarter + ch * BAND, BAND)
        m0, m1, m2 = max_scr[0, rows, :], max_scr[1, rows, :], max_scr[2, rows, :]
        mx = jnp.maximum(jnp.maximum(m0, m1), m2)
        w0 = jnp.exp2(m0 - mx)
        w1 = jnp.exp2(m1 - mx)
        w2 = jnp.exp2(m2 - mx)
        num = w0 * acc_scr[0, rows, :] + w1 * acc_scr[1, rows, :] + w2 * acc_scr[2, rows, :]
        den = w0 * den_scr[0, rows, :] + w1 * den_scr[1, rows, :] + w2 * den_scr[2, rows, :]
        nat_scr[pl.ds(r4 + ch * BAND * MIX_STRIDE, BAND, stride=MIX_STRIDE), :] = num / den

    def write_out():
        out_ref[...] = nat_scr[...].astype(BF16)

    def mix_quarter(r4):
        for ch in range(quarter // BAND):
            mix(r4, ch)

    per_round = ATTN_TILE // BAND // 2
    rounds = [[lambda n=n: unit(0, 0, n) for n in range(h * per_round, (h + 1) * per_round)]
              for h in range(2)]
    for r4 in range(MIX_STRIDE):
        rounds.append([lambda g=g, r=r, n=n: unit(g, r, n)
                       for g in range(1, N_GROUPS)
                       for r in range(r4, ATTN_GROUPS[g][1], MIX_STRIDE)
                       for n in range(ATTN_TILE // ATTN_GROUPS[g][1] // BAND)])
    return rounds, mix_quarter, write_out


def _attn_gates_kernel(*refs):
    n_attn = 5 * N_GROUPS
    in_refs = refs[:n_attn]
    bias_ref = refs[n_attn]
    proj_in = refs[n_attn + 1:n_attn + 11]
    attn_out = refs[n_attn + 11]
    proj_out = refs[n_attn + 12:n_attn + 17]
    acc_scr, max_scr, den_scr, nat_scr = refs[n_attn + 17:]
    first = (pl.program_id(1) == 0).astype(jnp.int32)
    rounds, mix_quarter, write_out = _attention_items(
        in_refs, bias_ref, attn_out, acc_scr, max_scr, den_scr, nat_scr, first)
    proj = _gate_proj_items(*proj_in, *proj_out)
    assert len(proj) == len(rounds)

    def advance(gen):
        next(gen, None)

    prev_units = []
    for r in range(len(rounds) + 1):
        units = [make() for make in rounds[r]] if r < len(rounds) else []
        for j in range(max(len(units), len(prev_units))):
            if j < len(units):
                advance(units[j])
            if j < len(prev_units):
                advance(prev_units[j])
        if r >= 3:
            mix_quarter(r - 3)
        if r < len(proj):
            for _ in proj[r]():
                pass
        for u in units:
            advance(u)
        prev_units = units
    write_out()


def _attn_bias_tables():
    rho = jnp.arange(BAND)[:, None]
    c = jnp.arange(2 * BAND)[None, :]
    sub = BAND // MIX_STRIDE
    tables = []
    for a in (rho, MIX_STRIDE * (rho % sub) + rho // sub):
        ok = (c >= a) & (c <= a + BAND)
        tables += [ok, ok & (c >= BAND)]
    return jnp.where(jnp.stack(tables), 0.0, NEG_BIG).astype(F32)


def _attn_gates(qkv_groups, x, lnw, wqk, w16, cos_t, sin_t, kdec, bg):
    B, S, _ = x.shape
    nt = S // ATTN_TILE
    cpt = GATE_ROWS // RET_CHUNK
    in_specs = []
    args = []
    for (_, dil), qkv in zip(ATTN_GROUPS, qkv_groups):
        rows = ATTN_TILE // dil
        ratio = rows // BAND

        def cur(col0, rows=rows, dil=dil):
            return pl.BlockSpec((None, dil, rows, HEAD_DIM),
                                lambda b, t, h: (b, 0, t, col0 + h))

        def prev(col0, ratio=ratio, dil=dil):
            return pl.BlockSpec((None, dil, BAND, HEAD_DIM),
                                lambda b, t, h: (b, 0, jnp.maximum(t * ratio - 1, 0), col0 + h))

        in_specs += [cur(0), cur(ATTN_HEADS), prev(ATTN_HEADS),
                     cur(2 * ATTN_HEADS), prev(2 * ATTN_HEADS)]
        args += [qkv] * 5
    bias = _attn_bias_tables()
    in_specs.append(pl.BlockSpec(bias.shape, lambda b, t, h: (0, 0, 0), pipeline_mode=RESIDENT))

    def wspec(n, col0=0):
        return pl.BlockSpec((D_MODEL, n), lambda b, t, h: (0, col0 // n), pipeline_mode=RESIDENT)

    def row_spec(n):
        return pl.BlockSpec((None, GATE_ROWS, n), lambda b, t, h: (b, t * ATTN_HEADS + h, 0))

    tspec = pl.BlockSpec((GATE_ROWS, LANES), lambda b, t, h: (t * ATTN_HEADS + h, 0))
    in_specs += [
        row_spec(D_MODEL),
        pl.BlockSpec((1, D_MODEL), lambda b, t, h: (0, 0)),
        wspec(2 * RET_QK_WIDTH),
        wspec(GROUP_WIDTH, COL_AG), wspec(RET_V_WIDTH, COL_RG), wspec(2 * D_MODEL, COL_MG),
        tspec, tspec,
        pl.BlockSpec((GATE_ROWS, RET_QK_WIDTH), lambda b, t, h: (0, 0), pipeline_mode=RESIDENT),
        pl.BlockSpec((1, 2 * D_MODEL), lambda b, t, h: (0, 0)),
    ]
    out_widths = (GROUP_WIDTH, RET_V_WIDTH, 2 * D_MODEL)
    per_group = pltpu.VMEM((N_GROUPS, ATTN_TILE, HEAD_DIM), F32)
    return pl.pallas_call(
        _attn_gates_kernel,
        grid=(B, nt, ATTN_HEADS),
        in_specs=in_specs,
        out_specs=[
            pl.BlockSpec((None, ATTN_TILE, HEAD_DIM), lambda b, t, h: (b, t, h)),
            row_spec(RET_QK_WIDTH),
            pl.BlockSpec((None, cpt, RET_QK_WIDTH, RET_CHUNK),
                         lambda b, t, h: (b, t * ATTN_HEADS + h, 0, 0)),
            *[row_spec(n) for n in out_widths],
        ],
        out_shape=[
            jax.ShapeDtypeStruct((B, S, GROUP_WIDTH), BF16),
            jax.ShapeDtypeStruct((B, S, RET_QK_WIDTH), BF16),
            jax.ShapeDtypeStruct((B, S // RET_CHUNK, RET_QK_WIDTH, RET_CHUNK), BF16),
            *[jax.ShapeDtypeStruct((B, S, n), BF16) for n in out_widths],
        ],
        scratch_shapes=[per_group, per_group, per_group,
                        pltpu.VMEM((ATTN_TILE, HEAD_DIM), F32)],
        compiler_params=pltpu.CompilerParams(
            dimension_semantics=("arbitrary", "arbitrary", "arbitrary"),
            vmem_limit_bytes=VMEM_LIMIT),
        name="attn_gates",
    )(*args, bias, x, lnw, wqk, w16, w16, w16, cos_t, sin_t, kdec, bg)


def _retention_kernel(q_ref, kt_ref, v_ref, gate_ref, gnw_ref, eps_ref, cd_ref,
                      out_ref, state, p_scr):
    @pl.when(pl.program_id(1) == 0)
    def _():
        state[...] = jnp.zeros_like(state)

    dk, dv, C = RET_KEY_DIM, RET_VALUE_DIM, RET_CHUNK
    row_k = lax.broadcasted_iota(jnp.int32, (2 * dk, C), 0)
    row = lax.broadcasted_iota(jnp.int32, (2 * dk, 2 * dv), 0)
    col = lax.broadcasted_iota(jnp.int32, (C, 2 * dv), 1)
    diag_blk = (row < dk) == (col < dv)
    causal = (col % C) <= lax.broadcasted_iota(jnp.int32, (C, 2 * C), 0)
    units = [(bi, j) for bi in range(RET_BATCH) for j in range(RET_HEADS // 2)]

    def q_pair(bi, j):
        return q_ref[bi, :, j * 2 * dk:(j + 1) * 2 * dk]

    def kt_pair(bi, j):
        return kt_ref[bi, j * 2 * dk:(j + 1) * 2 * dk, :]

    def v_pair(bi, j):
        return v_ref[bi, :, j * 2 * dv:(j + 1) * 2 * dv]

    for bi, j in units:
        kt = kt_pair(bi, j)
        zero = jnp.zeros_like(kt)
        kt_blk = jnp.concatenate([jnp.where(row_k < dk, kt, zero),
                                  jnp.where(row_k >= dk, kt, zero)], axis=1)
        s = _dot(q_pair(bi, j), kt_blk).astype(BF16)
        p_scr[bi, j] = jnp.where(causal, s, jnp.zeros_like(s))

    for bi, j in units:
        v = v_pair(bi, j)
        zero = jnp.zeros_like(v)
        v_blk = jnp.concatenate([jnp.where(col < dv, v, zero),
                                 jnp.where(col >= dv, v, zero)], axis=0)
        u = _dot(jnp.concatenate([p_scr[bi, j], q_pair(bi, j)], axis=1),
                 jnp.concatenate([v_blk, state[bi, j].astype(BF16)], axis=0))
        for hh in range(2):
            h = 2 * j + hh
            hs = slice(h * dv, (h + 1) * dv)
            uh = u[:, hh * dv:(hh + 1) * dv]
            uh = uh * lax.rsqrt(jnp.sum(uh * uh, axis=-1, keepdims=True) + eps_ref[h])
            out_ref[bi, :, hs] = (uh * gnw_ref[:, hs]).astype(BF16) * gate_ref[bi, :, hs]

    for bi, j in units:
        kv = _dot(kt_pair(bi, j), v_pair(bi, j))
        state[bi, j] = cd_ref[j] * (state[bi, j] + jnp.where(diag_blk, kv, 0.0))


def _retention(rq, rkt, rv, gate, gnw, eps_t, cd):
    B, S, _ = rv.shape
    nc = S // RET_CHUNK
    npair = RET_HEADS // 2
    pair_shape = (RET_BATCH, npair, 2 * RET_KEY_DIM, 2 * RET_VALUE_DIM)

    def const(shape):
        return pl.BlockSpec(shape, lambda b, n: (0,) * len(shape))

    def rows(width):
        return pl.BlockSpec((RET_BATCH, RET_CHUNK, width), lambda b, n: (b, n, 0))

    return pl.pallas_call(
        _retention_kernel,
        grid=(B // RET_BATCH, nc),
        in_specs=[
            rows(RET_QK_WIDTH),
            pl.BlockSpec((RET_BATCH, None, RET_QK_WIDTH, RET_CHUNK), lambda b, n: (b, n, 0, 0)),
            rows(RET_V_WIDTH),
            rows(RET_V_WIDTH),
            const((1, RET_V_WIDTH)),
            const((RET_HEADS, RET_CHUNK, RET_VALUE_DIM)),
            const(pair_shape[1:]),
        ],
        out_specs=rows(RET_V_WIDTH),
        out_shape=jax.ShapeDtypeStruct((B, S, RET_V_WIDTH), BF16),
        scratch_shapes=[pltpu.VMEM(pair_shape, F32),
                        pltpu.VMEM((RET_BATCH, npair, RET_CHUNK, 2 * RET_CHUNK), BF16)],
        compiler_params=pltpu.CompilerParams(
            dimension_semantics=("arbitrary", "arbitrary"), vmem_limit_bytes=VMEM_LIMIT),
        name="retention",
    )(rq, rkt, rv, gate, gnw, eps_t, cd)


def _merge_out_kernel(a_ref, ag_ref, r_ref, mg_ref, x_ref, wa_ref, wr_ref, wo_ref, lnf_ref,
                      out_ref):
    attn = (a_ref[...].astype(F32) * ag_ref[...].astype(F32)).astype(BF16)
    y_attn = _dot(attn, wa_ref[...])
    y_ret = _dot(r_ref[...], wr_ref[...])
    merged = (mg_ref[:, :D_MODEL].astype(F32) * y_attn
              + mg_ref[:, D_MODEL:].astype(F32) * y_ret)
    h = x_ref[...] + _dot(merged.astype(BF16), wo_ref[...])
    out_ref[...] = _rmsnorm_rows(h, lnf_ref[...])


def _merge_out(attn, ag, ret_g, mg, x, wa, wr, wo, lnf):
    B, S, _ = x.shape
    nt = S // OUT_ROWS

    def rows(n):
        return pl.BlockSpec((None, OUT_ROWS, n), lambda b, t: (b, t, 0))

    def const(shape):
        return pl.BlockSpec(shape, lambda b, t: (0,) * len(shape), pipeline_mode=RESIDENT)

    return pl.pallas_call(
        _merge_out_kernel,
        grid=(B, nt),
        in_specs=[rows(GROUP_WIDTH), rows(GROUP_WIDTH), rows(RET_V_WIDTH), rows(2 * D_MODEL),
                  rows(D_MODEL),
                  const((GROUP_WIDTH, D_MODEL)), const((RET_V_WIDTH, D_MODEL)),
                  const((D_MODEL, D_MODEL)), const((1, D_MODEL))],
        out_specs=rows(D_MODEL),
        out_shape=jax.ShapeDtypeStruct((B, S, D_MODEL), F32),
        compiler_params=pltpu.CompilerParams(
            dimension_semantics=("arbitrary", "arbitrary"), vmem_limit_bytes=VMEM_LIMIT),
        name="merge_out",
    )(attn, ag, ret_g, mg, x, wa, wr, wo, lnf)


def _rope_tables(S):
    half = RET_KEY_DIM // 2
    pos = jnp.arange(S, dtype=F32)
    inv_freq = ROPE_BASE ** (-jnp.linspace(0.0, 1.0, half, dtype=F32))
    ang = pos[:, None] * inv_freq[None, :]
    cos = jnp.cos(ang)
    sin = jnp.sin(ang)
    return (jnp.concatenate([cos, cos, cos, cos], axis=1),
            jnp.concatenate([-sin, sin, -sin, sin], axis=1))


def _retention_constants():
    H, C, dk, dv = RET_HEADS, RET_CHUNK, RET_KEY_DIM, RET_VALUE_DIM
    log_gamma = jnp.log(1.0 - 2.0 ** (-5.0 - jnp.arange(H, dtype=F32)))
    idx = jnp.arange(C, dtype=F32)
    inv_decay = jnp.exp(-(idx + 1.0)[None, :] * log_gamma[:, None])
    chunk_decay = jnp.exp(C * log_gamma)
    kdec = jnp.repeat(inv_decay.T, dk, axis=1) * dk ** -0.5
    kdec = jnp.tile(kdec, (GATE_ROWS // C, 1))
    eps_t = jnp.broadcast_to((dv * NORM_EPS * inv_decay * inv_decay)[:, :, None], (H, C, dv))
    cd = jnp.broadcast_to(chunk_decay[:, None, None], (H, dk, 2 * dv)).reshape(H // 2, 2 * dk, 2 * dv)
    return kdec, eps_t, cd


def kernel(x, ln1_w, w_in, b_gate, attn_proj, ret_proj, ret_gn_w, w_out, lnf_w):
    B, S, _ = x.shape
    assert w_in.shape[0] == 1, "single layer"
    w16 = w_in[0].astype(BF16)

    e = jnp.arange(RET_KEY_DIM // 2)
    head_perm = jnp.concatenate([2 * e, 2 * e + 1])
    perm = (jnp.arange(2 * RET_HEADS)[:, None] * RET_KEY_DIM + head_perm[None, :]).reshape(-1)
    wqk = w16[:, COL_RQ:COL_RQ + 2 * RET_QK_WIDTH][:, perm]

    lnw = ln1_w[0].reshape(1, D_MODEL)
    q1, q2, q3, rv = _qkv_proj(x, lnw, w16)
    cos_t, sin_t = _rope_tables(S)
    kdec, eps_t, cd = _retention_constants()
    attn, rq, rkt, ag, rg, mg = _attn_gates(
        (q1, q2, q3), x, lnw, wqk, w16, cos_t, sin_t, kdec, b_gate[0].reshape(1, 2 * D_MODEL))
    gnw = (ret_gn_w[0] * RET_VALUE_DIM ** 0.5).reshape(1, RET_V_WIDTH)
    ret_g = _retention(rq, rkt, rv, rg, gnw, eps_t, cd)

    return _merge_out(attn, ag, ret_g, mg, x,
                      attn_proj[0].astype(BF16), ret_proj[0].astype(BF16),
                      w_out[0].astype(BF16), lnf_w.reshape(1, D_MODEL))
```

```python
import jax
import jax.numpy as jnp
from jax import lax
from jax.experimental import pallas as pl
from jax.experimental.pallas import tpu as pltpu

D_MODEL = 1024
ATTN_GROUPS = ((128, 1), (512, 4), (2048, 16))
N_GROUPS = 3
ATTN_HEADS = 4
HEAD_DIM = 128
GROUP_WIDTH = ATTN_HEADS * HEAD_DIM
QKV_WIDTH = N_GROUPS * GROUP_WIDTH
BAND = 128
RET_HEADS = 8
RET_KEY_DIM = 64
RET_VALUE_DIM = 128
RET_QK_WIDTH = RET_HEADS * RET_KEY_DIM
RET_V_WIDTH = RET_HEADS * RET_VALUE_DIM
RET_CHUNK = 128
ROPE_BASE = 10000.0
COL_AG = 3 * QKV_WIDTH
COL_RQ = COL_AG + GROUP_WIDTH
COL_RV = COL_RQ + 2 * RET_QK_WIDTH
COL_RG = COL_RV + RET_V_WIDTH
COL_MG = COL_RG + RET_V_WIDTH
NORM_EPS = 1e-6
NEG_BIG = -1e30
LANES = 128
LOG2E = 1.4426950408889634
MIX_STRIDE = 4

PROJ_ROWS = 1024
ATTN_TILE = 2048
GATE_ROWS = ATTN_TILE // ATTN_HEADS
RET_BATCH = 8
OUT_ROWS = 1024
OUT_PART = 256
VMEM_LIMIT = 56 * 1024 * 1024
GATE_PIECES = (2, 2, 4, 2, 2)
RESIDENT = pl.Buffered(1)

BF16 = jnp.bfloat16
F32 = jnp.float32


def _rmsnorm_rows(x, w):
    return x * lax.rsqrt(jnp.mean(x * x, axis=-1, keepdims=True) + NORM_EPS) * w


def _dot(a, b):
    return jnp.dot(a, b, preferred_element_type=F32)


def _sigmoid(t):
    return 0.5 * jnp.tanh(0.5 * t) + 0.5


def _silu(t):
    return t * _sigmoid(t)


def _qkv_proj_kernel(x_ref, lnw_ref, *refs):
    w_refs = refs[:3 * N_GROUPS]
    wrv_ref = refs[3 * N_GROUPS]
    out_refs = refs[3 * N_GROUPS + 1:4 * N_GROUPS + 1]
    rv_ref, xn_scr = refs[4 * N_GROUPS + 1:]
    xn = _rmsnorm_rows(x_ref[...], lnw_ref[...])
    n_lane_tiles = D_MODEL // LANES
    for ct in range(n_lane_tiles):
        xn_scr[ct] = xn[:, ct * LANES:(ct + 1) * LANES]
    q_scale = HEAD_DIM ** -0.5 * LOG2E
    q_cols = slice(0, GROUP_WIDTH)
    k_cols = slice(GROUP_WIDTH, 2 * GROUP_WIDTH)
    v_cols = slice(2 * GROUP_WIDTH, 3 * GROUP_WIDTH)

    def strided_rows(r, rows, dil):
        return jnp.concatenate(
            [xn_scr[ct, pl.ds(r, rows, stride=dil), :] for ct in range(n_lane_tiles)], axis=1)

    for g, (_, dil) in enumerate(ATTN_GROUPS):
        wq_ref, wk_ref, wv_ref = w_refs[3 * g:3 * g + 3]
        o_ref = out_refs[g]
        if dil == 1:
            sub = BAND // MIX_STRIDE
            lhs_q = jnp.concatenate(
                [strided_rows(blk * BAND + r4, sub, MIX_STRIDE).astype(BF16)
                 for blk in range(PROJ_ROWS // BAND) for r4 in range(MIX_STRIDE)], axis=0)
            lhs = xn.astype(BF16)
            o_ref[0, :, q_cols] = (_dot(lhs_q, wq_ref[...]) * q_scale).astype(BF16)
            o_ref[0, :, k_cols] = _dot(lhs, wk_ref[...]).astype(BF16)
            o_ref[0, :, v_cols] = _dot(lhs, wv_ref[...]).astype(BF16)
            rv_ref[...] = _dot(lhs, wrv_ref[...]).astype(BF16)
        else:
            rows = PROJ_ROWS // dil
            lhs = jnp.concatenate(
                [strided_rows(r, rows, dil).astype(BF16) for r in range(dil)], axis=0)
            q = (_dot(lhs, wq_ref[...]) * q_scale).astype(BF16)
            k = _dot(lhs, wk_ref[...]).astype(BF16)
            v = _dot(lhs, wv_ref[...]).astype(BF16)
            for r in range(dil):
                rs = slice(r * rows, (r + 1) * rows)
                o_ref[r, :, q_cols] = q[rs]
                o_ref[r, :, k_cols] = k[rs]
                o_ref[r, :, v_cols] = v[rs]


def _qkv_proj(x, lnw, w16):
    B, S, _ = x.shape
    nt = S // PROJ_ROWS
    w_specs = [pl.BlockSpec((D_MODEL, GROUP_WIDTH), lambda b, t, cb=part * N_GROUPS + g: (0, cb),
                            pipeline_mode=RESIDENT)
               for g in range(N_GROUPS) for part in range(3)]
    w_specs.append(pl.BlockSpec((D_MODEL, RET_V_WIDTH), lambda b, t: (0, COL_RV // RET_V_WIDTH),
                                pipeline_mode=RESIDENT))
    return pl.pallas_call(
        _qkv_proj_kernel,
        grid=(B, nt),
        in_specs=[
            pl.BlockSpec((None, PROJ_ROWS, D_MODEL), lambda b, t: (b, t, 0)),
            pl.BlockSpec((1, D_MODEL), lambda b, t: (0, 0)),
            *w_specs,
        ],
        out_specs=[
            *[pl.BlockSpec((None, dil, PROJ_ROWS // dil, QKV_WIDTH), lambda b, t: (b, 0, t, 0))
              for _, dil in ATTN_GROUPS],
            pl.BlockSpec((None, PROJ_ROWS, RET_V_WIDTH), lambda b, t: (b, t, 0)),
        ],
        out_shape=[
            *[jax.ShapeDtypeStruct((B, dil, S // dil, QKV_WIDTH), BF16) for _, dil in ATTN_GROUPS],
            jax.ShapeDtypeStruct((B, S, RET_V_WIDTH), BF16),
        ],
        scratch_shapes=[pltpu.VMEM((D_MODEL // LANES, PROJ_ROWS, LANES), F32)],
        compiler_params=pltpu.CompilerParams(
            dimension_semantics=("arbitrary", "arbitrary"), vmem_limit_bytes=VMEM_LIMIT),
        name="qkv_proj",
    )(x, lnw, *([w16] * (3 * N_GROUPS + 1)))


def _swap_pair_halves(t):
    n = t.shape[-1]
    lane = lax.broadcasted_iota(jnp.int32, t.shape, 1)
    up = pltpu.roll(t, n - RET_KEY_DIM // 2, axis=1)
    down = pltpu.roll(t, RET_KEY_DIM // 2, axis=1)
    return jnp.where(lane % RET_KEY_DIM < RET_KEY_DIM // 2, up, down)


def _gate_proj_items(x_ref, lnw_ref, wqk_ref, wg_ref, wrg_ref, wmg_ref,
                     cos_ref, sin_ref, kdec_ref, bg_ref,
                     rq_ref, rkt_ref, ag_ref, rg_ref, mg_ref):
    cache = {}

    def lhs():
        if "xb" not in cache:
            cache["xb"] = _rmsnorm_rows(x_ref[...], lnw_ref[...]).astype(BF16)
        return cache["xb"]

    def rope(t):
        pairs = t.shape[1] // LANES
        cos = jnp.concatenate([cos_ref[...]] * pairs, axis=1)
        sin = jnp.concatenate([sin_ref[...]] * pairs, axis=1)
        return t * cos + _swap_pair_halves(t) * sin

    def retention_q(c0, width):
        cols = slice(c0, c0 + width)
        t = _dot(lhs(), wqk_ref[:, cols])
        yield
        rq_ref[:, cols] = rope(t).astype(BF16)

    def retention_k(c0, width):
        cols = slice(c0, c0 + width)
        t = _dot(lhs(), wqk_ref[:, RET_QK_WIDTH + c0:RET_QK_WIDTH + c0 + width])
        yield
        k = rope(t) * kdec_ref[:, cols]
        for c in range(GATE_ROWS // RET_CHUNK):
            rkt_ref[c, cols, :] = k[c * RET_CHUNK:(c + 1) * RET_CHUNK, :].T.astype(BF16)

    def merge_gate(c0, width):
        cols = slice(c0, c0 + width)
        t = _dot(lhs(), wmg_ref[:, cols])
        yield
        mg_ref[:, cols] = _sigmoid(t + bg_ref[:, cols]).astype(BF16)

    def retention_gate(c0, width):
        cols = slice(c0, c0 + width)
        t = _dot(lhs(), wrg_ref[:, cols])
        yield
        rg_ref[:, cols] = _silu(t).astype(BF16)

    def attention_gate(c0, width):
        cols = slice(c0, c0 + width)
        t = _dot(lhs(), wg_ref[:, cols])
        yield
        ag_ref[:, cols] = _silu(t).astype(BF16)

    items = []
    for fn, total, pieces in ((retention_q, RET_QK_WIDTH, GATE_PIECES[0]),
                              (retention_k, RET_QK_WIDTH, GATE_PIECES[1]),
                              (merge_gate, 2 * D_MODEL, GATE_PIECES[2]),
                              (retention_gate, RET_V_WIDTH, GATE_PIECES[3]),
                              (attention_gate, GROUP_WIDTH, GATE_PIECES[4])):
        width = total // pieces
        items += [lambda fn=fn, c0=i * width, width=width: fn(c0, width) for i in range(pieces)]
    return items


def _attention_items(in_refs, bias_ref, out_ref, acc_scr, max_scr, den_scr, nat_scr, first):
    ones = jnp.ones((2 * BAND, HEAD_DIM), BF16)
    quarter = ATTN_TILE // MIX_STRIDE

    def unit(g, r, n):
        dil = ATTN_GROUPS[g][1]
        q_ref, kc_ref, kp_ref, vc_ref, vp_ref = in_refs[5 * g:5 * g + 5]
        bias_base = 2 if dil == 1 else 0
        q = q_ref[r, n * BAND:(n + 1) * BAND, :]
        if n == 0:
            k = jnp.concatenate([kp_ref[r], kc_ref[r, 0:BAND, :]], axis=0)
            v = jnp.concatenate([vp_ref[r], vc_ref[r, 0:BAND, :]], axis=0)
            bias = bias_ref[bias_base + first]
        else:
            k = kc_ref[r, (n - 1) * BAND:(n + 1) * BAND, :]
            v = vc_ref[r, (n - 1) * BAND:(n + 1) * BAND, :]
            bias = bias_ref[bias_base]
        s = lax.dot_general(q, k, (((1,), (1,)), ((), ())), preferred_element_type=F32) + bias
        yield
        m = jnp.max(s, axis=-1, keepdims=True)
        p = jnp.exp2(s - m).astype(BF16)
        yield
        res = _dot(p, jnp.concatenate([v, ones], axis=1))
        acc, den = res[:, :HEAD_DIM], res[:, HEAD_DIM:]
        mb = jnp.broadcast_to(m, (BAND, HEAD_DIM))
        if dil == 1:
            sub = BAND // MIX_STRIDE
            for r4 in range(MIX_STRIDE):
                dst = pl.ds(r4 * quarter + n * sub, sub)
                src = slice(r4 * sub, (r4 + 1) * sub)
                acc_scr[g, dst, :] = acc[src]
                max_scr[g, dst, :] = mb[src]
                den_scr[g, dst, :] = den[src]
        else:
            if dil == MIX_STRIDE:
                dst = pl.ds(r * quarter + n * BAND, BAND)
            else:
                sub_stride = dil // MIX_STRIDE
                dst = pl.ds((r % MIX_STRIDE) * quarter + r // MIX_STRIDE
                            + n * BAND * sub_stride, BAND, stride=sub_stride)
            acc_scr[g, dst, :] = acc
            max_scr[g, dst, :] = mb
            den_scr[g, dst, :] = den

    def mix(r4, ch):
        rows = pl.ds(r4 * quarter + ch * BAND, BAND)
        m0, m1, m2 = max_scr[0, rows, :], max_scr[1, rows, :], max_scr[2, rows, :]
        mx = jnp.maximum(jnp.maximum(m0, m1), m2)
        w0 = jnp.exp2(m0 - mx)
        w1 = jnp.exp2(m1 - mx)
        w2 = jnp.exp2(m2 - mx)
        num = w0 * acc_scr[0, rows, :] + w1 * acc_scr[1, rows, :] + w2 * acc_scr[2, rows, :]
        den = w0 * den_scr[0, rows, :] + w1 * den_scr[1, rows, :] + w2 * den_scr[2, rows, :]
        nat_scr[pl.ds(r4 + ch * BAND * MIX_STRIDE, BAND, stride=MIX_STRIDE), :] = num / den

    def write_out():
        out_ref[...] = nat_scr[...].astype(BF16)

    def mix_quarter(r4):
        for ch in range(quarter // BAND):
            mix(r4, ch)

    units = [lambda n=n: unit(0, 0, n) for n in range(ATTN_TILE // BAND)]
    quarter_done = []
    for r4 in range(MIX_STRIDE):
        units += [lambda g=g, r=r, n=n: unit(g, r, n)
                  for g in range(1, N_GROUPS)
                  for r in range(r4, ATTN_GROUPS[g][1], MIX_STRIDE)
                  for n in range(ATTN_TILE // ATTN_GROUPS[g][1] // BAND)]
        quarter_done.append(len(units))
    return units, quarter_done, mix_quarter, write_out


def _attn_gates_kernel(*refs):
    n_attn = 5 * N_GROUPS
    in_refs = refs[:n_attn]
    bias_ref = refs[n_attn]
    proj_in = refs[n_attn + 1:n_attn + 11]
    attn_out = refs[n_attn + 11]
    proj_out = refs[n_attn + 12:n_attn + 17]
    acc_scr, max_scr, den_scr, nat_scr = refs[n_attn + 17:]
    first = (pl.program_id(1) == 0).astype(jnp.int32)
    unit_makers, quarter_done, mix_quarter, write_out = _attention_items(
        in_refs, bias_ref, attn_out, acc_scr, max_scr, den_scr, nat_scr, first)
    proj = _gate_proj_items(*proj_in, *proj_out)
    per_round = len(unit_makers) // len(proj)
    assert per_round * len(proj) == len(unit_makers)
    rounds = [unit_makers[i * per_round:(i + 1) * per_round] for i in range(len(proj))]

    def advance(gen):
        next(gen, None)

    prev_units = []
    mixed = 0
    for r in range(len(rounds) + 1):
        units = [make() for make in rounds[r]] if r < len(rounds) else []
        for j in range(max(len(units), len(prev_units))):
            if j < len(units):
                advance(units[j])
            if j < len(prev_units):
                advance(prev_units[j])
        while mixed < len(quarter_done) and quarter_done[mixed] <= r * per_round:
            mix_quarter(mixed)
            mixed += 1
        if r < len(proj):
            for _ in proj[r]():
                pass
        for u in units:
            advance(u)
        prev_units = units
    write_out()


def _attn_bias_tables():
    rho = jnp.arange(BAND)[:, None]
    c = jnp.arange(2 * BAND)[None, :]
    sub = BAND // MIX_STRIDE
    tables = []
    for a in (rho, MIX_STRIDE * (rho % sub) + rho // sub):
        ok = (c >= a) & (c <= a + BAND)
        tables += [ok, ok & (c >= BAND)]
    return jnp.where(jnp.stack(tables), 0.0, NEG_BIG).astype(F32)


def _attn_gates(qkv_groups, x, lnw, wqk, w16, cos_t, sin_t, kdec, bg):
    B, S, _ = x.shape
    nt = S // ATTN_TILE
    cpt = GATE_ROWS // RET_CHUNK
    in_specs = []
    args = []
    for (_, dil), qkv in zip(ATTN_GROUPS, qkv_groups):
        rows = ATTN_TILE // dil
        ratio = rows // BAND

        def cur(col0, rows=rows, dil=dil):
            return pl.BlockSpec((None, dil, rows, HEAD_DIM),
                                lambda b, t, h: (b, 0, t, col0 + h))

        def prev(col0, ratio=ratio, dil=dil):
            return pl.BlockSpec((None, dil, BAND, HEAD_DIM),
                                lambda b, t, h: (b, 0, jnp.maximum(t * ratio - 1, 0), col0 + h))

        in_specs += [cur(0), cur(ATTN_HEADS), prev(ATTN_HEADS),
                     cur(2 * ATTN_HEADS), prev(2 * ATTN_HEADS)]
        args += [qkv] * 5
    bias = _attn_bias_tables()
    in_specs.append(pl.BlockSpec(bias.shape, lambda b, t, h: (0, 0, 0), pipeline_mode=RESIDENT))

    def wspec(n, col0=0):
        return pl.BlockSpec((D_MODEL, n), lambda b, t, h: (0, col0 // n), pipeline_mode=RESIDENT)

    def row_spec(n):
        return pl.BlockSpec((None, GATE_ROWS, n), lambda b, t, h: (b, t * ATTN_HEADS + h, 0))

    tspec = pl.BlockSpec((GATE_ROWS, LANES), lambda b, t, h: (t * ATTN_HEADS + h, 0))
    in_specs += [
        row_spec(D_MODEL),
        pl.BlockSpec((1, D_MODEL), lambda b, t, h: (0, 0)),
        wspec(2 * RET_QK_WIDTH),
        wspec(GROUP_WIDTH, COL_AG), wspec(RET_V_WIDTH, COL_RG), wspec(2 * D_MODEL, COL_MG),
        tspec, tspec,
        pl.BlockSpec((GATE_ROWS, RET_QK_WIDTH), lambda b, t, h: (0, 0), pipeline_mode=RESIDENT),
        pl.BlockSpec((1, 2 * D_MODEL), lambda b, t, h: (0, 0)),
    ]
    out_widths = (GROUP_WIDTH, RET_V_WIDTH, 2 * D_MODEL)
    per_group = pltpu.VMEM((N_GROUPS, ATTN_TILE, HEAD_DIM), F32)
    return pl.pallas_call(
        _attn_gates_kernel,
        grid=(B, nt, ATTN_HEADS),
        in_specs=in_specs,
        out_specs=[
            pl.BlockSpec((None, ATTN_TILE, HEAD_DIM), lambda b, t, h: (b, t, h)),
            row_spec(RET_QK_WIDTH),
            pl.BlockSpec((None, cpt, RET_QK_WIDTH, RET_CHUNK),
                         lambda b, t, h: (b, t * ATTN_HEADS + h, 0, 0)),
            *[row_spec(n) for n in out_widths],
        ],
        out_shape=[
            jax.ShapeDtypeStruct((B, S, GROUP_WIDTH), BF16),
            jax.ShapeDtypeStruct((B, S, RET_QK_WIDTH), BF16),
            jax.ShapeDtypeStruct((B, S // RET_CHUNK, RET_QK_WIDTH, RET_CHUNK), BF16),
            *[jax.ShapeDtypeStruct((B, S, n), BF16) for n in out_widths],
        ],
        scratch_shapes=[per_group, per_group, per_group,
                        pltpu.VMEM((ATTN_TILE, HEAD_DIM), F32)],
        compiler_params=pltpu.CompilerParams(
            dimension_semantics=("arbitrary", "arbitrary", "arbitrary"),
            vmem_limit_bytes=VMEM_LIMIT),
        name="attn_gates",
    )(*args, bias, x, lnw, wqk, w16, w16, w16, cos_t, sin_t, kdec, bg)


def _retention_kernel(q_ref, kt_ref, v_ref, gate_ref, gnw_ref, eps_ref, cd_ref,
                      out_ref, state, p_scr):
    @pl.when(pl.program_id(1) == 0)
    def _():
        state[...] = jnp.zeros_like(state)

    dk, dv, C = RET_KEY_DIM, RET_VALUE_DIM, RET_CHUNK
    row_k = lax.broadcasted_iota(jnp.int32, (2 * dk, C), 0)
    row = lax.broadcasted_iota(jnp.int32, (2 * dk, 2 * dv), 0)
    col = lax.broadcasted_iota(jnp.int32, (C, 2 * dv), 1)
    diag_blk = (row < dk) == (col < dv)
    causal = (col % C) <= lax.broadcasted_iota(jnp.int32, (C, 2 * C), 0)
    units = [(bi, j) for bi in range(RET_BATCH) for j in range(RET_HEADS // 2)]

    def q_pair(bi, j):
        return q_ref[bi, :, j * 2 * dk:(j + 1) * 2 * dk]

    def kt_pair(bi, j):
        return kt_ref[bi, j * 2 * dk:(j + 1) * 2 * dk, :]

    def v_pair(bi, j):
        return v_ref[bi, :, j * 2 * dv:(j + 1) * 2 * dv]

    for bi, j in units:
        kt = kt_pair(bi, j)
        zero = jnp.zeros_like(kt)
        kt_blk = jnp.concatenate([jnp.where(row_k < dk, kt, zero),
                                  jnp.where(row_k >= dk, kt, zero)], axis=1)
        s = _dot(q_pair(bi, j), kt_blk).astype(BF16)
        p_scr[bi, j] = jnp.where(causal, s, jnp.zeros_like(s))

    for bi, j in units:
        v = v_pair(bi, j)
        zero = jnp.zeros_like(v)
        v_blk = jnp.concatenate([jnp.where(col < dv, v, zero),
                                 jnp.where(col >= dv, v, zero)], axis=0)
        u = _dot(jnp.concatenate([p_scr[bi, j], q_pair(bi, j)], axis=1),
                 jnp.concatenate([v_blk, state[bi, j].astype(BF16)], axis=0))
        for hh in range(2):
            h = 2 * j + hh
            hs = slice(h * dv, (h + 1) * dv)
            uh = u[:, hh * dv:(hh + 1) * dv]
            uh = uh * lax.rsqrt(jnp.sum(uh * uh, axis=-1, keepdims=True) + eps_ref[h])
            out_ref[bi, :, hs] = (uh * gnw_ref[:, hs]).astype(BF16) * gate_ref[bi, :, hs]

    for bi, j in units:
        kv = _dot(kt_pair(bi, j), v_pair(bi, j))
        state[bi, j] = cd_ref[j] * (state[bi, j] + jnp.where(diag_blk, kv, 0.0))


def _retention(rq, rkt, rv, gate, gnw, eps_t, cd):
    B, S, _ = rv.shape
    nc = S // RET_CHUNK
    npair = RET_HEADS // 2
    pair_shape = (RET_BATCH, npair, 2 * RET_KEY_DIM, 2 * RET_VALUE_DIM)

    def const(shape):
        return pl.BlockSpec(shape, lambda b, n: (0,) * len(shape))

    def rows(width):
        return pl.BlockSpec((RET_BATCH, RET_CHUNK, width), lambda b, n: (b, n, 0))

    return pl.pallas_call(
        _retention_kernel,
        grid=(B // RET_BATCH, nc),
        in_specs=[
            rows(RET_QK_WIDTH),
            pl.BlockSpec((RET_BATCH, None, RET_QK_WIDTH, RET_CHUNK), lambda b, n: (b, n, 0, 0)),
            rows(RET_V_WIDTH),
            rows(RET_V_WIDTH),
            const((1, RET_V_WIDTH)),
            const((RET_HEADS, RET_CHUNK, RET_VALUE_DIM)),
            const(pair_shape[1:]),
        ],
        out_specs=rows(RET_V_WIDTH),
        out_shape=jax.ShapeDtypeStruct((B, S, RET_V_WIDTH), BF16),
        scratch_shapes=[pltpu.VMEM(pair_shape, F32),
                        pltpu.VMEM((RET_BATCH, npair, RET_CHUNK, 2 * RET_CHUNK), BF16)],
        compiler_params=pltpu.CompilerParams(
            dimension_semantics=("arbitrary", "arbitrary"), vmem_limit_bytes=VMEM_LIMIT),
        name="retention",
    )(rq, rkt, rv, gate, gnw, eps_t, cd)


def _merge_out_kernel(a_ref, ag_ref, r_ref, mg_ref, x_ref, wa_ref, wr_ref, wo_ref, lnf_ref,
                      out_ref):
    def part(rows):
        y_attn = _dot(a_ref[rows, :] * ag_ref[rows, :], wa_ref[...])
        y_ret = _dot(r_ref[rows, :], wr_ref[...])
        yield
        merged = (mg_ref[rows, :D_MODEL].astype(F32) * y_attn
                  + mg_ref[rows, D_MODEL:].astype(F32) * y_ret)
        h = x_ref[rows, :] + _dot(merged.astype(BF16), wo_ref[...])
        yield
        out_ref[rows, :] = _rmsnorm_rows(h, lnf_ref[...])

    n_parts = OUT_ROWS // OUT_PART
    parts = [part(slice(i * OUT_PART, (i + 1) * OUT_PART)) for i in range(n_parts)]
    for step in range(n_parts + 2):
        for lag in range(3):
            i = step - lag
            if 0 <= i < n_parts:
                next(parts[i], None)


def _merge_out(attn, ag, ret_g, mg, x, wa, wr, wo, lnf):
    B, S, _ = x.shape
    nt = S // OUT_ROWS

    def rows(n):
        return pl.BlockSpec((None, OUT_ROWS, n), lambda b, t: (b, t, 0))

    def const(shape):
        return pl.BlockSpec(shape, lambda b, t: (0,) * len(shape), pipeline_mode=RESIDENT)

    return pl.pallas_call(
        _merge_out_kernel,
        grid=(B, nt),
        in_specs=[rows(GROUP_WIDTH), rows(GROUP_WIDTH), rows(RET_V_WIDTH), rows(2 * D_MODEL),
                  rows(D_MODEL),
                  const((GROUP_WIDTH, D_MODEL)), const((RET_V_WIDTH, D_MODEL)),
                  const((D_MODEL, D_MODEL)), const((1, D_MODEL))],
        out_specs=rows(D_MODEL),
        out_shape=jax.ShapeDtypeStruct((B, S, D_MODEL), F32),
        compiler_params=pltpu.CompilerParams(
            dimension_semantics=("arbitrary", "arbitrary"), vmem_limit_bytes=VMEM_LIMIT),
        name="merge_out",
    )(attn, ag, ret_g, mg, x, wa, wr, wo, lnf)


def _rope_tables(S):
    half = RET_KEY_DIM // 2
    pos = jnp.arange(S, dtype=F32)
    inv_freq = ROPE_BASE ** (-jnp.linspace(0.0, 1.0, half, dtype=F32))
    ang = pos[:, None] * inv_freq[None, :]
    cos = jnp.cos(ang)
    sin = jnp.sin(ang)
    return (jnp.concatenate([cos, cos, cos, cos], axis=1),
            jnp.concatenate([-sin, sin, -sin, sin], axis=1))


def _retention_constants():
    H, C, dk, dv = RET_HEADS, RET_CHUNK, RET_KEY_DIM, RET_VALUE_DIM
    log_gamma = jnp.log(1.0 - 2.0 ** (-5.0 - jnp.arange(H, dtype=F32)))
    idx = jnp.arange(C, dtype=F32)
    inv_decay = jnp.exp(-(idx + 1.0)[None, :] * log_gamma[:, None])
    chunk_decay = jnp.exp(C * log_gamma)
    kdec = jnp.repeat(inv_decay.T, dk, axis=1) * dk ** -0.5
    kdec = jnp.tile(kdec, (GATE_ROWS // C, 1))
    eps_t = jnp.broadcast_to((dv * NORM_EPS * inv_decay * inv_decay)[:, :, None], (H, C, dv))
    cd = jnp.broadcast_to(chunk_decay[:, None, None], (H, dk, 2 * dv)).reshape(H // 2, 2 * dk, 2 * dv)
    return kdec, eps_t, cd


def kernel(x, ln1_w, w_in, b_gate, attn_proj, ret_proj, ret_gn_w, w_out, lnf_w):
    B, S, _ = x.shape
    assert w_in.shape[0] == 1, "single layer"
    w16 = w_in[0].astype(BF16)

    e = jnp.arange(RET_KEY_DIM // 2)
    head_perm = jnp.concatenate([2 * e, 2 * e + 1])
    perm = (jnp.arange(2 * RET_HEADS)[:, None] * RET_KEY_DIM + head_perm[None, :]).reshape(-1)
    wqk = w16[:, COL_RQ:COL_RQ + 2 * RET_QK_WIDTH][:, perm]

    lnw = ln1_w[0].reshape(1, D_MODEL)
    q1, q2, q3, rv = _qkv_proj(x, lnw, w16)
    cos_t, sin_t = _rope_tables(S)
    kdec, eps_t, cd = _retention_constants()
    attn, rq, rkt, ag, rg, mg = _attn_gates(
        (q1, q2, q3), x, lnw, wqk, w16, cos_t, sin_t, kdec, b_gate[0].reshape(1, 2 * D_MODEL))
    gnw = (ret_gn_w[0] * RET_VALUE_DIM ** 0.5).reshape(1, RET_V_WIDTH)
    ret_g = _retention(rq, rkt, rv, rg, gnw, eps_t, cd)

    return _merge_out(attn, ag, ret_g, mg, x,
                      attn_proj[0].astype(BF16), ret_proj[0].astype(BF16),
                      w_out[0].astype(BF16), lnf_w.reshape(1, D_MODEL))
```

```python
import jax
import jax.numpy as jnp
from jax import lax
from jax.experimental import pallas as pl
from jax.experimental.pallas import tpu as pltpu

D_MODEL = 1024
ATTN_GROUPS = ((128, 1), (512, 4), (2048, 16))
N_GROUPS = 3
ATTN_HEADS = 4
HEAD_DIM = 128
GROUP_WIDTH = ATTN_HEADS * HEAD_DIM
QKV_WIDTH = N_GROUPS * GROUP_WIDTH
BAND = 128
RET_HEADS = 8
RET_KEY_DIM = 64
RET_VALUE_DIM = 128
RET_QK_WIDTH = RET_HEADS * RET_KEY_DIM
RET_V_WIDTH = RET_HEADS * RET_VALUE_DIM
RET_CHUNK = 128
ROPE_BASE = 10000.0
COL_AG = 3 * QKV_WIDTH
COL_RQ = COL_AG + GROUP_WIDTH
COL_RV = COL_RQ + 2 * RET_QK_WIDTH
COL_RG = COL_RV + RET_V_WIDTH
COL_MG = COL_RG + RET_V_WIDTH
NORM_EPS = 1e-6
NEG_BIG = -1e30
LANES = 128
LOG2E = 1.4426950408889634
MIX_STRIDE = 4

PROJ_ROWS = 1024
ATTN_TILE = 2048
GATE_ROWS = ATTN_TILE // ATTN_HEADS
RET_BATCH = 8
OUT_ROWS = 1024
OUT_PART = 256
VMEM_LIMIT = 56 * 1024 * 1024
GATE_PIECES = (2, 2, 8, 2, 2)
RESIDENT = pl.Buffered(1)

BF16 = jnp.bfloat16
F32 = jnp.float32


def _rmsnorm_rows(x, w):
    return x * lax.rsqrt(jnp.mean(x * x, axis=-1, keepdims=True) + NORM_EPS) * w


def _dot(a, b):
    return jnp.dot(a, b, preferred_element_type=F32)


def _sigmoid(t):
    return 0.5 * jnp.tanh(0.5 * t) + 0.5


def _silu(t):
    return t * _sigmoid(t)


def _qkv_proj_kernel(x_ref, lnw_ref, *refs):
    w_refs = refs[:3 * N_GROUPS]
    wrv_ref = refs[3 * N_GROUPS]
    out_refs = refs[3 * N_GROUPS + 1:4 * N_GROUPS + 1]
    rv_ref, xn_scr = refs[4 * N_GROUPS + 1:]
    xn = _rmsnorm_rows(x_ref[...], lnw_ref[...])
    n_lane_tiles = D_MODEL // LANES
    for ct in range(n_lane_tiles):
        xn_scr[ct] = xn[:, ct * LANES:(ct + 1) * LANES]
    q_scale = HEAD_DIM ** -0.5 * LOG2E
    q_cols = slice(0, GROUP_WIDTH)
    k_cols = slice(GROUP_WIDTH, 2 * GROUP_WIDTH)
    v_cols = slice(2 * GROUP_WIDTH, 3 * GROUP_WIDTH)

    def strided_rows(r, rows, dil):
        return jnp.concatenate(
            [xn_scr[ct, pl.ds(r, rows, stride=dil), :] for ct in range(n_lane_tiles)], axis=1)

    for g, (_, dil) in enumerate(ATTN_GROUPS):
        wq_ref, wk_ref, wv_ref = w_refs[3 * g:3 * g + 3]
        o_ref = out_refs[g]
        if dil == 1:
            sub = BAND // MIX_STRIDE
            lhs_q = jnp.concatenate(
                [strided_rows(blk * BAND + r4, sub, MIX_STRIDE).astype(BF16)
                 for blk in range(PROJ_ROWS // BAND) for r4 in range(MIX_STRIDE)], axis=0)
            lhs = xn.astype(BF16)
            o_ref[0, :, q_cols] = (_dot(lhs_q, wq_ref[...]) * q_scale).astype(BF16)
            o_ref[0, :, k_cols] = _dot(lhs, wk_ref[...]).astype(BF16)
            o_ref[0, :, v_cols] = _dot(lhs, wv_ref[...]).astype(BF16)
            rv_ref[...] = _dot(lhs, wrv_ref[...]).astype(BF16)
        else:
            rows = PROJ_ROWS // dil
            lhs = jnp.concatenate(
                [strided_rows(r, rows, dil).astype(BF16) for r in range(dil)], axis=0)
            q = (_dot(lhs, wq_ref[...]) * q_scale).astype(BF16)
            k = _dot(lhs, wk_ref[...]).astype(BF16)
            v = _dot(lhs, wv_ref[...]).astype(BF16)
            for r in range(dil):
                rs = slice(r * rows, (r + 1) * rows)
                o_ref[r, :, q_cols] = q[rs]
                o_ref[r, :, k_cols] = k[rs]
                o_ref[r, :, v_cols] = v[rs]


def _qkv_proj(x, lnw, w16):
    B, S, _ = x.shape
    nt = S // PROJ_ROWS
    w_specs = [pl.BlockSpec((D_MODEL, GROUP_WIDTH), lambda b, t, cb=part * N_GROUPS + g: (0, cb),
                            pipeline_mode=RESIDENT)
               for g in range(N_GROUPS) for part in range(3)]
    w_specs.append(pl.BlockSpec((D_MODEL, RET_V_WIDTH), lambda b, t: (0, COL_RV // RET_V_WIDTH),
                                pipeline_mode=RESIDENT))
    return pl.pallas_call(
        _qkv_proj_kernel,
        grid=(B, nt),
        in_specs=[
            pl.BlockSpec((None, PROJ_ROWS, D_MODEL), lambda b, t: (b, t, 0)),
            pl.BlockSpec((1, D_MODEL), lambda b, t: (0, 0)),
            *w_specs,
        ],
        out_specs=[
            *[pl.BlockSpec((None, dil, PROJ_ROWS // dil, QKV_WIDTH), lambda b, t: (b, 0, t, 0))
              for _, dil in ATTN_GROUPS],
            pl.BlockSpec((None, PROJ_ROWS, RET_V_WIDTH), lambda b, t: (b, t, 0)),
        ],
        out_shape=[
            *[jax.ShapeDtypeStruct((B, dil, S // dil, QKV_WIDTH), BF16) for _, dil in ATTN_GROUPS],
            jax.ShapeDtypeStruct((B, S, RET_V_WIDTH), BF16),
        ],
        scratch_shapes=[pltpu.VMEM((D_MODEL // LANES, PROJ_ROWS, LANES), F32)],
        compiler_params=pltpu.CompilerParams(
            dimension_semantics=("arbitrary", "arbitrary"), vmem_limit_bytes=VMEM_LIMIT),
        name="qkv_proj",
    )(x, lnw, *([w16] * (3 * N_GROUPS + 1)))


def _swap_lane_pairs(t):
    n = t.shape[-1]
    lane = lax.broadcasted_iota(jnp.int32, t.shape, 1)
    up = pltpu.roll(t, n - 1, axis=1)
    down = pltpu.roll(t, 1, axis=1)
    return jnp.where(lane % 2 == 0, up, down)


def _gate_proj_items(x_ref, lnw_ref, wqk_ref, wg_ref, wrg_ref, wmg_ref,
                     cos_ref, sin_ref, kdec_ref, bg_ref,
                     rq_ref, rkt_ref, ag_ref, rg_ref, mg_ref):
    cache = {}

    def lhs():
        if "xb" not in cache:
            cache["xb"] = _rmsnorm_rows(x_ref[...], lnw_ref[...]).astype(BF16)
        return cache["xb"]

    def rope(t):
        pairs = t.shape[1] // LANES
        cos = jnp.concatenate([cos_ref[...]] * pairs, axis=1)
        sin = jnp.concatenate([sin_ref[...]] * pairs, axis=1)
        return t * cos + _swap_lane_pairs(t) * sin

    def retention_q(c0, width):
        cols = slice(c0, c0 + width)
        t = _dot(lhs(), wqk_ref[:, cols])
        yield
        rq_ref[:, cols] = rope(t).astype(BF16)

    def retention_k(c0, width):
        cols = slice(c0, c0 + width)
        t = _dot(lhs(), wqk_ref[:, RET_QK_WIDTH + c0:RET_QK_WIDTH + c0 + width])
        yield
        k = rope(t) * kdec_ref[:, cols]
        for c in range(GATE_ROWS // RET_CHUNK):
            rkt_ref[c, cols, :] = k[c * RET_CHUNK:(c + 1) * RET_CHUNK, :].T.astype(BF16)

    def merge_gate(c0, width):
        cols = slice(c0, c0 + width)
        t = _dot(lhs(), wmg_ref[:, cols])
        yield
        mg_ref[:, cols] = _sigmoid(t + bg_ref[:, cols]).astype(BF16)

    def retention_gate(c0, width):
        cols = slice(c0, c0 + width)
        t = _dot(lhs(), wrg_ref[:, cols])
        yield
        rg_ref[:, cols] = _silu(t).astype(BF16)

    def attention_gate(c0, width):
        cols = slice(c0, c0 + width)
        t = _dot(lhs(), wg_ref[:, cols])
        yield
        ag_ref[:, cols] = _silu(t).astype(BF16)

    items = []
    for fn, total, pieces in ((retention_q, RET_QK_WIDTH, GATE_PIECES[0]),
                              (retention_k, RET_QK_WIDTH, GATE_PIECES[1]),
                              (merge_gate, 2 * D_MODEL, GATE_PIECES[2]),
                              (retention_gate, RET_V_WIDTH, GATE_PIECES[3]),
                              (attention_gate, GROUP_WIDTH, GATE_PIECES[4])):
        width = total // pieces
        items += [lambda fn=fn, c0=i * width, width=width: fn(c0, width) for i in range(pieces)]
    return items


def _attention_items(in_refs, bias_ref, out_ref, acc_scr, max_scr, den_scr, nat_scr, first):
    ones = jnp.ones((2 * BAND, HEAD_DIM), BF16)
    quarter = ATTN_TILE // MIX_STRIDE

    def unit(g, r, n):
        dil = ATTN_GROUPS[g][1]
        q_ref, kc_ref, kp_ref, vc_ref, vp_ref = in_refs[5 * g:5 * g + 5]
        bias_base = 2 if dil == 1 else 0
        q = q_ref[r, n * BAND:(n + 1) * BAND, :]
        if n == 0:
            k = jnp.concatenate([kp_ref[r], kc_ref[r, 0:BAND, :]], axis=0)
            v = jnp.concatenate([vp_ref[r], vc_ref[r, 0:BAND, :]], axis=0)
            bias = bias_ref[bias_base + first]
        else:
            k = kc_ref[r, (n - 1) * BAND:(n + 1) * BAND, :]
            v = vc_ref[r, (n - 1) * BAND:(n + 1) * BAND, :]
            bias = bias_ref[bias_base]
        s = lax.dot_general(q, k, (((1,), (1,)), ((), ())), preferred_element_type=F32) + bias
        yield
        m = jnp.max(s, axis=-1, keepdims=True)
        p = jnp.exp2(s - m).astype(BF16)
        yield
        res = _dot(p, jnp.concatenate([v, ones], axis=1))
        acc, den = res[:, :HEAD_DIM], res[:, HEAD_DIM:]
        mb = jnp.broadcast_to(m, (BAND, HEAD_DIM))
        if dil == 1:
            sub = BAND // MIX_STRIDE
            for r4 in range(MIX_STRIDE):
                dst = pl.ds(r4 * quarter + n * sub, sub)
                src = slice(r4 * sub, (r4 + 1) * sub)
                acc_scr[g, dst, :] = acc[src]
                max_scr[g, dst, :] = mb[src]
                den_scr[g, dst, :] = den[src]
        else:
            if dil == MIX_STRIDE:
                dst = pl.ds(r * quarter + n * BAND, BAND)
            else:
                sub_stride = dil // MIX_STRIDE
                dst = pl.ds((r % MIX_STRIDE) * quarter + r // MIX_STRIDE
                            + n * BAND * sub_stride, BAND, stride=sub_stride)
            acc_scr[g, dst, :] = acc
            max_scr[g, dst, :] = mb
            den_scr[g, dst, :] = den

    def mix(r4, ch):
        rows = pl.ds(r4 * quarter + ch * BAND, BAND)
        m0, m1, m2 = max_scr[0, rows, :], max_scr[1, rows, :], max_scr[2, rows, :]
        mx = jnp.maximum(jnp.maximum(m0, m1), m2)
        w0 = jnp.exp2(m0 - mx)
        w1 = jnp.exp2(m1 - mx)
        w2 = jnp.exp2(m2 - mx)
        num = w0 * acc_scr[0, rows, :] + w1 * acc_scr[1, rows, :] + w2 * acc_scr[2, rows, :]
        den = w0 * den_scr[0, rows, :] + w1 * den_scr[1, rows, :] + w2 * den_scr[2, rows, :]
        nat_scr[pl.ds(r4 + ch * BAND * MIX_STRIDE, BAND, stride=MIX_STRIDE), :] = num / den

    def write_out():
        out_ref[...] = nat_scr[...].astype(BF16)

    def mix_quarter(r4):
        for ch in range(quarter // BAND):
            mix(r4, ch)

    units = [lambda n=n: unit(0, 0, n) for n in range(ATTN_TILE // BAND)]
    quarter_done = []
    for r4 in range(MIX_STRIDE):
        units += [lambda g=g, r=r, n=n: unit(g, r, n)
                  for g in range(1, N_GROUPS)
                  for r in range(r4, ATTN_GROUPS[g][1], MIX_STRIDE)
                  for n in range(ATTN_TILE // ATTN_GROUPS[g][1] // BAND)]
        quarter_done.append(len(units))
    return units, quarter_done, mix_quarter, write_out


def _attn_gates_kernel(*refs):
    n_attn = 5 * N_GROUPS
    in_refs = refs[:n_attn]
    bias_ref = refs[n_attn]
    proj_in = refs[n_attn + 1:n_attn + 11]
    attn_out = refs[n_attn + 11]
    proj_out = refs[n_attn + 12:n_attn + 17]
    acc_scr, max_scr, den_scr, nat_scr = refs[n_attn + 17:]
    first = (pl.program_id(1) == 0).astype(jnp.int32)
    unit_makers, quarter_done, mix_quarter, write_out = _attention_items(
        in_refs, bias_ref, attn_out, acc_scr, max_scr, den_scr, nat_scr, first)
    proj = _gate_proj_items(*proj_in, *proj_out)
    per_round = len(unit_makers) // len(proj)
    assert per_round * len(proj) == len(unit_makers)
    rounds = [unit_makers[i * per_round:(i + 1) * per_round] for i in range(len(proj))]

    def advance(gen):
        next(gen, None)

    prev_units = []
    mixed = 0
    for r in range(len(rounds) + 1):
        units = [make() for make in rounds[r]] if r < len(rounds) else []
        for j in range(max(len(units), len(prev_units))):
            if j < len(units):
                advance(units[j])
            if j < len(prev_units):
                advance(prev_units[j])
        while mixed < len(quarter_done) and quarter_done[mixed] <= r * per_round:
            mix_quarter(mixed)
            mixed += 1
        if r < len(proj):
            for _ in proj[r]():
                pass
        for u in units:
            advance(u)
        prev_units = units
    write_out()


def _attn_bias_tables():
    rho = jnp.arange(BAND)[:, None]
    c = jnp.arange(2 * BAND)[None, :]
    sub = BAND // MIX_STRIDE
    tables = []
    for a in (rho, MIX_STRIDE * (rho % sub) + rho // sub):
        ok = (c >= a) & (c <= a + BAND)
        tables += [ok, ok & (c >= BAND)]
    return jnp.where(jnp.stack(tables), 0.0, NEG_BIG).astype(F32)


def _attn_gates(qkv_groups, x, lnw, w16, cos_t, sin_t, kdec, bg):
    B, S, _ = x.shape
    nt = S // ATTN_TILE
    cpt = GATE_ROWS // RET_CHUNK
    in_specs = []
    args = []
    for (_, dil), qkv in zip(ATTN_GROUPS, qkv_groups):
        rows = ATTN_TILE // dil
        ratio = rows // BAND

        def cur(col0, rows=rows, dil=dil):
            return pl.BlockSpec((None, dil, rows, HEAD_DIM),
                                lambda b, t, h: (b, 0, t, col0 + h))

        def prev(col0, ratio=ratio, dil=dil):
            return pl.BlockSpec((None, dil, BAND, HEAD_DIM),
                                lambda b, t, h: (b, 0, jnp.maximum(t * ratio - 1, 0), col0 + h))

        in_specs += [cur(0), cur(ATTN_HEADS), prev(ATTN_HEADS),
                     cur(2 * ATTN_HEADS), prev(2 * ATTN_HEADS)]
        args += [qkv] * 5
    bias = _attn_bias_tables()
    in_specs.append(pl.BlockSpec(bias.shape, lambda b, t, h: (0, 0, 0), pipeline_mode=RESIDENT))

    def wspec(n, col0=0):
        return pl.BlockSpec((D_MODEL, n), lambda b, t, h: (0, col0 // n), pipeline_mode=RESIDENT)

    def row_spec(n):
        return pl.BlockSpec((None, GATE_ROWS, n), lambda b, t, h: (b, t * ATTN_HEADS + h, 0))

    tspec = pl.BlockSpec((GATE_ROWS, LANES), lambda b, t, h: (t * ATTN_HEADS + h, 0))
    in_specs += [
        row_spec(D_MODEL),
        pl.BlockSpec((1, D_MODEL), lambda b, t, h: (0, 0)),
        wspec(2 * RET_QK_WIDTH, COL_RQ),
        wspec(GROUP_WIDTH, COL_AG), wspec(RET_V_WIDTH, COL_RG), wspec(2 * D_MODEL, COL_MG),
        tspec, tspec,
        pl.BlockSpec((GATE_ROWS, RET_QK_WIDTH), lambda b, t, h: (0, 0), pipeline_mode=RESIDENT),
        pl.BlockSpec((1, 2 * D_MODEL), lambda b, t, h: (0, 0)),
    ]
    out_widths = (GROUP_WIDTH, RET_V_WIDTH, 2 * D_MODEL)
    per_group = pltpu.VMEM((N_GROUPS, ATTN_TILE, HEAD_DIM), F32)
    return pl.pallas_call(
        _attn_gates_kernel,
        grid=(B, nt, ATTN_HEADS),
        in_specs=in_specs,
        out_specs=[
            pl.BlockSpec((None, ATTN_TILE, HEAD_DIM), lambda b, t, h: (b, t, h)),
            row_spec(RET_QK_WIDTH),
            pl.BlockSpec((None, cpt, RET_QK_WIDTH, RET_CHUNK),
                         lambda b, t, h: (b, t * ATTN_HEADS + h, 0, 0)),
            *[row_spec(n) for n in out_widths],
        ],
        out_shape=[
            jax.ShapeDtypeStruct((B, S, GROUP_WIDTH), BF16),
            jax.ShapeDtypeStruct((B, S, RET_QK_WIDTH), BF16),
            jax.ShapeDtypeStruct((B, S // RET_CHUNK, RET_QK_WIDTH, RET_CHUNK), BF16),
            *[jax.ShapeDtypeStruct((B, S, n), BF16) for n in out_widths],
        ],
        scratch_shapes=[per_group, per_group, per_group,
                        pltpu.VMEM((ATTN_TILE, HEAD_DIM), F32)],
        compiler_params=pltpu.CompilerParams(
            dimension_semantics=("arbitrary", "arbitrary", "arbitrary"),
            vmem_limit_bytes=VMEM_LIMIT),
        name="attn_gates",
    )(*args, bias, x, lnw, w16, w16, w16, w16, cos_t, sin_t, kdec, bg)


def _retention_kernel(q_ref, kt_ref, v_ref, gate_ref, gnw_ref, eps_ref, cd_ref,
                      out_ref, state, p_scr):
    @pl.when(pl.program_id(1) == 0)
    def _():
        state[...] = jnp.zeros_like(state)

    dk, dv, C = RET_KEY_DIM, RET_VALUE_DIM, RET_CHUNK
    row_k = lax.broadcasted_iota(jnp.int32, (2 * dk, C), 0)
    row = lax.broadcasted_iota(jnp.int32, (2 * dk, 2 * dv), 0)
    col = lax.broadcasted_iota(jnp.int32, (C, 2 * dv), 1)
    diag_blk = (row < dk) == (col < dv)
    causal = (col % C) <= lax.broadcasted_iota(jnp.int32, (C, 2 * C), 0)
    units = [(bi, j) for bi in range(RET_BATCH) for j in range(RET_HEADS // 2)]

    def q_pair(bi, j):
        return q_ref[bi, :, j * 2 * dk:(j + 1) * 2 * dk]

    def kt_pair(bi, j):
        return kt_ref[bi, j * 2 * dk:(j + 1) * 2 * dk, :]

    def v_pair(bi, j):
        return v_ref[bi, :, j * 2 * dv:(j + 1) * 2 * dv]

    for bi, j in units:
        kt = kt_pair(bi, j)
        zero = jnp.zeros_like(kt)
        kt_blk = jnp.concatenate([jnp.where(row_k < dk, kt, zero),
                                  jnp.where(row_k >= dk, kt, zero)], axis=1)
        s = _dot(q_pair(bi, j), kt_blk).astype(BF16)
        p_scr[bi, j] = jnp.where(causal, s, jnp.zeros_like(s))

    for bi, j in units:
        v = v_pair(bi, j)
        zero = jnp.zeros_like(v)
        v_blk = jnp.concatenate([jnp.where(col < dv, v, zero),
                                 jnp.where(col >= dv, v, zero)], axis=0)
        u = _dot(jnp.concatenate([p_scr[bi, j], q_pair(bi, j)], axis=1),
                 jnp.concatenate([v_blk, state[bi, j].astype(BF16)], axis=0))
        for hh in range(2):
            h = 2 * j + hh
            hs = slice(h * dv, (h + 1) * dv)
            uh = u[:, hh * dv:(hh + 1) * dv]
            uh = uh * lax.rsqrt(jnp.sum(uh * uh, axis=-1, keepdims=True) + eps_ref[h])
            out_ref[bi, :, hs] = (uh * gnw_ref[:, hs]).astype(BF16) * gate_ref[bi, :, hs]

    for bi, j in units:
        kv = _dot(kt_pair(bi, j), v_pair(bi, j))
        state[bi, j] = cd_ref[j] * (state[bi, j] + jnp.where(diag_blk, kv, 0.0))


def _retention(rq, rkt, rv, gate, gnw, eps_t, cd):
    B, S, _ = rv.shape
    nc = S // RET_CHUNK
    npair = RET_HEADS // 2
    pair_shape = (RET_BATCH, npair, 2 * RET_KEY_DIM, 2 * RET_VALUE_DIM)

    def const(shape):
        return pl.BlockSpec(shape, lambda b, n: (0,) * len(shape))

    def rows(width):
        return pl.BlockSpec((RET_BATCH, RET_CHUNK, width), lambda b, n: (b, n, 0))

    return pl.pallas_call(
        _retention_kernel,
        grid=(B // RET_BATCH, nc),
        in_specs=[
            rows(RET_QK_WIDTH),
            pl.BlockSpec((RET_BATCH, None, RET_QK_WIDTH, RET_CHUNK), lambda b, n: (b, n, 0, 0)),
            rows(RET_V_WIDTH),
            rows(RET_V_WIDTH),
            const((1, RET_V_WIDTH)),
            const((RET_HEADS, RET_CHUNK, RET_VALUE_DIM)),
            const(pair_shape[1:]),
        ],
        out_specs=rows(RET_V_WIDTH),
        out_shape=jax.ShapeDtypeStruct((B, S, RET_V_WIDTH), BF16),
        scratch_shapes=[pltpu.VMEM(pair_shape, F32),
                        pltpu.VMEM((RET_BATCH, npair, RET_CHUNK, 2 * RET_CHUNK), BF16)],
        compiler_params=pltpu.CompilerParams(
            dimension_semantics=("arbitrary", "arbitrary"), vmem_limit_bytes=VMEM_LIMIT),
        name="retention",
    )(rq, rkt, rv, gate, gnw, eps_t, cd)


def _merge_out_kernel(a_ref, ag_ref, r_ref, mg_ref, x_ref, wa_ref, wr_ref, wo_ref, lnf_ref,
                      out_ref):
    def part(rows):
        y_attn = _dot(a_ref[rows, :] * ag_ref[rows, :], wa_ref[...])
        y_ret = _dot(r_ref[rows, :], wr_ref[...])
        yield
        merged = (mg_ref[rows, :D_MODEL].astype(F32) * y_attn
                  + mg_ref[rows, D_MODEL:].astype(F32) * y_ret)
        h = x_ref[rows, :] + _dot(merged.astype(BF16), wo_ref[...])
        yield
        out_ref[rows, :] = _rmsnorm_rows(h, lnf_ref[...])

    n_parts = OUT_ROWS // OUT_PART
    parts = [part(slice(i * OUT_PART, (i + 1) * OUT_PART)) for i in range(n_parts)]
    for step in range(n_parts + 2):
        for lag in range(3):
            i = step - lag
            if 0 <= i < n_parts:
                next(parts[i], None)


def _merge_out(attn, ag, ret_g, mg, x, wa, wr, wo, lnf):
    B, S, _ = x.shape
    nt = S // OUT_ROWS

    def rows(n):
        return pl.BlockSpec((None, OUT_ROWS, n), lambda b, t: (b, t, 0))

    def const(shape):
        return pl.BlockSpec(shape, lambda b, t: (0,) * len(shape), pipeline_mode=RESIDENT)

    return pl.pallas_call(
        _merge_out_kernel,
        grid=(B, nt),
        in_specs=[rows(GROUP_WIDTH), rows(GROUP_WIDTH), rows(RET_V_WIDTH), rows(2 * D_MODEL),
                  rows(D_MODEL),
                  const((GROUP_WIDTH, D_MODEL)), const((RET_V_WIDTH, D_MODEL)),
                  const((D_MODEL, D_MODEL)), const((1, D_MODEL))],
        out_specs=rows(D_MODEL),
        out_shape=jax.ShapeDtypeStruct((B, S, D_MODEL), F32),
        compiler_params=pltpu.CompilerParams(
            dimension_semantics=("arbitrary", "arbitrary"), vmem_limit_bytes=VMEM_LIMIT),
        name="merge_out",
    )(attn, ag, ret_g, mg, x, wa, wr, wo, lnf)


def _rope_tables(S):
    half = RET_KEY_DIM // 2
    inv_freq = ROPE_BASE ** (-jnp.linspace(0.0, 1.0, half, dtype=F32))
    step = 64
    a = (jnp.arange(S // step, dtype=F32) * step)[:, None] * inv_freq[None, :]
    b = jnp.arange(step, dtype=F32)[:, None] * inv_freq[None, :]
    ca, sa, cb, sb = jnp.cos(a)[:, None], jnp.sin(a)[:, None], jnp.cos(b)[None], jnp.sin(b)[None]
    cos = (ca * cb - sa * sb).reshape(S, half)
    sin = (sa * cb + ca * sb).reshape(S, half)
    heads_per_tile = LANES // RET_KEY_DIM
    sign = jnp.where(jnp.arange(LANES) % 2 == 0, -1.0, 1.0).astype(F32)
    cos_t = jnp.tile(jnp.repeat(cos, 2, axis=1), (1, heads_per_tile))
    sin_t = jnp.tile(jnp.repeat(sin, 2, axis=1), (1, heads_per_tile)) * sign[None, :]
    return cos_t, sin_t


def _retention_constants():
    H, C, dk, dv = RET_HEADS, RET_CHUNK, RET_KEY_DIM, RET_VALUE_DIM
    log_gamma = jnp.log(1.0 - 2.0 ** (-5.0 - jnp.arange(H, dtype=F32)))
    idx = jnp.arange(C, dtype=F32)
    inv_decay = jnp.exp(-(idx + 1.0)[None, :] * log_gamma[:, None])
    chunk_decay = jnp.exp(C * log_gamma)
    kdec = jnp.repeat(inv_decay.T, dk, axis=1) * dk ** -0.5
    kdec = jnp.tile(kdec, (GATE_ROWS // C, 1))
    eps_t = jnp.broadcast_to((dv * NORM_EPS * inv_decay * inv_decay)[:, :, None], (H, C, dv))
    cd = jnp.broadcast_to(chunk_decay[:, None, None], (H, dk, 2 * dv)).reshape(H // 2, 2 * dk, 2 * dv)
    return kdec, eps_t, cd


def kernel(x, ln1_w, w_in, b_gate, attn_proj, ret_proj, ret_gn_w, w_out, lnf_w):
    B, S, _ = x.shape
    assert w_in.shape[0] == 1, "single layer"
    w16 = w_in[0].astype(BF16)

    lnw = ln1_w[0].reshape(1, D_MODEL)
    q1, q2, q3, rv = _qkv_proj(x, lnw, w16)
    cos_t, sin_t = _rope_tables(S)
    kdec, eps_t, cd = _retention_constants()
    attn, rq, rkt, ag, rg, mg = _attn_gates(
        (q1, q2, q3), x, lnw, w16, cos_t, sin_t, kdec, b_gate[0].reshape(1, 2 * D_MODEL))
    gnw = (ret_gn_w[0] * RET_VALUE_DIM ** 0.5).reshape(1, RET_V_WIDTH)
    ret_g = _retention(rq, rkt, rv, rg, gnw, eps_t, cd)

    return _merge_out(attn, ag, ret_g, mg, x,
                      attn_proj[0].astype(BF16), ret_proj[0].astype(BF16),
                      w_out[0].astype(BF16), lnf_w.reshape(1, D_MODEL))
```

```python
import jax
import jax.numpy as jnp
from jax import lax
from jax.experimental import pallas as pl
from jax.experimental.pallas import tpu as pltpu

D_MODEL = 1024
ATTN_GROUPS = ((128, 1), (512, 4), (2048, 16))
N_GROUPS = 3
ATTN_HEADS = 4
HEAD_DIM = 128
GROUP_WIDTH = ATTN_HEADS * HEAD_DIM
QKV_WIDTH = N_GROUPS * GROUP_WIDTH
BAND = 128
RET_HEADS = 8
RET_KEY_DIM = 64
RET_VALUE_DIM = 128
RET_QK_WIDTH = RET_HEADS * RET_KEY_DIM
RET_V_WIDTH = RET_HEADS * RET_VALUE_DIM
RET_CHUNK = 128
ROPE_BASE = 10000.0
COL_AG = 3 * QKV_WIDTH
COL_RQ = COL_AG + GROUP_WIDTH
COL_RV = COL_RQ + 2 * RET_QK_WIDTH
COL_RG = COL_RV + RET_V_WIDTH
COL_MG = COL_RG + RET_V_WIDTH
NORM_EPS = 1e-6
NEG_BIG = -1e30
LANES = 128
LOG2E = 1.4426950408889634
MIX_STRIDE = 4

PROJ_ROWS = 1024
ATTN_TILE = 2048
GATE_ROWS = ATTN_TILE // ATTN_HEADS
RET_BATCH = 8
OUT_ROWS = 1024
OUT_PART = 256
VMEM_LIMIT = 56 * 1024 * 1024
GATE_PIECES = (2, 2, 4, 2, 2)
RESIDENT = pl.Buffered(1)

BF16 = jnp.bfloat16
F32 = jnp.float32


def _rmsnorm_rows(x, w):
    return x * lax.rsqrt(jnp.mean(x * x, axis=-1, keepdims=True) + NORM_EPS) * w


def _dot(a, b):
    return jnp.dot(a, b, preferred_element_type=F32)


def _sigmoid(t):
    return 0.5 * jnp.tanh(0.5 * t) + 0.5


def _silu(t):
    return t * _sigmoid(t)


def _qkv_proj_kernel(x_ref, lnw_ref, *refs):
    w_refs = refs[:3 * N_GROUPS]
    wrv_ref = refs[3 * N_GROUPS]
    out_refs = refs[3 * N_GROUPS + 1:4 * N_GROUPS + 1]
    rv_ref, xn_scr, xn4_scr, lhs_scr = refs[4 * N_GROUPS + 1:]
    xn = _rmsnorm_rows(x_ref[...], lnw_ref[...])
    n_lane_tiles = D_MODEL // LANES
    for ct in range(n_lane_tiles):
        xn_scr[ct] = xn[:, ct * LANES:(ct + 1) * LANES]
    q_scale = HEAD_DIM ** -0.5 * LOG2E
    q_cols = slice(0, GROUP_WIDTH)
    k_cols = slice(GROUP_WIDTH, 2 * GROUP_WIDTH)
    v_cols = slice(2 * GROUP_WIDTH, 3 * GROUP_WIDTH)

    def strided_rows(r, rows, dil):
        return jnp.concatenate(
            [xn_scr[ct, pl.ds(r, rows, stride=dil), :] for ct in range(n_lane_tiles)], axis=1)

    def residue_major():
        d4, d16 = ATTN_GROUPS[1][1], ATTN_GROUPS[2][1]
        rows4, rows16 = PROJ_ROWS // d4, PROJ_ROWS // d16
        for ct in range(n_lane_tiles):
            by4 = jnp.concatenate(
                [xn_scr[ct, pl.ds(r, rows4, stride=d4), :] for r in range(d4)], axis=0)
            xn4_scr[ct] = by4
            lhs_scr[0, :, ct * LANES:(ct + 1) * LANES] = by4.astype(BF16)
        for ct in range(n_lane_tiles):
            by16 = jnp.concatenate(
                [xn4_scr[ct, pl.ds((r % d4) * rows4 + r // d4, rows16, stride=d16 // d4), :]
                 for r in range(d16)], axis=0)
            lhs_scr[1, :, ct * LANES:(ct + 1) * LANES] = by16.astype(BF16)

    def project(g, lhs_q, lhs_kv):
        wq_ref, wk_ref, wv_ref = w_refs[3 * g:3 * g + 3]
        dil = ATTN_GROUPS[g][1]
        rows = PROJ_ROWS // dil
        q = (_dot(lhs_q, wq_ref[...]) * q_scale).astype(BF16)
        k = _dot(lhs_kv, wk_ref[...]).astype(BF16)
        v = _dot(lhs_kv, wv_ref[...]).astype(BF16)
        for r in range(dil):
            rs = slice(r * rows, (r + 1) * rows)
            out_refs[g][r, :, q_cols] = q[rs]
            out_refs[g][r, :, k_cols] = k[rs]
            out_refs[g][r, :, v_cols] = v[rs]

    lhs = xn.astype(BF16)
    sub = BAND // MIX_STRIDE
    lhs_q = jnp.concatenate(
        [strided_rows(blk * BAND + r4, sub, MIX_STRIDE).astype(BF16)
         for blk in range(PROJ_ROWS // BAND) for r4 in range(MIX_STRIDE)], axis=0)
    rv_ref[...] = _dot(lhs, wrv_ref[...]).astype(BF16)
    residue_major()
    project(0, lhs_q, lhs)
    project(1, lhs_scr[0], lhs_scr[0])
    project(2, lhs_scr[1], lhs_scr[1])


def _qkv_proj(x, lnw, w16):
    B, S, _ = x.shape
    nt = S // PROJ_ROWS
    w_specs = [pl.BlockSpec((D_MODEL, GROUP_WIDTH), lambda b, t, cb=part * N_GROUPS + g: (0, cb),
                            pipeline_mode=RESIDENT)
               for g in range(N_GROUPS) for part in range(3)]
    w_specs.append(pl.BlockSpec((D_MODEL, RET_V_WIDTH), lambda b, t: (0, COL_RV // RET_V_WIDTH),
                                pipeline_mode=RESIDENT))
    return pl.pallas_call(
        _qkv_proj_kernel,
        grid=(B, nt),
        in_specs=[
            pl.BlockSpec((None, PROJ_ROWS, D_MODEL), lambda b, t: (b, t, 0)),
            pl.BlockSpec((1, D_MODEL), lambda b, t: (0, 0)),
            *w_specs,
        ],
        out_specs=[
            *[pl.BlockSpec((None, dil, PROJ_ROWS // dil, QKV_WIDTH), lambda b, t: (b, 0, t, 0))
              for _, dil in ATTN_GROUPS],
            pl.BlockSpec((None, PROJ_ROWS, RET_V_WIDTH), lambda b, t: (b, t, 0)),
        ],
        out_shape=[
            *[jax.ShapeDtypeStruct((B, dil, S // dil, QKV_WIDTH), BF16) for _, dil in ATTN_GROUPS],
            jax.ShapeDtypeStruct((B, S, RET_V_WIDTH), BF16),
        ],
        scratch_shapes=[pltpu.VMEM((D_MODEL // LANES, PROJ_ROWS, LANES), F32),
                        pltpu.VMEM((D_MODEL // LANES, PROJ_ROWS, LANES), F32),
                        pltpu.VMEM((N_GROUPS - 1, PROJ_ROWS, D_MODEL), BF16)],
        compiler_params=pltpu.CompilerParams(
            dimension_semantics=("arbitrary", "arbitrary"), vmem_limit_bytes=VMEM_LIMIT),
        name="qkv_proj",
    )(x, lnw, *([w16] * (3 * N_GROUPS + 1)))


def _swap_lane_pairs(t):
    n = t.shape[-1]
    lane = lax.broadcasted_iota(jnp.int32, t.shape, 1)
    up = pltpu.roll(t, n - 1, axis=1)
    down = pltpu.roll(t, 1, axis=1)
    return jnp.where(lane % 2 == 0, up, down)


def _gate_proj_items(x_ref, lnw_ref, wqk_ref, wg_ref, wrg_ref, wmg_ref,
                     cos_ref, sin_ref, kdec_ref, bg_ref,
                     rq_ref, rkt_ref, ag_ref, rg_ref, mg_ref):
    cache = {}

    def lhs():
        if "xb" not in cache:
            cache["xb"] = _rmsnorm_rows(x_ref[...], lnw_ref[...]).astype(BF16)
        return cache["xb"]

    def rope(t):
        pairs = t.shape[1] // LANES
        cos = jnp.concatenate([cos_ref[...]] * pairs, axis=1)
        sin = jnp.concatenate([sin_ref[...]] * pairs, axis=1)
        return t * cos + _swap_lane_pairs(t) * sin

    def retention_q(c0, width):
        cols = slice(c0, c0 + width)
        t = _dot(lhs(), wqk_ref[:, cols])
        yield
        rq_ref[:, cols] = rope(t).astype(BF16)

    def retention_k(c0, width):
        cols = slice(c0, c0 + width)
        t = _dot(lhs(), wqk_ref[:, RET_QK_WIDTH + c0:RET_QK_WIDTH + c0 + width])
        yield
        k = rope(t) * kdec_ref[:, cols]
        for c in range(GATE_ROWS // RET_CHUNK):
            rkt_ref[c, cols, :] = k[c * RET_CHUNK:(c + 1) * RET_CHUNK, :].T.astype(BF16)

    def merge_gate(c0, width):
        cols = slice(c0, c0 + width)
        t = _dot(lhs(), wmg_ref[:, cols])
        yield
        mg_ref[:, cols] = _sigmoid(t + bg_ref[:, cols]).astype(BF16)

    def retention_gate(c0, width):
        cols = slice(c0, c0 + width)
        t = _dot(lhs(), wrg_ref[:, cols])
        yield
        rg_ref[:, cols] = _silu(t).astype(BF16)

    def attention_gate(c0, width):
        cols = slice(c0, c0 + width)
        t = _dot(lhs(), wg_ref[:, cols])
        yield
        ag_ref[:, cols] = _silu(t).astype(BF16)

    items = []
    for fn, total, pieces in ((retention_q, RET_QK_WIDTH, GATE_PIECES[0]),
                              (retention_k, RET_QK_WIDTH, GATE_PIECES[1]),
                              (merge_gate, 2 * D_MODEL, GATE_PIECES[2]),
                              (retention_gate, RET_V_WIDTH, GATE_PIECES[3]),
                              (attention_gate, GROUP_WIDTH, GATE_PIECES[4])):
        width = total // pieces
        items += [lambda fn=fn, c0=i * width, width=width: fn(c0, width) for i in range(pieces)]
    return items


def _attention_items(in_refs, bias_ref, out_ref, acc_scr, max_scr, den_scr, nat_scr, first):
    ones = jnp.ones((2 * BAND, HEAD_DIM), BF16)
    quarter = ATTN_TILE // MIX_STRIDE

    def unit(g, r, n):
        dil = ATTN_GROUPS[g][1]
        q_ref, kc_ref, kp_ref, vc_ref, vp_ref = in_refs[5 * g:5 * g + 5]
        bias_base = 2 if dil == 1 else 0
        q = q_ref[r, n * BAND:(n + 1) * BAND, :]
        if n == 0:
            k = jnp.concatenate([kp_ref[r], kc_ref[r, 0:BAND, :]], axis=0)
            v = jnp.concatenate([vp_ref[r], vc_ref[r, 0:BAND, :]], axis=0)
            bias = bias_ref[bias_base + first]
        else:
            k = kc_ref[r, (n - 1) * BAND:(n + 1) * BAND, :]
            v = vc_ref[r, (n - 1) * BAND:(n + 1) * BAND, :]
            bias = bias_ref[bias_base]
        s = lax.dot_general(q, k, (((1,), (1,)), ((), ())), preferred_element_type=F32) + bias
        yield
        m = jnp.max(s, axis=-1, keepdims=True)
        p = jnp.exp2(s - m).astype(BF16)
        yield
        res = _dot(p, jnp.concatenate([v, ones], axis=1))
        acc, den = res[:, :HEAD_DIM], res[:, HEAD_DIM:]
        mb = jnp.broadcast_to(m, (BAND, HEAD_DIM))
        if dil == 1:
            sub = BAND // MIX_STRIDE
            for r4 in range(MIX_STRIDE):
                dst = pl.ds(r4 * quarter + n * sub, sub)
                src = slice(r4 * sub, (r4 + 1) * sub)
                acc_scr[g, dst, :] = acc[src]
                max_scr[g, dst, :] = mb[src]
                den_scr[g, dst, :] = den[src]
        else:
            if dil == MIX_STRIDE:
                dst = pl.ds(r * quarter + n * BAND, BAND)
            else:
                sub_stride = dil // MIX_STRIDE
                dst = pl.ds((r % MIX_STRIDE) * quarter + r // MIX_STRIDE
                            + n * BAND * sub_stride, BAND, stride=sub_stride)
            acc_scr[g, dst, :] = acc
            max_scr[g, dst, :] = mb
            den_scr[g, dst, :] = den

    def mix(r4, ch):
        rows = pl.ds(r4 * quarter + ch * BAND, BAND)
        m0, m1, m2 = max_scr[0, rows, :], max_scr[1, rows, :], max_scr[2, rows, :]
        mx = jnp.maximum(jnp.maximum(m0, m1), m2)
        w0 = jnp.exp2(m0 - mx)
        w1 = jnp.exp2(m1 - mx)
        w2 = jnp.exp2(m2 - mx)
        num = w0 * acc_scr[0, rows, :] + w1 * acc_scr[1, rows, :] + w2 * acc_scr[2, rows, :]
        den = w0 * den_scr[0, rows, :] + w1 * den_scr[1, rows, :] + w2 * den_scr[2, rows, :]
        nat_scr[pl.ds(r4 + ch * BAND * MIX_STRIDE, BAND, stride=MIX_STRIDE), :] = num / den

    def write_out():
        out_ref[...] = nat_scr[...].astype(BF16)

    def mix_quarter(r4):
        for ch in range(quarter // BAND):
            mix(r4, ch)

    units = [lambda n=n: unit(0, 0, n) for n in range(ATTN_TILE // BAND)]
    quarter_done = []
    for r4 in range(MIX_STRIDE):
        units += [lambda g=g, r=r, n=n: unit(g, r, n)
                  for g in range(1, N_GROUPS)
                  for r in range(r4, ATTN_GROUPS[g][1], MIX_STRIDE)
                  for n in range(ATTN_TILE // ATTN_GROUPS[g][1] // BAND)]
        quarter_done.append(len(units))
    return units, quarter_done, mix_quarter, write_out


def _attn_gates_kernel(*refs):
    n_attn = 5 * N_GROUPS
    in_refs = refs[:n_attn]
    bias_ref = refs[n_attn]
    proj_in = refs[n_attn + 1:n_attn + 11]
    attn_out = refs[n_attn + 11]
    proj_out = refs[n_attn + 12:n_attn + 17]
    acc_scr, max_scr, den_scr, nat_scr = refs[n_attn + 17:]
    first = (pl.program_id(1) == 0).astype(jnp.int32)
    unit_makers, quarter_done, mix_quarter, write_out = _attention_items(
        in_refs, bias_ref, attn_out, acc_scr, max_scr, den_scr, nat_scr, first)
    proj = _gate_proj_items(*proj_in, *proj_out)
    per_round = len(unit_makers) // len(proj)
    assert per_round * len(proj) == len(unit_makers)
    rounds = [unit_makers[i * per_round:(i + 1) * per_round] for i in range(len(proj))]

    def advance(gen):
        next(gen, None)

    prev_units = []
    mixed = 0
    for r in range(len(rounds) + 1):
        units = [make() for make in rounds[r]] if r < len(rounds) else []
        for j in range(max(len(units), len(prev_units))):
            if j < len(units):
                advance(units[j])
            if j < len(prev_units):
                advance(prev_units[j])
        while mixed < len(quarter_done) and quarter_done[mixed] <= r * per_round:
            mix_quarter(mixed)
            mixed += 1
        if r < len(proj):
            for _ in proj[r]():
                pass
        for u in units:
            advance(u)
        prev_units = units
    write_out()


def _attn_bias_tables():
    rho = jnp.arange(BAND)[:, None]
    c = jnp.arange(2 * BAND)[None, :]
    sub = BAND // MIX_STRIDE
    tables = []
    for a in (rho, MIX_STRIDE * (rho % sub) + rho // sub):
        ok = (c >= a) & (c <= a + BAND)
        tables += [ok, ok & (c >= BAND)]
    return jnp.where(jnp.stack(tables), 0.0, NEG_BIG).astype(F32)


def _attn_gates(qkv_groups, x, lnw, w16, cos_t, sin_t, kdec, bg):
    B, S, _ = x.shape
    nt = S // ATTN_TILE
    cpt = GATE_ROWS // RET_CHUNK
    in_specs = []
    args = []
    for (_, dil), qkv in zip(ATTN_GROUPS, qkv_groups):
        rows = ATTN_TILE // dil
        ratio = rows // BAND

        def cur(col0, rows=rows, dil=dil):
            return pl.BlockSpec((None, dil, rows, HEAD_DIM),
                                lambda b, t, h: (b, 0, t, col0 + h))

        def prev(col0, ratio=ratio, dil=dil):
            return pl.BlockSpec((None, dil, BAND, HEAD_DIM),
                                lambda b, t, h: (b, 0, jnp.maximum(t * ratio - 1, 0), col0 + h))

        in_specs += [cur(0), cur(ATTN_HEADS), prev(ATTN_HEADS),
                     cur(2 * ATTN_HEADS), prev(2 * ATTN_HEADS)]
        args += [qkv] * 5
    bias = _attn_bias_tables()
    in_specs.append(pl.BlockSpec(bias.shape, lambda b, t, h: (0, 0, 0), pipeline_mode=RESIDENT))

    def wspec(n, col0=0):
        return pl.BlockSpec((D_MODEL, n), lambda b, t, h: (0, col0 // n), pipeline_mode=RESIDENT)

    def row_spec(n):
        return pl.BlockSpec((None, GATE_ROWS, n), lambda b, t, h: (b, t * ATTN_HEADS + h, 0))

    tspec = pl.BlockSpec((GATE_ROWS, LANES), lambda b, t, h: (t * ATTN_HEADS + h, 0))
    in_specs += [
        row_spec(D_MODEL),
        pl.BlockSpec((1, D_MODEL), lambda b, t, h: (0, 0)),
        wspec(2 * RET_QK_WIDTH, COL_RQ),
        wspec(GROUP_WIDTH, COL_AG), wspec(RET_V_WIDTH, COL_RG), wspec(2 * D_MODEL, COL_MG),
        tspec, tspec,
        pl.BlockSpec((GATE_ROWS, RET_QK_WIDTH), lambda b, t, h: (0, 0), pipeline_mode=RESIDENT),
        pl.BlockSpec((1, 2 * D_MODEL), lambda b, t, h: (0, 0)),
    ]
    out_widths = (GROUP_WIDTH, RET_V_WIDTH, 2 * D_MODEL)
    per_group = pltpu.VMEM((N_GROUPS, ATTN_TILE, HEAD_DIM), F32)
    return pl.pallas_call(
        _attn_gates_kernel,
        grid=(B, nt, ATTN_HEADS),
        in_specs=in_specs,
        out_specs=[
            pl.BlockSpec((None, ATTN_TILE, HEAD_DIM), lambda b, t, h: (b, t, h)),
            row_spec(RET_QK_WIDTH),
            pl.BlockSpec((None, cpt, RET_QK_WIDTH, RET_CHUNK),
                         lambda b, t, h: (b, t * ATTN_HEADS + h, 0, 0)),
            *[row_spec(n) for n in out_widths],
        ],
        out_shape=[
            jax.ShapeDtypeStruct((B, S, GROUP_WIDTH), BF16),
            jax.ShapeDtypeStruct((B, S, RET_QK_WIDTH), BF16),
            jax.ShapeDtypeStruct((B, S // RET_CHUNK, RET_QK_WIDTH, RET_CHUNK), BF16),
            *[jax.ShapeDtypeStruct((B, S, n), BF16) for n in out_widths],
        ],
        scratch_shapes=[per_group, per_group, per_group,
                        pltpu.VMEM((ATTN_TILE, HEAD_DIM), F32)],
        compiler_params=pltpu.CompilerParams(
            dimension_semantics=("arbitrary", "arbitrary", "arbitrary"),
            vmem_limit_bytes=VMEM_LIMIT),
        name="attn_gates",
    )(*args, bias, x, lnw, w16, w16, w16, w16, cos_t, sin_t, kdec, bg)


def _retention_kernel(q_ref, kt_ref, v_ref, gate_ref, gnw_ref, eps_ref, cd_ref,
                      out_ref, state, p_scr):
    @pl.when(pl.program_id(1) == 0)
    def _():
        state[...] = jnp.zeros_like(state)

    dk, dv, C = RET_KEY_DIM, RET_VALUE_DIM, RET_CHUNK
    row_k = lax.broadcasted_iota(jnp.int32, (2 * dk, C), 0)
    row = lax.broadcasted_iota(jnp.int32, (2 * dk, 2 * dv), 0)
    col = lax.broadcasted_iota(jnp.int32, (C, 2 * dv), 1)
    diag_blk = (row < dk) == (col < dv)
    causal = (col % C) <= lax.broadcasted_iota(jnp.int32, (C, 2 * C), 0)
    units = [(bi, j) for bi in range(RET_BATCH) for j in range(RET_HEADS // 2)]

    def q_pair(bi, j):
        return q_ref[bi, :, j * 2 * dk:(j + 1) * 2 * dk]

    def kt_pair(bi, j):
        return kt_ref[bi, j * 2 * dk:(j + 1) * 2 * dk, :]

    def v_pair(bi, j):
        return v_ref[bi, :, j * 2 * dv:(j + 1) * 2 * dv]

    for bi, j in units:
        kt = kt_pair(bi, j)
        zero = jnp.zeros_like(kt)
        kt_blk = jnp.concatenate([jnp.where(row_k < dk, kt, zero),
                                  jnp.where(row_k >= dk, kt, zero)], axis=1)
        s = _dot(q_pair(bi, j), kt_blk).astype(BF16)
        p_scr[bi, j] = jnp.where(causal, s, jnp.zeros_like(s))

    for bi, j in units:
        v = v_pair(bi, j)
        zero = jnp.zeros_like(v)
        v_blk = jnp.concatenate([jnp.where(col < dv, v, zero),
                                 jnp.where(col >= dv, v, zero)], axis=0)
        u = _dot(jnp.concatenate([p_scr[bi, j], q_pair(bi, j)], axis=1),
                 jnp.concatenate([v_blk, state[bi, j].astype(BF16)], axis=0))
        for hh in range(2):
            h = 2 * j + hh
            hs = slice(h * dv, (h + 1) * dv)
            uh = u[:, hh * dv:(hh + 1) * dv]
            uh = uh * lax.rsqrt(jnp.sum(uh * uh, axis=-1, keepdims=True) + eps_ref[h])
            out_ref[bi, :, hs] = (uh * gnw_ref[:, hs]).astype(BF16) * gate_ref[bi, :, hs]

    for bi, j in units:
        kv = _dot(kt_pair(bi, j), v_pair(bi, j))
        state[bi, j] = cd_ref[j] * (state[bi, j] + jnp.where(diag_blk, kv, 0.0))


def _retention(rq, rkt, rv, gate, gnw, eps_t, cd):
    B, S, _ = rv.shape
    nc = S // RET_CHUNK
    npair = RET_HEADS // 2
    pair_shape = (RET_BATCH, npair, 2 * RET_KEY_DIM, 2 * RET_VALUE_DIM)

    def const(shape):
        return pl.BlockSpec(shape, lambda b, n: (0,) * len(shape))

    def rows(width):
        return pl.BlockSpec((RET_BATCH, RET_CHUNK, width), lambda b, n: (b, n, 0))

    return pl.pallas_call(
        _retention_kernel,
        grid=(B // RET_BATCH, nc),
        in_specs=[
            rows(RET_QK_WIDTH),
            pl.BlockSpec((RET_BATCH, None, RET_QK_WIDTH, RET_CHUNK), lambda b, n: (b, n, 0, 0)),
            rows(RET_V_WIDTH),
            rows(RET_V_WIDTH),
            const((1, RET_V_WIDTH)),
            const((RET_HEADS, RET_CHUNK, RET_VALUE_DIM)),
            const(pair_shape[1:]),
        ],
        out_specs=rows(RET_V_WIDTH),
        out_shape=jax.ShapeDtypeStruct((B, S, RET_V_WIDTH), BF16),
        scratch_shapes=[pltpu.VMEM(pair_shape, F32),
                        pltpu.VMEM((RET_BATCH, npair, RET_CHUNK, 2 * RET_CHUNK), BF16)],
        compiler_params=pltpu.CompilerParams(
            dimension_semantics=("arbitrary", "arbitrary"), vmem_limit_bytes=VMEM_LIMIT),
        name="retention",
    )(rq, rkt, rv, gate, gnw, eps_t, cd)


def _merge_out_kernel(a_ref, ag_ref, r_ref, mg_ref, x_ref, wa_ref, wr_ref, wo_ref, lnf_ref,
                      out_ref):
    def part(rows):
        y_attn = _dot(a_ref[rows, :] * ag_ref[rows, :], wa_ref[...])
        y_ret = _dot(r_ref[rows, :], wr_ref[...])
        yield
        merged = (mg_ref[rows, :D_MODEL].astype(F32) * y_attn
                  + mg_ref[rows, D_MODEL:].astype(F32) * y_ret)
        h = x_ref[rows, :] + _dot(merged.astype(BF16), wo_ref[...])
        yield
        out_ref[rows, :] = _rmsnorm_rows(h, lnf_ref[...])

    n_parts = OUT_ROWS // OUT_PART
    parts = [part(slice(i * OUT_PART, (i + 1) * OUT_PART)) for i in range(n_parts)]
    for step in range(n_parts + 2):
        for lag in range(3):
            i = step - lag
            if 0 <= i < n_parts:
                next(parts[i], None)


def _merge_out(attn, ag, ret_g, mg, x, wa, wr, wo, lnf):
    B, S, _ = x.shape
    nt = S // OUT_ROWS

    def rows(n):
        return pl.BlockSpec((None, OUT_ROWS, n), lambda b, t: (b, t, 0))

    def const(shape):
        return pl.BlockSpec(shape, lambda b, t: (0,) * len(shape), pipeline_mode=RESIDENT)

    return pl.pallas_call(
        _merge_out_kernel,
        grid=(B, nt),
        in_specs=[rows(GROUP_WIDTH), rows(GROUP_WIDTH), rows(RET_V_WIDTH), rows(2 * D_MODEL),
                  rows(D_MODEL),
                  const((GROUP_WIDTH, D_MODEL)), const((RET_V_WIDTH, D_MODEL)),
                  const((D_MODEL, D_MODEL)), const((1, D_MODEL))],
        out_specs=rows(D_MODEL),
        out_shape=jax.ShapeDtypeStruct((B, S, D_MODEL), F32),
        compiler_params=pltpu.CompilerParams(
            dimension_semantics=("arbitrary", "arbitrary"), vmem_limit_bytes=VMEM_LIMIT),
        name="merge_out",
    )(attn, ag, ret_g, mg, x, wa, wr, wo, lnf)


def _rope_tables(S):
    half = RET_KEY_DIM // 2
    inv_freq = ROPE_BASE ** (-jnp.linspace(0.0, 1.0, half, dtype=F32))
    step = 64
    a = (jnp.arange(S // step, dtype=F32) * step)[:, None] * inv_freq[None, :]
    b = jnp.arange(step, dtype=F32)[:, None] * inv_freq[None, :]
    ca, sa, cb, sb = jnp.cos(a)[:, None], jnp.sin(a)[:, None], jnp.cos(b)[None], jnp.sin(b)[None]
    cos = (ca * cb - sa * sb).reshape(S, half)
    sin = (sa * cb + ca * sb).reshape(S, half)
    heads_per_tile = LANES // RET_KEY_DIM
    sign = jnp.where(jnp.arange(LANES) % 2 == 0, -1.0, 1.0).astype(F32)
    cos_t = jnp.tile(jnp.repeat(cos, 2, axis=1), (1, heads_per_tile))
    sin_t = jnp.tile(jnp.repeat(sin, 2, axis=1), (1, heads_per_tile)) * sign[None, :]
    return cos_t, sin_t


def _retention_constants():
    H, C, dk, dv = RET_HEADS, RET_CHUNK, RET_KEY_DIM, RET_VALUE_DIM
    log_gamma = jnp.log(1.0 - 2.0 ** (-5.0 - jnp.arange(H, dtype=F32)))
    idx = jnp.arange(C, dtype=F32)
    inv_decay = jnp.exp(-(idx + 1.0)[None, :] * log_gamma[:, None])
    chunk_decay = jnp.exp(C * log_gamma)
    kdec = jnp.repeat(inv_decay.T, dk, axis=1) * dk ** -0.5
    kdec = jnp.tile(kdec, (GATE_ROWS // C, 1))
    eps_t = jnp.broadcast_to((dv * NORM_EPS * inv_decay * inv_decay)[:, :, None], (H, C, dv))
    cd = jnp.broadcast_to(chunk_decay[:, None, None], (H, dk, 2 * dv)).reshape(H // 2, 2 * dk, 2 * dv)
    return kdec, eps_t, cd


def kernel(x, ln1_w, w_in, b_gate, attn_proj, ret_proj, ret_gn_w, w_out, lnf_w):
    B, S, _ = x.shape
    assert w_in.shape[0] == 1, "single layer"
    w16 = w_in[0].astype(BF16)

    lnw = ln1_w[0].reshape(1, D_MODEL)
    q1, q2, q3, rv = _qkv_proj(x, lnw, w16)
    cos_t, sin_t = _rope_tables(S)
    kdec, eps_t, cd = _retention_constants()
    attn, rq, rkt, ag, rg, mg = _attn_gates(
        (q1, q2, q3), x, lnw, w16, cos_t, sin_t, kdec, b_gate[0].reshape(1, 2 * D_MODEL))
    gnw = (ret_gn_w[0] * RET_VALUE_DIM ** 0.5).reshape(1, RET_V_WIDTH)
    ret_g = _retention(rq, rkt, rv, rg, gnw, eps_t, cd)

    return _merge_out(attn, ag, ret_g, mg, x,
                      attn_proj[0].astype(BF16), ret_proj[0].astype(BF16),
                      w_out[0].astype(BF16), lnf_w.reshape(1, D_MODEL))
```

```python
import jax
import jax.numpy as jnp
from jax import lax
from jax.experimental import pallas as pl
from jax.experimental.pallas import tpu as pltpu

D_MODEL = 1024
ATTN_GROUPS = ((128, 1), (512, 4), (2048, 16))
N_GROUPS = 3
ATTN_HEADS = 4
HEAD_DIM = 128
GROUP_WIDTH = ATTN_HEADS * HEAD_DIM
QKV_WIDTH = N_GROUPS * GROUP_WIDTH
BAND = 128
RET_HEADS = 8
RET_KEY_DIM = 64
RET_VALUE_DIM = 128
RET_QK_WIDTH = RET_HEADS * RET_KEY_DIM
RET_V_WIDTH = RET_HEADS * RET_VALUE_DIM
RET_CHUNK = 128
ROPE_BASE = 10000.0
COL_AG = 3 * QKV_WIDTH
COL_RQ = COL_AG + GROUP_WIDTH
COL_RV = COL_RQ + 2 * RET_QK_WIDTH
COL_RG = COL_RV + RET_V_WIDTH
COL_MG = COL_RG + RET_V_WIDTH
NORM_EPS = 1e-6
NEG_BIG = -1e30
LANES = 128
LOG2E = 1.4426950408889634
MIX_STRIDE = 4

PROJ_ROWS = 1024
ATTN_TILE = 2048
GATE_ROWS = ATTN_TILE // ATTN_HEADS
RET_BATCH = 8
OUT_ROWS = 1024
OUT_PART = 256
V7X_VMEM_BYTES = 64 * 1024 * 1024
VMEM_LIMIT = V7X_VMEM_BYTES * 7 // 8
GATE_ROW_PARTS = 2
GATE_PIECES = (2, 2, 4, 2, 2)
RESIDENT = pl.Buffered(1)

BF16 = jnp.bfloat16
F32 = jnp.float32


def _rmsnorm_rows(x, w):
    return x * lax.rsqrt(jnp.mean(x * x, axis=-1, keepdims=True) + NORM_EPS) * w


def _dot(a, b):
    return jnp.dot(a, b, preferred_element_type=F32)


def _sigmoid(t):
    return 0.5 * jnp.tanh(0.5 * t) + 0.5


def _silu(t):
    return t * _sigmoid(t)


def _qkv_proj_kernel(x_ref, lnw_ref, *refs):
    w_refs = refs[:3 * N_GROUPS]
    wrv_ref = refs[3 * N_GROUPS]
    out_refs = refs[3 * N_GROUPS + 1:4 * N_GROUPS + 1]
    rv_ref, xn_scr, xn4_scr, lhs_scr = refs[4 * N_GROUPS + 1:]
    xn = _rmsnorm_rows(x_ref[...], lnw_ref[...])
    n_lane_tiles = D_MODEL // LANES
    for ct in range(n_lane_tiles):
        xn_scr[ct] = xn[:, ct * LANES:(ct + 1) * LANES]
    q_scale = HEAD_DIM ** -0.5 * LOG2E
    q_cols = slice(0, GROUP_WIDTH)
    k_cols = slice(GROUP_WIDTH, 2 * GROUP_WIDTH)
    v_cols = slice(2 * GROUP_WIDTH, 3 * GROUP_WIDTH)

    def strided_rows(r, rows, dil):
        return jnp.concatenate(
            [xn_scr[ct, pl.ds(r, rows, stride=dil), :] for ct in range(n_lane_tiles)], axis=1)

    def residue_major():
        d4, d16 = ATTN_GROUPS[1][1], ATTN_GROUPS[2][1]
        rows4, rows16 = PROJ_ROWS // d4, PROJ_ROWS // d16
        for ct in range(n_lane_tiles):
            by4 = jnp.concatenate(
                [xn_scr[ct, pl.ds(r, rows4, stride=d4), :] for r in range(d4)], axis=0)
            xn4_scr[ct] = by4
            lhs_scr[0, :, ct * LANES:(ct + 1) * LANES] = by4.astype(BF16)
        for ct in range(n_lane_tiles):
            by16 = jnp.concatenate(
                [xn4_scr[ct, pl.ds((r % d4) * rows4 + r // d4, rows16, stride=d16 // d4), :]
                 for r in range(d16)], axis=0)
            lhs_scr[1, :, ct * LANES:(ct + 1) * LANES] = by16.astype(BF16)

    def project(g, lhs_q, lhs_kv):
        wq_ref, wk_ref, wv_ref = w_refs[3 * g:3 * g + 3]
        dil = ATTN_GROUPS[g][1]
        rows = PROJ_ROWS // dil
        q = (_dot(lhs_q, wq_ref[...]) * q_scale).astype(BF16)
        k = _dot(lhs_kv, wk_ref[...]).astype(BF16)
        v = _dot(lhs_kv, wv_ref[...]).astype(BF16)
        for r in range(dil):
            rs = slice(r * rows, (r + 1) * rows)
            out_refs[g][r, :, q_cols] = q[rs]
            out_refs[g][r, :, k_cols] = k[rs]
            out_refs[g][r, :, v_cols] = v[rs]

    lhs = xn.astype(BF16)
    sub = BAND // MIX_STRIDE
    lhs_q = jnp.concatenate(
        [strided_rows(blk * BAND + r4, sub, MIX_STRIDE).astype(BF16)
         for blk in range(PROJ_ROWS // BAND) for r4 in range(MIX_STRIDE)], axis=0)
    rv_ref[...] = _dot(lhs, wrv_ref[...]).astype(BF16)
    residue_major()
    project(0, lhs_q, lhs)
    project(1, lhs_scr[0], lhs_scr[0])
    project(2, lhs_scr[1], lhs_scr[1])


def _qkv_proj(x, lnw, w16):
    B, S, _ = x.shape
    nt = S // PROJ_ROWS
    w_specs = [pl.BlockSpec((D_MODEL, GROUP_WIDTH), lambda b, t, cb=part * N_GROUPS + g: (0, cb),
                            pipeline_mode=RESIDENT)
               for g in range(N_GROUPS) for part in range(3)]
    w_specs.append(pl.BlockSpec((D_MODEL, RET_V_WIDTH), lambda b, t: (0, COL_RV // RET_V_WIDTH),
                                pipeline_mode=RESIDENT))
    return pl.pallas_call(
        _qkv_proj_kernel,
        grid=(B, nt),
        in_specs=[
            pl.BlockSpec((None, PROJ_ROWS, D_MODEL), lambda b, t: (b, t, 0)),
            pl.BlockSpec((1, D_MODEL), lambda b, t: (0, 0)),
            *w_specs,
        ],
        out_specs=[
            *[pl.BlockSpec((None, dil, PROJ_ROWS // dil, QKV_WIDTH), lambda b, t: (b, 0, t, 0))
              for _, dil in ATTN_GROUPS],
            pl.BlockSpec((None, PROJ_ROWS, RET_V_WIDTH), lambda b, t: (b, t, 0)),
        ],
        out_shape=[
            *[jax.ShapeDtypeStruct((B, dil, S // dil, QKV_WIDTH), BF16) for _, dil in ATTN_GROUPS],
            jax.ShapeDtypeStruct((B, S, RET_V_WIDTH), BF16),
        ],
        scratch_shapes=[pltpu.VMEM((D_MODEL // LANES, PROJ_ROWS, LANES), F32),
                        pltpu.VMEM((D_MODEL // LANES, PROJ_ROWS, LANES), F32),
                        pltpu.VMEM((N_GROUPS - 1, PROJ_ROWS, D_MODEL), BF16)],
        compiler_params=pltpu.CompilerParams(
            dimension_semantics=("arbitrary", "arbitrary"), vmem_limit_bytes=VMEM_LIMIT),
        name="qkv_proj",
    )(x, lnw, *([w16] * (3 * N_GROUPS + 1)))


def _swap_lane_pairs(t):
    n = t.shape[-1]
    lane = lax.broadcasted_iota(jnp.int32, t.shape, 1)
    up = pltpu.roll(t, n - 1, axis=1)
    down = pltpu.roll(t, 1, axis=1)
    return jnp.where(lane % 2 == 0, up, down)


def _gate_proj_items(x_ref, lnw_ref, wqk_ref, wg_ref, wrg_ref, wmg_ref,
                     cos_ref, sin_ref, kdec_ref, bg_ref,
                     rq_ref, rkt_ref, ag_ref, rg_ref, mg_ref):
    cache = {}

    def lhs():
        if "xb" not in cache:
            cache["xb"] = _rmsnorm_rows(x_ref[...], lnw_ref[...]).astype(BF16)
        return cache["xb"]

    def rope(t, rows):
        pairs = t.shape[1] // LANES
        cos = jnp.concatenate([cos_ref[rows, :]] * pairs, axis=1)
        sin = jnp.concatenate([sin_ref[rows, :]] * pairs, axis=1)
        return t * cos + _swap_lane_pairs(t) * sin

    def retention_q(cols, rows):
        t = _dot(lhs()[rows], wqk_ref[:, cols])
        yield
        rq_ref[rows, cols] = rope(t, rows).astype(BF16)

    def retention_k(cols, rows):
        t = _dot(lhs()[rows], wqk_ref[:, RET_QK_WIDTH + cols.start:RET_QK_WIDTH + cols.stop])
        yield
        k = rope(t, rows) * kdec_ref[rows, cols]
        for c in range((rows.stop - rows.start) // RET_CHUNK):
            rkt_ref[rows.start // RET_CHUNK + c, cols, :] = (
                k[c * RET_CHUNK:(c + 1) * RET_CHUNK, :].T.astype(BF16))

    def merge_gate(cols, rows):
        t = _dot(lhs()[rows], wmg_ref[:, cols])
        yield
        mg_ref[rows, cols] = _sigmoid(t + bg_ref[:, cols]).astype(BF16)

    def retention_gate(cols, rows):
        t = _dot(lhs()[rows], wrg_ref[:, cols])
        yield
        rg_ref[rows, cols] = _silu(t).astype(BF16)

    def attention_gate(cols, rows):
        t = _dot(lhs()[rows], wg_ref[:, cols])
        yield
        ag_ref[rows, cols] = _silu(t).astype(BF16)

    items = []
    part = GATE_ROWS // GATE_ROW_PARTS
    for fn, total, pieces in ((retention_q, RET_QK_WIDTH, GATE_PIECES[0]),
                              (retention_k, RET_QK_WIDTH, GATE_PIECES[1]),
                              (merge_gate, 2 * D_MODEL, GATE_PIECES[2]),
                              (retention_gate, RET_V_WIDTH, GATE_PIECES[3]),
                              (attention_gate, GROUP_WIDTH, GATE_PIECES[4])):
        width = total // pieces
        items += [lambda fn=fn, cols=slice(i * width, (i + 1) * width),
                  rows=slice(j * part, (j + 1) * part): fn(cols, rows)
                  for i in range(pieces) for j in range(GATE_ROW_PARTS)]
    return items


def _attention_items(in_refs, bias_ref, out_ref, acc_scr, max_scr, den_scr, nat_scr, first):
    ones = jnp.ones((2 * BAND, HEAD_DIM), BF16)
    quarter = ATTN_TILE // MIX_STRIDE

    def unit(g, r, n):
        dil = ATTN_GROUPS[g][1]
        q_ref, kc_ref, kp_ref, vc_ref, vp_ref = in_refs[5 * g:5 * g + 5]
        bias_base = 2 if dil == 1 else 0
        q = q_ref[r, n * BAND:(n + 1) * BAND, :]
        if n == 0:
            k = jnp.concatenate([kp_ref[r], kc_ref[r, 0:BAND, :]], axis=0)
            v = jnp.concatenate([vp_ref[r], vc_ref[r, 0:BAND, :]], axis=0)
            bias = bias_ref[bias_base + first]
        else:
            k = kc_ref[r, (n - 1) * BAND:(n + 1) * BAND, :]
            v = vc_ref[r, (n - 1) * BAND:(n + 1) * BAND, :]
            bias = bias_ref[bias_base]
        s = lax.dot_general(q, k, (((1,), (1,)), ((), ())), preferred_element_type=F32) + bias
        yield
        m = jnp.max(s, axis=-1, keepdims=True)
        p = jnp.exp2(s - m).astype(BF16)
        yield
        res = _dot(p, jnp.concatenate([v, ones], axis=1))
        acc, den = res[:, :HEAD_DIM], res[:, HEAD_DIM:]
        mb = jnp.broadcast_to(m, (BAND, HEAD_DIM))
        if dil == 1:
            sub = BAND // MIX_STRIDE
            for r4 in range(MIX_STRIDE):
                dst = pl.ds(r4 * quarter + n * sub, sub)
                src = slice(r4 * sub, (r4 + 1) * sub)
                acc_scr[g, dst, :] = acc[src]
                max_scr[g, dst, :] = mb[src]
                den_scr[g, dst, :] = den[src]
        else:
            if dil == MIX_STRIDE:
                dst = pl.ds(r * quarter + n * BAND, BAND)
            else:
                sub_stride = dil // MIX_STRIDE
                dst = pl.ds((r % MIX_STRIDE) * quarter + r // MIX_STRIDE
                            + n * BAND * sub_stride, BAND, stride=sub_stride)
            acc_scr[g, dst, :] = acc
            max_scr[g, dst, :] = mb
            den_scr[g, dst, :] = den

    def mix(r4, ch):
        rows = pl.ds(r4 * quarter + ch * BAND, BAND)
        m0, m1, m2 = max_scr[0, rows, :], max_scr[1, rows, :], max_scr[2, rows, :]
        mx = jnp.maximum(jnp.maximum(m0, m1), m2)
        w0 = jnp.exp2(m0 - mx)
        w1 = jnp.exp2(m1 - mx)
        w2 = jnp.exp2(m2 - mx)
        num = w0 * acc_scr[0, rows, :] + w1 * acc_scr[1, rows, :] + w2 * acc_scr[2, rows, :]
        den = w0 * den_scr[0, rows, :] + w1 * den_scr[1, rows, :] + w2 * den_scr[2, rows, :]
        nat_scr[pl.ds(r4 + ch * BAND * MIX_STRIDE, BAND, stride=MIX_STRIDE), :] = num / den

    def write_out():
        out_ref[...] = nat_scr[...].astype(BF16)

    def mix_quarter(r4):
        for ch in range(quarter // BAND):
            mix(r4, ch)

    units = [lambda n=n: unit(0, 0, n) for n in range(ATTN_TILE // BAND)]
    quarter_done = []
    for r4 in range(MIX_STRIDE):
        units += [lambda g=g, r=r, n=n: unit(g, r, n)
                  for g in range(1, N_GROUPS)
                  for r in range(r4, ATTN_GROUPS[g][1], MIX_STRIDE)
                  for n in range(ATTN_TILE // ATTN_GROUPS[g][1] // BAND)]
        quarter_done.append(len(units))
    return units, quarter_done, mix_quarter, write_out


def _attn_gates_kernel(*refs):
    n_attn = 5 * N_GROUPS
    in_refs = refs[:n_attn]
    bias_ref = refs[n_attn]
    proj_in = refs[n_attn + 1:n_attn + 11]
    attn_out = refs[n_attn + 11]
    proj_out = refs[n_attn + 12:n_attn + 17]
    acc_scr, max_scr, den_scr, nat_scr = refs[n_attn + 17:]
    first = (pl.program_id(1) == 0).astype(jnp.int32)
    unit_makers, quarter_done, mix_quarter, write_out = _attention_items(
        in_refs, bias_ref, attn_out, acc_scr, max_scr, den_scr, nat_scr, first)
    proj = _gate_proj_items(*proj_in, *proj_out)
    per_round = len(unit_makers) // len(proj)
    assert per_round * len(proj) == len(unit_makers)
    rounds = [unit_makers[i * per_round:(i + 1) * per_round] for i in range(len(proj))]

    def advance(gen):
        next(gen, None)

    prev_units = []
    mixed = 0
    for r in range(len(rounds) + 1):
        units = [make() for make in rounds[r]] if r < len(rounds) else []
        for j in range(max(len(units), len(prev_units))):
            if j < len(units):
                advance(units[j])
            if j < len(prev_units):
                advance(prev_units[j])
        while mixed < len(quarter_done) and quarter_done[mixed] <= r * per_round:
            mix_quarter(mixed)
            mixed += 1
        if r < len(proj):
            for _ in proj[r]():
                pass
        for u in units:
            advance(u)
        prev_units = units
    write_out()


def _attn_bias_tables():
    rho = jnp.arange(BAND)[:, None]
    c = jnp.arange(2 * BAND)[None, :]
    sub = BAND // MIX_STRIDE
    tables = []
    for a in (rho, MIX_STRIDE * (rho % sub) + rho // sub):
        ok = (c >= a) & (c <= a + BAND)
        tables += [ok, ok & (c >= BAND)]
    return jnp.where(jnp.stack(tables), 0.0, NEG_BIG).astype(F32)


def _attn_gates(qkv_groups, x, lnw, w16, cos_t, sin_t, kdec, bg):
    B, S, _ = x.shape
    nt = S // ATTN_TILE
    cpt = GATE_ROWS // RET_CHUNK
    in_specs = []
    args = []
    for (_, dil), qkv in zip(ATTN_GROUPS, qkv_groups):
        rows = ATTN_TILE // dil
        ratio = rows // BAND

        def cur(col0, rows=rows, dil=dil):
            return pl.BlockSpec((None, dil, rows, HEAD_DIM),
                                lambda b, t, h: (b, 0, t, col0 + h))

        def prev(col0, ratio=ratio, dil=dil):
            return pl.BlockSpec((None, dil, BAND, HEAD_DIM),
                                lambda b, t, h: (b, 0, jnp.maximum(t * ratio - 1, 0), col0 + h))

        in_specs += [cur(0), cur(ATTN_HEADS), prev(ATTN_HEADS),
                     cur(2 * ATTN_HEADS), prev(2 * ATTN_HEADS)]
        args += [qkv] * 5
    bias = _attn_bias_tables()
    in_specs.append(pl.BlockSpec(bias.shape, lambda b, t, h: (0, 0, 0), pipeline_mode=RESIDENT))

    def wspec(n, col0=0):
        return pl.BlockSpec((D_MODEL, n), lambda b, t, h: (0, col0 // n), pipeline_mode=RESIDENT)

    def row_spec(n):
        return pl.BlockSpec((None, GATE_ROWS, n), lambda b, t, h: (b, t * ATTN_HEADS + h, 0))

    tspec = pl.BlockSpec((GATE_ROWS, LANES), lambda b, t, h: (t * ATTN_HEADS + h, 0))
    in_specs += [
        row_spec(D_MODEL),
        pl.BlockSpec((1, D_MODEL), lambda b, t, h: (0, 0)),
        wspec(2 * RET_QK_WIDTH, COL_RQ),
        wspec(GROUP_WIDTH, COL_AG), wspec(RET_V_WIDTH, COL_RG), wspec(2 * D_MODEL, COL_MG),
        tspec, tspec,
        pl.BlockSpec((GATE_ROWS, RET_QK_WIDTH), lambda b, t, h: (0, 0), pipeline_mode=RESIDENT),
        pl.BlockSpec((1, 2 * D_MODEL), lambda b, t, h: (0, 0)),
    ]
    out_widths = (GROUP_WIDTH, RET_V_WIDTH, 2 * D_MODEL)
    per_group = pltpu.VMEM((N_GROUPS, ATTN_TILE, HEAD_DIM), F32)
    return pl.pallas_call(
        _attn_gates_kernel,
        grid=(B, nt, ATTN_HEADS),
        in_specs=in_specs,
        out_specs=[
            pl.BlockSpec((None, ATTN_TILE, HEAD_DIM), lambda b, t, h: (b, t, h)),
            row_spec(RET_QK_WIDTH),
            pl.BlockSpec((None, cpt, RET_QK_WIDTH, RET_CHUNK),
                         lambda b, t, h: (b, t * ATTN_HEADS + h, 0, 0)),
            *[row_spec(n) for n in out_widths],
        ],
        out_shape=[
            jax.ShapeDtypeStruct((B, S, GROUP_WIDTH), BF16),
            jax.ShapeDtypeStruct((B, S, RET_QK_WIDTH), BF16),
            jax.ShapeDtypeStruct((B, S // RET_CHUNK, RET_QK_WIDTH, RET_CHUNK), BF16),
            *[jax.ShapeDtypeStruct((B, S, n), BF16) for n in out_widths],
        ],
        scratch_shapes=[per_group, per_group, per_group,
                        pltpu.VMEM((ATTN_TILE, HEAD_DIM), F32)],
        compiler_params=pltpu.CompilerParams(
            dimension_semantics=("arbitrary", "arbitrary", "arbitrary"),
            vmem_limit_bytes=VMEM_LIMIT),
        name="attn_gates",
    )(*args, bias, x, lnw, w16, w16, w16, w16, cos_t, sin_t, kdec, bg)


def _retention_kernel(q_ref, kt_ref, v_ref, gate_ref, gnw_ref, eps_ref, cd_ref,
                      out_ref, state, p_scr):
    @pl.when(pl.program_id(1) == 0)
    def _():
        state[...] = jnp.zeros_like(state)

    dk, dv, C = RET_KEY_DIM, RET_VALUE_DIM, RET_CHUNK
    row_k = lax.broadcasted_iota(jnp.int32, (2 * dk, C), 0)
    row = lax.broadcasted_iota(jnp.int32, (2 * dk, 2 * dv), 0)
    col = lax.broadcasted_iota(jnp.int32, (C, 2 * dv), 1)
    diag_blk = (row < dk) == (col < dv)
    causal = (col % C) <= lax.broadcasted_iota(jnp.int32, (C, 2 * C), 0)
    units = [(bi, j) for bi in range(RET_BATCH) for j in range(RET_HEADS // 2)]

    def q_pair(bi, j):
        return q_ref[bi, :, j * 2 * dk:(j + 1) * 2 * dk]

    def kt_pair(bi, j):
        return kt_ref[bi, j * 2 * dk:(j + 1) * 2 * dk, :]

    def v_pair(bi, j):
        return v_ref[bi, :, j * 2 * dv:(j + 1) * 2 * dv]

    for bi, j in units:
        kt = kt_pair(bi, j)
        zero = jnp.zeros_like(kt)
        kt_blk = jnp.concatenate([jnp.where(row_k < dk, kt, zero),
                                  jnp.where(row_k >= dk, kt, zero)], axis=1)
        s = _dot(q_pair(bi, j), kt_blk).astype(BF16)
        p_scr[bi, j] = jnp.where(causal, s, jnp.zeros_like(s))

    for bi, j in units:
        v = v_pair(bi, j)
        zero = jnp.zeros_like(v)
        v_blk = jnp.concatenate([jnp.where(col < dv, v, zero),
                                 jnp.where(col >= dv, v, zero)], axis=0)
        u = _dot(jnp.concatenate([p_scr[bi, j], q_pair(bi, j)], axis=1),
                 jnp.concatenate([v_blk, state[bi, j].astype(BF16)], axis=0))
        for hh in range(2):
            h = 2 * j + hh
            hs = slice(h * dv, (h + 1) * dv)
            uh = u[:, hh * dv:(hh + 1) * dv]
            uh = uh * lax.rsqrt(jnp.sum(uh * uh, axis=-1, keepdims=True) + eps_ref[h])
            out_ref[bi, :, hs] = (uh * gnw_ref[:, hs]).astype(BF16) * gate_ref[bi, :, hs]

    for bi, j in units:
        kv = _dot(kt_pair(bi, j), v_pair(bi, j))
        state[bi, j] = cd_ref[j] * (state[bi, j] + jnp.where(diag_blk, kv, 0.0))


def _retention(rq, rkt, rv, gate, gnw, eps_t, cd):
    B, S, _ = rv.shape
    nc = S // RET_CHUNK
    npair = RET_HEADS // 2
    pair_shape = (RET_BATCH, npair, 2 * RET_KEY_DIM, 2 * RET_VALUE_DIM)

    def const(shape):
        return pl.BlockSpec(shape, lambda b, n: (0,) * len(shape))

    def rows(width):
        return pl.BlockSpec((RET_BATCH, RET_CHUNK, width), lambda b, n: (b, n, 0))

    return pl.pallas_call(
        _retention_kernel,
        grid=(B // RET_BATCH, nc),
        in_specs=[
            rows(RET_QK_WIDTH),
            pl.BlockSpec((RET_BATCH, None, RET_QK_WIDTH, RET_CHUNK), lambda b, n: (b, n, 0, 0)),
            rows(RET_V_WIDTH),
            rows(RET_V_WIDTH),
            const((1, RET_V_WIDTH)),
            const((RET_HEADS, RET_CHUNK, RET_VALUE_DIM)),
            const(pair_shape[1:]),
        ],
        out_specs=rows(RET_V_WIDTH),
        out_shape=jax.ShapeDtypeStruct((B, S, RET_V_WIDTH), BF16),
        scratch_shapes=[pltpu.VMEM(pair_shape, F32),
                        pltpu.VMEM((RET_BATCH, npair, RET_CHUNK, 2 * RET_CHUNK), BF16)],
        compiler_params=pltpu.CompilerParams(
            dimension_semantics=("arbitrary", "arbitrary"), vmem_limit_bytes=VMEM_LIMIT),
        name="retention",
    )(rq, rkt, rv, gate, gnw, eps_t, cd)


def _merge_out_kernel(a_ref, ag_ref, r_ref, mg_ref, x_ref, wa_ref, wr_ref, wo_ref, lnf_ref,
                      out_ref):
    def part(rows):
        y_attn = _dot(a_ref[rows, :] * ag_ref[rows, :], wa_ref[...])
        y_ret = _dot(r_ref[rows, :], wr_ref[...])
        yield
        merged = (mg_ref[rows, :D_MODEL].astype(F32) * y_attn
                  + mg_ref[rows, D_MODEL:].astype(F32) * y_ret)
        h = x_ref[rows, :] + _dot(merged.astype(BF16), wo_ref[...])
        yield
        out_ref[rows, :] = _rmsnorm_rows(h, lnf_ref[...])

    n_parts = OUT_ROWS // OUT_PART
    parts = [part(slice(i * OUT_PART, (i + 1) * OUT_PART)) for i in range(n_parts)]
    for step in range(n_parts + 2):
        for lag in range(3):
            i = step - lag
            if 0 <= i < n_parts:
                next(parts[i], None)


def _merge_out(attn, ag, ret_g, mg, x, wa, wr, wo, lnf):
    B, S, _ = x.shape
    nt = S // OUT_ROWS

    def rows(n):
        return pl.BlockSpec((None, OUT_ROWS, n), lambda b, t: (b, t, 0))

    def const(shape):
        return pl.BlockSpec(shape, lambda b, t: (0,) * len(shape), pipeline_mode=RESIDENT)

    return pl.pallas_call(
        _merge_out_kernel,
        grid=(B, nt),
        in_specs=[rows(GROUP_WIDTH), rows(GROUP_WIDTH), rows(RET_V_WIDTH), rows(2 * D_MODEL),
                  rows(D_MODEL),
                  const((GROUP_WIDTH, D_MODEL)), const((RET_V_WIDTH, D_MODEL)),
                  const((D_MODEL, D_MODEL)), const((1, D_MODEL))],
        out_specs=rows(D_MODEL),
        out_shape=jax.ShapeDtypeStruct((B, S, D_MODEL), F32),
        compiler_params=pltpu.CompilerParams(
            dimension_semantics=("arbitrary", "arbitrary"), vmem_limit_bytes=VMEM_LIMIT),
        name="merge_out",
    )(attn, ag, ret_g, mg, x, wa, wr, wo, lnf)


def _rope_tables(S):
    half = RET_KEY_DIM // 2
    inv_freq = ROPE_BASE ** (-jnp.linspace(0.0, 1.0, half, dtype=F32))
    step = 64
    a = (jnp.arange(S // step, dtype=F32) * step)[:, None] * inv_freq[None, :]
    b = jnp.arange(step, dtype=F32)[:, None] * inv_freq[None, :]
    ca, sa, cb, sb = jnp.cos(a)[:, None], jnp.sin(a)[:, None], jnp.cos(b)[None], jnp.sin(b)[None]
    cos = (ca * cb - sa * sb).reshape(S, half)
    sin = (sa * cb + ca * sb).reshape(S, half)
    heads_per_tile = LANES // RET_KEY_DIM
    sign = jnp.where(jnp.arange(LANES) % 2 == 0, -1.0, 1.0).astype(F32)
    cos_t = jnp.tile(jnp.repeat(cos, 2, axis=1), (1, heads_per_tile))
    sin_t = jnp.tile(jnp.repeat(sin, 2, axis=1), (1, heads_per_tile)) * sign[None, :]
    return cos_t, sin_t


def _retention_constants():
    H, C, dk, dv = RET_HEADS, RET_CHUNK, RET_KEY_DIM, RET_VALUE_DIM
    log_gamma = jnp.log(1.0 - 2.0 ** (-5.0 - jnp.arange(H, dtype=F32)))
    idx = jnp.arange(C, dtype=F32)
    inv_decay = jnp.exp(-(idx + 1.0)[None, :] * log_gamma[:, None])
    chunk_decay = jnp.exp(C * log_gamma)
    kdec = jnp.repeat(inv_decay.T, dk, axis=1) * dk ** -0.5
    kdec = jnp.tile(kdec, (GATE_ROWS // C, 1))
    eps_t = jnp.broadcast_to((dv * NORM_EPS * inv_decay * inv_decay)[:, :, None], (H, C, dv))
    cd = jnp.broadcast_to(chunk_decay[:, None, None], (H, dk, 2 * dv)).reshape(H // 2, 2 * dk, 2 * dv)
    return kdec, eps_t, cd


def kernel(x, ln1_w, w_in, b_gate, attn_proj, ret_proj, ret_gn_w, w_out, lnf_w):
    B, S, _ = x.shape
    assert w_in.shape[0] == 1, "single layer"
    w16 = w_in[0].astype(BF16)

    lnw = ln1_w[0].reshape(1, D_MODEL)
    q1, q2, q3, rv = _qkv_proj(x, lnw, w16)
    cos_t, sin_t = _rope_tables(S)
    kdec, eps_t, cd = _retention_constants()
    attn, rq, rkt, ag, rg, mg = _attn_gates(
        (q1, q2, q3), x, lnw, w16, cos_t, sin_t, kdec, b_gate[0].reshape(1, 2 * D_MODEL))
    gnw = (ret_gn_w[0] * RET_VALUE_DIM ** 0.5).reshape(1, RET_V_WIDTH)
    ret_g = _retention(rq, rkt, rv, rg, gnw, eps_t, cd)

    return _merge_out(attn, ag, ret_g, mg, x,
                      attn_proj[0].astype(BF16), ret_proj[0].astype(BF16),
                      w_out[0].astype(BF16), lnf_w.reshape(1, D_MODEL))
```

```python
import jax
import jax.numpy as jnp
from jax import lax
from jax.experimental import pallas as pl
from jax.experimental.pallas import tpu as pltpu

D_MODEL = 1024
ATTN_GROUPS = ((128, 1), (512, 4), (2048, 16))
N_GROUPS = 3
ATTN_HEADS = 4
HEAD_DIM = 128
GROUP_WIDTH = ATTN_HEADS * HEAD_DIM
QKV_WIDTH = N_GROUPS * GROUP_WIDTH
BAND = 128
RET_HEADS = 8
RET_KEY_DIM = 64
RET_VALUE_DIM = 128
RET_QK_WIDTH = RET_HEADS * RET_KEY_DIM
RET_V_WIDTH = RET_HEADS * RET_VALUE_DIM
RET_CHUNK = 128
ROPE_BASE = 10000.0
COL_AG = 3 * QKV_WIDTH
COL_RQ = COL_AG + GROUP_WIDTH
COL_RV = COL_RQ + 2 * RET_QK_WIDTH
COL_RG = COL_RV + RET_V_WIDTH
COL_MG = COL_RG + RET_V_WIDTH
NORM_EPS = 1e-6
NEG_BIG = -1e30
LANES = 128
LOG2E = 1.4426950408889634
MIX_STRIDE = 4

PROJ_ROWS = 1024
ATTN_TILE = 2048
GATE_ROWS = ATTN_TILE // ATTN_HEADS
RET_BATCH = 8
OUT_ROWS = 1024
OUT_PART = 256
V7X_VMEM_BYTES = 64 * 1024 * 1024
VMEM_LIMIT = V7X_VMEM_BYTES * 7 // 8
GATE_ROW_PARTS = 2
GATE_PIECES = (2, 2, 4, 2, 2)
RESIDENT = pl.Buffered(1)

BF16 = jnp.bfloat16
F32 = jnp.float32


def _rmsnorm_rows(x, w):
    return x * lax.rsqrt(jnp.mean(x * x, axis=-1, keepdims=True) + NORM_EPS) * w


def _dot(a, b):
    return jnp.dot(a, b, preferred_element_type=F32)


def _sigmoid(t):
    return 0.5 * jnp.tanh(0.5 * t) + 0.5


def _silu(t):
    return t * _sigmoid(t)


def _qkv_proj_kernel(x_ref, lnw_ref, *refs):
    wq_ref, wk_ref, wv_ref, wrv_ref = refs[:4]
    out_refs = refs[4:4 + N_GROUPS]
    rv_ref, xn_scr, xn4_scr, lhs_scr = refs[4 + N_GROUPS:]
    xn = _rmsnorm_rows(x_ref[...], lnw_ref[...])
    n_lane_tiles = D_MODEL // LANES
    for ct in range(n_lane_tiles):
        xn_scr[ct] = xn[:, ct * LANES:(ct + 1) * LANES]
    q_scale = HEAD_DIM ** -0.5 * LOG2E

    def strided_rows(r, rows, dil):
        return jnp.concatenate(
            [xn_scr[ct, pl.ds(r, rows, stride=dil), :] for ct in range(n_lane_tiles)], axis=1)

    def residue_major():
        d4, d16 = ATTN_GROUPS[1][1], ATTN_GROUPS[2][1]
        rows4, rows16 = PROJ_ROWS // d4, PROJ_ROWS // d16
        for ct in range(n_lane_tiles):
            by4 = jnp.concatenate(
                [xn_scr[ct, pl.ds(r, rows4, stride=d4), :] for r in range(d4)], axis=0)
            xn4_scr[ct] = by4
            lhs_scr[0, :, ct * LANES:(ct + 1) * LANES] = by4.astype(BF16)
        for ct in range(n_lane_tiles):
            by16 = jnp.concatenate(
                [xn4_scr[ct, pl.ds((r % d4) * rows4 + r // d4, rows16, stride=d16 // d4), :]
                 for r in range(d16)], axis=0)
            lhs_scr[1, :, ct * LANES:(ct + 1) * LANES] = by16.astype(BF16)

    def project(g, lhs_q, lhs_kv):
        gcols = slice(g * GROUP_WIDTH, (g + 1) * GROUP_WIDTH)
        dil = ATTN_GROUPS[g][1]
        rows = PROJ_ROWS // dil
        q = (_dot(lhs_q, wq_ref[:, gcols]) * q_scale).astype(BF16)
        k = _dot(lhs_kv, wk_ref[:, gcols]).astype(BF16)
        v = _dot(lhs_kv, wv_ref[:, gcols]).astype(BF16)
        for r in range(dil):
            rs = slice(r * rows, (r + 1) * rows)
            for h in range(ATTN_HEADS):
                hs = slice(h * HEAD_DIM, (h + 1) * HEAD_DIM)
                for part, val in enumerate((q, k, v)):
                    c0 = (3 * h + part) * HEAD_DIM
                    out_refs[g][r, :, c0:c0 + HEAD_DIM] = val[rs, hs]

    lhs = xn.astype(BF16)
    sub = BAND // MIX_STRIDE
    lhs_q = jnp.concatenate(
        [strided_rows(blk * BAND + r4, sub, MIX_STRIDE).astype(BF16)
         for blk in range(PROJ_ROWS // BAND) for r4 in range(MIX_STRIDE)], axis=0)
    rv_ref[...] = _dot(lhs, wrv_ref[...]).astype(BF16)
    residue_major()
    project(0, lhs_q, lhs)
    project(1, lhs_scr[0], lhs_scr[0])
    project(2, lhs_scr[1], lhs_scr[1])


def _qkv_proj(x, lnw, w16):
    B, S, _ = x.shape
    nt = S // PROJ_ROWS
    w_specs = [pl.BlockSpec((D_MODEL, QKV_WIDTH), lambda b, t, part=part: (0, part),
                            pipeline_mode=RESIDENT) for part in range(3)]
    w_specs.append(pl.BlockSpec((D_MODEL, RET_V_WIDTH), lambda b, t: (0, COL_RV // RET_V_WIDTH),
                                pipeline_mode=RESIDENT))
    return pl.pallas_call(
        _qkv_proj_kernel,
        grid=(B, nt),
        in_specs=[
            pl.BlockSpec((None, PROJ_ROWS, D_MODEL), lambda b, t: (b, t, 0)),
            pl.BlockSpec((1, D_MODEL), lambda b, t: (0, 0)),
            *w_specs,
        ],
        out_specs=[
            *[pl.BlockSpec((None, dil, PROJ_ROWS // dil, QKV_WIDTH), lambda b, t: (b, 0, t, 0))
              for _, dil in ATTN_GROUPS],
            pl.BlockSpec((None, PROJ_ROWS, RET_V_WIDTH), lambda b, t: (b, t, 0)),
        ],
        out_shape=[
            *[jax.ShapeDtypeStruct((B, dil, S // dil, QKV_WIDTH), BF16) for _, dil in ATTN_GROUPS],
            jax.ShapeDtypeStruct((B, S, RET_V_WIDTH), BF16),
        ],
        scratch_shapes=[pltpu.VMEM((D_MODEL // LANES, PROJ_ROWS, LANES), F32),
                        pltpu.VMEM((D_MODEL // LANES, PROJ_ROWS, LANES), F32),
                        pltpu.VMEM((N_GROUPS - 1, PROJ_ROWS, D_MODEL), BF16)],
        compiler_params=pltpu.CompilerParams(
            dimension_semantics=("arbitrary", "arbitrary"), vmem_limit_bytes=VMEM_LIMIT),
        name="qkv_proj",
    )(x, lnw, w16, w16, w16, w16)


def _swap_lane_pairs(t):
    n = t.shape[-1]
    lane = lax.broadcasted_iota(jnp.int32, t.shape, 1)
    up = pltpu.roll(t, n - 1, axis=1)
    down = pltpu.roll(t, 1, axis=1)
    return jnp.where(lane % 2 == 0, up, down)


def _gate_proj_items(x_ref, lnw_ref, wqk_ref, wg_ref, wrg_ref, wmg_ref,
                     cos_ref, sin_ref, kdec_ref, bg_ref,
                     rq_ref, rkt_ref, ag_ref, rg_ref, mg_ref):
    cache = {}

    def lhs():
        if "xb" not in cache:
            cache["xb"] = _rmsnorm_rows(x_ref[...], lnw_ref[...]).astype(BF16)
        return cache["xb"]

    def rope(t, rows):
        pairs = t.shape[1] // LANES
        cos = jnp.concatenate([cos_ref[rows, :]] * pairs, axis=1)
        sin = jnp.concatenate([sin_ref[rows, :]] * pairs, axis=1)
        return t * cos + _swap_lane_pairs(t) * sin

    def retention_q(cols, rows):
        t = _dot(lhs()[rows], wqk_ref[:, cols])
        yield
        rq_ref[rows, cols] = rope(t, rows).astype(BF16)

    def retention_k(cols, rows):
        t = _dot(lhs()[rows], wqk_ref[:, RET_QK_WIDTH + cols.start:RET_QK_WIDTH + cols.stop])
        yield
        k = rope(t, rows) * kdec_ref[rows, cols]
        for c in range((rows.stop - rows.start) // RET_CHUNK):
            rkt_ref[rows.start // RET_CHUNK + c, cols, :] = (
                k[c * RET_CHUNK:(c + 1) * RET_CHUNK, :].T.astype(BF16))

    def merge_gate(cols, rows):
        t = _dot(lhs()[rows], wmg_ref[:, cols])
        yield
        mg_ref[rows, cols] = _sigmoid(t + bg_ref[:, cols]).astype(BF16)

    def retention_gate(cols, rows):
        t = _dot(lhs()[rows], wrg_ref[:, cols])
        yield
        rg_ref[rows, cols] = _silu(t).astype(BF16)

    def attention_gate(cols, rows):
        t = _dot(lhs()[rows], wg_ref[:, cols])
        yield
        ag_ref[rows, cols] = _silu(t).astype(BF16)

    items = []
    part = GATE_ROWS // GATE_ROW_PARTS
    for fn, total, pieces in ((retention_q, RET_QK_WIDTH, GATE_PIECES[0]),
                              (retention_k, RET_QK_WIDTH, GATE_PIECES[1]),
                              (merge_gate, 2 * D_MODEL, GATE_PIECES[2]),
                              (retention_gate, RET_V_WIDTH, GATE_PIECES[3]),
                              (attention_gate, GROUP_WIDTH, GATE_PIECES[4])):
        width = total // pieces
        items += [lambda fn=fn, cols=slice(i * width, (i + 1) * width),
                  rows=slice(j * part, (j + 1) * part): fn(cols, rows)
                  for i in range(pieces) for j in range(GATE_ROW_PARTS)]
    return items


def _attention_items(in_refs, bias_ref, out_ref, acc_scr, lse_scr, nat_scr, first):
    ones = jnp.ones((2 * BAND, HEAD_DIM), BF16)
    quarter = ATTN_TILE // MIX_STRIDE

    def unit(g, r, n):
        dil = ATTN_GROUPS[g][1]
        cur_ref, prev_ref = in_refs[2 * g:2 * g + 2]
        q_cols, k_cols, v_cols = (slice(i * HEAD_DIM, (i + 1) * HEAD_DIM) for i in range(3))
        bias_base = 2 if dil == 1 else 0
        q = cur_ref[r, n * BAND:(n + 1) * BAND, q_cols]
        if n == 0:
            k = jnp.concatenate([prev_ref[r, :, k_cols], cur_ref[r, 0:BAND, k_cols]], axis=0)
            v = jnp.concatenate([prev_ref[r, :, v_cols], cur_ref[r, 0:BAND, v_cols]], axis=0)
            bias = bias_ref[bias_base + first]
        else:
            k = cur_ref[r, (n - 1) * BAND:(n + 1) * BAND, k_cols]
            v = cur_ref[r, (n - 1) * BAND:(n + 1) * BAND, v_cols]
            bias = bias_ref[bias_base]
        s = lax.dot_general(q, k, (((1,), (1,)), ((), ())), preferred_element_type=F32) + bias
        yield
        m = jnp.max(s, axis=-1, keepdims=True)
        p = jnp.exp2(s - m).astype(BF16)
        yield
        res = _dot(p, jnp.concatenate([v, ones], axis=1))
        den = res[:, HEAD_DIM:]
        acc = res[:, :HEAD_DIM] / den
        mb = m + jnp.log2(den)
        if dil == 1:
            sub = BAND // MIX_STRIDE
            for r4 in range(MIX_STRIDE):
                dst = pl.ds(r4 * quarter + n * sub, sub)
                src = slice(r4 * sub, (r4 + 1) * sub)
                acc_scr[g, dst, :] = acc[src]
                lse_scr[g, dst, :] = mb[src]
        else:
            if dil == MIX_STRIDE:
                dst = pl.ds(r * quarter + n * BAND, BAND)
            else:
                sub_stride = dil // MIX_STRIDE
                dst = pl.ds((r % MIX_STRIDE) * quarter + r // MIX_STRIDE
                            + n * BAND * sub_stride, BAND, stride=sub_stride)
            acc_scr[g, dst, :] = acc
            lse_scr[g, dst, :] = mb

    def mix(r4, ch):
        rows = pl.ds(r4 * quarter + ch * BAND, BAND)
        l0, l1, l2 = lse_scr[0, rows, :], lse_scr[1, rows, :], lse_scr[2, rows, :]
        mx = jnp.maximum(jnp.maximum(l0, l1), l2)
        w0 = jnp.exp2(l0 - mx)
        w1 = jnp.exp2(l1 - mx)
        w2 = jnp.exp2(l2 - mx)
        num = w0 * acc_scr[0, rows, :] + w1 * acc_scr[1, rows, :] + w2 * acc_scr[2, rows, :]
        den = w0 + w1 + w2
        nat_scr[pl.ds(r4 + ch * BAND * MIX_STRIDE, BAND, stride=MIX_STRIDE), :] = num / den

    def write_out():
        out_ref[...] = nat_scr[...].astype(BF16)

    def mix_quarter(r4):
        for ch in range(quarter // BAND):
            mix(r4, ch)

    units = [lambda n=n: unit(0, 0, n) for n in range(ATTN_TILE // BAND)]
    quarter_done = []
    for r4 in range(MIX_STRIDE):
        units += [lambda g=g, r=r, n=n: unit(g, r, n)
                  for g in range(1, N_GROUPS)
                  for r in range(r4, ATTN_GROUPS[g][1], MIX_STRIDE)
                  for n in range(ATTN_TILE // ATTN_GROUPS[g][1] // BAND)]
        quarter_done.append(len(units))
    return units, quarter_done, mix_quarter, write_out


def _attn_gates_kernel(*refs):
    n_attn = 2 * N_GROUPS
    in_refs = refs[:n_attn]
    bias_ref = refs[n_attn]
    proj_in = refs[n_attn + 1:n_attn + 11]
    attn_out = refs[n_attn + 11]
    proj_out = refs[n_attn + 12:n_attn + 17]
    acc_scr, lse_scr, nat_scr = refs[n_attn + 17:]
    first = (pl.program_id(1) == 0).astype(jnp.int32)
    unit_makers, quarter_done, mix_quarter, write_out = _attention_items(
        in_refs, bias_ref, attn_out, acc_scr, lse_scr, nat_scr, first)
    proj = _gate_proj_items(*proj_in, *proj_out)
    per_round = len(unit_makers) // len(proj)
    assert per_round * len(proj) == len(unit_makers)
    rounds = [unit_makers[i * per_round:(i + 1) * per_round] for i in range(len(proj))]

    def advance(gen):
        next(gen, None)

    prev_units = []
    mixed = 0
    for r in range(len(rounds) + 1):
        units = [make() for make in rounds[r]] if r < len(rounds) else []
        for j in range(max(len(units), len(prev_units))):
            if j < len(units):
                advance(units[j])
            if j < len(prev_units):
                advance(prev_units[j])
        while mixed < len(quarter_done) and quarter_done[mixed] <= r * per_round:
            mix_quarter(mixed)
            mixed += 1
        if r < len(proj):
            for _ in proj[r]():
                pass
        for u in units:
            advance(u)
        prev_units = units
    write_out()


def _attn_bias_tables():
    rho = jnp.arange(BAND)[:, None]
    c = jnp.arange(2 * BAND)[None, :]
    sub = BAND // MIX_STRIDE
    tables = []
    for a in (rho, MIX_STRIDE * (rho % sub) + rho // sub):
        ok = (c >= a) & (c <= a + BAND)
        tables += [ok, ok & (c >= BAND)]
    return jnp.where(jnp.stack(tables), 0.0, NEG_BIG).astype(F32)


def _attn_gates(qkv_groups, x, lnw, w16, cos_t, sin_t, kdec, bg):
    B, S, _ = x.shape
    nt = S // ATTN_TILE
    cpt = GATE_ROWS // RET_CHUNK
    in_specs = []
    args = []
    for (_, dil), qkv in zip(ATTN_GROUPS, qkv_groups):
        rows = ATTN_TILE // dil
        ratio = rows // BAND

        in_specs += [
            pl.BlockSpec((None, dil, rows, 3 * HEAD_DIM), lambda b, t, h: (b, 0, t, h)),
            pl.BlockSpec((None, dil, BAND, 3 * HEAD_DIM),
                         lambda b, t, h, ratio=ratio: (b, 0, jnp.maximum(t * ratio - 1, 0), h)),
        ]
        args += [qkv] * 2
    bias = _attn_bias_tables()
    in_specs.append(pl.BlockSpec(bias.shape, lambda b, t, h: (0, 0, 0), pipeline_mode=RESIDENT))

    def wspec(n, col0=0):
        return pl.BlockSpec((D_MODEL, n), lambda b, t, h: (0, col0 // n), pipeline_mode=RESIDENT)

    def row_spec(n):
        return pl.BlockSpec((None, GATE_ROWS, n), lambda b, t, h: (b, t * ATTN_HEADS + h, 0))

    tspec = pl.BlockSpec((GATE_ROWS, LANES), lambda b, t, h: (t * ATTN_HEADS + h, 0))
    in_specs += [
        row_spec(D_MODEL),
        pl.BlockSpec((1, D_MODEL), lambda b, t, h: (0, 0)),
        wspec(2 * RET_QK_WIDTH, COL_RQ),
        wspec(GROUP_WIDTH, COL_AG), wspec(RET_V_WIDTH, COL_RG), wspec(2 * D_MODEL, COL_MG),
        tspec, tspec,
        pl.BlockSpec((GATE_ROWS, RET_QK_WIDTH), lambda b, t, h: (0, 0), pipeline_mode=RESIDENT),
        pl.BlockSpec((1, 2 * D_MODEL), lambda b, t, h: (0, 0)),
    ]
    out_widths = (GROUP_WIDTH, RET_V_WIDTH, 2 * D_MODEL)
    per_group = pltpu.VMEM((N_GROUPS, ATTN_TILE, HEAD_DIM), F32)
    return pl.pallas_call(
        _attn_gates_kernel,
        grid=(B, nt, ATTN_HEADS),
        in_specs=in_specs,
        out_specs=[
            pl.BlockSpec((None, ATTN_TILE, HEAD_DIM), lambda b, t, h: (b, t, h)),
            row_spec(RET_QK_WIDTH),
            pl.BlockSpec((None, cpt, RET_QK_WIDTH, RET_CHUNK),
                         lambda b, t, h: (b, t * ATTN_HEADS + h, 0, 0)),
            *[row_spec(n) for n in out_widths],
        ],
        out_shape=[
            jax.ShapeDtypeStruct((B, S, GROUP_WIDTH), BF16),
            jax.ShapeDtypeStruct((B, S, RET_QK_WIDTH), BF16),
            jax.ShapeDtypeStruct((B, S // RET_CHUNK, RET_QK_WIDTH, RET_CHUNK), BF16),
            *[jax.ShapeDtypeStruct((B, S, n), BF16) for n in out_widths],
        ],
        scratch_shapes=[per_group, per_group,
                        pltpu.VMEM((ATTN_TILE, HEAD_DIM), F32)],
        compiler_params=pltpu.CompilerParams(
            dimension_semantics=("arbitrary", "arbitrary", "arbitrary"),
            vmem_limit_bytes=VMEM_LIMIT),
        name="attn_gates",
    )(*args, bias, x, lnw, w16, w16, w16, w16, cos_t, sin_t, kdec, bg)


def _retention_kernel(q_ref, kt_ref, v_ref, gate_ref, gnw_ref, eps_ref, cd_ref,
                      out_ref, state, p_scr):
    @pl.when(pl.program_id(1) == 0)
    def _():
        state[...] = jnp.zeros_like(state)

    dk, dv, C = RET_KEY_DIM, RET_VALUE_DIM, RET_CHUNK
    row_k = lax.broadcasted_iota(jnp.int32, (2 * dk, C), 0)
    row = lax.broadcasted_iota(jnp.int32, (2 * dk, 2 * dv), 0)
    col = lax.broadcasted_iota(jnp.int32, (C, 2 * dv), 1)
    diag_blk = (row < dk) == (col < dv)
    causal = (col % C) <= lax.broadcasted_iota(jnp.int32, (C, 2 * C), 0)
    units = [(bi, j) for bi in range(RET_BATCH) for j in range(RET_HEADS // 2)]

    def q_pair(bi, j):
        return q_ref[bi, :, j * 2 * dk:(j + 1) * 2 * dk]

    def kt_pair(bi, j):
        return kt_ref[bi, j * 2 * dk:(j + 1) * 2 * dk, :]

    def v_pair(bi, j):
        return v_ref[bi, :, j * 2 * dv:(j + 1) * 2 * dv]

    for bi, j in units:
        kt = kt_pair(bi, j)
        zero = jnp.zeros_like(kt)
        kt_blk = jnp.concatenate([jnp.where(row_k < dk, kt, zero),
                                  jnp.where(row_k >= dk, kt, zero)], axis=1)
        s = _dot(q_pair(bi, j), kt_blk).astype(BF16)
        p_scr[bi, j] = jnp.where(causal, s, jnp.zeros_like(s))

    for bi, j in units:
        v = v_pair(bi, j)
        zero = jnp.zeros_like(v)
        v_blk = jnp.concatenate([jnp.where(col < dv, v, zero),
                                 jnp.where(col >= dv, v, zero)], axis=0)
        u = _dot(jnp.concatenate([p_scr[bi, j], q_pair(bi, j)], axis=1),
                 jnp.concatenate([v_blk, state[bi, j].astype(BF16)], axis=0))
        for hh in range(2):
            h = 2 * j + hh
            hs = slice(h * dv, (h + 1) * dv)
            uh = u[:, hh * dv:(hh + 1) * dv]
            uh = uh * lax.rsqrt(jnp.sum(uh * uh, axis=-1, keepdims=True) + eps_ref[h])
            out_ref[bi, :, hs] = (uh * gnw_ref[:, hs]).astype(BF16) * gate_ref[bi, :, hs]

    for bi, j in units:
        kv = _dot(kt_pair(bi, j), v_pair(bi, j))
        state[bi, j] = cd_ref[j] * (state[bi, j] + jnp.where(diag_blk, kv, 0.0))


def _retention(rq, rkt, rv, gate, gnw, eps_t, cd):
    B, S, _ = rv.shape
    nc = S // RET_CHUNK
    npair = RET_HEADS // 2
    pair_shape = (RET_BATCH, npair, 2 * RET_KEY_DIM, 2 * RET_VALUE_DIM)

    def const(shape):
        return pl.BlockSpec(shape, lambda b, n: (0,) * len(shape))

    def rows(width):
        return pl.BlockSpec((RET_BATCH, RET_CHUNK, width), lambda b, n: (b, n, 0))

    return pl.pallas_call(
        _retention_kernel,
        grid=(B // RET_BATCH, nc),
        in_specs=[
            rows(RET_QK_WIDTH),
            pl.BlockSpec((RET_BATCH, None, RET_QK_WIDTH, RET_CHUNK), lambda b, n: (b, n, 0, 0)),
            rows(RET_V_WIDTH),
            rows(RET_V_WIDTH),
            const((1, RET_V_WIDTH)),
            const((RET_HEADS, RET_CHUNK, RET_VALUE_DIM)),
            const(pair_shape[1:]),
        ],
        out_specs=rows(RET_V_WIDTH),
        out_shape=jax.ShapeDtypeStruct((B, S, RET_V_WIDTH), BF16),
        scratch_shapes=[pltpu.VMEM(pair_shape, F32),
                        pltpu.VMEM((RET_BATCH, npair, RET_CHUNK, 2 * RET_CHUNK), BF16)],
        compiler_params=pltpu.CompilerParams(
            dimension_semantics=("arbitrary", "arbitrary"), vmem_limit_bytes=VMEM_LIMIT),
        name="retention",
    )(rq, rkt, rv, gate, gnw, eps_t, cd)


def _merge_out_kernel(a_ref, ag_ref, r_ref, mg_ref, x_ref, wa_ref, wr_ref, wo_ref, lnf_ref,
                      out_ref):
    def part(rows):
        y_attn = _dot(a_ref[rows, :] * ag_ref[rows, :], wa_ref[...])
        y_ret = _dot(r_ref[rows, :], wr_ref[...])
        yield
        merged = (mg_ref[rows, :D_MODEL].astype(F32) * y_attn
                  + mg_ref[rows, D_MODEL:].astype(F32) * y_ret)
        h = x_ref[rows, :] + _dot(merged.astype(BF16), wo_ref[...])
        yield
        out_ref[rows, :] = _rmsnorm_rows(h, lnf_ref[...])

    n_parts = OUT_ROWS // OUT_PART
    parts = [part(slice(i * OUT_PART, (i + 1) * OUT_PART)) for i in range(n_parts)]
    for step in range(n_parts + 2):
        for lag in range(3):
            i = step - lag
            if 0 <= i < n_parts:
                next(parts[i], None)


def _merge_out(attn, ag, ret_g, mg, x, wa, wr, wo, lnf):
    B, S, _ = x.shape
    nt = S // OUT_ROWS

    def rows(n):
        return pl.BlockSpec((None, OUT_ROWS, n), lambda b, t: (b, t, 0))

    def const(shape):
        return pl.BlockSpec(shape, lambda b, t: (0,) * len(shape), pipeline_mode=RESIDENT)

    return pl.pallas_call(
        _merge_out_kernel,
        grid=(B, nt),
        in_specs=[rows(GROUP_WIDTH), rows(GROUP_WIDTH), rows(RET_V_WIDTH), rows(2 * D_MODEL),
                  rows(D_MODEL),
                  const((GROUP_WIDTH, D_MODEL)), const((RET_V_WIDTH, D_MODEL)),
                  const((D_MODEL, D_MODEL)), const((1, D_MODEL))],
        out_specs=rows(D_MODEL),
        out_shape=jax.ShapeDtypeStruct((B, S, D_MODEL), F32),
        compiler_params=pltpu.CompilerParams(
            dimension_semantics=("arbitrary", "arbitrary"), vmem_limit_bytes=VMEM_LIMIT),
        name="merge_out",
    )(attn, ag, ret_g, mg, x, wa, wr, wo, lnf)


def _rope_tables(S):
    half = RET_KEY_DIM // 2
    inv_freq = ROPE_BASE ** (-jnp.linspace(0.0, 1.0, half, dtype=F32))
    step = 64
    a = (jnp.arange(S // step, dtype=F32) * step)[:, None] * inv_freq[None, :]
    b = jnp.arange(step, dtype=F32)[:, None] * inv_freq[None, :]
    ca, sa, cb, sb = jnp.cos(a)[:, None], jnp.sin(a)[:, None], jnp.cos(b)[None], jnp.sin(b)[None]
    cos = (ca * cb - sa * sb).reshape(S, half)
    sin = (sa * cb + ca * sb).reshape(S, half)
    heads_per_tile = LANES // RET_KEY_DIM
    sign = jnp.where(jnp.arange(LANES) % 2 == 0, -1.0, 1.0).astype(F32)
    cos_t = jnp.tile(jnp.repeat(cos, 2, axis=1), (1, heads_per_tile))
    sin_t = jnp.tile(jnp.repeat(sin, 2, axis=1), (1, heads_per_tile)) * sign[None, :]
    return cos_t, sin_t


def _retention_constants():
    H, C, dk, dv = RET_HEADS, RET_CHUNK, RET_KEY_DIM, RET_VALUE_DIM
    log_gamma = jnp.log(1.0 - 2.0 ** (-5.0 - jnp.arange(H, dtype=F32)))
    idx = jnp.arange(C, dtype=F32)
    inv_decay = jnp.exp(-(idx + 1.0)[None, :] * log_gamma[:, None])
    chunk_decay = jnp.exp(C * log_gamma)
    kdec = jnp.repeat(inv_decay.T, dk, axis=1) * dk ** -0.5
    kdec = jnp.tile(kdec, (GATE_ROWS // C, 1))
    eps_t = jnp.broadcast_to((dv * NORM_EPS * inv_decay * inv_decay)[:, :, None], (H, C, dv))
    cd = jnp.broadcast_to(chunk_decay[:, None, None], (H, dk, 2 * dv)).reshape(H // 2, 2 * dk, 2 * dv)
    return kdec, eps_t, cd


def kernel(x, ln1_w, w_in, b_gate, attn_proj, ret_proj, ret_gn_w, w_out, lnf_w):
    B, S, _ = x.shape
    assert w_in.shape[0] == 1, "single layer"
    w16 = w_in[0].astype(BF16)

    lnw = ln1_w[0].reshape(1, D_MODEL)
    q1, q2, q3, rv = _qkv_proj(x, lnw, w16)
    cos_t, sin_t = _rope_tables(S)
    kdec, eps_t, cd = _retention_constants()
    attn, rq, rkt, ag, rg, mg = _attn_gates(
        (q1, q2, q3), x, lnw, w16, cos_t, sin_t, kdec, b_gate[0].reshape(1, 2 * D_MODEL))
    gnw = (ret_gn_w[0] * RET_VALUE_DIM ** 0.5).reshape(1, RET_V_WIDTH)
    ret_g = _retention(rq, rkt, rv, rg, gnw, eps_t, cd)

    return _merge_out(attn, ag, ret_g, mg, x,
                      attn_proj[0].astype(BF16), ret_proj[0].astype(BF16),
                      w_out[0].astype(BF16), lnf_w.reshape(1, D_MODEL))
```

```python
import jax
import jax.numpy as jnp
from jax import lax
from jax.experimental import pallas as pl
from jax.experimental.pallas import tpu as pltpu

D_MODEL = 1024
ATTN_GROUPS = ((128, 1), (512, 4), (2048, 16))
N_GROUPS = 3
ATTN_HEADS = 4
HEAD_DIM = 128
GROUP_WIDTH = ATTN_HEADS * HEAD_DIM
QKV_WIDTH = N_GROUPS * GROUP_WIDTH
BAND = 128
RET_HEADS = 8
RET_KEY_DIM = 64
RET_VALUE_DIM = 128
RET_QK_WIDTH = RET_HEADS * RET_KEY_DIM
RET_V_WIDTH = RET_HEADS * RET_VALUE_DIM
RET_CHUNK = 128
ROPE_BASE = 10000.0
COL_AG = 3 * QKV_WIDTH
COL_RQ = COL_AG + GROUP_WIDTH
COL_RV = COL_RQ + 2 * RET_QK_WIDTH
COL_RG = COL_RV + RET_V_WIDTH
COL_MG = COL_RG + RET_V_WIDTH
NORM_EPS = 1e-6
NEG_BIG = -1e30
LANES = 128
LOG2E = 1.4426950408889634
MIX_STRIDE = 4

PROJ_ROWS = 1024
QKV_PARTS = 4
ATTN_TILE = 2048
GATE_ROWS = ATTN_TILE // ATTN_HEADS
RET_BATCH = 8
OUT_ROWS = 1024
OUT_PART = 256
V7X_VMEM_BYTES = 64 * 1024 * 1024
VMEM_LIMIT = V7X_VMEM_BYTES * 7 // 8
GATE_ROW_PARTS = 2
GATE_PIECES = (2, 2, 4, 2, 2)
RESIDENT = pl.Buffered(1)

BF16 = jnp.bfloat16
F32 = jnp.float32


def _rmsnorm_rows(x, w):
    return x * lax.rsqrt(jnp.mean(x * x, axis=-1, keepdims=True) + NORM_EPS) * w


def _dot(a, b):
    return jnp.dot(a, b, preferred_element_type=F32)


def _sigmoid(t):
    return 0.5 * jnp.tanh(0.5 * t) + 0.5


def _silu(t):
    return t * _sigmoid(t)


def _qkv_proj_kernel(x_ref, lnw_ref, *refs):
    wq_ref, wk_ref, wv_ref, wrv_ref = refs[:4]
    out_refs = refs[4:4 + N_GROUPS]
    rv_ref, xn_scr, xn4_scr, lhs_scr = refs[4 + N_GROUPS:]
    n_lane_tiles = D_MODEL // LANES
    q_scale = HEAD_DIM ** -0.5 * LOG2E
    part_rows = PROJ_ROWS // QKV_PARTS
    d4, d16 = ATTN_GROUPS[1][1], ATTN_GROUPS[2][1]
    rows4, rows16 = part_rows // d4, part_rows // d16
    NAT, QPERM, BY4, BY16 = range(4)

    def part(p):
        rows_p = slice(p * part_rows, (p + 1) * part_rows)
        lhs = lhs_scr.at[p]
        xn = _rmsnorm_rows(x_ref[rows_p, :], lnw_ref[...])
        lhs[NAT] = xn.astype(BF16)
        for ct in range(n_lane_tiles):
            xn_scr[ct] = xn[:, ct * LANES:(ct + 1) * LANES]
        sub = BAND // MIX_STRIDE
        for ct in range(n_lane_tiles):
            lanes = slice(ct * LANES, (ct + 1) * LANES)
            lhs[QPERM, :, lanes] = jnp.concatenate(
                [xn_scr[ct, pl.ds(blk * BAND + r4, sub, stride=MIX_STRIDE), :]
                 for blk in range(part_rows // BAND) for r4 in range(MIX_STRIDE)],
                axis=0).astype(BF16)
            by4 = jnp.concatenate(
                [xn_scr[ct, pl.ds(r, rows4, stride=d4), :] for r in range(d4)], axis=0)
            xn4_scr[ct] = by4
            lhs[BY4, :, lanes] = by4.astype(BF16)
        for ct in range(n_lane_tiles):
            by16 = jnp.concatenate(
                [xn4_scr[ct, pl.ds((r % d4) * rows4 + r // d4, rows16, stride=d16 // d4), :]
                 for r in range(d16)], axis=0)
            lhs[BY16, :, ct * LANES:(ct + 1) * LANES] = by16.astype(BF16)
        yield

        rv_ref[rows_p, :] = _dot(lhs[NAT], wrv_ref[...]).astype(BF16)
        for g, (order_q, order_kv) in enumerate(((QPERM, NAT), (BY4, BY4), (BY16, BY16))):
            gcols = slice(g * GROUP_WIDTH, (g + 1) * GROUP_WIDTH)
            dil = ATTN_GROUPS[g][1]
            rows = part_rows // dil
            q = (_dot(lhs[order_q], wq_ref[:, gcols]) * q_scale).astype(BF16)
            k = _dot(lhs[order_kv], wk_ref[:, gcols]).astype(BF16)
            v = _dot(lhs[order_kv], wv_ref[:, gcols]).astype(BF16)
            for r in range(dil):
                src_rows = slice(r * rows, (r + 1) * rows)
                dst_rows = slice(p * rows, (p + 1) * rows)
                for h in range(ATTN_HEADS):
                    hs = slice(h * HEAD_DIM, (h + 1) * HEAD_DIM)
                    for which, val in enumerate((q, k, v)):
                        c0 = (3 * h + which) * HEAD_DIM
                        out_refs[g][r, dst_rows, c0:c0 + HEAD_DIM] = val[src_rows, hs]

    parts = [part(p) for p in range(QKV_PARTS)]
    for gen in parts:
        next(gen)
    for gen in parts:
        next(gen, None)


def _qkv_proj(x, lnw, w16):
    B, S, _ = x.shape
    nt = S // PROJ_ROWS
    w_specs = [pl.BlockSpec((D_MODEL, QKV_WIDTH), lambda b, t, part=part: (0, part),
                            pipeline_mode=RESIDENT) for part in range(3)]
    w_specs.append(pl.BlockSpec((D_MODEL, RET_V_WIDTH), lambda b, t: (0, COL_RV // RET_V_WIDTH),
                                pipeline_mode=RESIDENT))
    return pl.pallas_call(
        _qkv_proj_kernel,
        grid=(B, nt),
        in_specs=[
            pl.BlockSpec((None, PROJ_ROWS, D_MODEL), lambda b, t: (b, t, 0)),
            pl.BlockSpec((1, D_MODEL), lambda b, t: (0, 0)),
            *w_specs,
        ],
        out_specs=[
            *[pl.BlockSpec((None, dil, PROJ_ROWS // dil, QKV_WIDTH), lambda b, t: (b, 0, t, 0))
              for _, dil in ATTN_GROUPS],
            pl.BlockSpec((None, PROJ_ROWS, RET_V_WIDTH), lambda b, t: (b, t, 0)),
        ],
        out_shape=[
            *[jax.ShapeDtypeStruct((B, dil, S // dil, QKV_WIDTH), BF16) for _, dil in ATTN_GROUPS],
            jax.ShapeDtypeStruct((B, S, RET_V_WIDTH), BF16),
        ],
        scratch_shapes=[pltpu.VMEM((D_MODEL // LANES, PROJ_ROWS // QKV_PARTS, LANES), F32),
                        pltpu.VMEM((D_MODEL // LANES, PROJ_ROWS // QKV_PARTS, LANES), F32),
                        pltpu.VMEM((QKV_PARTS, 4, PROJ_ROWS // QKV_PARTS, D_MODEL), BF16)],
        compiler_params=pltpu.CompilerParams(
            dimension_semantics=("arbitrary", "arbitrary"), vmem_limit_bytes=VMEM_LIMIT),
        name="qkv_proj",
    )(x, lnw, w16, w16, w16, w16)


def _swap_lane_pairs(t):
    n = t.shape[-1]
    lane = lax.broadcasted_iota(jnp.int32, t.shape, 1)
    up = pltpu.roll(t, n - 1, axis=1)
    down = pltpu.roll(t, 1, axis=1)
    return jnp.where(lane % 2 == 0, up, down)


def _gate_proj_items(x_ref, lnw_ref, wqk_ref, wg_ref, wrg_ref, wmg_ref,
                     cos_ref, sin_ref, kdec_ref, bg_ref,
                     rq_ref, rkt_ref, ag_ref, rg_ref, mg_ref):
    cache = {}

    def lhs():
        if "xb" not in cache:
            cache["xb"] = _rmsnorm_rows(x_ref[...], lnw_ref[...]).astype(BF16)
        return cache["xb"]

    def rope(t, rows):
        cos = cos_ref[rows, :]
        sin = sin_ref[rows, :]
        tiles = [t[:, i * LANES:(i + 1) * LANES] for i in range(t.shape[1] // LANES)]
        return jnp.concatenate([tt * cos + _swap_lane_pairs(tt) * sin for tt in tiles], axis=1)

    def retention_q(cols, rows):
        t = _dot(lhs()[rows], wqk_ref[:, cols])
        yield
        rq_ref[rows, cols] = rope(t, rows).astype(BF16)

    def retention_k(cols, rows):
        t = _dot(lhs()[rows], wqk_ref[:, RET_QK_WIDTH + cols.start:RET_QK_WIDTH + cols.stop])
        yield
        k = rope(t, rows) * kdec_ref[rows, cols]
        for c in range((rows.stop - rows.start) // RET_CHUNK):
            rkt_ref[rows.start // RET_CHUNK + c, cols, :] = (
                k[c * RET_CHUNK:(c + 1) * RET_CHUNK, :].T.astype(BF16))

    def merge_gate(cols, rows):
        t = _dot(lhs()[rows], wmg_ref[:, cols])
        yield
        mg_ref[rows, cols] = _sigmoid(t + bg_ref[:, cols]).astype(BF16)

    def retention_gate(cols, rows):
        t = _dot(lhs()[rows], wrg_ref[:, cols])
        yield
        rg_ref[rows, cols] = _silu(t).astype(BF16)

    def attention_gate(cols, rows):
        t = _dot(lhs()[rows], wg_ref[:, cols])
        yield
        ag_ref[rows, cols] = _silu(t).astype(BF16)

    items = []
    part = GATE_ROWS // GATE_ROW_PARTS
    for fn, total, pieces in ((retention_q, RET_QK_WIDTH, GATE_PIECES[0]),
                              (retention_k, RET_QK_WIDTH, GATE_PIECES[1]),
                              (merge_gate, 2 * D_MODEL, GATE_PIECES[2]),
                              (retention_gate, RET_V_WIDTH, GATE_PIECES[3]),
                              (attention_gate, GROUP_WIDTH, GATE_PIECES[4])):
        width = total // pieces
        items += [lambda fn=fn, cols=slice(i * width, (i + 1) * width),
                  rows=slice(j * part, (j + 1) * part): fn(cols, rows)
                  for i in range(pieces) for j in range(GATE_ROW_PARTS)]
    return items


def _attention_items(in_refs, bias_ref, out_ref, acc_scr, max_scr, den_scr, nat_scr, first):
    ones = jnp.ones((2 * BAND, HEAD_DIM), BF16)
    quarter = ATTN_TILE // MIX_STRIDE

    def unit(g, r, n):
        dil = ATTN_GROUPS[g][1]
        cur_ref, prev_ref = in_refs[2 * g:2 * g + 2]
        q_cols, k_cols, v_cols = (slice(i * HEAD_DIM, (i + 1) * HEAD_DIM) for i in range(3))
        bias_base = 2 if dil == 1 else 0
        q = cur_ref[r, n * BAND:(n + 1) * BAND, q_cols]
        if n == 0:
            k = jnp.concatenate([prev_ref[r, :, k_cols], cur_ref[r, 0:BAND, k_cols]], axis=0)
            v = jnp.concatenate([prev_ref[r, :, v_cols], cur_ref[r, 0:BAND, v_cols]], axis=0)
            bias = bias_ref[bias_base + first]
        else:
            k = cur_ref[r, (n - 1) * BAND:(n + 1) * BAND, k_cols]
            v = cur_ref[r, (n - 1) * BAND:(n + 1) * BAND, v_cols]
            bias = bias_ref[bias_base]
        s = lax.dot_general(q, k, (((1,), (1,)), ((), ())), preferred_element_type=F32) + bias
        yield
        m = jnp.max(s, axis=-1, keepdims=True)
        p = jnp.exp2(s - m).astype(BF16)
        yield
        res = _dot(p, jnp.concatenate([v, ones], axis=1))
        acc, den = res[:, :HEAD_DIM], res[:, HEAD_DIM:]
        mb = jnp.broadcast_to(m, (BAND, HEAD_DIM))
        if dil == 1:
            sub = BAND // MIX_STRIDE
            for r4 in range(MIX_STRIDE):
                dst = pl.ds(r4 * quarter + n * sub, sub)
                src = slice(r4 * sub, (r4 + 1) * sub)
                acc_scr[g, dst, :] = acc[src]
                max_scr[g, dst, :] = mb[src]
                den_scr[g, dst, :] = den[src]
        else:
            if dil == MIX_STRIDE:
                dst = pl.ds(r * quarter + n * BAND, BAND)
            else:
                sub_stride = dil // MIX_STRIDE
                dst = pl.ds((r % MIX_STRIDE) * quarter + r // MIX_STRIDE
                            + n * BAND * sub_stride, BAND, stride=sub_stride)
            acc_scr[g, dst, :] = acc
            max_scr[g, dst, :] = mb
            den_scr[g, dst, :] = den

    def mix(r4, ch):
        rows = pl.ds(r4 * quarter + ch * BAND, BAND)
        m0, m1, m2 = max_scr[0, rows, :], max_scr[1, rows, :], max_scr[2, rows, :]
        mx = jnp.maximum(jnp.maximum(m0, m1), m2)
        w0 = jnp.exp2(m0 - mx)
        w1 = jnp.exp2(m1 - mx)
        w2 = jnp.exp2(m2 - mx)
        num = w0 * acc_scr[0, rows, :] + w1 * acc_scr[1, rows, :] + w2 * acc_scr[2, rows, :]
        den = w0 * den_scr[0, rows, :] + w1 * den_scr[1, rows, :] + w2 * den_scr[2, rows, :]
        nat_scr[pl.ds(r4 + ch * BAND * MIX_STRIDE, BAND, stride=MIX_STRIDE), :] = num / den

    def write_out():
        out_ref[...] = nat_scr[...].astype(BF16)

    def mix_quarter(r4):
        for ch in range(quarter // BAND):
            mix(r4, ch)

    units = [lambda n=n: unit(0, 0, n) for n in range(ATTN_TILE // BAND)]
    quarter_done = []
    for r4 in range(MIX_STRIDE):
        units += [lambda g=g, r=r, n=n: unit(g, r, n)
                  for g in range(1, N_GROUPS)
                  for r in range(r4, ATTN_GROUPS[g][1], MIX_STRIDE)
                  for n in range(ATTN_TILE // ATTN_GROUPS[g][1] // BAND)]
        quarter_done.append(len(units))
    return units, quarter_done, mix_quarter, write_out


def _attn_gates_kernel(*refs):
    n_attn = 2 * N_GROUPS
    in_refs = refs[:n_attn]
    bias_ref = refs[n_attn]
    proj_in = refs[n_attn + 1:n_attn + 11]
    attn_out = refs[n_attn + 11]
    proj_out = refs[n_attn + 12:n_attn + 17]
    acc_scr, max_scr, den_scr, nat_scr = refs[n_attn + 17:]
    first = (pl.program_id(1) == 0).astype(jnp.int32)
    unit_makers, quarter_done, mix_quarter, write_out = _attention_items(
        in_refs, bias_ref, attn_out, acc_scr, max_scr, den_scr, nat_scr, first)
    proj = _gate_proj_items(*proj_in, *proj_out)
    per_round = len(unit_makers) // len(proj)
    assert per_round * len(proj) == len(unit_makers)
    rounds = [unit_makers[i * per_round:(i + 1) * per_round] for i in range(len(proj))]

    def advance(gen):
        next(gen, None)

    prev_units = []
    mixed = 0
    for r in range(len(rounds) + 1):
        units = [make() for make in rounds[r]] if r < len(rounds) else []
        for j in range(max(len(units), len(prev_units))):
            if j < len(units):
                advance(units[j])
            if j < len(prev_units):
                advance(prev_units[j])
        while mixed < len(quarter_done) and quarter_done[mixed] <= r * per_round:
            mix_quarter(mixed)
            mixed += 1
        if r < len(proj):
            for _ in proj[r]():
                pass
        for u in units:
            advance(u)
        prev_units = units
    write_out()


def _attn_bias_tables():
    rho = jnp.arange(BAND)[:, None]
    c = jnp.arange(2 * BAND)[None, :]
    sub = BAND // MIX_STRIDE
    tables = []
    for a in (rho, MIX_STRIDE * (rho % sub) + rho // sub):
        ok = (c >= a) & (c <= a + BAND)
        tables += [ok, ok & (c >= BAND)]
    return jnp.where(jnp.stack(tables), 0.0, NEG_BIG).astype(F32)


def _attn_gates(qkv_groups, x, lnw, w16, cos_t, sin_t, kdec, bg):
    B, S, _ = x.shape
    nt = S // ATTN_TILE
    cpt = GATE_ROWS // RET_CHUNK
    in_specs = []
    args = []
    for (_, dil), qkv in zip(ATTN_GROUPS, qkv_groups):
        rows = ATTN_TILE // dil
        ratio = rows // BAND

        in_specs += [
            pl.BlockSpec((None, dil, rows, 3 * HEAD_DIM), lambda b, t, h: (b, 0, t, h)),
            pl.BlockSpec((None, dil, BAND, 3 * HEAD_DIM),
                         lambda b, t, h, ratio=ratio: (b, 0, jnp.maximum(t * ratio - 1, 0), h)),
        ]
        args += [qkv] * 2
    bias = _attn_bias_tables()
    in_specs.append(pl.BlockSpec(bias.shape, lambda b, t, h: (0, 0, 0), pipeline_mode=RESIDENT))

    def wspec(n, col0=0):
        return pl.BlockSpec((D_MODEL, n), lambda b, t, h: (0, col0 // n), pipeline_mode=RESIDENT)

    def row_spec(n):
        return pl.BlockSpec((None, GATE_ROWS, n), lambda b, t, h: (b, t * ATTN_HEADS + h, 0))

    tspec = pl.BlockSpec((GATE_ROWS, LANES), lambda b, t, h: (t * ATTN_HEADS + h, 0))
    in_specs += [
        row_spec(D_MODEL),
        pl.BlockSpec((1, D_MODEL), lambda b, t, h: (0, 0)),
        wspec(2 * RET_QK_WIDTH, COL_RQ),
        wspec(GROUP_WIDTH, COL_AG), wspec(RET_V_WIDTH, COL_RG), wspec(2 * D_MODEL, COL_MG),
        tspec, tspec,
        pl.BlockSpec((GATE_ROWS, RET_QK_WIDTH), lambda b, t, h: (0, 0), pipeline_mode=RESIDENT),
        pl.BlockSpec((1, 2 * D_MODEL), lambda b, t, h: (0, 0)),
    ]
    out_widths = (GROUP_WIDTH, RET_V_WIDTH, 2 * D_MODEL)
    per_group = pltpu.VMEM((N_GROUPS, ATTN_TILE, HEAD_DIM), F32)
    return pl.pallas_call(
        _attn_gates_kernel,
        grid=(B, nt, ATTN_HEADS),
        in_specs=in_specs,
        out_specs=[
            pl.BlockSpec((None, ATTN_TILE, HEAD_DIM), lambda b, t, h: (b, t, h)),
            row_spec(RET_QK_WIDTH),
            pl.BlockSpec((None, cpt, RET_QK_WIDTH, RET_CHUNK),
                         lambda b, t, h: (b, t * ATTN_HEADS + h, 0, 0)),
            *[row_spec(n) for n in out_widths],
        ],
        out_shape=[
            jax.ShapeDtypeStruct((B, S, GROUP_WIDTH), BF16),
            jax.ShapeDtypeStruct((B, S, RET_QK_WIDTH), BF16),
            jax.ShapeDtypeStruct((B, S // RET_CHUNK, RET_QK_WIDTH, RET_CHUNK), BF16),
            *[jax.ShapeDtypeStruct((B, S, n), BF16) for n in out_widths],
        ],
        scratch_shapes=[per_group, per_group, per_group,
                        pltpu.VMEM((ATTN_TILE, HEAD_DIM), F32)],
        compiler_params=pltpu.CompilerParams(
            dimension_semantics=("arbitrary", "arbitrary", "arbitrary"),
            vmem_limit_bytes=VMEM_LIMIT),
        name="attn_gates",
    )(*args, bias, x, lnw, w16, w16, w16, w16, cos_t, sin_t, kdec, bg)


def _retention_kernel(q_ref, kt_ref, v_ref, gate_ref, gnw_ref, eps_ref, cd_ref,
                      out_ref, state, p_scr):
    @pl.when(pl.program_id(1) == 0)
    def _():
        state[...] = jnp.zeros_like(state)

    dk, dv, C = RET_KEY_DIM, RET_VALUE_DIM, RET_CHUNK
    row_k = lax.broadcasted_iota(jnp.int32, (2 * dk, C), 0)
    row = lax.broadcasted_iota(jnp.int32, (2 * dk, 2 * dv), 0)
    col = lax.broadcasted_iota(jnp.int32, (C, 2 * dv), 1)
    diag_blk = (row < dk) == (col < dv)
    causal = (col % C) <= lax.broadcasted_iota(jnp.int32, (C, 2 * C), 0)
    units = [(bi, j) for bi in range(RET_BATCH) for j in range(RET_HEADS // 2)]

    def q_pair(bi, j):
        return q_ref[bi, :, j * 2 * dk:(j + 1) * 2 * dk]

    def kt_pair(bi, j):
        return kt_ref[bi, j * 2 * dk:(j + 1) * 2 * dk, :]

    def v_pair(bi, j):
        return v_ref[bi, :, j * 2 * dv:(j + 1) * 2 * dv]

    for bi, j in units:
        kt = kt_pair(bi, j)
        zero = jnp.zeros_like(kt)
        kt_blk = jnp.concatenate([jnp.where(row_k < dk, kt, zero),
                                  jnp.where(row_k >= dk, kt, zero)], axis=1)
        s = _dot(q_pair(bi, j), kt_blk).astype(BF16)
        p_scr[bi, j] = jnp.where(causal, s, jnp.zeros_like(s))

    for bi, j in units:
        v = v_pair(bi, j)
        zero = jnp.zeros_like(v)
        v_blk = jnp.concatenate([jnp.where(col < dv, v, zero),
                                 jnp.where(col >= dv, v, zero)], axis=0)
        u = _dot(jnp.concatenate([p_scr[bi, j], q_pair(bi, j)], axis=1),
                 jnp.concatenate([v_blk, state[bi, j].astype(BF16)], axis=0))
        for hh in range(2):
            h = 2 * j + hh
            hs = slice(h * dv, (h + 1) * dv)
            uh = u[:, hh * dv:(hh + 1) * dv]
            uh = uh * lax.rsqrt(jnp.sum(uh * uh, axis=-1, keepdims=True) + eps_ref[h])
            out_ref[bi, :, hs] = (uh * gnw_ref[:, hs]).astype(BF16) * gate_ref[bi, :, hs]

    for bi, j in units:
        kv = _dot(kt_pair(bi, j), v_pair(bi, j))
        state[bi, j] = cd_ref[j] * (state[bi, j] + jnp.where(diag_blk, kv, 0.0))


def _retention(rq, rkt, rv, gate, gnw, eps_t, cd):
    B, S, _ = rv.shape
    nc = S // RET_CHUNK
    npair = RET_HEADS // 2
    pair_shape = (RET_BATCH, npair, 2 * RET_KEY_DIM, 2 * RET_VALUE_DIM)

    def const(shape):
        return pl.BlockSpec(shape, lambda b, n: (0,) * len(shape))

    def rows(width):
        return pl.BlockSpec((RET_BATCH, RET_CHUNK, width), lambda b, n: (b, n, 0))

    return pl.pallas_call(
        _retention_kernel,
        grid=(B // RET_BATCH, nc),
        in_specs=[
            rows(RET_QK_WIDTH),
            pl.BlockSpec((RET_BATCH, None, RET_QK_WIDTH, RET_CHUNK), lambda b, n: (b, n, 0, 0)),
            rows(RET_V_WIDTH),
            rows(RET_V_WIDTH),
            const((1, RET_V_WIDTH)),
            const((RET_HEADS, RET_CHUNK, RET_VALUE_DIM)),
            const(pair_shape[1:]),
        ],
        out_specs=rows(RET_V_WIDTH),
        out_shape=jax.ShapeDtypeStruct((B, S, RET_V_WIDTH), BF16),
        scratch_shapes=[pltpu.VMEM(pair_shape, F32),
                        pltpu.VMEM((RET_BATCH, npair, RET_CHUNK, 2 * RET_CHUNK), BF16)],
        compiler_params=pltpu.CompilerParams(
            dimension_semantics=("arbitrary", "arbitrary"), vmem_limit_bytes=VMEM_LIMIT),
        name="retention",
    )(rq, rkt, rv, gate, gnw, eps_t, cd)


def _merge_out_kernel(a_ref, ag_ref, r_ref, mg_ref, x_ref, wa_ref, wr_ref, wo_ref, lnf_ref,
                      out_ref):
    def part(rows):
        y_attn = _dot(a_ref[rows, :] * ag_ref[rows, :], wa_ref[...])
        y_ret = _dot(r_ref[rows, :], wr_ref[...])
        yield
        merged = (mg_ref[rows, :D_MODEL].astype(F32) * y_attn
                  + mg_ref[rows, D_MODEL:].astype(F32) * y_ret)
        h = x_ref[rows, :] + _dot(merged.astype(BF16), wo_ref[...])
        yield
        out_ref[rows, :] = _rmsnorm_rows(h, lnf_ref[...])

    n_parts = OUT_ROWS // OUT_PART
    parts = [part(slice(i * OUT_PART, (i + 1) * OUT_PART)) for i in range(n_parts)]
    for step in range(n_parts + 2):
        for lag in range(3):
            i = step - lag
            if 0 <= i < n_parts:
                next(parts[i], None)


def _merge_out(attn, ag, ret_g, mg, x, wa, wr, wo, lnf):
    B, S, _ = x.shape
    nt = S // OUT_ROWS

    def rows(n):
        return pl.BlockSpec((None, OUT_ROWS, n), lambda b, t: (b, t, 0))

    def const(shape):
        return pl.BlockSpec(shape, lambda b, t: (0,) * len(shape), pipeline_mode=RESIDENT)

    return pl.pallas_call(
        _merge_out_kernel,
        grid=(B, nt),
        in_specs=[rows(GROUP_WIDTH), rows(GROUP_WIDTH), rows(RET_V_WIDTH), rows(2 * D_MODEL),
                  rows(D_MODEL),
                  const((GROUP_WIDTH, D_MODEL)), const((RET_V_WIDTH, D_MODEL)),
                  const((D_MODEL, D_MODEL)), const((1, D_MODEL))],
        out_specs=rows(D_MODEL),
        out_shape=jax.ShapeDtypeStruct((B, S, D_MODEL), F32),
        compiler_params=pltpu.CompilerParams(
            dimension_semantics=("arbitrary", "arbitrary"), vmem_limit_bytes=VMEM_LIMIT),
        name="merge_out",
    )(attn, ag, ret_g, mg, x, wa, wr, wo, lnf)


def _rope_tables(S):
    half = RET_KEY_DIM // 2
    inv_freq = ROPE_BASE ** (-jnp.linspace(0.0, 1.0, half, dtype=F32))
    step = 64
    a = (jnp.arange(S // step, dtype=F32) * step)[:, None] * inv_freq[None, :]
    b = jnp.arange(step, dtype=F32)[:, None] * inv_freq[None, :]
    ca, sa, cb, sb = jnp.cos(a)[:, None], jnp.sin(a)[:, None], jnp.cos(b)[None], jnp.sin(b)[None]
    cos = (ca * cb - sa * sb).reshape(S, half)
    sin = (sa * cb + ca * sb).reshape(S, half)
    heads_per_tile = LANES // RET_KEY_DIM
    sign = jnp.where(jnp.arange(LANES) % 2 == 0, -1.0, 1.0).astype(F32)
    cos_t = jnp.tile(jnp.repeat(cos, 2, axis=1), (1, heads_per_tile))
    sin_t = jnp.tile(jnp.repeat(sin, 2, axis=1), (1, heads_per_tile)) * sign[None, :]
    return cos_t, sin_t


def _retention_constants():
    H, C, dk, dv = RET_HEADS, RET_CHUNK, RET_KEY_DIM, RET_VALUE_DIM
    log_gamma = jnp.log(1.0 - 2.0 ** (-5.0 - jnp.arange(H, dtype=F32)))
    idx = jnp.arange(C, dtype=F32)
    inv_decay = jnp.exp(-(idx + 1.0)[None, :] * log_gamma[:, None])
    chunk_decay = jnp.exp(C * log_gamma)
    kdec = jnp.repeat(inv_decay.T, dk, axis=1) * dk ** -0.5
    kdec = jnp.tile(kdec, (GATE_ROWS // C, 1))
    eps_t = jnp.broadcast_to((dv * NORM_EPS * inv_decay * inv_decay)[:, :, None], (H, C, dv))
    cd = jnp.broadcast_to(chunk_decay[:, None, None], (H, dk, 2 * dv)).reshape(H // 2, 2 * dk, 2 * dv)
    return kdec, eps_t, cd


def kernel(x, ln1_w, w_in, b_gate, attn_proj, ret_proj, ret_gn_w, w_out, lnf_w):
    B, S, _ = x.shape
    assert w_in.shape[0] == 1, "single layer"
    w16 = w_in[0].astype(BF16)

    lnw = ln1_w[0].reshape(1, D_MODEL)
    q1, q2, q3, rv = _qkv_proj(x, lnw, w16)
    cos_t, sin_t = _rope_tables(S)
    kdec, eps_t, cd = _retention_constants()
    attn, rq, rkt, ag, rg, mg = _attn_gates(
        (q1, q2, q3), x, lnw, w16, cos_t, sin_t, kdec, b_gate[0].reshape(1, 2 * D_MODEL))
    gnw = (ret_gn_w[0] * RET_VALUE_DIM ** 0.5).reshape(1, RET_V_WIDTH)
    ret_g = _retention(rq, rkt, rv, rg, gnw, eps_t, cd)

    return _merge_out(attn, ag, ret_g, mg, x,
                      attn_proj[0].astype(BF16), ret_proj[0].astype(BF16),
                      w_out[0].astype(BF16), lnf_w.reshape(1, D_MODEL))
```

```python
import jax
import jax.numpy as jnp
from jax import lax
from jax.experimental import pallas as pl
from jax.experimental.pallas import tpu as pltpu

D_MODEL = 1024
ATTN_GROUPS = ((128, 1), (512, 4), (2048, 16))
N_GROUPS = 3
ATTN_HEADS = 4
HEAD_DIM = 128
GROUP_WIDTH = ATTN_HEADS * HEAD_DIM
QKV_WIDTH = N_GROUPS * GROUP_WIDTH
BAND = 128
RET_HEADS = 8
RET_KEY_DIM = 64
RET_VALUE_DIM = 128
RET_QK_WIDTH = RET_HEADS * RET_KEY_DIM
RET_V_WIDTH = RET_HEADS * RET_VALUE_DIM
RET_CHUNK = 128
ROPE_BASE = 10000.0
COL_AG = 3 * QKV_WIDTH
COL_RQ = COL_AG + GROUP_WIDTH
COL_RV = COL_RQ + 2 * RET_QK_WIDTH
COL_RG = COL_RV + RET_V_WIDTH
COL_MG = COL_RG + RET_V_WIDTH
NORM_EPS = 1e-6
NEG_BIG = -1e30
LANES = 128
LOG2E = 1.4426950408889634
MIX_STRIDE = 4

PROJ_ROWS = 1024
QKV_PARTS = 4
ATTN_TILE = 2048
GATE_ROWS = ATTN_TILE // ATTN_HEADS
RET_BATCH = 8
OUT_ROWS = 1024
OUT_PART = 256
V7X_VMEM_BYTES = 64 * 1024 * 1024
VMEM_LIMIT = V7X_VMEM_BYTES * 7 // 8
GATE_ROW_PARTS = 2
GATE_PIECES = (2, 2, 4, 2, 2)
RESIDENT = pl.Buffered(1)

BF16 = jnp.bfloat16
F32 = jnp.float32


def _rmsnorm_rows(x, w):
    return x * lax.rsqrt(jnp.mean(x * x, axis=-1, keepdims=True) + NORM_EPS) * w


def _dot(a, b):
    return jnp.dot(a, b, preferred_element_type=F32)


def _sigmoid(t):
    return 0.5 * jnp.tanh(0.5 * t) + 0.5


def _silu(t):
    return t * _sigmoid(t)


def _qkv_proj_kernel(x_ref, lnw_ref, *refs):
    wq_ref, wk_ref, wv_ref, wrv_ref = refs[:4]
    out_refs = refs[4:4 + N_GROUPS]
    rv_ref, xn_scr, xn4_scr, lhs_scr = refs[4 + N_GROUPS:]
    n_lane_tiles = D_MODEL // LANES
    q_scale = HEAD_DIM ** -0.5 * LOG2E
    part_rows = PROJ_ROWS // QKV_PARTS
    d4, d16 = ATTN_GROUPS[1][1], ATTN_GROUPS[2][1]
    rows4, rows16 = part_rows // d4, part_rows // d16
    NAT, QPERM, BY4, BY16 = range(4)

    def part(p):
        rows_p = slice(p * part_rows, (p + 1) * part_rows)
        lhs = lhs_scr.at[p]
        xn = _rmsnorm_rows(x_ref[rows_p, :], lnw_ref[...])
        lhs[NAT] = xn.astype(BF16)
        for ct in range(n_lane_tiles):
            xn_scr[ct] = xn[:, ct * LANES:(ct + 1) * LANES]
        sub = BAND // MIX_STRIDE
        for ct in range(n_lane_tiles):
            lanes = slice(ct * LANES, (ct + 1) * LANES)
            lhs[QPERM, :, lanes] = jnp.concatenate(
                [xn_scr[ct, pl.ds(blk * BAND + r4, sub, stride=MIX_STRIDE), :]
                 for blk in range(part_rows // BAND) for r4 in range(MIX_STRIDE)],
                axis=0).astype(BF16)
            by4 = jnp.concatenate(
                [xn_scr[ct, pl.ds(r, rows4, stride=d4), :] for r in range(d4)], axis=0)
            xn4_scr[ct] = by4
            lhs[BY4, :, lanes] = by4.astype(BF16)
        for ct in range(n_lane_tiles):
            by16 = jnp.concatenate(
                [xn4_scr[ct, pl.ds((r % d4) * rows4 + r // d4, rows16, stride=d16 // d4), :]
                 for r in range(d16)], axis=0)
            lhs[BY16, :, ct * LANES:(ct + 1) * LANES] = by16.astype(BF16)
        yield

        rv_ref[rows_p, :] = _dot(lhs[NAT], wrv_ref[...]).astype(BF16)
        for g, order_q, order_kv in ((2, BY16, BY16), (1, BY4, BY4), (0, QPERM, NAT)):
            gcols = slice(g * GROUP_WIDTH, (g + 1) * GROUP_WIDTH)
            dil = ATTN_GROUPS[g][1]
            rows = part_rows // dil
            q = (_dot(lhs[order_q], wq_ref[:, gcols]) * q_scale).astype(BF16)
            k = _dot(lhs[order_kv], wk_ref[:, gcols]).astype(BF16)
            v = _dot(lhs[order_kv], wv_ref[:, gcols]).astype(BF16)
            for r in range(dil):
                src_rows = slice(r * rows, (r + 1) * rows)
                dst_rows = slice(p * rows, (p + 1) * rows)
                for h in range(ATTN_HEADS):
                    hs = slice(h * HEAD_DIM, (h + 1) * HEAD_DIM)
                    for which, val in enumerate((q, k, v)):
                        c0 = (3 * h + which) * HEAD_DIM
                        out_refs[g][r, dst_rows, c0:c0 + HEAD_DIM] = val[src_rows, hs]

    parts = [part(p) for p in range(QKV_PARTS)]
    for gen in parts:
        next(gen)
    for gen in parts:
        next(gen, None)


def _qkv_proj(x, lnw, w16):
    B, S, _ = x.shape
    nt = S // PROJ_ROWS
    w_specs = [pl.BlockSpec((D_MODEL, QKV_WIDTH), lambda b, t, part=part: (0, part),
                            pipeline_mode=RESIDENT) for part in range(3)]
    w_specs.append(pl.BlockSpec((D_MODEL, RET_V_WIDTH), lambda b, t: (0, COL_RV // RET_V_WIDTH),
                                pipeline_mode=RESIDENT))
    return pl.pallas_call(
        _qkv_proj_kernel,
        grid=(B, nt),
        in_specs=[
            pl.BlockSpec((None, PROJ_ROWS, D_MODEL), lambda b, t: (b, t, 0)),
            pl.BlockSpec((1, D_MODEL), lambda b, t: (0, 0)),
            *w_specs,
        ],
        out_specs=[
            *[pl.BlockSpec((None, dil, PROJ_ROWS // dil, QKV_WIDTH), lambda b, t: (b, 0, t, 0))
              for _, dil in ATTN_GROUPS],
            pl.BlockSpec((None, PROJ_ROWS, RET_V_WIDTH), lambda b, t: (b, t, 0)),
        ],
        out_shape=[
            *[jax.ShapeDtypeStruct((B, dil, S // dil, QKV_WIDTH), BF16) for _, dil in ATTN_GROUPS],
            jax.ShapeDtypeStruct((B, S, RET_V_WIDTH), BF16),
        ],
        scratch_shapes=[pltpu.VMEM((D_MODEL // LANES, PROJ_ROWS // QKV_PARTS, LANES), F32),
                        pltpu.VMEM((D_MODEL // LANES, PROJ_ROWS // QKV_PARTS, LANES), F32),
                        pltpu.VMEM((QKV_PARTS, 4, PROJ_ROWS // QKV_PARTS, D_MODEL), BF16)],
        compiler_params=pltpu.CompilerParams(
            dimension_semantics=("arbitrary", "arbitrary"), vmem_limit_bytes=VMEM_LIMIT),
        name="qkv_proj",
    )(x, lnw, w16, w16, w16, w16)


def _swap_lane_pairs(t):
    n = t.shape[-1]
    lane = lax.broadcasted_iota(jnp.int32, t.shape, 1)
    up = pltpu.roll(t, n - 1, axis=1)
    down = pltpu.roll(t, 1, axis=1)
    return jnp.where(lane % 2 == 0, up, down)


def _gate_proj_items(x_ref, lnw_ref, wqk_ref, wg_ref, wrg_ref, wmg_ref,
                     cos_ref, sin_ref, kdec_ref, bg_ref,
                     rq_ref, rkt_ref, ag_ref, rg_ref, mg_ref):
    cache = {}

    def lhs():
        if "xb" not in cache:
            cache["xb"] = _rmsnorm_rows(x_ref[...], lnw_ref[...]).astype(BF16)
        return cache["xb"]

    def rope(t, rows):
        cos = cos_ref[rows, :]
        sin = sin_ref[rows, :]
        tiles = [t[:, i * LANES:(i + 1) * LANES] for i in range(t.shape[1] // LANES)]
        return jnp.concatenate([tt * cos + _swap_lane_pairs(tt) * sin for tt in tiles], axis=1)

    def retention_q(cols, rows):
        t = _dot(lhs()[rows], wqk_ref[:, cols])
        yield
        rq_ref[rows, cols] = rope(t, rows).astype(BF16)

    def retention_k(cols, rows):
        t = _dot(lhs()[rows], wqk_ref[:, RET_QK_WIDTH + cols.start:RET_QK_WIDTH + cols.stop])
        yield
        k = rope(t, rows) * kdec_ref[rows, cols]
        for c in range((rows.stop - rows.start) // RET_CHUNK):
            rkt_ref[rows.start // RET_CHUNK + c, cols, :] = (
                k[c * RET_CHUNK:(c + 1) * RET_CHUNK, :].T.astype(BF16))

    def merge_gate(cols, rows):
        t = _dot(lhs()[rows], wmg_ref[:, cols])
        yield
        mg_ref[rows, cols] = _sigmoid(t + bg_ref[:, cols]).astype(BF16)

    def retention_gate(cols, rows):
        t = _dot(lhs()[rows], wrg_ref[:, cols])
        yield
        rg_ref[rows, cols] = _silu(t).astype(BF16)

    def attention_gate(cols, rows):
        t = _dot(lhs()[rows], wg_ref[:, cols])
        yield
        ag_ref[rows, cols] = _silu(t).astype(BF16)

    items = []
    part = GATE_ROWS // GATE_ROW_PARTS
    for fn, total, pieces in ((retention_k, RET_QK_WIDTH, GATE_PIECES[1]),
                              (retention_q, RET_QK_WIDTH, GATE_PIECES[0]),
                              (merge_gate, 2 * D_MODEL, GATE_PIECES[2]),
                              (retention_gate, RET_V_WIDTH, GATE_PIECES[3]),
                              (attention_gate, GROUP_WIDTH, GATE_PIECES[4])):
        width = total // pieces
        items += [lambda fn=fn, cols=slice(i * width, (i + 1) * width),
                  rows=slice(j * part, (j + 1) * part): fn(cols, rows)
                  for i in range(pieces) for j in range(GATE_ROW_PARTS)]
    return items


def _attention_items(in_refs, bias_ref, out_ref, acc_scr, max_scr, den_scr, nat_scr, first):
    ones = jnp.ones((2 * BAND, HEAD_DIM), BF16)
    quarter = ATTN_TILE // MIX_STRIDE

    def unit(g, r, n):
        dil = ATTN_GROUPS[g][1]
        cur_ref, prev_ref = in_refs[2 * g:2 * g + 2]
        q_cols, k_cols, v_cols = (slice(i * HEAD_DIM, (i + 1) * HEAD_DIM) for i in range(3))
        bias_base = 2 if dil == 1 else 0
        q = cur_ref[r, n * BAND:(n + 1) * BAND, q_cols]
        if n == 0:
            k = jnp.concatenate([prev_ref[r, :, k_cols], cur_ref[r, 0:BAND, k_cols]], axis=0)
            v = jnp.concatenate([prev_ref[r, :, v_cols], cur_ref[r, 0:BAND, v_cols]], axis=0)
            bias = bias_ref[bias_base + first]
        else:
            k = cur_ref[r, (n - 1) * BAND:(n + 1) * BAND, k_cols]
            v = cur_ref[r, (n - 1) * BAND:(n + 1) * BAND, v_cols]
            bias = bias_ref[bias_base]
        s = lax.dot_general(q, k, (((1,), (1,)), ((), ())), preferred_element_type=F32) + bias
        yield
        m = jnp.max(s, axis=-1, keepdims=True)
        p = jnp.exp2(s - m).astype(BF16)
        yield
        res = _dot(p, jnp.concatenate([v, ones], axis=1))
        acc, den = res[:, :HEAD_DIM], res[:, HEAD_DIM:]
        mb = jnp.broadcast_to(m, (BAND, HEAD_DIM))
        if dil == 1:
            sub = BAND // MIX_STRIDE
            for r4 in range(MIX_STRIDE):
                dst = pl.ds(r4 * quarter + n * sub, sub)
                src = slice(r4 * sub, (r4 + 1) * sub)
                acc_scr[g, dst, :] = acc[src]
                max_scr[g, dst, :] = mb[src]
                den_scr[g, dst, :] = den[src]
        else:
            if dil == MIX_STRIDE:
                dst = pl.ds(r * quarter + n * BAND, BAND)
            else:
                sub_stride = dil // MIX_STRIDE
                dst = pl.ds((r % MIX_STRIDE) * quarter + r // MIX_STRIDE
                            + n * BAND * sub_stride, BAND, stride=sub_stride)
            acc_scr[g, dst, :] = acc
            max_scr[g, dst, :] = mb
            den_scr[g, dst, :] = den

    def mix(r4, ch):
        rows = pl.ds(r4 * quarter + ch * BAND, BAND)
        m0, m1, m2 = max_scr[0, rows, :], max_scr[1, rows, :], max_scr[2, rows, :]
        mx = jnp.maximum(jnp.maximum(m0, m1), m2)
        w0 = jnp.exp2(m0 - mx)
        w1 = jnp.exp2(m1 - mx)
        w2 = jnp.exp2(m2 - mx)
        num = w0 * acc_scr[0, rows, :] + w1 * acc_scr[1, rows, :] + w2 * acc_scr[2, rows, :]
        den = w0 * den_scr[0, rows, :] + w1 * den_scr[1, rows, :] + w2 * den_scr[2, rows, :]
        nat_scr[pl.ds(r4 + ch * BAND * MIX_STRIDE, BAND, stride=MIX_STRIDE), :] = num / den

    def write_out():
        out_ref[...] = nat_scr[...].astype(BF16)

    def mix_quarter(r4):
        for ch in range(quarter // BAND):
            mix(r4, ch)

    units = [lambda n=n: unit(0, 0, n) for n in range(ATTN_TILE // BAND)]
    quarter_done = []
    for r4 in range(MIX_STRIDE):
        per_group = [[lambda g=g, r=r, n=n: unit(g, r, n)
                      for r in range(r4, ATTN_GROUPS[g][1], MIX_STRIDE)
                      for n in range(ATTN_TILE // ATTN_GROUPS[g][1] // BAND)]
                     for g in range(1, N_GROUPS)]
        units += [u for pair in zip(*per_group) for u in pair]
        quarter_done.append(len(units))
    return units, quarter_done, mix_quarter, write_out


def _attn_gates_kernel(*refs):
    n_attn = 2 * N_GROUPS
    in_refs = refs[:n_attn]
    bias_ref = refs[n_attn]
    proj_in = refs[n_attn + 1:n_attn + 11]
    attn_out = refs[n_attn + 11]
    proj_out = refs[n_attn + 12:n_attn + 17]
    acc_scr, max_scr, den_scr, nat_scr = refs[n_attn + 17:]
    first = (pl.program_id(1) == 0).astype(jnp.int32)
    unit_makers, quarter_done, mix_quarter, write_out = _attention_items(
        in_refs, bias_ref, attn_out, acc_scr, max_scr, den_scr, nat_scr, first)
    proj = _gate_proj_items(*proj_in, *proj_out)
    per_round = len(unit_makers) // len(proj)
    assert per_round * len(proj) == len(unit_makers)
    rounds = [unit_makers[i * per_round:(i + 1) * per_round] for i in range(len(proj))]

    def advance(gen):
        next(gen, None)

    prev_units = []
    mixed = 0
    for r in range(len(rounds) + 1):
        units = [make() for make in rounds[r]] if r < len(rounds) else []
        for u in units:
            advance(u)
        for u in prev_units:
            advance(u)
        while mixed < len(quarter_done) and quarter_done[mixed] <= r * per_round:
            mix_quarter(mixed)
            mixed += 1
        if r < len(proj):
            for _ in proj[r]():
                pass
        for u in units:
            advance(u)
        prev_units = units
    write_out()


def _attn_bias_tables():
    rho = jnp.arange(BAND)[:, None]
    c = jnp.arange(2 * BAND)[None, :]
    sub = BAND // MIX_STRIDE
    tables = []
    for a in (rho, MIX_STRIDE * (rho % sub) + rho // sub):
        ok = (c >= a) & (c <= a + BAND)
        tables += [ok, ok & (c >= BAND)]
    return jnp.where(jnp.stack(tables), 0.0, NEG_BIG).astype(F32)


def _attn_gates(qkv_groups, x, lnw, w16, cos_t, sin_t, kdec, bg):
    B, S, _ = x.shape
    nt = S // ATTN_TILE
    cpt = GATE_ROWS // RET_CHUNK
    in_specs = []
    args = []
    for (_, dil), qkv in zip(ATTN_GROUPS, qkv_groups):
        rows = ATTN_TILE // dil
        ratio = rows // BAND

        in_specs += [
            pl.BlockSpec((None, dil, rows, 3 * HEAD_DIM), lambda b, t, h: (b, 0, t, h)),
            pl.BlockSpec((None, dil, BAND, 3 * HEAD_DIM),
                         lambda b, t, h, ratio=ratio: (b, 0, jnp.maximum(t * ratio - 1, 0), h)),
        ]
        args += [qkv] * 2
    bias = _attn_bias_tables()
    in_specs.append(pl.BlockSpec(bias.shape, lambda b, t, h: (0, 0, 0), pipeline_mode=RESIDENT))

    def wspec(n, col0=0):
        return pl.BlockSpec((D_MODEL, n), lambda b, t, h: (0, col0 // n), pipeline_mode=RESIDENT)

    def row_spec(n):
        return pl.BlockSpec((None, GATE_ROWS, n), lambda b, t, h: (b, t * ATTN_HEADS + h, 0))

    tspec = pl.BlockSpec((GATE_ROWS, LANES), lambda b, t, h: (t * ATTN_HEADS + h, 0))
    in_specs += [
        row_spec(D_MODEL),
        pl.BlockSpec((1, D_MODEL), lambda b, t, h: (0, 0)),
        wspec(2 * RET_QK_WIDTH, COL_RQ),
        wspec(GROUP_WIDTH, COL_AG), wspec(RET_V_WIDTH, COL_RG), wspec(2 * D_MODEL, COL_MG),
        tspec, tspec,
        pl.BlockSpec((GATE_ROWS, RET_QK_WIDTH), lambda b, t, h: (0, 0), pipeline_mode=RESIDENT),
        pl.BlockSpec((1, 2 * D_MODEL), lambda b, t, h: (0, 0)),
    ]
    out_widths = (GROUP_WIDTH, RET_V_WIDTH, 2 * D_MODEL)
    per_group = pltpu.VMEM((N_GROUPS, ATTN_TILE, HEAD_DIM), F32)
    return pl.pallas_call(
        _attn_gates_kernel,
        grid=(B, nt, ATTN_HEADS),
        in_specs=in_specs,
        out_specs=[
            pl.BlockSpec((None, ATTN_TILE, HEAD_DIM), lambda b, t, h: (b, t, h)),
            row_spec(RET_QK_WIDTH),
            pl.BlockSpec((None, cpt, RET_QK_WIDTH, RET_CHUNK),
                         lambda b, t, h: (b, t * ATTN_HEADS + h, 0, 0)),
            *[row_spec(n) for n in out_widths],
        ],
        out_shape=[
            jax.ShapeDtypeStruct((B, S, GROUP_WIDTH), BF16),
            jax.ShapeDtypeStruct((B, S, RET_QK_WIDTH), BF16),
            jax.ShapeDtypeStruct((B, S // RET_CHUNK, RET_QK_WIDTH, RET_CHUNK), BF16),
            *[jax.ShapeDtypeStruct((B, S, n), BF16) for n in out_widths],
        ],
        scratch_shapes=[per_group, per_group, per_group,
                        pltpu.VMEM((ATTN_TILE, HEAD_DIM), F32)],
        compiler_params=pltpu.CompilerParams(
            dimension_semantics=("arbitrary", "arbitrary", "arbitrary"),
            vmem_limit_bytes=VMEM_LIMIT),
        name="attn_gates",
    )(*args, bias, x, lnw, w16, w16, w16, w16, cos_t, sin_t, kdec, bg)


def _retention_kernel(q_ref, kt_ref, v_ref, gate_ref, gnw_ref, eps_ref, cd_ref,
                      out_ref, state, p_scr):
    @pl.when(pl.program_id(1) == 0)
    def _():
        state[...] = jnp.zeros_like(state)

    dk, dv, C = RET_KEY_DIM, RET_VALUE_DIM, RET_CHUNK
    row_k = lax.broadcasted_iota(jnp.int32, (2 * dk, C), 0)
    row = lax.broadcasted_iota(jnp.int32, (2 * dk, 2 * dv), 0)
    col = lax.broadcasted_iota(jnp.int32, (C, 2 * dv), 1)
    diag_blk = (row < dk) == (col < dv)
    causal = (col % C) <= lax.broadcasted_iota(jnp.int32, (C, 2 * C), 0)
    units = [(bi, j) for bi in range(RET_BATCH) for j in range(RET_HEADS // 2)]

    def q_pair(bi, j):
        return q_ref[bi, :, j * 2 * dk:(j + 1) * 2 * dk]

    def kt_pair(bi, j):
        return kt_ref[bi, j * 2 * dk:(j + 1) * 2 * dk, :]

    def v_pair(bi, j):
        return v_ref[bi, :, j * 2 * dv:(j + 1) * 2 * dv]

    for bi, j in units:
        kt = kt_pair(bi, j)
        zero = jnp.zeros_like(kt)
        kt_blk = jnp.concatenate([jnp.where(row_k < dk, kt, zero),
                                  jnp.where(row_k >= dk, kt, zero)], axis=1)
        s = _dot(q_pair(bi, j), kt_blk).astype(BF16)
        p_scr[bi, j] = jnp.where(causal, s, jnp.zeros_like(s))

    for bi, j in units:
        v = v_pair(bi, j)
        zero = jnp.zeros_like(v)
        v_blk = jnp.concatenate([jnp.where(col < dv, v, zero),
                                 jnp.where(col >= dv, v, zero)], axis=0)
        u = _dot(jnp.concatenate([p_scr[bi, j], q_pair(bi, j)], axis=1),
                 jnp.concatenate([v_blk, state[bi, j].astype(BF16)], axis=0))
        for hh in range(2):
            h = 2 * j + hh
            hs = slice(h * dv, (h + 1) * dv)
            uh = u[:, hh * dv:(hh + 1) * dv]
            uh = uh * lax.rsqrt(jnp.sum(uh * uh, axis=-1, keepdims=True) + eps_ref[h])
            out_ref[bi, :, hs] = (uh * gnw_ref[:, hs]).astype(BF16) * gate_ref[bi, :, hs]

    for bi, j in units:
        kv = _dot(kt_pair(bi, j), v_pair(bi, j))
        state[bi, j] = cd_ref[j] * (state[bi, j] + jnp.where(diag_blk, kv, 0.0))


def _retention(rq, rkt, rv, gate, gnw, eps_t, cd):
    B, S, _ = rv.shape
    nc = S // RET_CHUNK
    npair = RET_HEADS // 2
    pair_shape = (RET_BATCH, npair, 2 * RET_KEY_DIM, 2 * RET_VALUE_DIM)

    def const(shape):
        return pl.BlockSpec(shape, lambda b, n: (0,) * len(shape))

    def rows(width):
        return pl.BlockSpec((RET_BATCH, RET_CHUNK, width), lambda b, n: (b, n, 0))

    return pl.pallas_call(
        _retention_kernel,
        grid=(B // RET_BATCH, nc),
        in_specs=[
            rows(RET_QK_WIDTH),
            pl.BlockSpec((RET_BATCH, None, RET_QK_WIDTH, RET_CHUNK), lambda b, n: (b, n, 0, 0)),
            rows(RET_V_WIDTH),
            rows(RET_V_WIDTH),
            const((1, RET_V_WIDTH)),
            const((RET_HEADS, RET_CHUNK, RET_VALUE_DIM)),
            const(pair_shape[1:]),
        ],
        out_specs=rows(RET_V_WIDTH),
        out_shape=jax.ShapeDtypeStruct((B, S, RET_V_WIDTH), BF16),
        scratch_shapes=[pltpu.VMEM(pair_shape, F32),
                        pltpu.VMEM((RET_BATCH, npair, RET_CHUNK, 2 * RET_CHUNK), BF16)],
        compiler_params=pltpu.CompilerParams(
            dimension_semantics=("arbitrary", "arbitrary"), vmem_limit_bytes=VMEM_LIMIT),
        name="retention",
    )(rq, rkt, rv, gate, gnw, eps_t, cd)


def _merge_out_kernel(a_ref, ag_ref, r_ref, mg_ref, x_ref, wa_ref, wr_ref, wo_ref, lnf_ref,
                      out_ref):
    def part(rows):
        y_attn = _dot(a_ref[rows, :] * ag_ref[rows, :], wa_ref[...])
        y_ret = _dot(r_ref[rows, :], wr_ref[...])
        yield
        merged = (mg_ref[rows, :D_MODEL].astype(F32) * y_attn
                  + mg_ref[rows, D_MODEL:].astype(F32) * y_ret)
        h = x_ref[rows, :] + _dot(merged.astype(BF16), wo_ref[...])
        yield
        out_ref[rows, :] = _rmsnorm_rows(h, lnf_ref[...])

    n_parts = OUT_ROWS // OUT_PART
    parts = [part(slice(i * OUT_PART, (i + 1) * OUT_PART)) for i in range(n_parts)]
    for step in range(n_parts + 2):
        for lag in range(3):
            i = step - lag
            if 0 <= i < n_parts:
                next(parts[i], None)


def _merge_out(attn, ag, ret_g, mg, x, wa, wr, wo, lnf):
    B, S, _ = x.shape
    nt = S // OUT_ROWS

    def rows(n):
        return pl.BlockSpec((None, OUT_ROWS, n), lambda b, t: (b, t, 0))

    def const(shape):
        return pl.BlockSpec(shape, lambda b, t: (0,) * len(shape), pipeline_mode=RESIDENT)

    return pl.pallas_call(
        _merge_out_kernel,
        grid=(B, nt),
        in_specs=[rows(GROUP_WIDTH), rows(GROUP_WIDTH), rows(RET_V_WIDTH), rows(2 * D_MODEL),
                  rows(D_MODEL),
                  const((GROUP_WIDTH, D_MODEL)), const((RET_V_WIDTH, D_MODEL)),
                  const((D_MODEL, D_MODEL)), const((1, D_MODEL))],
        out_specs=rows(D_MODEL),
        out_shape=jax.ShapeDtypeStruct((B, S, D_MODEL), F32),
        compiler_params=pltpu.CompilerParams(
            dimension_semantics=("arbitrary", "arbitrary"), vmem_limit_bytes=VMEM_LIMIT),
        name="merge_out",
    )(attn, ag, ret_g, mg, x, wa, wr, wo, lnf)


def _rope_tables(S):
    half = RET_KEY_DIM // 2
    inv_freq = ROPE_BASE ** (-jnp.linspace(0.0, 1.0, half, dtype=F32))
    step = 64
    a = (jnp.arange(S // step, dtype=F32) * step)[:, None] * inv_freq[None, :]
    b = jnp.arange(step, dtype=F32)[:, None] * inv_freq[None, :]
    ca, sa, cb, sb = jnp.cos(a)[:, None], jnp.sin(a)[:, None], jnp.cos(b)[None], jnp.sin(b)[None]
    cos = (ca * cb - sa * sb).reshape(S, half)
    sin = (sa * cb + ca * sb).reshape(S, half)
    heads_per_tile = LANES // RET_KEY_DIM
    sign = jnp.where(jnp.arange(LANES) % 2 == 0, -1.0, 1.0).astype(F32)
    cos_t = jnp.tile(jnp.repeat(cos, 2, axis=1), (1, heads_per_tile))
    sin_t = jnp.tile(jnp.repeat(sin, 2, axis=1), (1, heads_per_tile)) * sign[None, :]
    return cos_t, sin_t


def _retention_constants():
    H, C, dk, dv = RET_HEADS, RET_CHUNK, RET_KEY_DIM, RET_VALUE_DIM
    log_gamma = jnp.log(1.0 - 2.0 ** (-5.0 - jnp.arange(H, dtype=F32)))
    idx = jnp.arange(C, dtype=F32)
    inv_decay = jnp.exp(-(idx + 1.0)[None, :] * log_gamma[:, None])
    chunk_decay = jnp.exp(C * log_gamma)
    kdec = jnp.repeat(inv_decay.T, dk, axis=1) * dk ** -0.5
    kdec = jnp.tile(kdec, (GATE_ROWS // C, 1))
    eps_t = jnp.broadcast_to((dv * NORM_EPS * inv_decay * inv_decay)[:, :, None], (H, C, dv))
    cd = jnp.broadcast_to(chunk_decay[:, None, None], (H, dk, 2 * dv)).reshape(H // 2, 2 * dk, 2 * dv)
    return kdec, eps_t, cd


def kernel(x, ln1_w, w_in, b_gate, attn_proj, ret_proj, ret_gn_w, w_out, lnf_w):
    B, S, _ = x.shape
    assert w_in.shape[0] == 1, "single layer"
    w16 = w_in[0].astype(BF16)

    lnw = ln1_w[0].reshape(1, D_MODEL)
    q1, q2, q3, rv = _qkv_proj(x, lnw, w16)
    cos_t, sin_t = _rope_tables(S)
    kdec, eps_t, cd = _retention_constants()
    attn, rq, rkt, ag, rg, mg = _attn_gates(
        (q1, q2, q3), x, lnw, w16, cos_t, sin_t, kdec, b_gate[0].reshape(1, 2 * D_MODEL))
    gnw = (ret_gn_w[0] * RET_VALUE_DIM ** 0.5).reshape(1, RET_V_WIDTH)
    ret_g = _retention(rq, rkt, rv, rg, gnw, eps_t, cd)

    return _merge_out(attn, ag, ret_g, mg, x,
                      attn_proj[0].astype(BF16), ret_proj[0].astype(BF16),
                      w_out[0].astype(BF16), lnf_w.reshape(1, D_MODEL))
```

```python
import jax
import jax.numpy as jnp
from jax import lax
from jax.experimental import pallas as pl
from jax.experimental.pallas import tpu as pltpu

D_MODEL = 1024
ATTN_GROUPS = ((128, 1), (512, 4), (2048, 16))
N_GROUPS = 3
ATTN_HEADS = 4
HEAD_DIM = 128
GROUP_WIDTH = ATTN_HEADS * HEAD_DIM
QKV_WIDTH = N_GROUPS * GROUP_WIDTH
BAND = 128
RET_HEADS = 8
RET_KEY_DIM = 64
RET_VALUE_DIM = 128
RET_QK_WIDTH = RET_HEADS * RET_KEY_DIM
RET_V_WIDTH = RET_HEADS * RET_VALUE_DIM
RET_CHUNK = 128
ROPE_BASE = 10000.0
COL_AG = 3 * QKV_WIDTH
COL_RQ = COL_AG + GROUP_WIDTH
COL_RV = COL_RQ + 2 * RET_QK_WIDTH
COL_RG = COL_RV + RET_V_WIDTH
COL_MG = COL_RG + RET_V_WIDTH
NORM_EPS = 1e-6
NEG_BIG = -1e30
LANES = 128
LOG2E = 1.4426950408889634
MIX_STRIDE = 4

PROJ_ROWS = 1024
QKV_PARTS = 4
ATTN_TILE = 2048
GATE_ROWS = ATTN_TILE // ATTN_HEADS
RET_BATCH = 8
OUT_ROWS = 1024
OUT_PART = 256
V7X_VMEM_BYTES = 64 * 1024 * 1024
VMEM_LIMIT = V7X_VMEM_BYTES * 7 // 8
GATE_ROW_PARTS = 2
GATE_PIECES = (2, 2, 4, 2, 2)
RESIDENT = pl.Buffered(1)

BF16 = jnp.bfloat16
F32 = jnp.float32


def _rmsnorm_rows(x, w):
    return x * lax.rsqrt(jnp.mean(x * x, axis=-1, keepdims=True) + NORM_EPS) * w


def _dot(a, b):
    return jnp.dot(a, b, preferred_element_type=F32)


def _sigmoid(t):
    return 0.5 * jnp.tanh(0.5 * t) + 0.5


def _silu(t):
    return t * _sigmoid(t)


def _qkv_proj_kernel(x_ref, lnw_ref, *refs):
    wq_ref, wk_ref, wv_ref, wrv_ref = refs[:4]
    out_refs = refs[4:4 + N_GROUPS]
    rv_ref, xn_scr, xn4_scr, lhs_scr = refs[4 + N_GROUPS:]
    n_lane_tiles = D_MODEL // LANES
    q_scale = HEAD_DIM ** -0.5 * LOG2E
    part_rows = PROJ_ROWS // QKV_PARTS
    d4, d16 = ATTN_GROUPS[1][1], ATTN_GROUPS[2][1]
    rows4, rows16 = part_rows // d4, part_rows // d16
    NAT, QPERM, BY4, BY16 = range(4)

    def part(p):
        rows_p = slice(p * part_rows, (p + 1) * part_rows)
        lhs = lhs_scr.at[p]
        xn = _rmsnorm_rows(x_ref[rows_p, :], lnw_ref[...])
        lhs[NAT] = xn.astype(BF16)
        for ct in range(n_lane_tiles):
            xn_scr[ct] = xn[:, ct * LANES:(ct + 1) * LANES]
        sub = BAND // MIX_STRIDE
        for ct in range(n_lane_tiles):
            lanes = slice(ct * LANES, (ct + 1) * LANES)
            lhs[QPERM, :, lanes] = jnp.concatenate(
                [xn_scr[ct, pl.ds(blk * BAND + r4, sub, stride=MIX_STRIDE), :]
                 for blk in range(part_rows // BAND) for r4 in range(MIX_STRIDE)],
                axis=0).astype(BF16)
            by4 = jnp.concatenate(
                [xn_scr[ct, pl.ds(r, rows4, stride=d4), :] for r in range(d4)], axis=0)
            xn4_scr[ct] = by4
            lhs[BY4, :, lanes] = by4.astype(BF16)
        for ct in range(n_lane_tiles):
            by16 = jnp.concatenate(
                [xn4_scr[ct, pl.ds((r % d4) * rows4 + r // d4, rows16, stride=d16 // d4), :]
                 for r in range(d16)], axis=0)
            lhs[BY16, :, ct * LANES:(ct + 1) * LANES] = by16.astype(BF16)
        yield

        rv_ref[rows_p, :] = _dot(lhs[NAT], wrv_ref[...]).astype(BF16)
        for g, (order_q, order_kv) in enumerate(((QPERM, NAT), (BY4, BY4), (BY16, BY16))):
            gcols = slice(g * GROUP_WIDTH, (g + 1) * GROUP_WIDTH)
            dil = ATTN_GROUPS[g][1]
            rows = part_rows // dil
            q = (_dot(lhs[order_q], wq_ref[:, gcols]) * q_scale).astype(BF16)
            k = _dot(lhs[order_kv], wk_ref[:, gcols]).astype(BF16)
            v = _dot(lhs[order_kv], wv_ref[:, gcols]).astype(BF16)
            for r in range(dil):
                src_rows = slice(r * rows, (r + 1) * rows)
                dst_rows = slice(p * rows, (p + 1) * rows)
                for h in range(ATTN_HEADS):
                    hs = slice(h * HEAD_DIM, (h + 1) * HEAD_DIM)
                    for which, val in enumerate((q, k, v)):
                        c0 = (3 * h + which) * HEAD_DIM
                        out_refs[g][r, dst_rows, c0:c0 + HEAD_DIM] = val[src_rows, hs]

    parts = [part(p) for p in range(QKV_PARTS)]
    for gen in parts:
        next(gen)
    for gen in parts:
        next(gen, None)


def _qkv_proj(x, lnw, w16):
    B, S, _ = x.shape
    nt = S // PROJ_ROWS
    w_specs = [pl.BlockSpec((D_MODEL, QKV_WIDTH), lambda b, t, part=part: (0, part),
                            pipeline_mode=RESIDENT) for part in range(3)]
    w_specs.append(pl.BlockSpec((D_MODEL, RET_V_WIDTH), lambda b, t: (0, COL_RV // RET_V_WIDTH),
                                pipeline_mode=RESIDENT))
    return pl.pallas_call(
        _qkv_proj_kernel,
        grid=(B, nt),
        in_specs=[
            pl.BlockSpec((None, PROJ_ROWS, D_MODEL), lambda b, t: (b, t, 0)),
            pl.BlockSpec((1, D_MODEL), lambda b, t: (0, 0)),
            *w_specs,
        ],
        out_specs=[
            *[pl.BlockSpec((None, dil, PROJ_ROWS // dil, QKV_WIDTH), lambda b, t: (b, 0, t, 0))
              for _, dil in ATTN_GROUPS],
            pl.BlockSpec((None, PROJ_ROWS, RET_V_WIDTH), lambda b, t: (b, t, 0)),
        ],
        out_shape=[
            *[jax.ShapeDtypeStruct((B, dil, S // dil, QKV_WIDTH), BF16) for _, dil in ATTN_GROUPS],
            jax.ShapeDtypeStruct((B, S, RET_V_WIDTH), BF16),
        ],
        scratch_shapes=[pltpu.VMEM((D_MODEL // LANES, PROJ_ROWS // QKV_PARTS, LANES), F32),
                        pltpu.VMEM((D_MODEL // LANES, PROJ_ROWS // QKV_PARTS, LANES), F32),
                        pltpu.VMEM((QKV_PARTS, 4, PROJ_ROWS // QKV_PARTS, D_MODEL), BF16)],
        compiler_params=pltpu.CompilerParams(
            dimension_semantics=("arbitrary", "arbitrary"), vmem_limit_bytes=VMEM_LIMIT),
        name="qkv_proj",
    )(x, lnw, w16, w16, w16, w16)


def _swap_lane_pairs(t):
    n = t.shape[-1]
    lane = lax.broadcasted_iota(jnp.int32, t.shape, 1)
    up = pltpu.roll(t, n - 1, axis=1)
    down = pltpu.roll(t, 1, axis=1)
    return jnp.where(lane % 2 == 0, up, down)


def _gate_proj_items(x_ref, lnw_ref, wqk_ref, wg_ref, wrg_ref, wmg_ref,
                     cos_ref, sin_ref, kdec_ref, bg_ref,
                     rq_ref, rkt_ref, ag_ref, rg_ref, mg_ref):
    cache = {}

    def lhs():
        if "xb" not in cache:
            cache["xb"] = _rmsnorm_rows(x_ref[...], lnw_ref[...]).astype(BF16)
        return cache["xb"]

    def rope(t, rows):
        cos = cos_ref[rows, :]
        sin = sin_ref[rows, :]
        tiles = [t[:, i * LANES:(i + 1) * LANES] for i in range(t.shape[1] // LANES)]
        return jnp.concatenate([tt * cos + _swap_lane_pairs(tt) * sin for tt in tiles], axis=1)

    def retention_q(cols, rows):
        t = _dot(lhs()[rows], wqk_ref[:, cols])
        yield
        rq_ref[rows, cols] = rope(t, rows).astype(BF16)

    def retention_k(cols, rows):
        t = _dot(lhs()[rows], wqk_ref[:, RET_QK_WIDTH + cols.start:RET_QK_WIDTH + cols.stop])
        yield
        k = rope(t, rows) * kdec_ref[rows, cols]
        for c in range((rows.stop - rows.start) // RET_CHUNK):
            rkt_ref[rows.start // RET_CHUNK + c, cols, :] = (
                k[c * RET_CHUNK:(c + 1) * RET_CHUNK, :].T.astype(BF16))

    def merge_gate(cols, rows):
        t = _dot(lhs()[rows], wmg_ref[:, cols])
        yield
        mg_ref[rows, cols] = _sigmoid(t + bg_ref[:, cols]).astype(BF16)

    def retention_gate(cols, rows):
        t = _dot(lhs()[rows], wrg_ref[:, cols])
        yield
        rg_ref[rows, cols] = _silu(t).astype(BF16)

    def attention_gate(cols, rows):
        t = _dot(lhs()[rows], wg_ref[:, cols])
        yield
        ag_ref[rows, cols] = _silu(t).astype(BF16)

    items = []
    part = GATE_ROWS // GATE_ROW_PARTS
    for fn, total, pieces in ((retention_k, RET_QK_WIDTH, GATE_PIECES[1]),
                              (retention_q, RET_QK_WIDTH, GATE_PIECES[0]),
                              (merge_gate, 2 * D_MODEL, GATE_PIECES[2]),
                              (retention_gate, RET_V_WIDTH, GATE_PIECES[3]),
                              (attention_gate, GROUP_WIDTH, GATE_PIECES[4])):
        width = total // pieces
        items += [lambda fn=fn, cols=slice(i * width, (i + 1) * width),
                  rows=slice(j * part, (j + 1) * part): fn(cols, rows)
                  for i in range(pieces) for j in range(GATE_ROW_PARTS)]
    return items


def _attention_items(in_refs, bias_ref, out_ref, acc_scr, max_scr, den_scr, nat_scr, first):
    ones = jnp.ones((2 * BAND, HEAD_DIM), BF16)
    quarter = ATTN_TILE // MIX_STRIDE

    def unit(g, r, n):
        dil = ATTN_GROUPS[g][1]
        cur_ref, prev_ref = in_refs[2 * g:2 * g + 2]
        q_cols, k_cols, v_cols = (slice(i * HEAD_DIM, (i + 1) * HEAD_DIM) for i in range(3))
        bias_base = 2 if dil == 1 else 0
        q = cur_ref[r, n * BAND:(n + 1) * BAND, q_cols]
        if n == 0:
            k = jnp.concatenate([prev_ref[r, :, k_cols], cur_ref[r, 0:BAND, k_cols]], axis=0)
            v = jnp.concatenate([prev_ref[r, :, v_cols], cur_ref[r, 0:BAND, v_cols]], axis=0)
            bias = bias_ref[bias_base + first]
        else:
            k = cur_ref[r, (n - 1) * BAND:(n + 1) * BAND, k_cols]
            v = cur_ref[r, (n - 1) * BAND:(n + 1) * BAND, v_cols]
            bias = bias_ref[bias_base]
        s = lax.dot_general(q, k, (((1,), (1,)), ((), ())), preferred_element_type=F32) + bias
        yield
        m = jnp.max(s, axis=-1, keepdims=True)
        p = jnp.exp2(s - m).astype(BF16)
        yield
        res = _dot(p, jnp.concatenate([v, ones], axis=1))
        acc, den = res[:, :HEAD_DIM], res[:, HEAD_DIM:]
        mb = jnp.broadcast_to(m, (BAND, HEAD_DIM))
        if dil == 1:
            sub = BAND // MIX_STRIDE
            for r4 in range(MIX_STRIDE):
                dst = pl.ds(r4 * quarter + n * sub, sub)
                src = slice(r4 * sub, (r4 + 1) * sub)
                acc_scr[g, dst, :] = acc[src]
                max_scr[g, dst, :] = mb[src]
                den_scr[g, dst, :] = den[src]
        else:
            if dil == MIX_STRIDE:
                dst = pl.ds(r * quarter + n * BAND, BAND)
            else:
                sub_stride = dil // MIX_STRIDE
                dst = pl.ds((r % MIX_STRIDE) * quarter + r // MIX_STRIDE
                            + n * BAND * sub_stride, BAND, stride=sub_stride)
            acc_scr[g, dst, :] = acc
            max_scr[g, dst, :] = mb
            den_scr[g, dst, :] = den

    def mix(r4, ch):
        rows = pl.ds(r4 * quarter + ch * BAND, BAND)
        m0, m1, m2 = max_scr[0, rows, :], max_scr[1, rows, :], max_scr[2, rows, :]
        mx = jnp.maximum(jnp.maximum(m0, m1), m2)
        w0 = jnp.exp2(m0 - mx)
        w1 = jnp.exp2(m1 - mx)
        w2 = jnp.exp2(m2 - mx)
        num = w0 * acc_scr[0, rows, :] + w1 * acc_scr[1, rows, :] + w2 * acc_scr[2, rows, :]
        den = w0 * den_scr[0, rows, :] + w1 * den_scr[1, rows, :] + w2 * den_scr[2, rows, :]
        nat_scr[pl.ds(r4 + ch * BAND * MIX_STRIDE, BAND, stride=MIX_STRIDE), :] = num / den

    def write_out():
        out_ref[...] = nat_scr[...].astype(BF16)

    def mix_quarter(r4):
        for ch in range(quarter // BAND):
            mix(r4, ch)

    units = [lambda n=n: unit(0, 0, n) for n in range(ATTN_TILE // BAND)]
    quarter_done = []
    for r4 in range(MIX_STRIDE):
        per_group = [[lambda g=g, r=r, n=n: unit(g, r, n)
                      for r in range(r4, ATTN_GROUPS[g][1], MIX_STRIDE)
                      for n in range(ATTN_TILE // ATTN_GROUPS[g][1] // BAND)]
                     for g in range(1, N_GROUPS)]
        units += [u for pair in zip(*per_group) for u in pair]
        quarter_done.append(len(units))
    return units, quarter_done, mix_quarter, write_out


def _attn_gates_kernel(*refs):
    n_attn = 2 * N_GROUPS
    in_refs = refs[:n_attn]
    bias_ref = refs[n_attn]
    proj_in = refs[n_attn + 1:n_attn + 11]
    attn_out = refs[n_attn + 11]
    proj_out = refs[n_attn + 12:n_attn + 17]
    acc_scr, max_scr, den_scr, nat_scr = refs[n_attn + 17:]
    first = (pl.program_id(1) == 0).astype(jnp.int32)
    unit_makers, quarter_done, mix_quarter, write_out = _attention_items(
        in_refs, bias_ref, attn_out, acc_scr, max_scr, den_scr, nat_scr, first)
    proj = _gate_proj_items(*proj_in, *proj_out)
    per_round = len(unit_makers) // len(proj)
    assert per_round * len(proj) == len(unit_makers)
    rounds = [unit_makers[i * per_round:(i + 1) * per_round] for i in range(len(proj))]

    def advance(gen):
        next(gen, None)

    prev_units = []
    mixed = 0
    for r in range(len(rounds) + 1):
        units = [make() for make in rounds[r]] if r < len(rounds) else []
        for u in units:
            advance(u)
        for u in prev_units:
            advance(u)
        while mixed < len(quarter_done) and quarter_done[mixed] <= r * per_round:
            mix_quarter(mixed)
            mixed += 1
        if r < len(proj):
            for _ in proj[r]():
                pass
        for u in units:
            advance(u)
        prev_units = units
    write_out()


def _attn_bias_tables():
    rho = jnp.arange(BAND)[:, None]
    c = jnp.arange(2 * BAND)[None, :]
    sub = BAND // MIX_STRIDE
    tables = []
    for a in (rho, MIX_STRIDE * (rho % sub) + rho // sub):
        ok = (c >= a) & (c <= a + BAND)
        tables += [ok, ok & (c >= BAND)]
    return jnp.where(jnp.stack(tables), 0.0, NEG_BIG).astype(F32)


def _attn_gates(qkv_groups, x, lnw, w16, cos_t, sin_t, kdec, bg):
    B, S, _ = x.shape
    nt = S // ATTN_TILE
    cpt = GATE_ROWS // RET_CHUNK
    in_specs = []
    args = []
    for (_, dil), qkv in zip(ATTN_GROUPS, qkv_groups):
        rows = ATTN_TILE // dil
        ratio = rows // BAND

        in_specs += [
            pl.BlockSpec((None, dil, rows, 3 * HEAD_DIM), lambda b, t, h: (b, 0, t, h)),
            pl.BlockSpec((None, dil, BAND, 3 * HEAD_DIM),
                         lambda b, t, h, ratio=ratio: (b, 0, jnp.maximum(t * ratio - 1, 0), h)),
        ]
        args += [qkv] * 2
    bias = _attn_bias_tables()
    in_specs.append(pl.BlockSpec(bias.shape, lambda b, t, h: (0, 0, 0), pipeline_mode=RESIDENT))

    def wspec(n, col0=0):
        return pl.BlockSpec((D_MODEL, n), lambda b, t, h: (0, col0 // n), pipeline_mode=RESIDENT)

    def row_spec(n):
        return pl.BlockSpec((None, GATE_ROWS, n), lambda b, t, h: (b, t * ATTN_HEADS + h, 0))

    tspec = pl.BlockSpec((GATE_ROWS, LANES), lambda b, t, h: (t * ATTN_HEADS + h, 0))
    in_specs += [
        row_spec(D_MODEL),
        pl.BlockSpec((1, D_MODEL), lambda b, t, h: (0, 0)),
        wspec(2 * RET_QK_WIDTH, COL_RQ),
        wspec(GROUP_WIDTH, COL_AG), wspec(RET_V_WIDTH, COL_RG), wspec(2 * D_MODEL, COL_MG),
        tspec, tspec,
        pl.BlockSpec((GATE_ROWS, RET_QK_WIDTH), lambda b, t, h: (0, 0), pipeline_mode=RESIDENT),
        pl.BlockSpec((1, 2 * D_MODEL), lambda b, t, h: (0, 0)),
    ]
    out_widths = (GROUP_WIDTH, RET_V_WIDTH, 2 * D_MODEL)
    per_group = pltpu.VMEM((N_GROUPS, ATTN_TILE, HEAD_DIM), F32)
    return pl.pallas_call(
        _attn_gates_kernel,
        grid=(B, nt, ATTN_HEADS),
        in_specs=in_specs,
        out_specs=[
            pl.BlockSpec((None, ATTN_TILE, HEAD_DIM), lambda b, t, h: (b, t, h)),
            row_spec(RET_QK_WIDTH),
            pl.BlockSpec((None, cpt, RET_QK_WIDTH, RET_CHUNK),
                         lambda b, t, h: (b, t * ATTN_HEADS + h, 0, 0)),
            *[row_spec(n) for n in out_widths],
        ],
        out_shape=[
            jax.ShapeDtypeStruct((B, S, GROUP_WIDTH), BF16),
            jax.ShapeDtypeStruct((B, S, RET_QK_WIDTH), BF16),
            jax.ShapeDtypeStruct((B, S // RET_CHUNK, RET_QK_WIDTH, RET_CHUNK), BF16),
            *[jax.ShapeDtypeStruct((B, S, n), BF16) for n in out_widths],
        ],
        scratch_shapes=[per_group, per_group, per_group,
                        pltpu.VMEM((ATTN_TILE, HEAD_DIM), F32)],
        compiler_params=pltpu.CompilerParams(
            dimension_semantics=("arbitrary", "arbitrary", "arbitrary"),
            vmem_limit_bytes=VMEM_LIMIT),
        name="attn_gates",
    )(*args, bias, x, lnw, w16, w16, w16, w16, cos_t, sin_t, kdec, bg)


def _retention_kernel(q_ref, kt_ref, v_ref, gate_ref, gnw_ref, eps_ref, cd_ref,
                      out_ref, state, p_scr):
    @pl.when(pl.program_id(1) == 0)
    def _():
        state[...] = jnp.zeros_like(state)

    dk, dv, C = RET_KEY_DIM, RET_VALUE_DIM, RET_CHUNK
    row_k = lax.broadcasted_iota(jnp.int32, (2 * dk, C), 0)
    row = lax.broadcasted_iota(jnp.int32, (2 * dk, 2 * dv), 0)
    col = lax.broadcasted_iota(jnp.int32, (C, 2 * dv), 1)
    diag_blk = (row < dk) == (col < dv)
    causal = (col % C) <= lax.broadcasted_iota(jnp.int32, (C, 2 * C), 0)
    units = [(bi, j) for bi in range(RET_BATCH) for j in range(RET_HEADS // 2)]

    def q_pair(bi, j):
        return q_ref[bi, :, j * 2 * dk:(j + 1) * 2 * dk]

    def kt_pair(bi, j):
        return kt_ref[bi, j * 2 * dk:(j + 1) * 2 * dk, :]

    def v_pair(bi, j):
        return v_ref[bi, :, j * 2 * dv:(j + 1) * 2 * dv]

    for bi, j in units:
        kt = kt_pair(bi, j)
        zero = jnp.zeros_like(kt)
        kt_blk = jnp.concatenate([jnp.where(row_k < dk, kt, zero),
                                  jnp.where(row_k >= dk, kt, zero)], axis=1)
        s = _dot(q_pair(bi, j), kt_blk).astype(BF16)
        p_scr[bi, j] = jnp.where(causal, s, jnp.zeros_like(s))

    for bi, j in units:
        v = v_pair(bi, j)
        zero = jnp.zeros_like(v)
        v_blk = jnp.concatenate([jnp.where(col < dv, v, zero),
                                 jnp.where(col >= dv, v, zero)], axis=0)
        u = _dot(jnp.concatenate([p_scr[bi, j], q_pair(bi, j)], axis=1),
                 jnp.concatenate([v_blk, state[bi, j].astype(BF16)], axis=0))
        for hh in range(2):
            h = 2 * j + hh
            hs = slice(h * dv, (h + 1) * dv)
            uh = u[:, hh * dv:(hh + 1) * dv]
            uh = uh * lax.rsqrt(jnp.sum(uh * uh, axis=-1, keepdims=True) + eps_ref[h])
            out_ref[bi, :, hs] = (uh * gnw_ref[:, hs]).astype(BF16) * gate_ref[bi, :, hs]

    for bi, j in units:
        kv = _dot(kt_pair(bi, j), v_pair(bi, j))
        state[bi, j] = cd_ref[j] * (state[bi, j] + jnp.where(diag_blk, kv, 0.0))


def _retention(rq, rkt, rv, gate, gnw, eps_t, cd):
    B, S, _ = rv.shape
    nc = S // RET_CHUNK
    npair = RET_HEADS // 2
    pair_shape = (RET_BATCH, npair, 2 * RET_KEY_DIM, 2 * RET_VALUE_DIM)

    def const(shape):
        return pl.BlockSpec(shape, lambda b, n: (0,) * len(shape))

    def rows(width):
        return pl.BlockSpec((RET_BATCH, RET_CHUNK, width), lambda b, n: (b, n, 0))

    return pl.pallas_call(
        _retention_kernel,
        grid=(B // RET_BATCH, nc),
        in_specs=[
            rows(RET_QK_WIDTH),
            pl.BlockSpec((RET_BATCH, None, RET_QK_WIDTH, RET_CHUNK), lambda b, n: (b, n, 0, 0)),
            rows(RET_V_WIDTH),
            rows(RET_V_WIDTH),
            const((1, RET_V_WIDTH)),
            const((RET_HEADS, RET_CHUNK, RET_VALUE_DIM)),
            const(pair_shape[1:]),
        ],
        out_specs=rows(RET_V_WIDTH),
        out_shape=jax.ShapeDtypeStruct((B, S, RET_V_WIDTH), BF16),
        scratch_shapes=[pltpu.VMEM(pair_shape, F32),
                        pltpu.VMEM((RET_BATCH, npair, RET_CHUNK, 2 * RET_CHUNK), BF16)],
        compiler_params=pltpu.CompilerParams(
            dimension_semantics=("arbitrary", "arbitrary"), vmem_limit_bytes=VMEM_LIMIT),
        name="retention",
    )(rq, rkt, rv, gate, gnw, eps_t, cd)


def _merge_out_kernel(a_ref, ag_ref, r_ref, mg_ref, x_ref, wa_ref, wr_ref, wo_ref, lnf_ref,
                      out_ref):
    def part(rows):
        y_attn = _dot(a_ref[rows, :] * ag_ref[rows, :], wa_ref[...])
        y_ret = _dot(r_ref[rows, :], wr_ref[...])
        yield
        merged = (mg_ref[rows, :D_MODEL].astype(F32) * y_attn
                  + mg_ref[rows, D_MODEL:].astype(F32) * y_ret)
        h = x_ref[rows, :] + _dot(merged.astype(BF16), wo_ref[...])
        yield
        out_ref[rows, :] = _rmsnorm_rows(h, lnf_ref[...])

    n_parts = OUT_ROWS // OUT_PART
    parts = [part(slice(i * OUT_PART, (i + 1) * OUT_PART)) for i in range(n_parts)]
    for step in range(n_parts + 2):
        for lag in range(3):
            i = step - lag
            if 0 <= i < n_parts:
                next(parts[i], None)


def _merge_out(attn, ag, ret_g, mg, x, wa, wr, wo, lnf):
    B, S, _ = x.shape
    nt = S // OUT_ROWS

    def rows(n):
        return pl.BlockSpec((None, OUT_ROWS, n), lambda b, t: (b, t, 0))

    def const(shape):
        return pl.BlockSpec(shape, lambda b, t: (0,) * len(shape), pipeline_mode=RESIDENT)

    return pl.pallas_call(
        _merge_out_kernel,
        grid=(B, nt),
        in_specs=[rows(GROUP_WIDTH), rows(GROUP_WIDTH), rows(RET_V_WIDTH), rows(2 * D_MODEL),
                  rows(D_MODEL),
                  const((GROUP_WIDTH, D_MODEL)), const((RET_V_WIDTH, D_MODEL)),
                  const((D_MODEL, D_MODEL)), const((1, D_MODEL))],
        out_specs=rows(D_MODEL),
        out_shape=jax.ShapeDtypeStruct((B, S, D_MODEL), F32),
        compiler_params=pltpu.CompilerParams(
            dimension_semantics=("arbitrary", "arbitrary"), vmem_limit_bytes=VMEM_LIMIT),
        name="merge_out",
    )(attn, ag, ret_g, mg, x, wa, wr, wo, lnf)


def _rope_tables(S):
    half = RET_KEY_DIM // 2
    inv_freq = ROPE_BASE ** (-jnp.linspace(0.0, 1.0, half, dtype=F32))
    step = 64
    a = (jnp.arange(S // step, dtype=F32) * step)[:, None] * inv_freq[None, :]
    b = jnp.arange(step, dtype=F32)[:, None] * inv_freq[None, :]
    ca, sa, cb, sb = jnp.cos(a)[:, None], jnp.sin(a)[:, None], jnp.cos(b)[None], jnp.sin(b)[None]
    cos = (ca * cb - sa * sb).reshape(S, half)
    sin = (sa * cb + ca * sb).reshape(S, half)
    heads_per_tile = LANES // RET_KEY_DIM
    sign = jnp.where(jnp.arange(LANES) % 2 == 0, -1.0, 1.0).astype(F32)
    cos_t = jnp.tile(jnp.repeat(cos, 2, axis=1), (1, heads_per_tile))
    sin_t = jnp.tile(jnp.repeat(sin, 2, axis=1), (1, heads_per_tile)) * sign[None, :]
    return cos_t, sin_t


def _retention_constants():
    H, C, dk, dv = RET_HEADS, RET_CHUNK, RET_KEY_DIM, RET_VALUE_DIM
    log_gamma = jnp.log(1.0 - 2.0 ** (-5.0 - jnp.arange(H, dtype=F32)))
    idx = jnp.arange(C, dtype=F32)
    inv_decay = jnp.exp(-(idx + 1.0)[None, :] * log_gamma[:, None])
    chunk_decay = jnp.exp(C * log_gamma)
    kdec = jnp.repeat(inv_decay.T, dk, axis=1) * dk ** -0.5
    kdec = jnp.tile(kdec, (GATE_ROWS // C, 1))
    eps_t = jnp.broadcast_to((dv * NORM_EPS * inv_decay * inv_decay)[:, :, None], (H, C, dv))
    cd = jnp.broadcast_to(chunk_decay[:, None, None], (H, dk, 2 * dv)).reshape(H // 2, 2 * dk, 2 * dv)
    return kdec, eps_t, cd


def kernel(x, ln1_w, w_in, b_gate, attn_proj, ret_proj, ret_gn_w, w_out, lnf_w):
    B, S, _ = x.shape
    assert w_in.shape[0] == 1, "single layer"
    w16 = w_in[0].astype(BF16)

    lnw = ln1_w[0].reshape(1, D_MODEL)
    q1, q2, q3, rv = _qkv_proj(x, lnw, w16)
    cos_t, sin_t = _rope_tables(S)
    kdec, eps_t, cd = _retention_constants()
    attn, rq, rkt, ag, rg, mg = _attn_gates(
        (q1, q2, q3), x, lnw, w16, cos_t, sin_t, kdec, b_gate[0].reshape(1, 2 * D_MODEL))
    gnw = (ret_gn_w[0] * RET_VALUE_DIM ** 0.5).reshape(1, RET_V_WIDTH)
    ret_g = _retention(rq, rkt, rv, rg, gnw, eps_t, cd)

    return _merge_out(attn, ag, ret_g, mg, x,
                      attn_proj[0].astype(BF16), ret_proj[0].astype(BF16),
                      w_out[0].astype(BF16), lnf_w.reshape(1, D_MODEL))
```

```python
import jax
import jax.numpy as jnp
from jax import lax
from jax.experimental import pallas as pl
from jax.experimental.pallas import tpu as pltpu

D_MODEL = 1024
ATTN_GROUPS = ((128, 1), (512, 4), (2048, 16))
N_GROUPS = 3
ATTN_HEADS = 4
HEAD_DIM = 128
GROUP_WIDTH = ATTN_HEADS * HEAD_DIM
QKV_WIDTH = N_GROUPS * GROUP_WIDTH
BAND = 128
RET_HEADS = 8
RET_KEY_DIM = 64
RET_VALUE_DIM = 128
RET_QK_WIDTH = RET_HEADS * RET_KEY_DIM
RET_V_WIDTH = RET_HEADS * RET_VALUE_DIM
RET_CHUNK = 128
ROPE_BASE = 10000.0
COL_AG = 3 * QKV_WIDTH
COL_RQ = COL_AG + GROUP_WIDTH
COL_RV = COL_RQ + 2 * RET_QK_WIDTH
COL_RG = COL_RV + RET_V_WIDTH
COL_MG = COL_RG + RET_V_WIDTH
NORM_EPS = 1e-6
NEG_BIG = -1e30
LANES = 128
LOG2E = 1.4426950408889634
MIX_STRIDE = 4

PROJ_ROWS = 1024
QKV_PARTS = 4
ATTN_TILE = 2048
GATE_ROWS = ATTN_TILE // ATTN_HEADS
RET_BATCH = 8
OUT_ROWS = 1024
OUT_PART = 256
V7X_VMEM_BYTES = 64 * 1024 * 1024
VMEM_LIMIT = V7X_VMEM_BYTES * 7 // 8
GATE_ROW_PARTS = 2
GATE_PIECES = (2, 2, 4, 2, 2)
RESIDENT = pl.Buffered(1)

BF16 = jnp.bfloat16
F32 = jnp.float32


def _rmsnorm_rows(x, w):
    return x * lax.rsqrt(jnp.mean(x * x, axis=-1, keepdims=True) + NORM_EPS) * w


def _dot(a, b):
    return jnp.dot(a, b, preferred_element_type=F32)


def _sigmoid(t):
    return 0.5 * jnp.tanh(0.5 * t) + 0.5


def _silu(t):
    return t * _sigmoid(t)


def _qkv_proj_kernel(x_ref, lnw_ref, *refs):
    wq_ref, wk_ref, wv_ref, wrv_ref = refs[:4]
    out_refs = refs[4:4 + N_GROUPS]
    rv_ref, xn_scr, xn4_scr, lhs_scr = refs[4 + N_GROUPS:]
    n_lane_tiles = D_MODEL // LANES
    q_scale = HEAD_DIM ** -0.5 * LOG2E
    part_rows = PROJ_ROWS // QKV_PARTS
    d4, d16 = ATTN_GROUPS[1][1], ATTN_GROUPS[2][1]
    rows4, rows16 = part_rows // d4, part_rows // d16
    NAT, QPERM, BY4, BY16 = range(4)

    def part(p):
        rows_p = slice(p * part_rows, (p + 1) * part_rows)
        lhs = lhs_scr.at[p]
        xn = _rmsnorm_rows(x_ref[rows_p, :], lnw_ref[...])
        lhs[NAT] = xn.astype(BF16)
        for ct in range(n_lane_tiles):
            xn_scr[ct] = xn[:, ct * LANES:(ct + 1) * LANES]
        sub = BAND // MIX_STRIDE
        for ct in range(n_lane_tiles):
            lanes = slice(ct * LANES, (ct + 1) * LANES)
            lhs[QPERM, :, lanes] = jnp.concatenate(
                [xn_scr[ct, pl.ds(blk * BAND + r4, sub, stride=MIX_STRIDE), :]
                 for blk in range(part_rows // BAND) for r4 in range(MIX_STRIDE)],
                axis=0).astype(BF16)
            by4 = jnp.concatenate(
                [xn_scr[ct, pl.ds(r, rows4, stride=d4), :] for r in range(d4)], axis=0)
            xn4_scr[ct] = by4
            lhs[BY4, :, lanes] = by4.astype(BF16)
        for ct in range(n_lane_tiles):
            by16 = jnp.concatenate(
                [xn4_scr[ct, pl.ds((r % d4) * rows4 + r // d4, rows16, stride=d16 // d4), :]
                 for r in range(d16)], axis=0)
            lhs[BY16, :, ct * LANES:(ct + 1) * LANES] = by16.astype(BF16)
        yield

        rv_ref[rows_p, :] = _dot(lhs[NAT], wrv_ref[...]).astype(BF16)
        for g, (order_q, order_kv) in enumerate(((QPERM, NAT), (BY4, BY4), (BY16, BY16))):
            gcols = slice(g * GROUP_WIDTH, (g + 1) * GROUP_WIDTH)
            dil = ATTN_GROUPS[g][1]
            rows = part_rows // dil
            q = (_dot(lhs[order_q], wq_ref[:, gcols]) * q_scale).astype(BF16)
            k = _dot(lhs[order_kv], wk_ref[:, gcols]).astype(BF16)
            v = _dot(lhs[order_kv], wv_ref[:, gcols]).astype(BF16)
            for r in range(dil):
                src_rows = slice(r * rows, (r + 1) * rows)
                dst_rows = slice(p * rows, (p + 1) * rows)
                for h in range(ATTN_HEADS):
                    hs = slice(h * HEAD_DIM, (h + 1) * HEAD_DIM)
                    for which, val in enumerate((q, k, v)):
                        c0 = (3 * h + which) * HEAD_DIM
                        out_refs[g][r, dst_rows, c0:c0 + HEAD_DIM] = val[src_rows, hs]

    parts = [part(p) for p in range(QKV_PARTS)]
    for gen in parts:
        next(gen)
    for gen in parts:
        next(gen, None)


def _qkv_proj(x, lnw, w16):
    B, S, _ = x.shape
    nt = S // PROJ_ROWS
    w_specs = [pl.BlockSpec((D_MODEL, QKV_WIDTH), lambda b, t, part=part: (0, part),
                            pipeline_mode=RESIDENT) for part in range(3)]
    w_specs.append(pl.BlockSpec((D_MODEL, RET_V_WIDTH), lambda b, t: (0, COL_RV // RET_V_WIDTH),
                                pipeline_mode=RESIDENT))
    return pl.pallas_call(
        _qkv_proj_kernel,
        grid=(B, nt),
        in_specs=[
            pl.BlockSpec((None, PROJ_ROWS, D_MODEL), lambda b, t: (b, t, 0)),
            pl.BlockSpec((1, D_MODEL), lambda b, t: (0, 0)),
            *w_specs,
        ],
        out_specs=[
            *[pl.BlockSpec((None, dil, PROJ_ROWS // dil, QKV_WIDTH), lambda b, t: (b, 0, t, 0))
              for _, dil in ATTN_GROUPS],
            pl.BlockSpec((None, PROJ_ROWS, RET_V_WIDTH), lambda b, t: (b, t, 0)),
        ],
        out_shape=[
            *[jax.ShapeDtypeStruct((B, dil, S // dil, QKV_WIDTH), BF16) for _, dil in ATTN_GROUPS],
            jax.ShapeDtypeStruct((B, S, RET_V_WIDTH), BF16),
        ],
        scratch_shapes=[pltpu.VMEM((D_MODEL // LANES, PROJ_ROWS // QKV_PARTS, LANES), F32),
                        pltpu.VMEM((D_MODEL // LANES, PROJ_ROWS // QKV_PARTS, LANES), F32),
                        pltpu.VMEM((QKV_PARTS, 4, PROJ_ROWS // QKV_PARTS, D_MODEL), BF16)],
        compiler_params=pltpu.CompilerParams(
            dimension_semantics=("arbitrary", "arbitrary"), vmem_limit_bytes=VMEM_LIMIT),
        name="qkv_proj",
    )(x, lnw, w16, w16, w16, w16)


def _swap_lane_pairs(t):
    n = t.shape[-1]
    lane = lax.broadcasted_iota(jnp.int32, t.shape, 1)
    up = pltpu.roll(t, n - 1, axis=1)
    down = pltpu.roll(t, 1, axis=1)
    return jnp.where(lane % 2 == 0, up, down)


def _gate_proj_items(x_ref, lnw_ref, wqk_ref, wg_ref, wrg_ref, wmg_ref,
                     cos_ref, sin_ref, kdec_ref, bg_ref,
                     rq_ref, rkt_ref, ag_ref, rg_ref, mg_ref):
    cache = {}

    def lhs():
        if "xb" not in cache:
            cache["xb"] = _rmsnorm_rows(x_ref[...], lnw_ref[...]).astype(BF16)
        return cache["xb"]

    def rope(t, rows):
        cos = cos_ref[rows, :]
        sin = sin_ref[rows, :]
        tiles = [t[:, i * LANES:(i + 1) * LANES] for i in range(t.shape[1] // LANES)]
        return jnp.concatenate([tt * cos + _swap_lane_pairs(tt) * sin for tt in tiles], axis=1)

    def retention_q(cols, rows):
        t = _dot(lhs()[rows], wqk_ref[:, cols])
        yield
        rq_ref[rows, cols] = rope(t, rows).astype(BF16)

    def retention_k(cols, rows):
        t = _dot(lhs()[rows], wqk_ref[:, RET_QK_WIDTH + cols.start:RET_QK_WIDTH + cols.stop])
        yield
        k = rope(t, rows) * kdec_ref[rows, cols]
        for c in range((rows.stop - rows.start) // RET_CHUNK):
            rkt_ref[rows.start // RET_CHUNK + c, cols, :] = (
                k[c * RET_CHUNK:(c + 1) * RET_CHUNK, :].T.astype(BF16))

    def merge_gate(cols, rows):
        t = _dot(lhs()[rows], wmg_ref[:, cols])
        yield
        mg_ref[rows, cols] = _sigmoid(t + bg_ref[:, cols]).astype(BF16)

    def retention_gate(cols, rows):
        t = _dot(lhs()[rows], wrg_ref[:, cols])
        yield
        rg_ref[rows, cols] = _silu(t).astype(BF16)

    def attention_gate(cols, rows):
        t = _dot(lhs()[rows], wg_ref[:, cols])
        yield
        ag_ref[rows, cols] = _silu(t).astype(BF16)

    items = []
    part = GATE_ROWS // GATE_ROW_PARTS
    for fn, total, pieces in ((retention_q, RET_QK_WIDTH, GATE_PIECES[0]),
                              (retention_k, RET_QK_WIDTH, GATE_PIECES[1]),
                              (merge_gate, 2 * D_MODEL, GATE_PIECES[2]),
                              (retention_gate, RET_V_WIDTH, GATE_PIECES[3]),
                              (attention_gate, GROUP_WIDTH, GATE_PIECES[4])):
        width = total // pieces
        items += [lambda fn=fn, cols=slice(i * width, (i + 1) * width),
                  rows=slice(j * part, (j + 1) * part): fn(cols, rows)
                  for i in range(pieces) for j in range(GATE_ROW_PARTS)]
    return items


def _attention_items(in_refs, bias_ref, out_ref, acc_scr, max_scr, den_scr, nat_scr, first):
    ones = jnp.ones((2 * BAND, HEAD_DIM), BF16)
    quarter = ATTN_TILE // MIX_STRIDE

    def unit(g, r, n):
        dil = ATTN_GROUPS[g][1]
        cur_ref, prev_ref = in_refs[2 * g:2 * g + 2]
        q_cols, k_cols, v_cols = (slice(i * HEAD_DIM, (i + 1) * HEAD_DIM) for i in range(3))
        bias_base = 2 if dil == 1 else 0
        q = cur_ref[r, n * BAND:(n + 1) * BAND, q_cols]
        if n == 0:
            k = jnp.concatenate([prev_ref[r, :, k_cols], cur_ref[r, 0:BAND, k_cols]], axis=0)
            v = jnp.concatenate([prev_ref[r, :, v_cols], cur_ref[r, 0:BAND, v_cols]], axis=0)
            bias = bias_ref[bias_base + first]
        else:
            k = cur_ref[r, (n - 1) * BAND:(n + 1) * BAND, k_cols]
            v = cur_ref[r, (n - 1) * BAND:(n + 1) * BAND, v_cols]
            bias = bias_ref[bias_base]
        s = lax.dot_general(q, k, (((1,), (1,)), ((), ())), preferred_element_type=F32) + bias
        yield
        m = jnp.max(s, axis=-1, keepdims=True)
        p = jnp.exp2(s - m).astype(BF16)
        yield
        res = _dot(p, jnp.concatenate([v, ones], axis=1))
        acc, den = res[:, :HEAD_DIM], res[:, HEAD_DIM:]
        mb = jnp.broadcast_to(m, (BAND, HEAD_DIM))
        if dil == 1:
            sub = BAND // MIX_STRIDE
            for r4 in range(MIX_STRIDE):
                dst = pl.ds(r4 * quarter + n * sub, sub)
                src = slice(r4 * sub, (r4 + 1) * sub)
                acc_scr[g, dst, :] = acc[src]
                max_scr[g, dst, :] = mb[src]
                den_scr[g, dst, :] = den[src]
        else:
            if dil == MIX_STRIDE:
                dst = pl.ds(r * quarter + n * BAND, BAND)
            else:
                sub_stride = dil // MIX_STRIDE
                dst = pl.ds((r % MIX_STRIDE) * quarter + r // MIX_STRIDE
                            + n * BAND * sub_stride, BAND, stride=sub_stride)
            acc_scr[g, dst, :] = acc
            max_scr[g, dst, :] = mb
            den_scr[g, dst, :] = den

    def mix(r4, ch):
        rows = pl.ds(r4 * quarter + ch * BAND, BAND)
        m0, m1, m2 = max_scr[0, rows, :], max_scr[1, rows, :], max_scr[2, rows, :]
        mx = jnp.maximum(jnp.maximum(m0, m1), m2)
        w0 = jnp.exp2(m0 - mx)
        w1 = jnp.exp2(m1 - mx)
        w2 = jnp.exp2(m2 - mx)
        num = w0 * acc_scr[0, rows, :] + w1 * acc_scr[1, rows, :] + w2 * acc_scr[2, rows, :]
        den = w0 * den_scr[0, rows, :] + w1 * den_scr[1, rows, :] + w2 * den_scr[2, rows, :]
        nat_scr[pl.ds(r4 + ch * BAND * MIX_STRIDE, BAND, stride=MIX_STRIDE), :] = num / den

    def write_out():
        out_ref[...] = nat_scr[...].astype(BF16)

    def mix_quarter(r4):
        for ch in range(quarter // BAND):
            mix(r4, ch)

    units = [lambda n=n: unit(0, 0, n) for n in range(ATTN_TILE // BAND)]
    quarter_done = []
    for r4 in range(MIX_STRIDE):
        units += [lambda g=g, r=r, n=n: unit(g, r, n)
                  for g in range(1, N_GROUPS)
                  for r in range(r4, ATTN_GROUPS[g][1], MIX_STRIDE)
                  for n in range(ATTN_TILE // ATTN_GROUPS[g][1] // BAND)]
        quarter_done.append(len(units))
    return units, quarter_done, mix_quarter, write_out


def _attn_gates_kernel(*refs):
    n_attn = 2 * N_GROUPS
    in_refs = refs[:n_attn]
    bias_ref = refs[n_attn]
    proj_in = refs[n_attn + 1:n_attn + 11]
    attn_out = refs[n_attn + 11]
    proj_out = refs[n_attn + 12:n_attn + 17]
    acc_scr, max_scr, den_scr, nat_scr = refs[n_attn + 17:]
    first = (pl.program_id(1) == 0).astype(jnp.int32)
    unit_makers, quarter_done, mix_quarter, write_out = _attention_items(
        in_refs, bias_ref, attn_out, acc_scr, max_scr, den_scr, nat_scr, first)
    proj = _gate_proj_items(*proj_in, *proj_out)
    per_round = len(unit_makers) // len(proj)
    assert per_round * len(proj) == len(unit_makers)
    rounds = [unit_makers[i * per_round:(i + 1) * per_round] for i in range(len(proj))]

    def advance(gen):
        next(gen, None)

    prev_units = []
    mixed = 0
    for r in range(len(rounds) + 1):
        units = [make() for make in rounds[r]] if r < len(rounds) else []
        for j in range(max(len(units), len(prev_units))):
            if j < len(units):
                advance(units[j])
            if j < len(prev_units):
                advance(prev_units[j])
        while mixed < len(quarter_done) and quarter_done[mixed] <= r * per_round:
            mix_quarter(mixed)
            mixed += 1
        if r < len(proj):
            for _ in proj[r]():
                pass
        for u in units:
            advance(u)
        prev_units = units
    write_out()


def _attn_bias_tables():
    rho = jnp.arange(BAND)[:, None]
    c = jnp.arange(2 * BAND)[None, :]
    sub = BAND // MIX_STRIDE
    tables = []
    for a in (rho, MIX_STRIDE * (rho % sub) + rho // sub):
        ok = (c >= a) & (c <= a + BAND)
        tables += [ok, ok & (c >= BAND)]
    return jnp.where(jnp.stack(tables), 0.0, NEG_BIG).astype(F32)


def _attn_gates(qkv_groups, x, lnw, w16, cos_t, sin_t, kdec, bg):
    B, S, _ = x.shape
    nt = S // ATTN_TILE
    cpt = GATE_ROWS // RET_CHUNK
    in_specs = []
    args = []
    for (_, dil), qkv in zip(ATTN_GROUPS, qkv_groups):
        rows = ATTN_TILE // dil
        ratio = rows // BAND

        in_specs += [
            pl.BlockSpec((None, dil, rows, 3 * HEAD_DIM), lambda b, t, h: (b, 0, t, h)),
            pl.BlockSpec((None, dil, BAND, 3 * HEAD_DIM),
                         lambda b, t, h, ratio=ratio: (b, 0, jnp.maximum(t * ratio - 1, 0), h)),
        ]
        args += [qkv] * 2
    bias = _attn_bias_tables()
    in_specs.append(pl.BlockSpec(bias.shape, lambda b, t, h: (0, 0, 0), pipeline_mode=RESIDENT))

    def wspec(n, col0=0):
        return pl.BlockSpec((D_MODEL, n), lambda b, t, h: (0, col0 // n), pipeline_mode=RESIDENT)

    def row_spec(n):
        return pl.BlockSpec((None, GATE_ROWS, n), lambda b, t, h: (b, t * ATTN_HEADS + h, 0))

    tspec = pl.BlockSpec((GATE_ROWS, LANES), lambda b, t, h: (t * ATTN_HEADS + h, 0))
    in_specs += [
        row_spec(D_MODEL),
        pl.BlockSpec((1, D_MODEL), lambda b, t, h: (0, 0)),
        wspec(2 * RET_QK_WIDTH, COL_RQ),
        wspec(GROUP_WIDTH, COL_AG), wspec(RET_V_WIDTH, COL_RG), wspec(2 * D_MODEL, COL_MG),
        tspec, tspec,
        pl.BlockSpec((GATE_ROWS, RET_QK_WIDTH), lambda b, t, h: (0, 0), pipeline_mode=RESIDENT),
        pl.BlockSpec((1, 2 * D_MODEL), lambda b, t, h: (0, 0)),
    ]
    out_widths = (GROUP_WIDTH, RET_V_WIDTH, 2 * D_MODEL)
    per_group = pltpu.VMEM((N_GROUPS, ATTN_TILE, HEAD_DIM), F32)
    return pl.pallas_call(
        _attn_gates_kernel,
        grid=(B, nt, ATTN_HEADS),
        in_specs=in_specs,
        out_specs=[
            pl.BlockSpec((None, ATTN_TILE, HEAD_DIM), lambda b, t, h: (b, t, h)),
            row_spec(RET_QK_WIDTH),
            pl.BlockSpec((None, cpt, RET_QK_WIDTH, RET_CHUNK),
                         lambda b, t, h: (b, t * ATTN_HEADS + h, 0, 0)),
            *[row_spec(n) for n in out_widths],
        ],
        out_shape=[
            jax.ShapeDtypeStruct((B, S, GROUP_WIDTH), BF16),
            jax.ShapeDtypeStruct((B, S, RET_QK_WIDTH), BF16),
            jax.ShapeDtypeStruct((B, S // RET_CHUNK, RET_QK_WIDTH, RET_CHUNK), BF16),
            *[jax.ShapeDtypeStruct((B, S, n), BF16) for n in out_widths],
        ],
        scratch_shapes=[per_group, per_group, per_group,
                        pltpu.VMEM((ATTN_TILE, HEAD_DIM), F32)],
        compiler_params=pltpu.CompilerParams(
            dimension_semantics=("arbitrary", "arbitrary", "arbitrary"),
            vmem_limit_bytes=VMEM_LIMIT),
        name="attn_gates",
    )(*args, bias, x, lnw, w16, w16, w16, w16, cos_t, sin_t, kdec, bg)


def _retention_kernel(q_ref, kt_ref, v_ref, gate_ref, gnw_ref, eps_ref, cd_ref,
                      out_ref, state, p_scr):
    @pl.when(pl.program_id(1) == 0)
    def _():
        state[...] = jnp.zeros_like(state)

    dk, dv, C = RET_KEY_DIM, RET_VALUE_DIM, RET_CHUNK
    row_k = lax.broadcasted_iota(jnp.int32, (2 * dk, C), 0)
    row = lax.broadcasted_iota(jnp.int32, (2 * dk, 2 * dv), 0)
    col = lax.broadcasted_iota(jnp.int32, (C, 2 * dv), 1)
    diag_blk = (row < dk) == (col < dv)
    causal = (col % C) <= lax.broadcasted_iota(jnp.int32, (C, 2 * C), 0)
    units = [(bi, j) for bi in range(RET_BATCH) for j in range(RET_HEADS // 2)]

    def q_pair(bi, j):
        return q_ref[bi, :, j * 2 * dk:(j + 1) * 2 * dk]

    def kt_pair(bi, j):
        return kt_ref[bi, j * 2 * dk:(j + 1) * 2 * dk, :]

    def v_pair(bi, j):
        return v_ref[bi, :, j * 2 * dv:(j + 1) * 2 * dv]

    for bi, j in units:
        kt = kt_pair(bi, j)
        zero = jnp.zeros_like(kt)
        kt_blk = jnp.concatenate([jnp.where(row_k < dk, kt, zero),
                                  jnp.where(row_k >= dk, kt, zero)], axis=1)
        s = _dot(q_pair(bi, j), kt_blk).astype(BF16)
        p_scr[bi, j] = jnp.where(causal, s, jnp.zeros_like(s))

    for bi, j in units:
        v = v_pair(bi, j)
        zero = jnp.zeros_like(v)
        v_blk = jnp.concatenate([jnp.where(col < dv, v, zero),
                                 jnp.where(col >= dv, v, zero)], axis=0)
        u = _dot(jnp.concatenate([p_scr[bi, j], q_pair(bi, j)], axis=1),
                 jnp.concatenate([v_blk, state[bi, j].astype(BF16)], axis=0))
        for hh in range(2):
            h = 2 * j + hh
            hs = slice(h * dv, (h + 1) * dv)
            uh = u[:, hh * dv:(hh + 1) * dv]
            uh = uh * lax.rsqrt(jnp.sum(uh * uh, axis=-1, keepdims=True) + eps_ref[h])
            out_ref[bi, :, hs] = (uh * gnw_ref[:, hs]).astype(BF16) * gate_ref[bi, :, hs]

    for bi, j in units:
        kv = _dot(kt_pair(bi, j), v_pair(bi, j))
        state[bi, j] = cd_ref[j] * (state[bi, j] + jnp.where(diag_blk, kv, 0.0))


def _retention(rq, rkt, rv, gate, gnw, eps_t, cd):
    B, S, _ = rv.shape
    nc = S // RET_CHUNK
    npair = RET_HEADS // 2
    pair_shape = (RET_BATCH, npair, 2 * RET_KEY_DIM, 2 * RET_VALUE_DIM)

    def const(shape):
        return pl.BlockSpec(shape, lambda b, n: (0,) * len(shape))

    def rows(width):
        return pl.BlockSpec((RET_BATCH, RET_CHUNK, width), lambda b, n: (b, n, 0))

    return pl.pallas_call(
        _retention_kernel,
        grid=(B // RET_BATCH, nc),
        in_specs=[
            rows(RET_QK_WIDTH),
            pl.BlockSpec((RET_BATCH, None, RET_QK_WIDTH, RET_CHUNK), lambda b, n: (b, n, 0, 0)),
            rows(RET_V_WIDTH),
            rows(RET_V_WIDTH),
            const((1, RET_V_WIDTH)),
            const((RET_HEADS, RET_CHUNK, RET_VALUE_DIM)),
            const(pair_shape[1:]),
        ],
        out_specs=rows(RET_V_WIDTH),
        out_shape=jax.ShapeDtypeStruct((B, S, RET_V_WIDTH), BF16),
        scratch_shapes=[pltpu.VMEM(pair_shape, F32),
                        pltpu.VMEM((RET_BATCH, npair, RET_CHUNK, 2 * RET_CHUNK), BF16)],
        compiler_params=pltpu.CompilerParams(
            dimension_semantics=("arbitrary", "arbitrary"), vmem_limit_bytes=VMEM_LIMIT),
        name="retention",
    )(rq, rkt, rv, gate, gnw, eps_t, cd)


def _merge_out_kernel(a_ref, ag_ref, r_ref, mg_ref, x_ref, wa_ref, wr_ref, wo_ref, lnf_ref,
                      out_ref):
    def part(rows):
        y_attn = _dot(a_ref[rows, :] * ag_ref[rows, :], wa_ref[...])
        y_ret = _dot(r_ref[rows, :], wr_ref[...])
        yield
        merged = (mg_ref[rows, :D_MODEL].astype(F32) * y_attn
                  + mg_ref[rows, D_MODEL:].astype(F32) * y_ret)
        h = x_ref[rows, :] + _dot(merged.astype(BF16), wo_ref[...])
        yield
        out_ref[rows, :] = _rmsnorm_rows(h, lnf_ref[...])

    n_parts = OUT_ROWS // OUT_PART
    parts = [part(slice(i * OUT_PART, (i + 1) * OUT_PART)) for i in range(n_parts)]
    for step in range(n_parts + 2):
        for lag in range(3):
            i = step - lag
            if 0 <= i < n_parts:
                next(parts[i], None)


def _merge_out(attn, ag, ret_g, mg, x, wa, wr, wo, lnf):
    B, S, _ = x.shape
    nt = S // OUT_ROWS

    def rows(n):
        return pl.BlockSpec((None, OUT_ROWS, n), lambda b, t: (b, t, 0))

    def const(shape):
        return pl.BlockSpec(shape, lambda b, t: (0,) * len(shape), pipeline_mode=RESIDENT)

    return pl.pallas_call(
        _merge_out_kernel,
        grid=(B, nt),
        in_specs=[rows(GROUP_WIDTH), rows(GROUP_WIDTH), rows(RET_V_WIDTH), rows(2 * D_MODEL),
                  rows(D_MODEL),
                  const((GROUP_WIDTH, D_MODEL)), const((RET_V_WIDTH, D_MODEL)),
                  const((D_MODEL, D_MODEL)), const((1, D_MODEL))],
        out_specs=rows(D_MODEL),
        out_shape=jax.ShapeDtypeStruct((B, S, D_MODEL), F32),
        compiler_params=pltpu.CompilerParams(
            dimension_semantics=("arbitrary", "arbitrary"), vmem_limit_bytes=VMEM_LIMIT),
        name="merge_out",
    )(attn, ag, ret_g, mg, x, wa, wr, wo, lnf)


def _rope_tables(S):
    half = RET_KEY_DIM // 2
    inv_freq = ROPE_BASE ** (-jnp.linspace(0.0, 1.0, half, dtype=F32))
    step = 64
    a = (jnp.arange(S // step, dtype=F32) * step)[:, None] * inv_freq[None, :]
    b = jnp.arange(step, dtype=F32)[:, None] * inv_freq[None, :]
    ca, sa, cb, sb = jnp.cos(a)[:, None], jnp.sin(a)[:, None], jnp.cos(b)[None], jnp.sin(b)[None]
    cos = (ca * cb - sa * sb).reshape(S, half)
    sin = (sa * cb + ca * sb).reshape(S, half)
    heads_per_tile = LANES // RET_KEY_DIM
    sign = jnp.where(jnp.arange(LANES) % 2 == 0, -1.0, 1.0).astype(F32)
    cos_t = jnp.tile(jnp.repeat(cos, 2, axis=1), (1, heads_per_tile))
    sin_t = jnp.tile(jnp.repeat(sin, 2, axis=1), (1, heads_per_tile)) * sign[None, :]
    return cos_t, sin_t


def _retention_constants():
    H, C, dk, dv = RET_HEADS, RET_CHUNK, RET_KEY_DIM, RET_VALUE_DIM
    log_gamma = jnp.log(1.0 - 2.0 ** (-5.0 - jnp.arange(H, dtype=F32)))
    idx = jnp.arange(C, dtype=F32)
    inv_decay = jnp.exp(-(idx + 1.0)[None, :] * log_gamma[:, None])
    chunk_decay = jnp.exp(C * log_gamma)
    kdec = jnp.repeat(inv_decay.T, dk, axis=1) * dk ** -0.5
    kdec = jnp.tile(kdec, (GATE_ROWS // C, 1))
    eps_t = jnp.broadcast_to((dv * NORM_EPS * inv_decay * inv_decay)[:, :, None], (H, C, dv))
    cd = jnp.broadcast_to(chunk_decay[:, None, None], (H, dk, 2 * dv)).reshape(H // 2, 2 * dk, 2 * dv)
    return kdec, eps_t, cd


def kernel(x, ln1_w, w_in, b_gate, attn_proj, ret_proj, ret_gn_w, w_out, lnf_w):
    B, S, _ = x.shape
    assert w_in.shape[0] == 1, "single layer"
    w16 = w_in[0].astype(BF16)

    lnw = ln1_w[0].reshape(1, D_MODEL)
    q1, q2, q3, rv = _qkv_proj(x, lnw, w16)
    cos_t, sin_t = _rope_tables(S)
    kdec, eps_t, cd = _retention_constants()
    attn, rq, rkt, ag, rg, mg = _attn_gates(
        (q1, q2, q3), x, lnw, w16, cos_t, sin_t, kdec, b_gate[0].reshape(1, 2 * D_MODEL))
    gnw = (ret_gn_w[0] * RET_VALUE_DIM ** 0.5).reshape(1, RET_V_WIDTH)
    ret_g = _retention(rq, rkt, rv, rg, gnw, eps_t, cd)

    return _merge_out(attn, ag, ret_g, mg, x,
                      attn_proj[0].astype(BF16), ret_proj[0].astype(BF16),
                      w_out[0].astype(BF16), lnf_w.reshape(1, D_MODEL))
```

```python
import jax
import jax.numpy as jnp
from jax import lax
from jax.experimental import pallas as pl
from jax.experimental.pallas import tpu as pltpu

D_MODEL = 1024
ATTN_GROUPS = ((128, 1), (512, 4), (2048, 16))
N_GROUPS = 3
ATTN_HEADS = 4
HEAD_DIM = 128
GROUP_WIDTH = ATTN_HEADS * HEAD_DIM
QKV_WIDTH = N_GROUPS * GROUP_WIDTH
BAND = 128
RET_HEADS = 8
RET_KEY_DIM = 64
RET_VALUE_DIM = 128
RET_QK_WIDTH = RET_HEADS * RET_KEY_DIM
RET_V_WIDTH = RET_HEADS * RET_VALUE_DIM
RET_CHUNK = 128
ROPE_BASE = 10000.0
COL_AG = 3 * QKV_WIDTH
COL_RQ = COL_AG + GROUP_WIDTH
COL_RV = COL_RQ + 2 * RET_QK_WIDTH
COL_RG = COL_RV + RET_V_WIDTH
COL_MG = COL_RG + RET_V_WIDTH
NORM_EPS = 1e-6
NEG_BIG = -1e30
LANES = 128
LOG2E = 1.4426950408889634
MIX_STRIDE = 4

PROJ_ROWS = 1024
QKV_PARTS = 4
ATTN_TILE = 2048
GATE_ROWS = ATTN_TILE // ATTN_HEADS
RET_BATCH = 8
OUT_ROWS = 1024
OUT_PART = 256
V7X_VMEM_BYTES = 64 * 1024 * 1024
VMEM_LIMIT = V7X_VMEM_BYTES * 7 // 8
GATE_ROW_PARTS = 2
GATE_PIECES = (2, 2, 4, 2, 2)
RESIDENT = pl.Buffered(1)

BF16 = jnp.bfloat16
F32 = jnp.float32


def _rmsnorm_rows(x, w):
    return x * lax.rsqrt(jnp.mean(x * x, axis=-1, keepdims=True) + NORM_EPS) * w


def _dot(a, b):
    return jnp.dot(a, b, preferred_element_type=F32)


def _sigmoid(t):
    return 0.5 * jnp.tanh(0.5 * t) + 0.5


def _silu(t):
    return t * _sigmoid(t)


def _qkv_proj_kernel(x_ref, lnw_ref, *refs):
    wq_ref, wk_ref, wv_ref, wrv_ref = refs[:4]
    out_refs = refs[4:4 + N_GROUPS]
    rv_ref, xn_scr, xn4_scr, lhs_scr = refs[4 + N_GROUPS:]
    n_lane_tiles = D_MODEL // LANES
    q_scale = HEAD_DIM ** -0.5 * LOG2E
    part_rows = PROJ_ROWS // QKV_PARTS
    d4, d16 = ATTN_GROUPS[1][1], ATTN_GROUPS[2][1]
    rows4, rows16 = part_rows // d4, part_rows // d16
    NAT, QPERM, BY4, BY16 = range(4)

    def part(p):
        rows_p = slice(p * part_rows, (p + 1) * part_rows)
        lhs = lhs_scr.at[p]
        xn = _rmsnorm_rows(x_ref[rows_p, :], lnw_ref[...])
        lhs[NAT] = xn.astype(BF16)
        for ct in range(n_lane_tiles):
            xn_scr[ct] = xn[:, ct * LANES:(ct + 1) * LANES]
        sub = BAND // MIX_STRIDE
        for ct in range(n_lane_tiles):
            lanes = slice(ct * LANES, (ct + 1) * LANES)
            lhs[QPERM, :, lanes] = jnp.concatenate(
                [xn_scr[ct, pl.ds(blk * BAND + r4, sub, stride=MIX_STRIDE), :]
                 for blk in range(part_rows // BAND) for r4 in range(MIX_STRIDE)],
                axis=0).astype(BF16)
            by4 = jnp.concatenate(
                [xn_scr[ct, pl.ds(r, rows4, stride=d4), :] for r in range(d4)], axis=0)
            xn4_scr[ct] = by4
            lhs[BY4, :, lanes] = by4.astype(BF16)
        for ct in range(n_lane_tiles):
            by16 = jnp.concatenate(
                [xn4_scr[ct, pl.ds((r % d4) * rows4 + r // d4, rows16, stride=d16 // d4), :]
                 for r in range(d16)], axis=0)
            lhs[BY16, :, ct * LANES:(ct + 1) * LANES] = by16.astype(BF16)
        yield

        rv_ref[rows_p, :] = _dot(lhs[NAT], wrv_ref[...]).astype(BF16)
        for g, (order_q, order_kv) in enumerate(((QPERM, NAT), (BY4, BY4), (BY16, BY16))):
            gcols = slice(g * GROUP_WIDTH, (g + 1) * GROUP_WIDTH)
            dil = ATTN_GROUPS[g][1]
            rows = part_rows // dil
            q = (_dot(lhs[order_q], wq_ref[:, gcols]) * q_scale).astype(BF16)
            k = _dot(lhs[order_kv], wk_ref[:, gcols]).astype(BF16)
            v = _dot(lhs[order_kv], wv_ref[:, gcols]).astype(BF16)
            for r in range(dil):
                src_rows = slice(r * rows, (r + 1) * rows)
                dst_rows = slice(p * rows, (p + 1) * rows)
                for h in range(ATTN_HEADS):
                    hs = slice(h * HEAD_DIM, (h + 1) * HEAD_DIM)
                    for which, val in enumerate((q, k, v)):
                        c0 = (3 * h + which) * HEAD_DIM
                        out_refs[g][r, dst_rows, c0:c0 + HEAD_DIM] = val[src_rows, hs]

    parts = [part(p) for p in range(QKV_PARTS)]
    for gen in parts:
        next(gen)
    for gen in parts:
        next(gen, None)


def _qkv_proj(x, lnw, w16):
    B, S, _ = x.shape
    nt = S // PROJ_ROWS
    w_specs = [pl.BlockSpec((D_MODEL, QKV_WIDTH), lambda b, t, part=part: (0, part),
                            pipeline_mode=RESIDENT) for part in range(3)]
    w_specs.append(pl.BlockSpec((D_MODEL, RET_V_WIDTH), lambda b, t: (0, COL_RV // RET_V_WIDTH),
                                pipeline_mode=RESIDENT))
    return pl.pallas_call(
        _qkv_proj_kernel,
        grid=(B, nt),
        in_specs=[
            pl.BlockSpec((None, PROJ_ROWS, D_MODEL), lambda b, t: (b, t, 0)),
            pl.BlockSpec((1, D_MODEL), lambda b, t: (0, 0)),
            *w_specs,
        ],
        out_specs=[
            *[pl.BlockSpec((None, dil, PROJ_ROWS // dil, QKV_WIDTH), lambda b, t: (b, 0, t, 0))
              for _, dil in ATTN_GROUPS],
            pl.BlockSpec((None, PROJ_ROWS, RET_V_WIDTH), lambda b, t: (b, t, 0)),
        ],
        out_shape=[
            *[jax.ShapeDtypeStruct((B, dil, S // dil, QKV_WIDTH), BF16) for _, dil in ATTN_GROUPS],
            jax.ShapeDtypeStruct((B, S, RET_V_WIDTH), BF16),
        ],
        scratch_shapes=[pltpu.VMEM((D_MODEL // LANES, PROJ_ROWS // QKV_PARTS, LANES), F32),
                        pltpu.VMEM((D_MODEL // LANES, PROJ_ROWS // QKV_PARTS, LANES), F32),
                        pltpu.VMEM((QKV_PARTS, 4, PROJ_ROWS // QKV_PARTS, D_MODEL), BF16)],
        compiler_params=pltpu.CompilerParams(
            dimension_semantics=("arbitrary", "arbitrary"), vmem_limit_bytes=VMEM_LIMIT),
        name="qkv_proj",
    )(x, lnw, w16, w16, w16, w16)


def _swap_lane_pairs(t):
    n = t.shape[-1]
    lane = lax.broadcasted_iota(jnp.int32, t.shape, 1)
    up = pltpu.roll(t, n - 1, axis=1)
    down = pltpu.roll(t, 1, axis=1)
    return jnp.where(lane % 2 == 0, up, down)


def _gate_proj_items(x_ref, lnw_ref, wqk_ref, wg_ref, wrg_ref, wmg_ref,
                     cos_ref, sin_ref, kdec_ref, bg_ref,
                     rq_ref, rkt_ref, ag_ref, rg_ref, mg_ref):
    cache = {}

    def lhs():
        if "xb" not in cache:
            cache["xb"] = _rmsnorm_rows(x_ref[...], lnw_ref[...]).astype(BF16)
        return cache["xb"]

    def rope(t, rows):
        cos = cos_ref[rows, :]
        sin = sin_ref[rows, :]
        tiles = [t[:, i * LANES:(i + 1) * LANES] for i in range(t.shape[1] // LANES)]
        return jnp.concatenate([tt * cos + _swap_lane_pairs(tt) * sin for tt in tiles], axis=1)

    def retention_q(cols, rows):
        t = _dot(lhs()[rows], wqk_ref[:, cols])
        yield
        rq_ref[rows, cols] = rope(t, rows).astype(BF16)

    def retention_k(cols, rows):
        t = _dot(lhs()[rows], wqk_ref[:, RET_QK_WIDTH + cols.start:RET_QK_WIDTH + cols.stop])
        yield
        k = rope(t, rows) * kdec_ref[rows, cols]
        for c in range((rows.stop - rows.start) // RET_CHUNK):
            rkt_ref[rows.start // RET_CHUNK + c, cols, :] = (
                k[c * RET_CHUNK:(c + 1) * RET_CHUNK, :].T.astype(BF16))

    def merge_gate(cols, rows):
        t = _dot(lhs()[rows], wmg_ref[:, cols])
        yield
        mg_ref[rows, cols] = _sigmoid(t + bg_ref[:, cols]).astype(BF16)

    def retention_gate(cols, rows):
        t = _dot(lhs()[rows], wrg_ref[:, cols])
        yield
        rg_ref[rows, cols] = _silu(t).astype(BF16)

    def attention_gate(cols, rows):
        t = _dot(lhs()[rows], wg_ref[:, cols])
        yield
        ag_ref[rows, cols] = _silu(t).astype(BF16)

    items = []
    part = GATE_ROWS // GATE_ROW_PARTS
    for fn, total, pieces in ((retention_q, RET_QK_WIDTH, GATE_PIECES[0]),
                              (retention_k, RET_QK_WIDTH, GATE_PIECES[1]),
                              (merge_gate, 2 * D_MODEL, GATE_PIECES[2]),
                              (retention_gate, RET_V_WIDTH, GATE_PIECES[3]),
                              (attention_gate, GROUP_WIDTH, GATE_PIECES[4])):
        width = total // pieces
        items += [lambda fn=fn, cols=slice(i * width, (i + 1) * width),
                  rows=slice(j * part, (j + 1) * part): fn(cols, rows)
                  for i in range(pieces) for j in range(GATE_ROW_PARTS)]
    return items


def _attention_items(in_refs, bias_ref, out_ref, acc_scr, max_scr, den_scr, nat_scr, first):
    ones = jnp.ones((2 * BAND, HEAD_DIM), BF16)
    quarter = ATTN_TILE // MIX_STRIDE

    def unit(g, r, n):
        dil = ATTN_GROUPS[g][1]
        cur_ref, prev_ref = in_refs[2 * g:2 * g + 2]
        q_cols, k_cols, v_cols = (slice(i * HEAD_DIM, (i + 1) * HEAD_DIM) for i in range(3))
        bias_base = 2 if dil == 1 else 0
        q = cur_ref[r, n * BAND:(n + 1) * BAND, q_cols]
        if n == 0:
            k = jnp.concatenate([prev_ref[r, :, k_cols], cur_ref[r, 0:BAND, k_cols]], axis=0)
            v = jnp.concatenate([prev_ref[r, :, v_cols], cur_ref[r, 0:BAND, v_cols]], axis=0)
            bias = bias_ref[bias_base + first]
        else:
            k = cur_ref[r, (n - 1) * BAND:(n + 1) * BAND, k_cols]
            v = cur_ref[r, (n - 1) * BAND:(n + 1) * BAND, v_cols]
            bias = bias_ref[bias_base]
        s = lax.dot_general(q, k, (((1,), (1,)), ((), ())), preferred_element_type=F32) + bias
        yield
        m = jnp.max(s, axis=-1, keepdims=True)
        p = jnp.exp2(s - m).astype(BF16)
        yield
        res = _dot(p, jnp.concatenate([v, ones], axis=1))
        acc, den = res[:, :HEAD_DIM], res[:, HEAD_DIM:]
        mb = jnp.broadcast_to(m, (BAND, HEAD_DIM))
        if dil == 1:
            sub = BAND // MIX_STRIDE
            for r4 in range(MIX_STRIDE):
                dst = pl.ds(r4 * quarter + n * sub, sub)
                src = slice(r4 * sub, (r4 + 1) * sub)
                acc_scr[g, dst, :] = acc[src]
                max_scr[g, dst, :] = mb[src]
                den_scr[g, dst, :] = den[src]
        else:
            if dil == MIX_STRIDE:
                dst = pl.ds(r * quarter + n * BAND, BAND)
            else:
                sub_stride = dil // MIX_STRIDE
                dst = pl.ds((r % MIX_STRIDE) * quarter + r // MIX_STRIDE
                            + n * BAND * sub_stride, BAND, stride=sub_stride)
            acc_scr[g, dst, :] = acc
            max_scr[g, dst, :] = mb
            den_scr[g, dst, :] = den

    def mix(r4, ch):
        rows = pl.ds(r4 * quarter + ch * BAND, BAND)
        m0, m1, m2 = max_scr[0, rows, :], max_scr[1, rows, :], max_scr[2, rows, :]
        mx = jnp.maximum(jnp.maximum(m0, m1), m2)
        w0 = jnp.exp2(m0 - mx)
        w1 = jnp.exp2(m1 - mx)
        w2 = jnp.exp2(m2 - mx)
        num = w0 * acc_scr[0, rows, :] + w1 * acc_scr[1, rows, :] + w2 * acc_scr[2, rows, :]
        den = w0 * den_scr[0, rows, :] + w1 * den_scr[1, rows, :] + w2 * den_scr[2, rows, :]
        nat_scr[pl.ds(r4 + ch * BAND * MIX_STRIDE, BAND, stride=MIX_STRIDE), :] = num / den

    def write_out():
        out_ref[...] = nat_scr[...].astype(BF16)

    def mix_quarter(r4):
        for ch in range(quarter // BAND):
            mix(r4, ch)

    units = [lambda n=n: unit(0, 0, n) for n in range(ATTN_TILE // BAND)]
    quarter_done = []
    for r4 in range(MIX_STRIDE):
        units += [lambda g=g, r=r, n=n: unit(g, r, n)
                  for g in range(1, N_GROUPS)
                  for r in range(r4, ATTN_GROUPS[g][1], MIX_STRIDE)
                  for n in range(ATTN_TILE // ATTN_GROUPS[g][1] // BAND)]
        quarter_done.append(len(units))
    return units, quarter_done, mix_quarter, write_out


def _attn_gates_kernel(*refs):
    n_attn = 2 * N_GROUPS
    in_refs = refs[:n_attn]
    bias_ref = refs[n_attn]
    proj_in = refs[n_attn + 1:n_attn + 11]
    attn_out = refs[n_attn + 11]
    proj_out = refs[n_attn + 12:n_attn + 17]
    acc_scr, max_scr, den_scr, nat_scr = refs[n_attn + 17:]
    first = (pl.program_id(1) == 0).astype(jnp.int32)
    unit_makers, quarter_done, mix_quarter, write_out = _attention_items(
        in_refs, bias_ref, attn_out, acc_scr, max_scr, den_scr, nat_scr, first)
    proj = _gate_proj_items(*proj_in, *proj_out)
    per_round = len(unit_makers) // len(proj)
    assert per_round * len(proj) == len(unit_makers)
    rounds = [unit_makers[i * per_round:(i + 1) * per_round] for i in range(len(proj))]

    def advance(gen):
        next(gen, None)

    prev_units = []
    mixed = 0
    for r in range(len(rounds) + 1):
        units = [make() for make in rounds[r]] if r < len(rounds) else []
        for j in range(max(len(units), len(prev_units))):
            if j < len(units):
                advance(units[j])
            if j < len(prev_units):
                advance(prev_units[j])
        while mixed < len(quarter_done) and quarter_done[mixed] <= r * per_round:
            mix_quarter(mixed)
            mixed += 1
        if r < len(proj):
            for _ in proj[r]():
                pass
        for u in units:
            advance(u)
        prev_units = units
    write_out()


def _attn_bias_tables():
    rho = jnp.arange(BAND)[:, None]
    c = jnp.arange(2 * BAND)[None, :]
    sub = BAND // MIX_STRIDE
    tables = []
    for a in (rho, MIX_STRIDE * (rho % sub) + rho // sub):
        ok = (c >= a) & (c <= a + BAND)
        tables += [ok, ok & (c >= BAND)]
    return jnp.where(jnp.stack(tables), 0.0, NEG_BIG).astype(F32)


def _attn_gates(qkv_groups, x, lnw, w16, cos_t, sin_t, kdec, bg):
    B, S, _ = x.shape
    nt = S // ATTN_TILE
    cpt = GATE_ROWS // RET_CHUNK
    in_specs = []
    args = []
    for (_, dil), qkv in zip(ATTN_GROUPS, qkv_groups):
        rows = ATTN_TILE // dil
        ratio = rows // BAND

        in_specs += [
            pl.BlockSpec((None, dil, rows, 3 * HEAD_DIM), lambda b, t, h: (b, 0, t, h)),
            pl.BlockSpec((None, dil, BAND, 3 * HEAD_DIM),
                         lambda b, t, h, ratio=ratio: (b, 0, jnp.maximum(t * ratio - 1, 0), h)),
        ]
        args += [qkv] * 2
    bias = _attn_bias_tables()
    in_specs.append(pl.BlockSpec(bias.shape, lambda b, t, h: (0, 0, 0), pipeline_mode=RESIDENT))

    def wspec(n, col0=0):
        return pl.BlockSpec((D_MODEL, n), lambda b, t, h: (0, col0 // n), pipeline_mode=RESIDENT)

    def row_spec(n):
        return pl.BlockSpec((None, GATE_ROWS, n), lambda b, t, h: (b, t * ATTN_HEADS + h, 0))

    tspec = pl.BlockSpec((GATE_ROWS, LANES), lambda b, t, h: (t * ATTN_HEADS + h, 0))
    in_specs += [
        row_spec(D_MODEL),
        pl.BlockSpec((1, D_MODEL), lambda b, t, h: (0, 0)),
        wspec(2 * RET_QK_WIDTH, COL_RQ),
        wspec(GROUP_WIDTH, COL_AG), wspec(RET_V_WIDTH, COL_RG), wspec(2 * D_MODEL, COL_MG),
        tspec, tspec,
        pl.BlockSpec((GATE_ROWS, RET_QK_WIDTH), lambda b, t, h: (0, 0), pipeline_mode=RESIDENT),
        pl.BlockSpec((1, 2 * D_MODEL), lambda b, t, h: (0, 0)),
    ]
    out_widths = (GROUP_WIDTH, RET_V_WIDTH, 2 * D_MODEL)
    per_group = pltpu.VMEM((N_GROUPS, ATTN_TILE, HEAD_DIM), F32)
    return pl.pallas_call(
        _attn_gates_kernel,
        grid=(B, nt, ATTN_HEADS),
        in_specs=in_specs,
        out_specs=[
            pl.BlockSpec((None, ATTN_TILE, HEAD_DIM), lambda b, t, h: (b, t, h)),
            row_spec(RET_QK_WIDTH),
            pl.BlockSpec((None, cpt, RET_QK_WIDTH, RET_CHUNK),
                         lambda b, t, h: (b, t * ATTN_HEADS + h, 0, 0)),
            *[row_spec(n) for n in out_widths],
        ],
        out_shape=[
            jax.ShapeDtypeStruct((B, S, GROUP_WIDTH), BF16),
            jax.ShapeDtypeStruct((B, S, RET_QK_WIDTH), BF16),
            jax.ShapeDtypeStruct((B, S // RET_CHUNK, RET_QK_WIDTH, RET_CHUNK), BF16),
            *[jax.ShapeDtypeStruct((B, S, n), BF16) for n in out_widths],
        ],
        scratch_shapes=[per_group, per_group, per_group,
                        pltpu.VMEM((ATTN_TILE, HEAD_DIM), F32)],
        compiler_params=pltpu.CompilerParams(
            dimension_semantics=("arbitrary", "arbitrary", "arbitrary"),
            vmem_limit_bytes=VMEM_LIMIT),
        name="attn_gates",
    )(*args, bias, x, lnw, w16, w16, w16, w16, cos_t, sin_t, kdec, bg)


def _retention_kernel(q_ref, kt_ref, v_ref, gate_ref, gnw_ref, eps_ref, cd_ref,
                      out_ref, state, p_scr):
    @pl.when(pl.program_id(1) == 0)
    def _():
        state[...] = jnp.zeros_like(state)

    dk, dv, C = RET_KEY_DIM, RET_VALUE_DIM, RET_CHUNK
    row_k = lax.broadcasted_iota(jnp.int32, (2 * dk, C), 0)
    row = lax.broadcasted_iota(jnp.int32, (2 * dk, 2 * dv), 0)
    col = lax.broadcasted_iota(jnp.int32, (C, 2 * dv), 1)
    diag_blk = (row < dk) == (col < dv)
    causal = (col % C) <= lax.broadcasted_iota(jnp.int32, (C, 2 * C), 0)
    units = [(bi, j) for bi in range(RET_BATCH) for j in range(RET_HEADS // 2)]

    def q_pair(bi, j):
        return q_ref[bi, :, j * 2 * dk:(j + 1) * 2 * dk]

    def kt_pair(bi, j):
        return kt_ref[bi, j * 2 * dk:(j + 1) * 2 * dk, :]

    def v_pair(bi, j):
        return v_ref[bi, :, j * 2 * dv:(j + 1) * 2 * dv]

    for bi, j in units:
        kt = kt_pair(bi, j)
        zero = jnp.zeros_like(kt)
        kt_blk = jnp.concatenate([jnp.where(row_k < dk, kt, zero),
                                  jnp.where(row_k >= dk, kt, zero)], axis=1)
        s = _dot(q_pair(bi, j), kt_blk).astype(BF16)
        p_scr[bi, j] = jnp.where(causal, s, jnp.zeros_like(s))

    for bi, j in units:
        v = v_pair(bi, j)
        zero = jnp.zeros_like(v)
        v_blk = jnp.concatenate([jnp.where(col < dv, v, zero),
                                 jnp.where(col >= dv, v, zero)], axis=0)
        u = _dot(jnp.concatenate([p_scr[bi, j], q_pair(bi, j)], axis=1),
                 jnp.concatenate([v_blk, state[bi, j].astype(BF16)], axis=0))
        for hh in range(2):
            h = 2 * j + hh
            hs = slice(h * dv, (h + 1) * dv)
            uh = u[:, hh * dv:(hh + 1) * dv]
            uh = uh * lax.rsqrt(jnp.sum(uh * uh, axis=-1, keepdims=True) + eps_ref[h])
            out_ref[bi, :, hs] = (uh * gnw_ref[:, hs]).astype(BF16) * gate_ref[bi, :, hs]

    for bi, j in units:
        kv = _dot(kt_pair(bi, j), v_pair(bi, j))
        state[bi, j] = cd_ref[j] * (state[bi, j] + jnp.where(diag_blk, kv, 0.0))


def _retention(rq, rkt, rv, gate, gnw, eps_t, cd):
    B, S, _ = rv.shape
    nc = S // RET_CHUNK
    npair = RET_HEADS // 2
    pair_shape = (RET_BATCH, npair, 2 * RET_KEY_DIM, 2 * RET_VALUE_DIM)

    def const(shape):
        return pl.BlockSpec(shape, lambda b, n: (0,) * len(shape))

    def rows(width):
        return pl.BlockSpec((RET_BATCH, RET_CHUNK, width), lambda b, n: (b, n, 0))

    return pl.pallas_call(
        _retention_kernel,
        grid=(B // RET_BATCH, nc),
        in_specs=[
            rows(RET_QK_WIDTH),
            pl.BlockSpec((RET_BATCH, None, RET_QK_WIDTH, RET_CHUNK), lambda b, n: (b, n, 0, 0)),
            rows(RET_V_WIDTH),
            rows(RET_V_WIDTH),
            const((1, RET_V_WIDTH)),
            const((RET_HEADS, RET_CHUNK, RET_VALUE_DIM)),
            const(pair_shape[1:]),
        ],
        out_specs=rows(RET_V_WIDTH),
        out_shape=jax.ShapeDtypeStruct((B, S, RET_V_WIDTH), BF16),
        scratch_shapes=[pltpu.VMEM(pair_shape, F32),
                        pltpu.VMEM((RET_BATCH, npair, RET_CHUNK, 2 * RET_CHUNK), BF16)],
        compiler_params=pltpu.CompilerParams(
            dimension_semantics=("arbitrary", "arbitrary"), vmem_limit_bytes=VMEM_LIMIT),
        name="retention",
    )(rq, rkt, rv, gate, gnw, eps_t, cd)


def _merge_out_kernel(a_ref, ag_ref, r_ref, mg_ref, x_ref, wa_ref, wr_ref, wo_ref, lnf_ref,
                      out_ref):
    def part(rows):
        y_attn = _dot(a_ref[rows, :] * ag_ref[rows, :], wa_ref[...])
        y_ret = _dot(r_ref[rows, :], wr_ref[...])
        yield
        merged = (mg_ref[rows, :D_MODEL].astype(F32) * y_attn
                  + mg_ref[rows, D_MODEL:].astype(F32) * y_ret)
        h = x_ref[rows, :] + _dot(merged.astype(BF16), wo_ref[...])
        yield
        out_ref[rows, :] = _rmsnorm_rows(h, lnf_ref[...])

    n_parts = OUT_ROWS // OUT_PART
    parts = [part(slice(i * OUT_PART, (i + 1) * OUT_PART)) for i in range(n_parts)]
    for step in range(n_parts + 2):
        for lag in range(3):
            i = step - lag
            if 0 <= i < n_parts:
                next(parts[i], None)


def _merge_out(attn, ag, ret_g, mg, x, wa, wr, wo, lnf):
    B, S, _ = x.shape
    nt = S // OUT_ROWS

    def rows(n):
        return pl.BlockSpec((None, OUT_ROWS, n), lambda b, t: (b, t, 0))

    def const(shape):
        return pl.BlockSpec(shape, lambda b, t: (0,) * len(shape), pipeline_mode=RESIDENT)

    return pl.pallas_call(
        _merge_out_kernel,
        grid=(B, nt),
        in_specs=[rows(GROUP_WIDTH), rows(GROUP_WIDTH), rows(RET_V_WIDTH), rows(2 * D_MODEL),
                  rows(D_MODEL),
                  const((GROUP_WIDTH, D_MODEL)), const((RET_V_WIDTH, D_MODEL)),
                  const((D_MODEL, D_MODEL)), const((1, D_MODEL))],
        out_specs=rows(D_MODEL),
        out_shape=jax.ShapeDtypeStruct((B, S, D_MODEL), F32),
        compiler_params=pltpu.CompilerParams(
            dimension_semantics=("arbitrary", "arbitrary"), vmem_limit_bytes=VMEM_LIMIT),
        name="merge_out",
    )(attn, ag, ret_g, mg, x, wa, wr, wo, lnf)


def _rope_tables(S):
    half = RET_KEY_DIM // 2
    lane = jnp.arange(LANES)
    inv_freq = (ROPE_BASE ** (-jnp.linspace(0.0, 1.0, half, dtype=F32)))[(lane % RET_KEY_DIM) // 2]
    sign = jnp.where(lane % 2 == 0, -1.0, 1.0).astype(F32)
    step = 64
    a = (jnp.arange(S // step, dtype=F32) * step)[:, None] * inv_freq[None, :]
    b = jnp.arange(step, dtype=F32)[:, None] * inv_freq[None, :]
    ca, sa, cb, sb = jnp.cos(a)[:, None], jnp.sin(a)[:, None], jnp.cos(b)[None], jnp.sin(b)[None]
    cos_t = (ca * cb - sa * sb).reshape(S, LANES)
    sin_t = ((sa * cb + ca * sb) * sign).reshape(S, LANES)
    return cos_t, sin_t


def _retention_constants():
    H, C, dk, dv = RET_HEADS, RET_CHUNK, RET_KEY_DIM, RET_VALUE_DIM
    log_gamma = jnp.log(1.0 - 2.0 ** (-5.0 - jnp.arange(H, dtype=F32)))
    idx = jnp.arange(C, dtype=F32)
    inv_decay = jnp.exp(-(idx + 1.0)[None, :] * log_gamma[:, None])
    chunk_decay = jnp.exp(C * log_gamma)
    kdec = jnp.repeat(inv_decay.T, dk, axis=1) * dk ** -0.5
    kdec = jnp.tile(kdec, (GATE_ROWS // C, 1))
    eps_t = jnp.broadcast_to((dv * NORM_EPS * inv_decay * inv_decay)[:, :, None], (H, C, dv))
    cd = jnp.broadcast_to(chunk_decay[:, None, None], (H, dk, 2 * dv)).reshape(H // 2, 2 * dk, 2 * dv)
    return kdec, eps_t, cd


def kernel(x, ln1_w, w_in, b_gate, attn_proj, ret_proj, ret_gn_w, w_out, lnf_w):
    B, S, _ = x.shape
    assert w_in.shape[0] == 1, "single layer"
    w16 = w_in[0].astype(BF16)

    lnw = ln1_w[0].reshape(1, D_MODEL)
    q1, q2, q3, rv = _qkv_proj(x, lnw, w16)
    cos_t, sin_t = _rope_tables(S)
    kdec, eps_t, cd = _retention_constants()
    attn, rq, rkt, ag, rg, mg = _attn_gates(
        (q1, q2, q3), x, lnw, w16, cos_t, sin_t, kdec, b_gate[0].reshape(1, 2 * D_MODEL))
    gnw = (ret_gn_w[0] * RET_VALUE_DIM ** 0.5).reshape(1, RET_V_WIDTH)
    ret_g = _retention(rq, rkt, rv, rg, gnw, eps_t, cd)

    return _merge_out(attn, ag, ret_g, mg, x,
                      attn_proj[0].astype(BF16), ret_proj[0].astype(BF16),
                      w_out[0].astype(BF16), lnf_w.reshape(1, D_MODEL))
```

```python
import jax
import jax.numpy as jnp
from jax import lax
from jax.experimental import pallas as pl
from jax.experimental.pallas import tpu as pltpu

D_MODEL = 1024
ATTN_GROUPS = ((128, 1), (512, 4), (2048, 16))
N_GROUPS = 3
ATTN_HEADS = 4
HEAD_DIM = 128
GROUP_WIDTH = ATTN_HEADS * HEAD_DIM
QKV_WIDTH = N_GROUPS * GROUP_WIDTH
BAND = 128
RET_HEADS = 8
RET_KEY_DIM = 64
RET_VALUE_DIM = 128
RET_QK_WIDTH = RET_HEADS * RET_KEY_DIM
RET_V_WIDTH = RET_HEADS * RET_VALUE_DIM
RET_CHUNK = 128
ROPE_BASE = 10000.0
COL_AG = 3 * QKV_WIDTH
COL_RQ = COL_AG + GROUP_WIDTH
COL_RV = COL_RQ + 2 * RET_QK_WIDTH
COL_RG = COL_RV + RET_V_WIDTH
COL_MG = COL_RG + RET_V_WIDTH
NORM_EPS = 1e-6
NEG_BIG = -1e30
LANES = 128
LOG2E = 1.4426950408889634
MIX_STRIDE = 4

PROJ_ROWS = 1024
QKV_PARTS = 4
ATTN_TILE = 2048
GATE_ROWS = ATTN_TILE // ATTN_HEADS
RET_BATCH = 8
OUT_ROWS = 1024
OUT_PART = 256
V7X_VMEM_BYTES = 64 * 1024 * 1024
VMEM_LIMIT = V7X_VMEM_BYTES * 7 // 8
GATE_ROW_PARTS = 2
GATE_PIECES = (2, 2, 4, 2, 2)
RESIDENT = pl.Buffered(1)

BF16 = jnp.bfloat16
F32 = jnp.float32


def _rmsnorm_rows(x, w):
    return x * lax.rsqrt(jnp.mean(x * x, axis=-1, keepdims=True) + NORM_EPS) * w


def _dot(a, b):
    return jnp.dot(a, b, preferred_element_type=F32)


def _sigmoid(t):
    return 0.5 * jnp.tanh(0.5 * t) + 0.5


def _silu(t):
    return t * _sigmoid(t)


def _qkv_proj_kernel(x_ref, lnw_ref, *refs):
    wq_ref, wk_ref, wv_ref, wrv_ref = refs[:4]
    out_refs = refs[4:4 + N_GROUPS]
    rv_ref, xn_scr, xn4_scr, lhs_scr = refs[4 + N_GROUPS:]
    n_lane_tiles = D_MODEL // LANES
    q_scale = HEAD_DIM ** -0.5 * LOG2E
    part_rows = PROJ_ROWS // QKV_PARTS
    d4, d16 = ATTN_GROUPS[1][1], ATTN_GROUPS[2][1]
    rows4, rows16 = part_rows // d4, part_rows // d16
    NAT, QPERM, BY4, BY16 = range(4)

    def part(p):
        rows_p = slice(p * part_rows, (p + 1) * part_rows)
        lhs = lhs_scr.at[p]
        xn = _rmsnorm_rows(x_ref[rows_p, :], lnw_ref[...])
        lhs[NAT] = xn.astype(BF16)
        for ct in range(n_lane_tiles):
            xn_scr[ct] = xn[:, ct * LANES:(ct + 1) * LANES]
        sub = BAND // MIX_STRIDE
        for ct in range(n_lane_tiles):
            lanes = slice(ct * LANES, (ct + 1) * LANES)
            lhs[QPERM, :, lanes] = jnp.concatenate(
                [xn_scr[ct, pl.ds(blk * BAND + r4, sub, stride=MIX_STRIDE), :]
                 for blk in range(part_rows // BAND) for r4 in range(MIX_STRIDE)],
                axis=0).astype(BF16)
            by4 = jnp.concatenate(
                [xn_scr[ct, pl.ds(r, rows4, stride=d4), :] for r in range(d4)], axis=0)
            xn4_scr[ct] = by4
            lhs[BY4, :, lanes] = by4.astype(BF16)
        for ct in range(n_lane_tiles):
            by16 = jnp.concatenate(
                [xn4_scr[ct, pl.ds((r % d4) * rows4 + r // d4, rows16, stride=d16 // d4), :]
                 for r in range(d16)], axis=0)
            lhs[BY16, :, ct * LANES:(ct + 1) * LANES] = by16.astype(BF16)
        yield

        rv_ref[rows_p, :] = _dot(lhs[NAT], wrv_ref[...]).astype(BF16)
        for g, (order_q, order_kv) in enumerate(((QPERM, NAT), (BY4, BY4), (BY16, BY16))):
            gcols = slice(g * GROUP_WIDTH, (g + 1) * GROUP_WIDTH)
            dil = ATTN_GROUPS[g][1]
            rows = part_rows // dil
            q = (_dot(lhs[order_q], wq_ref[:, gcols]) * q_scale).astype(BF16)
            k = _dot(lhs[order_kv], wk_ref[:, gcols]).astype(BF16)
            v = _dot(lhs[order_kv], wv_ref[:, gcols]).astype(BF16)
            for r in range(dil):
                src_rows = slice(r * rows, (r + 1) * rows)
                dst_rows = slice(p * rows, (p + 1) * rows)
                for h in range(ATTN_HEADS):
                    hs = slice(h * HEAD_DIM, (h + 1) * HEAD_DIM)
                    for which, val in enumerate((q, k, v)):
                        c0 = (3 * h + which) * HEAD_DIM
                        out_refs[g][r, dst_rows, c0:c0 + HEAD_DIM] = val[src_rows, hs]

    parts = [part(p) for p in range(QKV_PARTS)]
    for gen in parts:
        next(gen)
    for gen in parts:
        next(gen, None)


def _qkv_proj(x, lnw, w16):
    B, S, _ = x.shape
    nt = S // PROJ_ROWS
    w_specs = [pl.BlockSpec((D_MODEL, QKV_WIDTH), lambda b, t, part=part: (0, part),
                            pipeline_mode=RESIDENT) for part in range(3)]
    w_specs.append(pl.BlockSpec((D_MODEL, RET_V_WIDTH), lambda b, t: (0, COL_RV // RET_V_WIDTH),
                                pipeline_mode=RESIDENT))
    return pl.pallas_call(
        _qkv_proj_kernel,
        grid=(B, nt),
        in_specs=[
            pl.BlockSpec((None, PROJ_ROWS, D_MODEL), lambda b, t: (b, t, 0)),
            pl.BlockSpec((1, D_MODEL), lambda b, t: (0, 0)),
            *w_specs,
        ],
        out_specs=[
            *[pl.BlockSpec((None, dil, PROJ_ROWS // dil, QKV_WIDTH), lambda b, t: (b, 0, t, 0))
              for _, dil in ATTN_GROUPS],
            pl.BlockSpec((None, PROJ_ROWS, RET_V_WIDTH), lambda b, t: (b, t, 0)),
        ],
        out_shape=[
            *[jax.ShapeDtypeStruct((B, dil, S // dil, QKV_WIDTH), BF16) for _, dil in ATTN_GROUPS],
            jax.ShapeDtypeStruct((B, S, RET_V_WIDTH), BF16),
        ],
        scratch_shapes=[pltpu.VMEM((D_MODEL // LANES, PROJ_ROWS // QKV_PARTS, LANES), F32),
                        pltpu.VMEM((D_MODEL // LANES, PROJ_ROWS // QKV_PARTS, LANES), F32),
                        pltpu.VMEM((QKV_PARTS, 4, PROJ_ROWS // QKV_PARTS, D_MODEL), BF16)],
        compiler_params=pltpu.CompilerParams(
            dimension_semantics=("arbitrary", "arbitrary"), vmem_limit_bytes=VMEM_LIMIT),
        name="qkv_proj",
    )(x, lnw, w16, w16, w16, w16)


def _swap_lane_pairs(t):
    n = t.shape[-1]
    lane = lax.broadcasted_iota(jnp.int32, t.shape, 1)
    up = pltpu.roll(t, n - 1, axis=1)
    down = pltpu.roll(t, 1, axis=1)
    return jnp.where(lane % 2 == 0, up, down)


def _gate_proj_items(x_ref, lnw_ref, wqk_ref, wg_ref, wrg_ref, wmg_ref,
                     cos_ref, sin_ref, kdec_ref, bg_ref,
                     rq_ref, rkt_ref, ag_ref, rg_ref, mg_ref):
    cache = {}

    def lhs():
        if "xb" not in cache:
            cache["xb"] = _rmsnorm_rows(x_ref[...], lnw_ref[...]).astype(BF16)
        return cache["xb"]

    def rope(t, rows):
        cos = cos_ref[rows, :]
        sin = sin_ref[rows, :]
        tiles = [t[:, i * LANES:(i + 1) * LANES] for i in range(t.shape[1] // LANES)]
        return jnp.concatenate([tt * cos + _swap_lane_pairs(tt) * sin for tt in tiles], axis=1)

    def retention_q(cols, rows):
        t = _dot(lhs()[rows], wqk_ref[:, cols])
        yield
        rq_ref[rows, cols] = rope(t, rows).astype(BF16)

    def retention_k(cols, rows):
        t = _dot(lhs()[rows], wqk_ref[:, RET_QK_WIDTH + cols.start:RET_QK_WIDTH + cols.stop])
        yield
        k = rope(t, rows) * kdec_ref[rows, cols]
        for c in range((rows.stop - rows.start) // RET_CHUNK):
            rkt_ref[rows.start // RET_CHUNK + c, cols, :] = (
                k[c * RET_CHUNK:(c + 1) * RET_CHUNK, :].T.astype(BF16))

    def merge_gate(cols, rows):
        t = _dot(lhs()[rows], wmg_ref[:, cols])
        yield
        mg_ref[rows, cols] = _sigmoid(t + bg_ref[:, cols]).astype(BF16)

    def retention_gate(cols, rows):
        t = _dot(lhs()[rows], wrg_ref[:, cols])
        yield
        rg_ref[rows, cols] = _silu(t).astype(BF16)

    def attention_gate(cols, rows):
        t = _dot(lhs()[rows], wg_ref[:, cols])
        yield
        ag_ref[rows, cols] = _silu(t).astype(BF16)

    items = []
    part = GATE_ROWS // GATE_ROW_PARTS
    for fn, total, pieces in ((retention_q, RET_QK_WIDTH, GATE_PIECES[0]),
                              (retention_k, RET_QK_WIDTH, GATE_PIECES[1]),
                              (merge_gate, 2 * D_MODEL, GATE_PIECES[2]),
                              (retention_gate, RET_V_WIDTH, GATE_PIECES[3]),
                              (attention_gate, GROUP_WIDTH, GATE_PIECES[4])):
        width = total // pieces
        items += [lambda fn=fn, cols=slice(i * width, (i + 1) * width),
                  rows=slice(j * part, (j + 1) * part): fn(cols, rows)
                  for i in range(pieces) for j in range(GATE_ROW_PARTS)]
    return items


def _attention_items(in_refs, bias_ref, out_ref, acc_scr, max_scr, den_scr, nat_scr, first):
    ones = jnp.ones((2 * BAND, HEAD_DIM), BF16)
    quarter = ATTN_TILE // MIX_STRIDE

    def unit(g, r, n):
        dil = ATTN_GROUPS[g][1]
        cur_ref, prev_ref = in_refs[2 * g:2 * g + 2]
        q_cols, k_cols, v_cols = (slice(i * HEAD_DIM, (i + 1) * HEAD_DIM) for i in range(3))
        bias_base = 2 if dil == 1 else 0
        q = cur_ref[r, n * BAND:(n + 1) * BAND, q_cols]
        if n == 0:
            k = jnp.concatenate([prev_ref[r, :, k_cols], cur_ref[r, 0:BAND, k_cols]], axis=0)
            v = jnp.concatenate([prev_ref[r, :, v_cols], cur_ref[r, 0:BAND, v_cols]], axis=0)
            bias = bias_ref[bias_base + first]
        else:
            k = cur_ref[r, (n - 1) * BAND:(n + 1) * BAND, k_cols]
            v = cur_ref[r, (n - 1) * BAND:(n + 1) * BAND, v_cols]
            bias = bias_ref[bias_base]
        s = lax.dot_general(q, k, (((1,), (1,)), ((), ())), preferred_element_type=F32) + bias
        yield
        m = jnp.max(s, axis=-1, keepdims=True)
        p = jnp.exp2(s - m).astype(BF16)
        yield
        res = _dot(p, jnp.concatenate([v, ones], axis=1))
        acc, den = res[:, :HEAD_DIM], res[:, HEAD_DIM:]
        mb = jnp.broadcast_to(m, (BAND, HEAD_DIM))
        if dil == 1:
            sub = BAND // MIX_STRIDE
            for r4 in range(MIX_STRIDE):
                dst = pl.ds(r4 * quarter + n * sub, sub)
                src = slice(r4 * sub, (r4 + 1) * sub)
                acc_scr[g, dst, :] = acc[src]
                max_scr[g, dst, :] = mb[src]
                den_scr[g, dst, :] = den[src]
        else:
            if dil == MIX_STRIDE:
                dst = pl.ds(r * quarter + n * BAND, BAND)
            else:
                sub_stride = dil // MIX_STRIDE
                dst = pl.ds((r % MIX_STRIDE) * quarter + r // MIX_STRIDE
                            + n * BAND * sub_stride, BAND, stride=sub_stride)
            acc_scr[g, dst, :] = acc
            max_scr[g, dst, :] = mb
            den_scr[g, dst, :] = den

    def mix(r4, ch):
        rows = pl.ds(r4 * quarter + ch * BAND, BAND)
        m0, m1, m2 = max_scr[0, rows, :], max_scr[1, rows, :], max_scr[2, rows, :]
        mx = jnp.maximum(jnp.maximum(m0, m1), m2)
        w0 = jnp.exp2(m0 - mx)
        w1 = jnp.exp2(m1 - mx)
        w2 = jnp.exp2(m2 - mx)
        num = w0 * acc_scr[0, rows, :] + w1 * acc_scr[1, rows, :] + w2 * acc_scr[2, rows, :]
        den = w0 * den_scr[0, rows, :] + w1 * den_scr[1, rows, :] + w2 * den_scr[2, rows, :]
        nat_scr[pl.ds(r4 + ch * BAND * MIX_STRIDE, BAND, stride=MIX_STRIDE), :] = num / den

    def write_out():
        out_ref[...] = nat_scr[...].astype(BF16)

    def mix_quarter(r4):
        for ch in range(quarter // BAND):
            mix(r4, ch)

    units = [lambda n=n: unit(0, 0, n) for n in range(ATTN_TILE // BAND)]
    quarter_done = []
    for r4 in range(MIX_STRIDE):
        units += [lambda g=g, r=r, n=n: unit(g, r, n)
                  for g in range(1, N_GROUPS)
                  for r in range(r4, ATTN_GROUPS[g][1], MIX_STRIDE)
                  for n in range(ATTN_TILE // ATTN_GROUPS[g][1] // BAND)]
        quarter_done.append(len(units))
    return units, quarter_done, mix_quarter, write_out


def _attn_gates_kernel(*refs):
    n_attn = 2 * N_GROUPS
    in_refs = refs[:n_attn]
    bias_ref = refs[n_attn]
    proj_in = refs[n_attn + 1:n_attn + 11]
    attn_out = refs[n_attn + 11]
    proj_out = refs[n_attn + 12:n_attn + 17]
    acc_scr, max_scr, den_scr, nat_scr = refs[n_attn + 17:]
    first = (pl.program_id(1) == 0).astype(jnp.int32)
    unit_makers, quarter_done, mix_quarter, write_out = _attention_items(
        in_refs, bias_ref, attn_out, acc_scr, max_scr, den_scr, nat_scr, first)
    proj = _gate_proj_items(*proj_in, *proj_out)
    per_round = len(unit_makers) // len(proj)
    assert per_round * len(proj) == len(unit_makers)
    rounds = [unit_makers[i * per_round:(i + 1) * per_round] for i in range(len(proj))]

    def advance(gen):
        next(gen, None)

    prev_units = []
    mixed = 0
    for r in range(len(rounds) + 1):
        units = [make() for make in rounds[r]] if r < len(rounds) else []
        for j in range(max(len(units), len(prev_units))):
            if j < len(units):
                advance(units[j])
            if j < len(prev_units):
                advance(prev_units[j])
        while mixed < len(quarter_done) and quarter_done[mixed] <= r * per_round:
            mix_quarter(mixed)
            mixed += 1
        if r < len(proj):
            for _ in proj[r]():
                pass
        for u in units:
            advance(u)
        prev_units = units
    write_out()


def _attn_bias_tables():
    rho = jnp.arange(BAND)[:, None]
    c = jnp.arange(2 * BAND)[None, :]
    sub = BAND // MIX_STRIDE
    tables = []
    for a in (rho, MIX_STRIDE * (rho % sub) + rho // sub):
        ok = (c >= a) & (c <= a + BAND)
        tables += [ok, ok & (c >= BAND)]
    return jnp.where(jnp.stack(tables), 0.0, NEG_BIG).astype(F32)


def _attn_gates(qkv_groups, x, lnw, w16, cos_t, sin_t, kdec, bg):
    B, S, _ = x.shape
    nt = S // ATTN_TILE
    cpt = GATE_ROWS // RET_CHUNK
    in_specs = []
    args = []
    for (_, dil), qkv in zip(ATTN_GROUPS, qkv_groups):
        rows = ATTN_TILE // dil
        ratio = rows // BAND

        in_specs += [
            pl.BlockSpec((None, dil, rows, 3 * HEAD_DIM), lambda b, t, h: (b, 0, t, h)),
            pl.BlockSpec((None, dil, BAND, 3 * HEAD_DIM),
                         lambda b, t, h, ratio=ratio: (b, 0, jnp.maximum(t * ratio - 1, 0), h)),
        ]
        args += [qkv] * 2
    bias = _attn_bias_tables()
    in_specs.append(pl.BlockSpec(bias.shape, lambda b, t, h: (0, 0, 0), pipeline_mode=RESIDENT))

    def wspec(n, col0=0):
        return pl.BlockSpec((D_MODEL, n), lambda b, t, h: (0, col0 // n), pipeline_mode=RESIDENT)

    def row_spec(n):
        return pl.BlockSpec((None, GATE_ROWS, n), lambda b, t, h: (b, t * ATTN_HEADS + h, 0))

    tspec = pl.BlockSpec((GATE_ROWS, LANES), lambda b, t, h: (t * ATTN_HEADS + h, 0))
    in_specs += [
        row_spec(D_MODEL),
        pl.BlockSpec((1, D_MODEL), lambda b, t, h: (0, 0)),
        wspec(2 * RET_QK_WIDTH, COL_RQ),
        wspec(GROUP_WIDTH, COL_AG), wspec(RET_V_WIDTH, COL_RG), wspec(2 * D_MODEL, COL_MG),
        tspec, tspec,
        pl.BlockSpec((GATE_ROWS, RET_QK_WIDTH), lambda b, t, h: (0, 0), pipeline_mode=RESIDENT),
        pl.BlockSpec((1, 2 * D_MODEL), lambda b, t, h: (0, 0)),
    ]
    out_widths = (GROUP_WIDTH, RET_V_WIDTH, 2 * D_MODEL)
    per_group = pltpu.VMEM((N_GROUPS, ATTN_TILE, HEAD_DIM), F32)
    return pl.pallas_call(
        _attn_gates_kernel,
        grid=(B, nt, ATTN_HEADS),
        in_specs=in_specs,
        out_specs=[
            pl.BlockSpec((None, ATTN_TILE, HEAD_DIM), lambda b, t, h: (b, t, h)),
            row_spec(RET_QK_WIDTH),
            pl.BlockSpec((None, cpt, RET_QK_WIDTH, RET_CHUNK),
                         lambda b, t, h: (b, t * ATTN_HEADS + h, 0, 0)),
            *[row_spec(n) for n in out_widths],
        ],
        out_shape=[
            jax.ShapeDtypeStruct((B, S, GROUP_WIDTH), BF16),
            jax.ShapeDtypeStruct((B, S, RET_QK_WIDTH), BF16),
            jax.ShapeDtypeStruct((B, S // RET_CHUNK, RET_QK_WIDTH, RET_CHUNK), BF16),
            *[jax.ShapeDtypeStruct((B, S, n), BF16) for n in out_widths],
        ],
        scratch_shapes=[per_group, per_group, per_group,
                        pltpu.VMEM((ATTN_TILE, HEAD_DIM), F32)],
        compiler_params=pltpu.CompilerParams(
            dimension_semantics=("arbitrary", "arbitrary", "arbitrary"),
            vmem_limit_bytes=VMEM_LIMIT),
        name="attn_gates",
    )(*args, bias, x, lnw, w16, w16, w16, w16, cos_t, sin_t, kdec, bg)


def _retention_kernel(q_ref, kt_ref, v_ref, gnw_ref, eps_ref, cd_ref,
                      out_ref, state, p_scr):
    @pl.when(pl.program_id(1) == 0)
    def _():
        state[...] = jnp.zeros_like(state)

    dk, dv, C = RET_KEY_DIM, RET_VALUE_DIM, RET_CHUNK
    row_k = lax.broadcasted_iota(jnp.int32, (2 * dk, C), 0)
    row = lax.broadcasted_iota(jnp.int32, (2 * dk, 2 * dv), 0)
    col = lax.broadcasted_iota(jnp.int32, (C, 2 * dv), 1)
    diag_blk = (row < dk) == (col < dv)
    causal = (col % C) <= lax.broadcasted_iota(jnp.int32, (C, 2 * C), 0)
    units = [(bi, j) for bi in range(RET_BATCH) for j in range(RET_HEADS // 2)]

    def q_pair(bi, j):
        return q_ref[bi, :, j * 2 * dk:(j + 1) * 2 * dk]

    def kt_pair(bi, j):
        return kt_ref[bi, j * 2 * dk:(j + 1) * 2 * dk, :]

    def v_pair(bi, j):
        return v_ref[bi, :, j * 2 * dv:(j + 1) * 2 * dv]

    for bi, j in units:
        kt = kt_pair(bi, j)
        zero = jnp.zeros_like(kt)
        kt_blk = jnp.concatenate([jnp.where(row_k < dk, kt, zero),
                                  jnp.where(row_k >= dk, kt, zero)], axis=1)
        s = _dot(q_pair(bi, j), kt_blk).astype(BF16)
        p_scr[bi, j] = jnp.where(causal, s, jnp.zeros_like(s))

    for bi, j in units:
        v = v_pair(bi, j)
        zero = jnp.zeros_like(v)
        v_blk = jnp.concatenate([jnp.where(col < dv, v, zero),
                                 jnp.where(col >= dv, v, zero)], axis=0)
        u = _dot(jnp.concatenate([p_scr[bi, j], q_pair(bi, j)], axis=1),
                 jnp.concatenate([v_blk, state[bi, j].astype(BF16)], axis=0))
        for hh in range(2):
            h = 2 * j + hh
            hs = slice(h * dv, (h + 1) * dv)
            uh = u[:, hh * dv:(hh + 1) * dv]
            uh = uh * lax.rsqrt(jnp.sum(uh * uh, axis=-1, keepdims=True) + eps_ref[h])
            out_ref[bi, :, hs] = (uh * gnw_ref[:, hs]).astype(BF16)

    for bi, j in units:
        kv = _dot(kt_pair(bi, j), v_pair(bi, j))
        state[bi, j] = cd_ref[j] * (state[bi, j] + jnp.where(diag_blk, kv, 0.0))


def _retention(rq, rkt, rv, gnw, eps_t, cd):
    B, S, _ = rv.shape
    nc = S // RET_CHUNK
    npair = RET_HEADS // 2
    pair_shape = (RET_BATCH, npair, 2 * RET_KEY_DIM, 2 * RET_VALUE_DIM)

    def const(shape):
        return pl.BlockSpec(shape, lambda b, n: (0,) * len(shape))

    def rows(width):
        return pl.BlockSpec((RET_BATCH, RET_CHUNK, width), lambda b, n: (b, n, 0))

    return pl.pallas_call(
        _retention_kernel,
        grid=(B // RET_BATCH, nc),
        in_specs=[
            rows(RET_QK_WIDTH),
            pl.BlockSpec((RET_BATCH, None, RET_QK_WIDTH, RET_CHUNK), lambda b, n: (b, n, 0, 0)),
            rows(RET_V_WIDTH),
            const((1, RET_V_WIDTH)),
            const((RET_HEADS, RET_CHUNK, RET_VALUE_DIM)),
            const(pair_shape[1:]),
        ],
        out_specs=rows(RET_V_WIDTH),
        out_shape=jax.ShapeDtypeStruct((B, S, RET_V_WIDTH), BF16),
        scratch_shapes=[pltpu.VMEM(pair_shape, F32),
                        pltpu.VMEM((RET_BATCH, npair, RET_CHUNK, 2 * RET_CHUNK), BF16)],
        compiler_params=pltpu.CompilerParams(
            dimension_semantics=("arbitrary", "arbitrary"), vmem_limit_bytes=VMEM_LIMIT),
        name="retention",
    )(rq, rkt, rv, gnw, eps_t, cd)


def _merge_out_kernel(a_ref, ag_ref, r_ref, rg_ref, mg_ref, x_ref, wa_ref, wr_ref, wo_ref,
                      lnf_ref, out_ref):
    def part(rows):
        y_attn = _dot(a_ref[rows, :] * ag_ref[rows, :], wa_ref[...])
        y_ret = _dot(r_ref[rows, :] * rg_ref[rows, :], wr_ref[...])
        yield
        merged = (mg_ref[rows, :D_MODEL].astype(F32) * y_attn
                  + mg_ref[rows, D_MODEL:].astype(F32) * y_ret)
        h = x_ref[rows, :] + _dot(merged.astype(BF16), wo_ref[...])
        yield
        out_ref[rows, :] = _rmsnorm_rows(h, lnf_ref[...])

    n_parts = OUT_ROWS // OUT_PART
    parts = [part(slice(i * OUT_PART, (i + 1) * OUT_PART)) for i in range(n_parts)]
    for step in range(n_parts + 2):
        for lag in range(3):
            i = step - lag
            if 0 <= i < n_parts:
                next(parts[i], None)


def _merge_out(attn, ag, ret, rg, mg, x, wa, wr, wo, lnf):
    B, S, _ = x.shape
    nt = S // OUT_ROWS

    def rows(n):
        return pl.BlockSpec((None, OUT_ROWS, n), lambda b, t: (b, t, 0))

    def const(shape):
        return pl.BlockSpec(shape, lambda b, t: (0,) * len(shape), pipeline_mode=RESIDENT)

    return pl.pallas_call(
        _merge_out_kernel,
        grid=(B, nt),
        in_specs=[rows(GROUP_WIDTH), rows(GROUP_WIDTH), rows(RET_V_WIDTH), rows(RET_V_WIDTH),
                  rows(2 * D_MODEL), rows(D_MODEL),
                  const((GROUP_WIDTH, D_MODEL)), const((RET_V_WIDTH, D_MODEL)),
                  const((D_MODEL, D_MODEL)), const((1, D_MODEL))],
        out_specs=rows(D_MODEL),
        out_shape=jax.ShapeDtypeStruct((B, S, D_MODEL), F32),
        compiler_params=pltpu.CompilerParams(
            dimension_semantics=("arbitrary", "arbitrary"), vmem_limit_bytes=VMEM_LIMIT),
        name="merge_out",
    )(attn, ag, ret, rg, mg, x, wa, wr, wo, lnf)


def _rope_tables(S):
    half = RET_KEY_DIM // 2
    inv_freq = ROPE_BASE ** (-jnp.linspace(0.0, 1.0, half, dtype=F32))
    step = 64
    a = (jnp.arange(S // step, dtype=F32) * step)[:, None] * inv_freq[None, :]
    b = jnp.arange(step, dtype=F32)[:, None] * inv_freq[None, :]
    ca, sa, cb, sb = jnp.cos(a)[:, None], jnp.sin(a)[:, None], jnp.cos(b)[None], jnp.sin(b)[None]
    cos = (ca * cb - sa * sb).reshape(S, half)
    sin = (sa * cb + ca * sb).reshape(S, half)
    heads_per_tile = LANES // RET_KEY_DIM
    sign = jnp.where(jnp.arange(LANES) % 2 == 0, -1.0, 1.0).astype(F32)
    cos_t = jnp.tile(jnp.repeat(cos, 2, axis=1), (1, heads_per_tile))
    sin_t = jnp.tile(jnp.repeat(sin, 2, axis=1), (1, heads_per_tile)) * sign[None, :]
    return cos_t, sin_t


def _retention_constants():
    H, C, dk, dv = RET_HEADS, RET_CHUNK, RET_KEY_DIM, RET_VALUE_DIM
    log_gamma = jnp.log(1.0 - 2.0 ** (-5.0 - jnp.arange(H, dtype=F32)))
    idx = jnp.arange(C, dtype=F32)
    inv_decay = jnp.exp(-(idx + 1.0)[None, :] * log_gamma[:, None])
    chunk_decay = jnp.exp(C * log_gamma)
    kdec = jnp.repeat(inv_decay.T, dk, axis=1) * dk ** -0.5
    kdec = jnp.tile(kdec, (GATE_ROWS // C, 1))
    eps_t = jnp.broadcast_to((dv * NORM_EPS * inv_decay * inv_decay)[:, :, None], (H, C, dv))
    cd = jnp.broadcast_to(chunk_decay[:, None, None], (H, dk, 2 * dv)).reshape(H // 2, 2 * dk, 2 * dv)
    return kdec, eps_t, cd


def kernel(x, ln1_w, w_in, b_gate, attn_proj, ret_proj, ret_gn_w, w_out, lnf_w):
    B, S, _ = x.shape
    assert w_in.shape[0] == 1, "single layer"
    w16 = w_in[0].astype(BF16)

    lnw = ln1_w[0].reshape(1, D_MODEL)
    q1, q2, q3, rv = _qkv_proj(x, lnw, w16)
    cos_t, sin_t = _rope_tables(S)
    kdec, eps_t, cd = _retention_constants()
    attn, rq, rkt, ag, rg, mg = _attn_gates(
        (q1, q2, q3), x, lnw, w16, cos_t, sin_t, kdec, b_gate[0].reshape(1, 2 * D_MODEL))
    gnw = (ret_gn_w[0] * RET_VALUE_DIM ** 0.5).reshape(1, RET_V_WIDTH)
    ret = _retention(rq, rkt, rv, gnw, eps_t, cd)

    return _merge_out(attn, ag, ret, rg, mg, x,
                      attn_proj[0].astype(BF16), ret_proj[0].astype(BF16),
                      w_out[0].astype(BF16), lnf_w.reshape(1, D_MODEL))
```

```python
import jax
import jax.numpy as jnp
from jax import lax
from jax.experimental import pallas as pl
from jax.experimental.pallas import tpu as pltpu

D_MODEL = 1024
ATTN_GROUPS = ((128, 1), (512, 4), (2048, 16))
N_GROUPS = 3
ATTN_HEADS = 4
HEAD_DIM = 128
GROUP_WIDTH = ATTN_HEADS * HEAD_DIM
QKV_WIDTH = N_GROUPS * GROUP_WIDTH
BAND = 128
RET_HEADS = 8
RET_KEY_DIM = 64
RET_VALUE_DIM = 128
RET_QK_WIDTH = RET_HEADS * RET_KEY_DIM
RET_V_WIDTH = RET_HEADS * RET_VALUE_DIM
RET_CHUNK = 128
ROPE_BASE = 10000.0
COL_AG = 3 * QKV_WIDTH
COL_RQ = COL_AG + GROUP_WIDTH
COL_RV = COL_RQ + 2 * RET_QK_WIDTH
COL_RG = COL_RV + RET_V_WIDTH
COL_MG = COL_RG + RET_V_WIDTH
NORM_EPS = 1e-6
NEG_BIG = -1e30
LANES = 128
LOG2E = 1.4426950408889634
MIX_STRIDE = 4

PROJ_ROWS = 1024
QKV_PARTS = 4
ATTN_TILE = 2048
GATE_ROWS = ATTN_TILE // ATTN_HEADS
RET_BATCH = 8
OUT_ROWS = 1024
OUT_PART = 512
V7X_VMEM_BYTES = 64 * 1024 * 1024
VMEM_LIMIT = V7X_VMEM_BYTES * 7 // 8
GATE_ROW_PARTS = 2
GATE_PIECES = (2, 2, 4, 2, 2)
RESIDENT = pl.Buffered(1)

BF16 = jnp.bfloat16
F32 = jnp.float32


def _rmsnorm_rows(x, w):
    return x * lax.rsqrt(jnp.mean(x * x, axis=-1, keepdims=True) + NORM_EPS) * w


def _dot(a, b):
    return jnp.dot(a, b, preferred_element_type=F32)


def _sigmoid(t):
    return 0.5 * jnp.tanh(0.5 * t) + 0.5


def _silu(t):
    return t * _sigmoid(t)


def _qkv_proj_kernel(x_ref, lnw_ref, *refs):
    wq_ref, wk_ref, wv_ref, wrv_ref = refs[:4]
    out_refs = refs[4:4 + N_GROUPS]
    rv_ref, xn_scr, xn4_scr, lhs_scr = refs[4 + N_GROUPS:]
    n_lane_tiles = D_MODEL // LANES
    q_scale = HEAD_DIM ** -0.5 * LOG2E
    part_rows = PROJ_ROWS // QKV_PARTS
    d4, d16 = ATTN_GROUPS[1][1], ATTN_GROUPS[2][1]
    rows4, rows16 = part_rows // d4, part_rows // d16
    NAT, QPERM, BY4, BY16 = range(4)

    def part(p):
        rows_p = slice(p * part_rows, (p + 1) * part_rows)
        lhs = lhs_scr.at[p]
        xn = _rmsnorm_rows(x_ref[rows_p, :], lnw_ref[...])
        lhs[NAT] = xn.astype(BF16)
        for ct in range(n_lane_tiles):
            xn_scr[ct] = xn[:, ct * LANES:(ct + 1) * LANES]
        sub = BAND // MIX_STRIDE
        for ct in range(n_lane_tiles):
            lanes = slice(ct * LANES, (ct + 1) * LANES)
            lhs[QPERM, :, lanes] = jnp.concatenate(
                [xn_scr[ct, pl.ds(blk * BAND + r4, sub, stride=MIX_STRIDE), :]
                 for blk in range(part_rows // BAND) for r4 in range(MIX_STRIDE)],
                axis=0).astype(BF16)
            by4 = jnp.concatenate(
                [xn_scr[ct, pl.ds(r, rows4, stride=d4), :] for r in range(d4)], axis=0)
            xn4_scr[ct] = by4
            lhs[BY4, :, lanes] = by4.astype(BF16)
        for ct in range(n_lane_tiles):
            by16 = jnp.concatenate(
                [xn4_scr[ct, pl.ds((r % d4) * rows4 + r // d4, rows16, stride=d16 // d4), :]
                 for r in range(d16)], axis=0)
            lhs[BY16, :, ct * LANES:(ct + 1) * LANES] = by16.astype(BF16)
        yield

        rv_ref[rows_p, :] = _dot(lhs[NAT], wrv_ref[...]).astype(BF16)
        for g, (order_q, order_kv) in enumerate(((QPERM, NAT), (BY4, BY4), (BY16, BY16))):
            gcols = slice(g * GROUP_WIDTH, (g + 1) * GROUP_WIDTH)
            dil = ATTN_GROUPS[g][1]
            rows = part_rows // dil
            q = (_dot(lhs[order_q], wq_ref[:, gcols]) * q_scale).astype(BF16)
            k = _dot(lhs[order_kv], wk_ref[:, gcols]).astype(BF16)
            v = _dot(lhs[order_kv], wv_ref[:, gcols]).astype(BF16)
            for r in range(dil):
                src_rows = slice(r * rows, (r + 1) * rows)
                dst_rows = slice(p * rows, (p + 1) * rows)
                for h in range(ATTN_HEADS):
                    hs = slice(h * HEAD_DIM, (h + 1) * HEAD_DIM)
                    for which, val in enumerate((q, k, v)):
                        c0 = (3 * h + which) * HEAD_DIM
                        out_refs[g][r, dst_rows, c0:c0 + HEAD_DIM] = val[src_rows, hs]

    parts = [part(p) for p in range(QKV_PARTS)]
    for gen in parts:
        next(gen)
    for gen in parts:
        next(gen, None)


def _qkv_proj(x, lnw, w16):
    B, S, _ = x.shape
    nt = S // PROJ_ROWS
    w_specs = [pl.BlockSpec((D_MODEL, QKV_WIDTH), lambda b, t, part=part: (0, part),
                            pipeline_mode=RESIDENT) for part in range(3)]
    w_specs.append(pl.BlockSpec((D_MODEL, RET_V_WIDTH), lambda b, t: (0, COL_RV // RET_V_WIDTH),
                                pipeline_mode=RESIDENT))
    return pl.pallas_call(
        _qkv_proj_kernel,
        grid=(B, nt),
        in_specs=[
            pl.BlockSpec((None, PROJ_ROWS, D_MODEL), lambda b, t: (b, t, 0)),
            pl.BlockSpec((1, D_MODEL), lambda b, t: (0, 0)),
            *w_specs,
        ],
        out_specs=[
            *[pl.BlockSpec((None, dil, PROJ_ROWS // dil, QKV_WIDTH), lambda b, t: (b, 0, t, 0))
              for _, dil in ATTN_GROUPS],
            pl.BlockSpec((None, PROJ_ROWS, RET_V_WIDTH), lambda b, t: (b, t, 0)),
        ],
        out_shape=[
            *[jax.ShapeDtypeStruct((B, dil, S // dil, QKV_WIDTH), BF16) for _, dil in ATTN_GROUPS],
            jax.ShapeDtypeStruct((B, S, RET_V_WIDTH), BF16),
        ],
        scratch_shapes=[pltpu.VMEM((D_MODEL // LANES, PROJ_ROWS // QKV_PARTS, LANES), F32),
                        pltpu.VMEM((D_MODEL // LANES, PROJ_ROWS // QKV_PARTS, LANES), F32),
                        pltpu.VMEM((QKV_PARTS, 4, PROJ_ROWS // QKV_PARTS, D_MODEL), BF16)],
        compiler_params=pltpu.CompilerParams(
            dimension_semantics=("arbitrary", "arbitrary"), vmem_limit_bytes=VMEM_LIMIT),
        name="qkv_proj",
    )(x, lnw, w16, w16, w16, w16)


def _swap_lane_pairs(t):
    n = t.shape[-1]
    lane = lax.broadcasted_iota(jnp.int32, t.shape, 1)
    up = pltpu.roll(t, n - 1, axis=1)
    down = pltpu.roll(t, 1, axis=1)
    return jnp.where(lane % 2 == 0, up, down)


def _gate_proj_items(x_ref, lnw_ref, wqk_ref, wg_ref, wrg_ref, wmg_ref,
                     cos_ref, sin_ref, kdec_ref, bg_ref,
                     rq_ref, rkt_ref, ag_ref, rg_ref, mg_ref):
    cache = {}

    def lhs():
        if "xb" not in cache:
            cache["xb"] = _rmsnorm_rows(x_ref[...], lnw_ref[...]).astype(BF16)
        return cache["xb"]

    def rope(t, rows):
        cos = cos_ref[rows, :]
        sin = sin_ref[rows, :]
        tiles = [t[:, i * LANES:(i + 1) * LANES] for i in range(t.shape[1] // LANES)]
        return jnp.concatenate([tt * cos + _swap_lane_pairs(tt) * sin for tt in tiles], axis=1)

    def retention_q(cols, rows):
        t = _dot(lhs()[rows], wqk_ref[:, cols])
        yield
        rq_ref[rows, cols] = rope(t, rows).astype(BF16)

    def retention_k(cols, rows):
        t = _dot(lhs()[rows], wqk_ref[:, RET_QK_WIDTH + cols.start:RET_QK_WIDTH + cols.stop])
        yield
        k = rope(t, rows) * kdec_ref[rows, cols]
        for c in range((rows.stop - rows.start) // RET_CHUNK):
            rkt_ref[rows.start // RET_CHUNK + c, cols, :] = (
                k[c * RET_CHUNK:(c + 1) * RET_CHUNK, :].T.astype(BF16))

    def merge_gate(cols, rows):
        t = _dot(lhs()[rows], wmg_ref[:, cols])
        yield
        mg_ref[rows, cols] = _sigmoid(t + bg_ref[:, cols]).astype(BF16)

    def retention_gate(cols, rows):
        t = _dot(lhs()[rows], wrg_ref[:, cols])
        yield
        rg_ref[rows, cols] = _silu(t).astype(BF16)

    def attention_gate(cols, rows):
        t = _dot(lhs()[rows], wg_ref[:, cols])
        yield
        ag_ref[rows, cols] = _silu(t).astype(BF16)

    items = []
    part = GATE_ROWS // GATE_ROW_PARTS
    for fn, total, pieces in ((retention_q, RET_QK_WIDTH, GATE_PIECES[0]),
                              (retention_k, RET_QK_WIDTH, GATE_PIECES[1]),
                              (merge_gate, 2 * D_MODEL, GATE_PIECES[2]),
                              (retention_gate, RET_V_WIDTH, GATE_PIECES[3]),
                              (attention_gate, GROUP_WIDTH, GATE_PIECES[4])):
        width = total // pieces
        items += [lambda fn=fn, cols=slice(i * width, (i + 1) * width),
                  rows=slice(j * part, (j + 1) * part): fn(cols, rows)
                  for i in range(pieces) for j in range(GATE_ROW_PARTS)]
    return items


def _attention_items(in_refs, bias_ref, out_ref, acc_scr, max_scr, den_scr, nat_scr, first):
    ones = jnp.ones((2 * BAND, HEAD_DIM), BF16)
    quarter = ATTN_TILE // MIX_STRIDE

    def unit(g, r, n):
        dil = ATTN_GROUPS[g][1]
        cur_ref, prev_ref = in_refs[2 * g:2 * g + 2]
        q_cols, k_cols, v_cols = (slice(i * HEAD_DIM, (i + 1) * HEAD_DIM) for i in range(3))
        bias_base = 2 if dil == 1 else 0
        q = cur_ref[r, n * BAND:(n + 1) * BAND, q_cols]
        if n == 0:
            k = jnp.concatenate([prev_ref[r, :, k_cols], cur_ref[r, 0:BAND, k_cols]], axis=0)
            v = jnp.concatenate([prev_ref[r, :, v_cols], cur_ref[r, 0:BAND, v_cols]], axis=0)
            bias = bias_ref[bias_base + first]
        else:
            k = cur_ref[r, (n - 1) * BAND:(n + 1) * BAND, k_cols]
            v = cur_ref[r, (n - 1) * BAND:(n + 1) * BAND, v_cols]
            bias = bias_ref[bias_base]
        s = lax.dot_general(q, k, (((1,), (1,)), ((), ())), preferred_element_type=F32) + bias
        yield
        m = jnp.max(s, axis=-1, keepdims=True)
        p = jnp.exp2(s - m).astype(BF16)
        yield
        res = _dot(p, jnp.concatenate([v, ones], axis=1))
        acc, den = res[:, :HEAD_DIM], res[:, HEAD_DIM:]
        mb = jnp.broadcast_to(m, (BAND, HEAD_DIM))
        if dil == 1:
            sub = BAND // MIX_STRIDE
            for r4 in range(MIX_STRIDE):
                dst = pl.ds(r4 * quarter + n * sub, sub)
                src = slice(r4 * sub, (r4 + 1) * sub)
                acc_scr[g, dst, :] = acc[src]
                max_scr[g, dst, :] = mb[src]
                den_scr[g, dst, :] = den[src]
        else:
            if dil == MIX_STRIDE:
                dst = pl.ds(r * quarter + n * BAND, BAND)
            else:
                sub_stride = dil // MIX_STRIDE
                dst = pl.ds((r % MIX_STRIDE) * quarter + r // MIX_STRIDE
                            + n * BAND * sub_stride, BAND, stride=sub_stride)
            acc_scr[g, dst, :] = acc
            max_scr[g, dst, :] = mb
            den_scr[g, dst, :] = den

    def mix(r4, ch):
        rows = pl.ds(r4 * quarter + ch * BAND, BAND)
        m0, m1, m2 = max_scr[0, rows, :], max_scr[1, rows, :], max_scr[2, rows, :]
        mx = jnp.maximum(jnp.maximum(m0, m1), m2)
        w0 = jnp.exp2(m0 - mx)
        w1 = jnp.exp2(m1 - mx)
        w2 = jnp.exp2(m2 - mx)
        num = w0 * acc_scr[0, rows, :] + w1 * acc_scr[1, rows, :] + w2 * acc_scr[2, rows, :]
        den = w0 * den_scr[0, rows, :] + w1 * den_scr[1, rows, :] + w2 * den_scr[2, rows, :]
        nat_scr[pl.ds(r4 + ch * BAND * MIX_STRIDE, BAND, stride=MIX_STRIDE), :] = num / den

    def write_out():
        out_ref[...] = nat_scr[...].astype(BF16)

    def mix_quarter(r4):
        for ch in range(quarter // BAND):
            mix(r4, ch)

    units = [lambda n=n: unit(0, 0, n) for n in range(ATTN_TILE // BAND)]
    quarter_done = []
    for r4 in range(MIX_STRIDE):
        units += [lambda g=g, r=r, n=n: unit(g, r, n)
                  for g in range(1, N_GROUPS)
                  for r in range(r4, ATTN_GROUPS[g][1], MIX_STRIDE)
                  for n in range(ATTN_TILE // ATTN_GROUPS[g][1] // BAND)]
        quarter_done.append(len(units))
    return units, quarter_done, mix_quarter, write_out


def _attn_gates_kernel(*refs):
    n_attn = 2 * N_GROUPS
    in_refs = refs[:n_attn]
    bias_ref = refs[n_attn]
    proj_in = refs[n_attn + 1:n_attn + 11]
    attn_out = refs[n_attn + 11]
    proj_out = refs[n_attn + 12:n_attn + 17]
    acc_scr, max_scr, den_scr, nat_scr = refs[n_attn + 17:]
    first = (pl.program_id(1) == 0).astype(jnp.int32)
    unit_makers, quarter_done, mix_quarter, write_out = _attention_items(
        in_refs, bias_ref, attn_out, acc_scr, max_scr, den_scr, nat_scr, first)
    proj = _gate_proj_items(*proj_in, *proj_out)
    per_round = len(unit_makers) // len(proj)
    assert per_round * len(proj) == len(unit_makers)
    rounds = [unit_makers[i * per_round:(i + 1) * per_round] for i in range(len(proj))]

    def advance(gen):
        next(gen, None)

    prev_units = []
    mixed = 0
    for r in range(len(rounds) + 1):
        units = [make() for make in rounds[r]] if r < len(rounds) else []
        for j in range(max(len(units), len(prev_units))):
            if j < len(units):
                advance(units[j])
            if j < len(prev_units):
                advance(prev_units[j])
        while mixed < len(quarter_done) and quarter_done[mixed] <= r * per_round:
            mix_quarter(mixed)
            mixed += 1
        if r < len(proj):
            for _ in proj[r]():
                pass
        for u in units:
            advance(u)
        prev_units = units
    write_out()


def _attn_bias_tables():
    rho = jnp.arange(BAND)[:, None]
    c = jnp.arange(2 * BAND)[None, :]
    sub = BAND // MIX_STRIDE
    tables = []
    for a in (rho, MIX_STRIDE * (rho % sub) + rho // sub):
        ok = (c >= a) & (c <= a + BAND)
        tables += [ok, ok & (c >= BAND)]
    return jnp.where(jnp.stack(tables), 0.0, NEG_BIG).astype(F32)


def _attn_gates(qkv_groups, x, lnw, w16, cos_t, sin_t, kdec, bg):
    B, S, _ = x.shape
    nt = S // ATTN_TILE
    cpt = GATE_ROWS // RET_CHUNK
    in_specs = []
    args = []
    for (_, dil), qkv in zip(ATTN_GROUPS, qkv_groups):
        rows = ATTN_TILE // dil
        ratio = rows // BAND

        in_specs += [
            pl.BlockSpec((None, dil, rows, 3 * HEAD_DIM), lambda b, t, h: (b, 0, t, h)),
            pl.BlockSpec((None, dil, BAND, 3 * HEAD_DIM),
                         lambda b, t, h, ratio=ratio: (b, 0, jnp.maximum(t * ratio - 1, 0), h)),
        ]
        args += [qkv] * 2
    bias = _attn_bias_tables()
    in_specs.append(pl.BlockSpec(bias.shape, lambda b, t, h: (0, 0, 0), pipeline_mode=RESIDENT))

    def wspec(n, col0=0):
        return pl.BlockSpec((D_MODEL, n), lambda b, t, h: (0, col0 // n), pipeline_mode=RESIDENT)

    def row_spec(n):
        return pl.BlockSpec((None, GATE_ROWS, n), lambda b, t, h: (b, t * ATTN_HEADS + h, 0))

    tspec = pl.BlockSpec((GATE_ROWS, LANES), lambda b, t, h: (t * ATTN_HEADS + h, 0))
    in_specs += [
        row_spec(D_MODEL),
        pl.BlockSpec((1, D_MODEL), lambda b, t, h: (0, 0)),
        wspec(2 * RET_QK_WIDTH, COL_RQ),
        wspec(GROUP_WIDTH, COL_AG), wspec(RET_V_WIDTH, COL_RG), wspec(2 * D_MODEL, COL_MG),
        tspec, tspec,
        pl.BlockSpec((GATE_ROWS, RET_QK_WIDTH), lambda b, t, h: (0, 0), pipeline_mode=RESIDENT),
        pl.BlockSpec((1, 2 * D_MODEL), lambda b, t, h: (0, 0)),
    ]
    out_widths = (GROUP_WIDTH, RET_V_WIDTH, 2 * D_MODEL)
    per_group = pltpu.VMEM((N_GROUPS, ATTN_TILE, HEAD_DIM), F32)
    return pl.pallas_call(
        _attn_gates_kernel,
        grid=(B, nt, ATTN_HEADS),
        in_specs=in_specs,
        out_specs=[
            pl.BlockSpec((None, ATTN_TILE, HEAD_DIM), lambda b, t, h: (b, t, h)),
            row_spec(RET_QK_WIDTH),
            pl.BlockSpec((None, cpt, RET_QK_WIDTH, RET_CHUNK),
                         lambda b, t, h: (b, t * ATTN_HEADS + h, 0, 0)),
            *[row_spec(n) for n in out_widths],
        ],
        out_shape=[
            jax.ShapeDtypeStruct((B, S, GROUP_WIDTH), BF16),
            jax.ShapeDtypeStruct((B, S, RET_QK_WIDTH), BF16),
            jax.ShapeDtypeStruct((B, S // RET_CHUNK, RET_QK_WIDTH, RET_CHUNK), BF16),
            *[jax.ShapeDtypeStruct((B, S, n), BF16) for n in out_widths],
        ],
        scratch_shapes=[per_group, per_group, per_group,
                        pltpu.VMEM((ATTN_TILE, HEAD_DIM), F32)],
        compiler_params=pltpu.CompilerParams(
            dimension_semantics=("arbitrary", "arbitrary", "arbitrary"),
            vmem_limit_bytes=VMEM_LIMIT),
        name="attn_gates",
    )(*args, bias, x, lnw, w16, w16, w16, w16, cos_t, sin_t, kdec, bg)


def _retention_kernel(q_ref, kt_ref, v_ref, gnw_ref, eps_ref, cd_ref,
                      out_ref, state, p_scr):
    @pl.when(pl.program_id(1) == 0)
    def _():
        state[...] = jnp.zeros_like(state)

    dk, dv, C = RET_KEY_DIM, RET_VALUE_DIM, RET_CHUNK
    row_k = lax.broadcasted_iota(jnp.int32, (2 * dk, C), 0)
    row = lax.broadcasted_iota(jnp.int32, (2 * dk, 2 * dv), 0)
    col = lax.broadcasted_iota(jnp.int32, (C, 2 * dv), 1)
    diag_blk = (row < dk) == (col < dv)
    causal = (col % C) <= lax.broadcasted_iota(jnp.int32, (C, 2 * C), 0)
    units = [(bi, j) for bi in range(RET_BATCH) for j in range(RET_HEADS // 2)]

    def q_pair(bi, j):
        return q_ref[bi, :, j * 2 * dk:(j + 1) * 2 * dk]

    def kt_pair(bi, j):
        return kt_ref[bi, j * 2 * dk:(j + 1) * 2 * dk, :]

    def v_pair(bi, j):
        return v_ref[bi, :, j * 2 * dv:(j + 1) * 2 * dv]

    for bi, j in units:
        kt = kt_pair(bi, j)
        zero = jnp.zeros_like(kt)
        kt_blk = jnp.concatenate([jnp.where(row_k < dk, kt, zero),
                                  jnp.where(row_k >= dk, kt, zero)], axis=1)
        s = _dot(q_pair(bi, j), kt_blk).astype(BF16)
        p_scr[bi, j] = jnp.where(causal, s, jnp.zeros_like(s))

    for bi, j in units:
        v = v_pair(bi, j)
        zero = jnp.zeros_like(v)
        v_blk = jnp.concatenate([jnp.where(col < dv, v, zero),
                                 jnp.where(col >= dv, v, zero)], axis=0)
        u = _dot(jnp.concatenate([p_scr[bi, j], q_pair(bi, j)], axis=1),
                 jnp.concatenate([v_blk, state[bi, j].astype(BF16)], axis=0))
        for hh in range(2):
            h = 2 * j + hh
            hs = slice(h * dv, (h + 1) * dv)
            uh = u[:, hh * dv:(hh + 1) * dv]
            uh = uh * lax.rsqrt(jnp.sum(uh * uh, axis=-1, keepdims=True) + eps_ref[h])
            out_ref[bi, :, hs] = (uh * gnw_ref[:, hs]).astype(BF16)

    for bi, j in units:
        kv = _dot(kt_pair(bi, j), v_pair(bi, j))
        state[bi, j] = cd_ref[j] * (state[bi, j] + jnp.where(diag_blk, kv, 0.0))


def _retention(rq, rkt, rv, gnw, eps_t, cd):
    B, S, _ = rv.shape
    nc = S // RET_CHUNK
    npair = RET_HEADS // 2
    pair_shape = (RET_BATCH, npair, 2 * RET_KEY_DIM, 2 * RET_VALUE_DIM)

    def const(shape):
        return pl.BlockSpec(shape, lambda b, n: (0,) * len(shape))

    def rows(width):
        return pl.BlockSpec((RET_BATCH, RET_CHUNK, width), lambda b, n: (b, n, 0))

    return pl.pallas_call(
        _retention_kernel,
        grid=(B // RET_BATCH, nc),
        in_specs=[
            rows(RET_QK_WIDTH),
            pl.BlockSpec((RET_BATCH, None, RET_QK_WIDTH, RET_CHUNK), lambda b, n: (b, n, 0, 0)),
            rows(RET_V_WIDTH),
            const((1, RET_V_WIDTH)),
            const((RET_HEADS, RET_CHUNK, RET_VALUE_DIM)),
            const(pair_shape[1:]),
        ],
        out_specs=rows(RET_V_WIDTH),
        out_shape=jax.ShapeDtypeStruct((B, S, RET_V_WIDTH), BF16),
        scratch_shapes=[pltpu.VMEM(pair_shape, F32),
                        pltpu.VMEM((RET_BATCH, npair, RET_CHUNK, 2 * RET_CHUNK), BF16)],
        compiler_params=pltpu.CompilerParams(
            dimension_semantics=("arbitrary", "arbitrary"), vmem_limit_bytes=VMEM_LIMIT),
        name="retention",
    )(rq, rkt, rv, gnw, eps_t, cd)


def _merge_out_kernel(a_ref, ag_ref, r_ref, rg_ref, mg_ref, x_ref, wa_ref, wr_ref, wo_ref,
                      lnf_ref, out_ref):
    def part(rows):
        y_attn = _dot(a_ref[rows, :] * ag_ref[rows, :], wa_ref[...])
        y_ret = _dot(r_ref[rows, :] * rg_ref[rows, :], wr_ref[...])
        yield
        merged = (mg_ref[rows, :D_MODEL].astype(F32) * y_attn
                  + mg_ref[rows, D_MODEL:].astype(F32) * y_ret)
        h = x_ref[rows, :] + _dot(merged.astype(BF16), wo_ref[...])
        yield
        out_ref[rows, :] = _rmsnorm_rows(h, lnf_ref[...])

    n_parts = OUT_ROWS // OUT_PART
    parts = [part(slice(i * OUT_PART, (i + 1) * OUT_PART)) for i in range(n_parts)]
    for step in range(n_parts + 2):
        for lag in range(3):
            i = step - lag
            if 0 <= i < n_parts:
                next(parts[i], None)


def _merge_out(attn, ag, ret, rg, mg, x, wa, wr, wo, lnf):
    B, S, _ = x.shape
    nt = S // OUT_ROWS

    def rows(n):
        return pl.BlockSpec((None, OUT_ROWS, n), lambda b, t: (b, t, 0))

    def const(shape):
        return pl.BlockSpec(shape, lambda b, t: (0,) * len(shape), pipeline_mode=RESIDENT)

    return pl.pallas_call(
        _merge_out_kernel,
        grid=(B, nt),
        in_specs=[rows(GROUP_WIDTH), rows(GROUP_WIDTH), rows(RET_V_WIDTH), rows(RET_V_WIDTH),
                  rows(2 * D_MODEL), rows(D_MODEL),
                  const((GROUP_WIDTH, D_MODEL)), const((RET_V_WIDTH, D_MODEL)),
                  const((D_MODEL, D_MODEL)), const((1, D_MODEL))],
        out_specs=rows(D_MODEL),
        out_shape=jax.ShapeDtypeStruct((B, S, D_MODEL), F32),
        compiler_params=pltpu.CompilerParams(
            dimension_semantics=("arbitrary", "arbitrary"), vmem_limit_bytes=VMEM_LIMIT),
        name="merge_out",
    )(attn, ag, ret, rg, mg, x, wa, wr, wo, lnf)


def _rope_tables(S):
    half = RET_KEY_DIM // 2
    inv_freq = ROPE_BASE ** (-jnp.linspace(0.0, 1.0, half, dtype=F32))
    step = 64
    a = (jnp.arange(S // step, dtype=F32) * step)[:, None] * inv_freq[None, :]
    b = jnp.arange(step, dtype=F32)[:, None] * inv_freq[None, :]
    ca, sa, cb, sb = jnp.cos(a)[:, None], jnp.sin(a)[:, None], jnp.cos(b)[None], jnp.sin(b)[None]
    cos = (ca * cb - sa * sb).reshape(S, half)
    sin = (sa * cb + ca * sb).reshape(S, half)
    heads_per_tile = LANES // RET_KEY_DIM
    sign = jnp.where(jnp.arange(LANES) % 2 == 0, -1.0, 1.0).astype(F32)
    cos_t = jnp.tile(jnp.repeat(cos, 2, axis=1), (1, heads_per_tile))
    sin_t = jnp.tile(jnp.repeat(sin, 2, axis=1), (1, heads_per_tile)) * sign[None, :]
    return cos_t, sin_t


def _retention_constants():
    H, C, dk, dv = RET_HEADS, RET_CHUNK, RET_KEY_DIM, RET_VALUE_DIM
    log_gamma = jnp.log(1.0 - 2.0 ** (-5.0 - jnp.arange(H, dtype=F32)))
    idx = jnp.arange(C, dtype=F32)
    inv_decay = jnp.exp(-(idx + 1.0)[None, :] * log_gamma[:, None])
    chunk_decay = jnp.exp(C * log_gamma)
    kdec = jnp.repeat(inv_decay.T, dk, axis=1) * dk ** -0.5
    kdec = jnp.tile(kdec, (GATE_ROWS // C, 1))
    eps_t = jnp.broadcast_to((dv * NORM_EPS * inv_decay * inv_decay)[:, :, None], (H, C, dv))
    cd = jnp.broadcast_to(chunk_decay[:, None, None], (H, dk, 2 * dv)).reshape(H // 2, 2 * dk, 2 * dv)
    return kdec, eps_t, cd


def kernel(x, ln1_w, w_in, b_gate, attn_proj, ret_proj, ret_gn_w, w_out, lnf_w):
    B, S, _ = x.shape
    assert w_in.shape[0] == 1, "single layer"
    w16 = w_in[0].astype(BF16)

    lnw = ln1_w[0].reshape(1, D_MODEL)
    q1, q2, q3, rv = _qkv_proj(x, lnw, w16)
    cos_t, sin_t = _rope_tables(S)
    kdec, eps_t, cd = _retention_constants()
    attn, rq, rkt, ag, rg, mg = _attn_gates(
        (q1, q2, q3), x, lnw, w16, cos_t, sin_t, kdec, b_gate[0].reshape(1, 2 * D_MODEL))
    gnw = (ret_gn_w[0] * RET_VALUE_DIM ** 0.5).reshape(1, RET_V_WIDTH)
    ret = _retention(rq, rkt, rv, gnw, eps_t, cd)

    return _merge_out(attn, ag, ret, rg, mg, x,
                      attn_proj[0].astype(BF16), ret_proj[0].astype(BF16),
                      w_out[0].astype(BF16), lnf_w.reshape(1, D_MODEL))
```

```python
import jax
import jax.numpy as jnp
from jax import lax
from jax.experimental import pallas as pl
from jax.experimental.pallas import tpu as pltpu

D_MODEL = 1024
ATTN_GROUPS = ((128, 1), (512, 4), (2048, 16))
N_GROUPS = 3
ATTN_HEADS = 4
HEAD_DIM = 128
GROUP_WIDTH = ATTN_HEADS * HEAD_DIM
QKV_WIDTH = N_GROUPS * GROUP_WIDTH
BAND = 128
RET_HEADS = 8
RET_KEY_DIM = 64
RET_VALUE_DIM = 128
RET_QK_WIDTH = RET_HEADS * RET_KEY_DIM
RET_V_WIDTH = RET_HEADS * RET_VALUE_DIM
RET_CHUNK = 128
ROPE_BASE = 10000.0
COL_AG = 3 * QKV_WIDTH
COL_RQ = COL_AG + GROUP_WIDTH
COL_RV = COL_RQ + 2 * RET_QK_WIDTH
COL_RG = COL_RV + RET_V_WIDTH
COL_MG = COL_RG + RET_V_WIDTH
NORM_EPS = 1e-6
NEG_BIG = -1e30
LANES = 128
LOG2E = 1.4426950408889634
MIX_STRIDE = 4

PROJ_ROWS = 1024
QKV_PARTS = 4
ATTN_TILE = 2048
GATE_ROWS = ATTN_TILE // ATTN_HEADS
RET_BATCH = 8
OUT_ROWS = 1024
OUT_PART = 1024
V7X_VMEM_BYTES = 64 * 1024 * 1024
VMEM_LIMIT = V7X_VMEM_BYTES * 7 // 8
GATE_ROW_PARTS = 2
GATE_PIECES = (2, 2, 4, 2, 2)
RESIDENT = pl.Buffered(1)

BF16 = jnp.bfloat16
F32 = jnp.float32


def _rmsnorm_rows(x, w):
    return x * lax.rsqrt(jnp.mean(x * x, axis=-1, keepdims=True) + NORM_EPS) * w


def _dot(a, b):
    return jnp.dot(a, b, preferred_element_type=F32)


def _sigmoid(t):
    return 0.5 * jnp.tanh(0.5 * t) + 0.5


def _silu(t):
    return t * _sigmoid(t)


def _qkv_proj_kernel(x_ref, lnw_ref, *refs):
    wq_ref, wk_ref, wv_ref, wrv_ref = refs[:4]
    out_refs = refs[4:4 + N_GROUPS]
    rv_ref, xn_scr, xn4_scr, lhs_scr = refs[4 + N_GROUPS:]
    n_lane_tiles = D_MODEL // LANES
    q_scale = HEAD_DIM ** -0.5 * LOG2E
    part_rows = PROJ_ROWS // QKV_PARTS
    d4, d16 = ATTN_GROUPS[1][1], ATTN_GROUPS[2][1]
    rows4, rows16 = part_rows // d4, part_rows // d16
    NAT, QPERM, BY4, BY16 = range(4)

    def part(p):
        rows_p = slice(p * part_rows, (p + 1) * part_rows)
        lhs = lhs_scr.at[p]
        xn = _rmsnorm_rows(x_ref[rows_p, :], lnw_ref[...])
        lhs[NAT] = xn.astype(BF16)
        for ct in range(n_lane_tiles):
            xn_scr[ct] = xn[:, ct * LANES:(ct + 1) * LANES]
        sub = BAND // MIX_STRIDE
        for ct in range(n_lane_tiles):
            lanes = slice(ct * LANES, (ct + 1) * LANES)
            lhs[QPERM, :, lanes] = jnp.concatenate(
                [xn_scr[ct, pl.ds(blk * BAND + r4, sub, stride=MIX_STRIDE), :]
                 for blk in range(part_rows // BAND) for r4 in range(MIX_STRIDE)],
                axis=0).astype(BF16)
            by4 = jnp.concatenate(
                [xn_scr[ct, pl.ds(r, rows4, stride=d4), :] for r in range(d4)], axis=0)
            xn4_scr[ct] = by4
            lhs[BY4, :, lanes] = by4.astype(BF16)
        for ct in range(n_lane_tiles):
            by16 = jnp.concatenate(
                [xn4_scr[ct, pl.ds((r % d4) * rows4 + r // d4, rows16, stride=d16 // d4), :]
                 for r in range(d16)], axis=0)
            lhs[BY16, :, ct * LANES:(ct + 1) * LANES] = by16.astype(BF16)
        yield

        rv_ref[rows_p, :] = _dot(lhs[NAT], wrv_ref[...]).astype(BF16)
        for g, (order_q, order_kv) in enumerate(((QPERM, NAT), (BY4, BY4), (BY16, BY16))):
            gcols = slice(g * GROUP_WIDTH, (g + 1) * GROUP_WIDTH)
            dil = ATTN_GROUPS[g][1]
            rows = part_rows // dil
            q = (_dot(lhs[order_q], wq_ref[:, gcols]) * q_scale).astype(BF16)
            k = _dot(lhs[order_kv], wk_ref[:, gcols]).astype(BF16)
            v = _dot(lhs[order_kv], wv_ref[:, gcols]).astype(BF16)
            for r in range(dil):
                src_rows = slice(r * rows, (r + 1) * rows)
                dst_rows = slice(p * rows, (p + 1) * rows)
                for h in range(ATTN_HEADS):
                    hs = slice(h * HEAD_DIM, (h + 1) * HEAD_DIM)
                    for which, val in enumerate((q, k, v)):
                        c0 = (3 * h + which) * HEAD_DIM
                        out_refs[g][r, dst_rows, c0:c0 + HEAD_DIM] = val[src_rows, hs]

    parts = [part(p) for p in range(QKV_PARTS)]
    for gen in parts:
        next(gen)
    for gen in parts:
        next(gen, None)


def _qkv_proj(x, lnw, w16):
    B, S, _ = x.shape
    nt = S // PROJ_ROWS
    w_specs = [pl.BlockSpec((D_MODEL, QKV_WIDTH), lambda b, t, part=part: (0, part),
                            pipeline_mode=RESIDENT) for part in range(3)]
    w_specs.append(pl.BlockSpec((D_MODEL, RET_V_WIDTH), lambda b, t: (0, COL_RV // RET_V_WIDTH),
                                pipeline_mode=RESIDENT))
    return pl.pallas_call(
        _qkv_proj_kernel,
        grid=(B, nt),
        in_specs=[
            pl.BlockSpec((None, PROJ_ROWS, D_MODEL), lambda b, t: (b, t, 0)),
            pl.BlockSpec((1, D_MODEL), lambda b, t: (0, 0)),
            *w_specs,
        ],
        out_specs=[
            *[pl.BlockSpec((None, dil, PROJ_ROWS // dil, QKV_WIDTH), lambda b, t: (b, 0, t, 0))
              for _, dil in ATTN_GROUPS],
            pl.BlockSpec((None, PROJ_ROWS, RET_V_WIDTH), lambda b, t: (b, t, 0)),
        ],
        out_shape=[
            *[jax.ShapeDtypeStruct((B, dil, S // dil, QKV_WIDTH), BF16) for _, dil in ATTN_GROUPS],
            jax.ShapeDtypeStruct((B, S, RET_V_WIDTH), BF16),
        ],
        scratch_shapes=[pltpu.VMEM((D_MODEL // LANES, PROJ_ROWS // QKV_PARTS, LANES), F32),
                        pltpu.VMEM((D_MODEL // LANES, PROJ_ROWS // QKV_PARTS, LANES), F32),
                        pltpu.VMEM((QKV_PARTS, 4, PROJ_ROWS // QKV_PARTS, D_MODEL), BF16)],
        compiler_params=pltpu.CompilerParams(
            dimension_semantics=("arbitrary", "arbitrary"), vmem_limit_bytes=VMEM_LIMIT),
        name="qkv_proj",
    )(x, lnw, w16, w16, w16, w16)


def _swap_lane_pairs(t):
    n = t.shape[-1]
    lane = lax.broadcasted_iota(jnp.int32, t.shape, 1)
    up = pltpu.roll(t, n - 1, axis=1)
    down = pltpu.roll(t, 1, axis=1)
    return jnp.where(lane % 2 == 0, up, down)


def _gate_proj_items(x_ref, lnw_ref, wqk_ref, wg_ref, wrg_ref, wmg_ref,
                     cos_ref, sin_ref, kdec_ref, bg_ref,
                     rq_ref, rkt_ref, ag_ref, rg_ref, mg_ref):
    cache = {}

    def lhs():
        if "xb" not in cache:
            cache["xb"] = _rmsnorm_rows(x_ref[...], lnw_ref[...]).astype(BF16)
        return cache["xb"]

    def rope(t, rows):
        cos = cos_ref[rows, :]
        sin = sin_ref[rows, :]
        tiles = [t[:, i * LANES:(i + 1) * LANES] for i in range(t.shape[1] // LANES)]
        return jnp.concatenate([tt * cos + _swap_lane_pairs(tt) * sin for tt in tiles], axis=1)

    def retention_q(cols, rows):
        t = _dot(lhs()[rows], wqk_ref[:, cols])
        yield
        rq_ref[rows, cols] = rope(t, rows).astype(BF16)

    def retention_k(cols, rows):
        t = _dot(lhs()[rows], wqk_ref[:, RET_QK_WIDTH + cols.start:RET_QK_WIDTH + cols.stop])
        yield
        k = rope(t, rows) * kdec_ref[rows, cols]
        for c in range((rows.stop - rows.start) // RET_CHUNK):
            rkt_ref[rows.start // RET_CHUNK + c, cols, :] = (
                k[c * RET_CHUNK:(c + 1) * RET_CHUNK, :].T.astype(BF16))

    def merge_gate(cols, rows):
        t = _dot(lhs()[rows], wmg_ref[:, cols])
        yield
        mg_ref[rows, cols] = _sigmoid(t + bg_ref[:, cols]).astype(BF16)

    def retention_gate(cols, rows):
        t = _dot(lhs()[rows], wrg_ref[:, cols])
        yield
        rg_ref[rows, cols] = _silu(t).astype(BF16)

    def attention_gate(cols, rows):
        t = _dot(lhs()[rows], wg_ref[:, cols])
        yield
        ag_ref[rows, cols] = _silu(t).astype(BF16)

    items = []
    part = GATE_ROWS // GATE_ROW_PARTS
    for fn, total, pieces in ((retention_q, RET_QK_WIDTH, GATE_PIECES[0]),
                              (retention_k, RET_QK_WIDTH, GATE_PIECES[1]),
                              (merge_gate, 2 * D_MODEL, GATE_PIECES[2]),
                              (retention_gate, RET_V_WIDTH, GATE_PIECES[3]),
                              (attention_gate, GROUP_WIDTH, GATE_PIECES[4])):
        width = total // pieces
        items += [lambda fn=fn, cols=slice(i * width, (i + 1) * width),
                  rows=slice(j * part, (j + 1) * part): fn(cols, rows)
                  for i in range(pieces) for j in range(GATE_ROW_PARTS)]
    return items


def _attention_items(in_refs, bias_ref, out_ref, acc_scr, max_scr, den_scr, nat_scr, first):
    ones = jnp.ones((2 * BAND, HEAD_DIM), BF16)
    quarter = ATTN_TILE // MIX_STRIDE

    def unit(g, r, n):
        dil = ATTN_GROUPS[g][1]
        cur_ref, prev_ref = in_refs[2 * g:2 * g + 2]
        q_cols, k_cols, v_cols = (slice(i * HEAD_DIM, (i + 1) * HEAD_DIM) for i in range(3))
        bias_base = 2 if dil == 1 else 0
        q = cur_ref[r, n * BAND:(n + 1) * BAND, q_cols]
        if n == 0:
            k = jnp.concatenate([prev_ref[r, :, k_cols], cur_ref[r, 0:BAND, k_cols]], axis=0)
            v = jnp.concatenate([prev_ref[r, :, v_cols], cur_ref[r, 0:BAND, v_cols]], axis=0)
            bias = bias_ref[bias_base + first]
        else:
            k = cur_ref[r, (n - 1) * BAND:(n + 1) * BAND, k_cols]
            v = cur_ref[r, (n - 1) * BAND:(n + 1) * BAND, v_cols]
            bias = bias_ref[bias_base]
        s = lax.dot_general(q, k, (((1,), (1,)), ((), ())), preferred_element_type=F32) + bias
        yield
        m = jnp.max(s, axis=-1, keepdims=True)
        p = jnp.exp2(s - m).astype(BF16)
        yield
        res = _dot(p, jnp.concatenate([v, ones], axis=1))
        acc, den = res[:, :HEAD_DIM], res[:, HEAD_DIM:]
        mb = jnp.broadcast_to(m, (BAND, HEAD_DIM))
        if dil == 1:
            sub = BAND // MIX_STRIDE
            for r4 in range(MIX_STRIDE):
                dst = pl.ds(r4 * quarter + n * sub, sub)
                src = slice(r4 * sub, (r4 + 1) * sub)
                acc_scr[g, dst, :] = acc[src]
                max_scr[g, dst, :] = mb[src]
                den_scr[g, dst, :] = den[src]
        else:
            if dil == MIX_STRIDE:
                dst = pl.ds(r * quarter + n * BAND, BAND)
            else:
                sub_stride = dil // MIX_STRIDE
                dst = pl.ds((r % MIX_STRIDE) * quarter + r // MIX_STRIDE
                            + n * BAND * sub_stride, BAND, stride=sub_stride)
            acc_scr[g, dst, :] = acc
            max_scr[g, dst, :] = mb
            den_scr[g, dst, :] = den

    def mix(r4, ch):
        rows = pl.ds(r4 * quarter + ch * BAND, BAND)
        m0, m1, m2 = max_scr[0, rows, :], max_scr[1, rows, :], max_scr[2, rows, :]
        mx = jnp.maximum(jnp.maximum(m0, m1), m2)
        w0 = jnp.exp2(m0 - mx)
        w1 = jnp.exp2(m1 - mx)
        w2 = jnp.exp2(m2 - mx)
        num = w0 * acc_scr[0, rows, :] + w1 * acc_scr[1, rows, :] + w2 * acc_scr[2, rows, :]
        den = w0 * den_scr[0, rows, :] + w1 * den_scr[1, rows, :] + w2 * den_scr[2, rows, :]
        nat_scr[pl.ds(r4 + ch * BAND * MIX_STRIDE, BAND, stride=MIX_STRIDE), :] = num / den

    def write_out():
        out_ref[...] = nat_scr[...].astype(BF16)

    def mix_quarter(r4):
        for ch in range(quarter // BAND):
            mix(r4, ch)

    units = [lambda n=n: unit(0, 0, n) for n in range(ATTN_TILE // BAND)]
    quarter_done = []
    for r4 in range(MIX_STRIDE):
        units += [lambda g=g, r=r, n=n: unit(g, r, n)
                  for g in range(1, N_GROUPS)
                  for r in range(r4, ATTN_GROUPS[g][1], MIX_STRIDE)
                  for n in range(ATTN_TILE // ATTN_GROUPS[g][1] // BAND)]
        quarter_done.append(len(units))
    return units, quarter_done, mix_quarter, write_out


def _attn_gates_kernel(*refs):
    n_attn = 2 * N_GROUPS
    in_refs = refs[:n_attn]
    bias_ref = refs[n_attn]
    proj_in = refs[n_attn + 1:n_attn + 11]
    attn_out = refs[n_attn + 11]
    proj_out = refs[n_attn + 12:n_attn + 17]
    acc_scr, max_scr, den_scr, nat_scr = refs[n_attn + 17:]
    first = (pl.program_id(1) == 0).astype(jnp.int32)
    unit_makers, quarter_done, mix_quarter, write_out = _attention_items(
        in_refs, bias_ref, attn_out, acc_scr, max_scr, den_scr, nat_scr, first)
    proj = _gate_proj_items(*proj_in, *proj_out)
    per_round = len(unit_makers) // len(proj)
    assert per_round * len(proj) == len(unit_makers)
    rounds = [unit_makers[i * per_round:(i + 1) * per_round] for i in range(len(proj))]

    def advance(gen):
        next(gen, None)

    prev_units = []
    mixed = 0
    for r in range(len(rounds) + 1):
        units = [make() for make in rounds[r]] if r < len(rounds) else []
        for j in range(max(len(units), len(prev_units))):
            if j < len(units):
                advance(units[j])
            if j < len(prev_units):
                advance(prev_units[j])
        while mixed < len(quarter_done) and quarter_done[mixed] <= r * per_round:
            mix_quarter(mixed)
            mixed += 1
        if r < len(proj):
            for _ in proj[r]():
                pass
        for u in units:
            advance(u)
        prev_units = units
    write_out()


def _attn_bias_tables():
    rho = jnp.arange(BAND)[:, None]
    c = jnp.arange(2 * BAND)[None, :]
    sub = BAND // MIX_STRIDE
    tables = []
    for a in (rho, MIX_STRIDE * (rho % sub) + rho // sub):
        ok = (c >= a) & (c <= a + BAND)
        tables += [ok, ok & (c >= BAND)]
    return jnp.where(jnp.stack(tables), 0.0, NEG_BIG).astype(F32)


def _attn_gates(qkv_groups, x, lnw, w16, cos_t, sin_t, kdec, bg):
    B, S, _ = x.shape
    nt = S // ATTN_TILE
    cpt = GATE_ROWS // RET_CHUNK
    in_specs = []
    args = []
    for (_, dil), qkv in zip(ATTN_GROUPS, qkv_groups):
        rows = ATTN_TILE // dil
        ratio = rows // BAND

        in_specs += [
            pl.BlockSpec((None, dil, rows, 3 * HEAD_DIM), lambda b, t, h: (b, 0, t, h)),
            pl.BlockSpec((None, dil, BAND, 3 * HEAD_DIM),
                         lambda b, t, h, ratio=ratio: (b, 0, jnp.maximum(t * ratio - 1, 0), h)),
        ]
        args += [qkv] * 2
    bias = _attn_bias_tables()
    in_specs.append(pl.BlockSpec(bias.shape, lambda b, t, h: (0, 0, 0), pipeline_mode=RESIDENT))

    def wspec(n, col0=0):
        return pl.BlockSpec((D_MODEL, n), lambda b, t, h: (0, col0 // n), pipeline_mode=RESIDENT)

    def row_spec(n):
        return pl.BlockSpec((None, GATE_ROWS, n), lambda b, t, h: (b, t * ATTN_HEADS + h, 0))

    tspec = pl.BlockSpec((GATE_ROWS, LANES), lambda b, t, h: (t * ATTN_HEADS + h, 0))
    in_specs += [
        row_spec(D_MODEL),
        pl.BlockSpec((1, D_MODEL), lambda b, t, h: (0, 0)),
        wspec(2 * RET_QK_WIDTH, COL_RQ),
        wspec(GROUP_WIDTH, COL_AG), wspec(RET_V_WIDTH, COL_RG), wspec(2 * D_MODEL, COL_MG),
        tspec, tspec,
        pl.BlockSpec((GATE_ROWS, RET_QK_WIDTH), lambda b, t, h: (0, 0), pipeline_mode=RESIDENT),
        pl.BlockSpec((1, 2 * D_MODEL), lambda b, t, h: (0, 0)),
    ]
    out_widths = (GROUP_WIDTH, RET_V_WIDTH, 2 * D_MODEL)
    per_group = pltpu.VMEM((N_GROUPS, ATTN_TILE, HEAD_DIM), F32)
    return pl.pallas_call(
        _attn_gates_kernel,
        grid=(B, nt, ATTN_HEADS),
        in_specs=in_specs,
        out_specs=[
            pl.BlockSpec((None, ATTN_TILE, HEAD_DIM), lambda b, t, h: (b, t, h)),
            row_spec(RET_QK_WIDTH),
            pl.BlockSpec((None, cpt, RET_QK_WIDTH, RET_CHUNK),
                         lambda b, t, h: (b, t * ATTN_HEADS + h, 0, 0)),
            *[row_spec(n) for n in out_widths],
        ],
        out_shape=[
            jax.ShapeDtypeStruct((B, S, GROUP_WIDTH), BF16),
            jax.ShapeDtypeStruct((B, S, RET_QK_WIDTH), BF16),
            jax.ShapeDtypeStruct((B, S // RET_CHUNK, RET_QK_WIDTH, RET_CHUNK), BF16),
            *[jax.ShapeDtypeStruct((B, S, n), BF16) for n in out_widths],
        ],
        scratch_shapes=[per_group, per_group, per_group,
                        pltpu.VMEM((ATTN_TILE, HEAD_DIM), F32)],
        compiler_params=pltpu.CompilerParams(
            dimension_semantics=("arbitrary", "arbitrary", "arbitrary"),
            vmem_limit_bytes=VMEM_LIMIT),
        name="attn_gates",
    )(*args, bias, x, lnw, w16, w16, w16, w16, cos_t, sin_t, kdec, bg)


def _retention_kernel(q_ref, kt_ref, v_ref, gnw_ref, eps_ref, cd_ref,
                      out_ref, state, p_scr):
    @pl.when(pl.program_id(1) == 0)
    def _():
        state[...] = jnp.zeros_like(state)

    dk, dv, C = RET_KEY_DIM, RET_VALUE_DIM, RET_CHUNK
    row_k = lax.broadcasted_iota(jnp.int32, (2 * dk, C), 0)
    row = lax.broadcasted_iota(jnp.int32, (2 * dk, 2 * dv), 0)
    col = lax.broadcasted_iota(jnp.int32, (C, 2 * dv), 1)
    diag_blk = (row < dk) == (col < dv)
    causal = (col % C) <= lax.broadcasted_iota(jnp.int32, (C, 2 * C), 0)
    units = [(bi, j) for bi in range(RET_BATCH) for j in range(RET_HEADS // 2)]

    def q_pair(bi, j):
        return q_ref[bi, :, j * 2 * dk:(j + 1) * 2 * dk]

    def kt_pair(bi, j):
        return kt_ref[bi, j * 2 * dk:(j + 1) * 2 * dk, :]

    def v_pair(bi, j):
        return v_ref[bi, :, j * 2 * dv:(j + 1) * 2 * dv]

    for bi, j in units:
        kt = kt_pair(bi, j)
        zero = jnp.zeros_like(kt)
        kt_blk = jnp.concatenate([jnp.where(row_k < dk, kt, zero),
                                  jnp.where(row_k >= dk, kt, zero)], axis=1)
        s = _dot(q_pair(bi, j), kt_blk).astype(BF16)
        p_scr[bi, j] = jnp.where(causal, s, jnp.zeros_like(s))

    for bi, j in units:
        v = v_pair(bi, j)
        zero = jnp.zeros_like(v)
        v_blk = jnp.concatenate([jnp.where(col < dv, v, zero),
                                 jnp.where(col >= dv, v, zero)], axis=0)
        u = _dot(jnp.concatenate([p_scr[bi, j], q_pair(bi, j)], axis=1),
                 jnp.concatenate([v_blk, state[bi, j].astype(BF16)], axis=0))
        for hh in range(2):
            h = 2 * j + hh
            hs = slice(h * dv, (h + 1) * dv)
            uh = u[:, hh * dv:(hh + 1) * dv]
            uh = uh * lax.rsqrt(jnp.sum(uh * uh, axis=-1, keepdims=True) + eps_ref[h])
            out_ref[bi, :, hs] = (uh * gnw_ref[:, hs]).astype(BF16)

    for bi, j in units:
        kv = _dot(kt_pair(bi, j), v_pair(bi, j))
        state[bi, j] = cd_ref[j] * (state[bi, j] + jnp.where(diag_blk, kv, 0.0))


def _retention(rq, rkt, rv, gnw, eps_t, cd):
    B, S, _ = rv.shape
    nc = S // RET_CHUNK
    npair = RET_HEADS // 2
    pair_shape = (RET_BATCH, npair, 2 * RET_KEY_DIM, 2 * RET_VALUE_DIM)

    def const(shape):
        return pl.BlockSpec(shape, lambda b, n: (0,) * len(shape))

    def rows(width):
        return pl.BlockSpec((RET_BATCH, RET_CHUNK, width), lambda b, n: (b, n, 0))

    return pl.pallas_call(
        _retention_kernel,
        grid=(B // RET_BATCH, nc),
        in_specs=[
            rows(RET_QK_WIDTH),
            pl.BlockSpec((RET_BATCH, None, RET_QK_WIDTH, RET_CHUNK), lambda b, n: (b, n, 0, 0)),
            rows(RET_V_WIDTH),
            const((1, RET_V_WIDTH)),
            const((RET_HEADS, RET_CHUNK, RET_VALUE_DIM)),
            const(pair_shape[1:]),
        ],
        out_specs=rows(RET_V_WIDTH),
        out_shape=jax.ShapeDtypeStruct((B, S, RET_V_WIDTH), BF16),
        scratch_shapes=[pltpu.VMEM(pair_shape, F32),
                        pltpu.VMEM((RET_BATCH, npair, RET_CHUNK, 2 * RET_CHUNK), BF16)],
        compiler_params=pltpu.CompilerParams(
            dimension_semantics=("arbitrary", "arbitrary"), vmem_limit_bytes=VMEM_LIMIT),
        name="retention",
    )(rq, rkt, rv, gnw, eps_t, cd)


def _merge_out_kernel(a_ref, ag_ref, r_ref, rg_ref, mg_ref, x_ref, wa_ref, wr_ref, wo_ref,
                      lnf_ref, out_ref):
    def part(rows):
        y_attn = _dot(a_ref[rows, :] * ag_ref[rows, :], wa_ref[...])
        y_ret = _dot(r_ref[rows, :] * rg_ref[rows, :], wr_ref[...])
        yield
        merged = (mg_ref[rows, :D_MODEL].astype(F32) * y_attn
                  + mg_ref[rows, D_MODEL:].astype(F32) * y_ret)
        h = x_ref[rows, :] + _dot(merged.astype(BF16), wo_ref[...])
        yield
        out_ref[rows, :] = _rmsnorm_rows(h, lnf_ref[...])

    n_parts = OUT_ROWS // OUT_PART
    parts = [part(slice(i * OUT_PART, (i + 1) * OUT_PART)) for i in range(n_parts)]
    for step in range(n_parts + 2):
        for lag in range(3):
            i = step - lag
            if 0 <= i < n_parts:
                next(parts[i], None)


def _merge_out(attn, ag, ret, rg, mg, x, wa, wr, wo, lnf):
    B, S, _ = x.shape
    nt = S // OUT_ROWS

    def rows(n):
        return pl.BlockSpec((None, OUT_ROWS, n), lambda b, t: (b, t, 0))

    def const(shape):
        return pl.BlockSpec(shape, lambda b, t: (0,) * len(shape), pipeline_mode=RESIDENT)

    return pl.pallas_call(
        _merge_out_kernel,
        grid=(B, nt),
        in_specs=[rows(GROUP_WIDTH), rows(GROUP_WIDTH), rows(RET_V_WIDTH), rows(RET_V_WIDTH),
                  rows(2 * D_MODEL), rows(D_MODEL),
                  const((GROUP_WIDTH, D_MODEL)), const((RET_V_WIDTH, D_MODEL)),
                  const((D_MODEL, D_MODEL)), const((1, D_MODEL))],
        out_specs=rows(D_MODEL),
        out_shape=jax.ShapeDtypeStruct((B, S, D_MODEL), F32),
        compiler_params=pltpu.CompilerParams(
            dimension_semantics=("arbitrary", "arbitrary"), vmem_limit_bytes=VMEM_LIMIT),
        name="merge_out",
    )(attn, ag, ret, rg, mg, x, wa, wr, wo, lnf)


def _rope_tables(S):
    half = RET_KEY_DIM // 2
    inv_freq = ROPE_BASE ** (-jnp.linspace(0.0, 1.0, half, dtype=F32))
    step = 64
    a = (jnp.arange(S // step, dtype=F32) * step)[:, None] * inv_freq[None, :]
    b = jnp.arange(step, dtype=F32)[:, None] * inv_freq[None, :]
    ca, sa, cb, sb = jnp.cos(a)[:, None], jnp.sin(a)[:, None], jnp.cos(b)[None], jnp.sin(b)[None]
    cos = (ca * cb - sa * sb).reshape(S, half)
    sin = (sa * cb + ca * sb).reshape(S, half)
    heads_per_tile = LANES // RET_KEY_DIM
    sign = jnp.where(jnp.arange(LANES) % 2 == 0, -1.0, 1.0).astype(F32)
    cos_t = jnp.tile(jnp.repeat(cos, 2, axis=1), (1, heads_per_tile))
    sin_t = jnp.tile(jnp.repeat(sin, 2, axis=1), (1, heads_per_tile)) * sign[None, :]
    return cos_t, sin_t


def _retention_constants():
    H, C, dk, dv = RET_HEADS, RET_CHUNK, RET_KEY_DIM, RET_VALUE_DIM
    log_gamma = jnp.log(1.0 - 2.0 ** (-5.0 - jnp.arange(H, dtype=F32)))
    idx = jnp.arange(C, dtype=F32)
    inv_decay = jnp.exp(-(idx + 1.0)[None, :] * log_gamma[:, None])
    chunk_decay = jnp.exp(C * log_gamma)
    kdec = jnp.repeat(inv_decay.T, dk, axis=1) * dk ** -0.5
    kdec = jnp.tile(kdec, (GATE_ROWS // C, 1))
    eps_t = jnp.broadcast_to((dv * NORM_EPS * inv_decay * inv_decay)[:, :, None], (H, C, dv))
    cd = jnp.broadcast_to(chunk_decay[:, None, None], (H, dk, 2 * dv)).reshape(H // 2, 2 * dk, 2 * dv)
    return kdec, eps_t, cd


def kernel(x, ln1_w, w_in, b_gate, attn_proj, ret_proj, ret_gn_w, w_out, lnf_w):
    B, S, _ = x.shape
    assert w_in.shape[0] == 1, "single layer"
    w16 = w_in[0].astype(BF16)

    lnw = ln1_w[0].reshape(1, D_MODEL)
    q1, q2, q3, rv = _qkv_proj(x, lnw, w16)
    cos_t, sin_t = _rope_tables(S)
    kdec, eps_t, cd = _retention_constants()
    attn, rq, rkt, ag, rg, mg = _attn_gates(
        (q1, q2, q3), x, lnw, w16, cos_t, sin_t, kdec, b_gate[0].reshape(1, 2 * D_MODEL))
    gnw = (ret_gn_w[0] * RET_VALUE_DIM ** 0.5).reshape(1, RET_V_WIDTH)
    ret = _retention(rq, rkt, rv, gnw, eps_t, cd)

    return _merge_out(attn, ag, ret, rg, mg, x,
                      attn_proj[0].astype(BF16), ret_proj[0].astype(BF16),
                      w_out[0].astype(BF16), lnf_w.reshape(1, D_MODEL))
```

```python
import jax
import jax.numpy as jnp
from jax import lax
from jax.experimental import pallas as pl
from jax.experimental.pallas import tpu as pltpu

D_MODEL = 1024
ATTN_GROUPS = ((128, 1), (512, 4), (2048, 16))
N_GROUPS = 3
ATTN_HEADS = 4
HEAD_DIM = 128
GROUP_WIDTH = ATTN_HEADS * HEAD_DIM
QKV_WIDTH = N_GROUPS * GROUP_WIDTH
BAND = 128
RET_HEADS = 8
RET_KEY_DIM = 64
RET_VALUE_DIM = 128
RET_QK_WIDTH = RET_HEADS * RET_KEY_DIM
RET_V_WIDTH = RET_HEADS * RET_VALUE_DIM
RET_CHUNK = 128
ROPE_BASE = 10000.0
COL_AG = 3 * QKV_WIDTH
COL_RQ = COL_AG + GROUP_WIDTH
COL_RV = COL_RQ + 2 * RET_QK_WIDTH
COL_RG = COL_RV + RET_V_WIDTH
COL_MG = COL_RG + RET_V_WIDTH
NORM_EPS = 1e-6
NEG_BIG = -1e30
LANES = 128
LOG2E = 1.4426950408889634
MIX_STRIDE = 4

PROJ_ROWS = 1024
QKV_PARTS = 4
ATTN_TILE = 2048
GATE_ROWS = ATTN_TILE // ATTN_HEADS
RET_BATCH = 8
OUT_ROWS = 1024
OUT_PART = 512
V7X_VMEM_BYTES = 64 * 1024 * 1024
VMEM_LIMIT = V7X_VMEM_BYTES * 7 // 8
GATE_ROW_PARTS = 2
GATE_PIECES = (2, 2, 4, 2, 2)
RESIDENT = pl.Buffered(1)

BF16 = jnp.bfloat16
F32 = jnp.float32


def _rmsnorm_rows(x, w):
    return x * lax.rsqrt(jnp.mean(x * x, axis=-1, keepdims=True) + NORM_EPS) * w


def _dot(a, b):
    return jnp.dot(a, b, preferred_element_type=F32)


def _sigmoid(t):
    return 0.5 * jnp.tanh(0.5 * t) + 0.5


def _silu(t):
    return t * _sigmoid(t)


def _qkv_proj_kernel(x_ref, lnw_ref, *refs):
    wq_ref, wk_ref, wv_ref, wrv_ref = refs[:4]
    out_refs = refs[4:4 + N_GROUPS]
    rv_ref, xn_scr, xn4_scr, lhs_scr = refs[4 + N_GROUPS:]
    n_lane_tiles = D_MODEL // LANES
    q_scale = HEAD_DIM ** -0.5 * LOG2E
    part_rows = PROJ_ROWS // QKV_PARTS
    d4, d16 = ATTN_GROUPS[1][1], ATTN_GROUPS[2][1]
    rows4, rows16 = part_rows // d4, part_rows // d16
    NAT, QPERM, BY4, BY16 = range(4)

    def part(p):
        rows_p = slice(p * part_rows, (p + 1) * part_rows)
        lhs = lhs_scr.at[p]
        xn = _rmsnorm_rows(x_ref[rows_p, :], lnw_ref[...])
        lhs[NAT] = xn.astype(BF16)
        for ct in range(n_lane_tiles):
            xn_scr[ct] = xn[:, ct * LANES:(ct + 1) * LANES]
        sub = BAND // MIX_STRIDE
        for ct in range(n_lane_tiles):
            lanes = slice(ct * LANES, (ct + 1) * LANES)
            lhs[QPERM, :, lanes] = jnp.concatenate(
                [xn_scr[ct, pl.ds(blk * BAND + r4, sub, stride=MIX_STRIDE), :]
                 for blk in range(part_rows // BAND) for r4 in range(MIX_STRIDE)],
                axis=0).astype(BF16)
            by4 = jnp.concatenate(
                [xn_scr[ct, pl.ds(r, rows4, stride=d4), :] for r in range(d4)], axis=0)
            xn4_scr[ct] = by4
            lhs[BY4, :, lanes] = by4.astype(BF16)
        for ct in range(n_lane_tiles):
            by16 = jnp.concatenate(
                [xn4_scr[ct, pl.ds((r % d4) * rows4 + r // d4, rows16, stride=d16 // d4), :]
                 for r in range(d16)], axis=0)
            lhs[BY16, :, ct * LANES:(ct + 1) * LANES] = by16.astype(BF16)
        yield

        rv_ref[rows_p, :] = _dot(lhs[NAT], wrv_ref[...]).astype(BF16)
        for g, (order_q, order_kv) in enumerate(((QPERM, NAT), (BY4, BY4), (BY16, BY16))):
            gcols = slice(g * GROUP_WIDTH, (g + 1) * GROUP_WIDTH)
            dil = ATTN_GROUPS[g][1]
            rows = part_rows // dil
            q = (_dot(lhs[order_q], wq_ref[:, gcols]) * q_scale).astype(BF16)
            k = _dot(lhs[order_kv], wk_ref[:, gcols]).astype(BF16)
            v = _dot(lhs[order_kv], wv_ref[:, gcols]).astype(BF16)
            for r in range(dil):
                src_rows = slice(r * rows, (r + 1) * rows)
                dst_rows = slice(p * rows, (p + 1) * rows)
                for h in range(ATTN_HEADS):
                    hs = slice(h * HEAD_DIM, (h + 1) * HEAD_DIM)
                    for which, val in enumerate((q, k, v)):
                        c0 = (3 * h + which) * HEAD_DIM
                        out_refs[g][r, dst_rows, c0:c0 + HEAD_DIM] = val[src_rows, hs]

    parts = [part(p) for p in range(QKV_PARTS)]
    for gen in parts:
        next(gen)
    for gen in parts:
        next(gen, None)


def _qkv_proj(x, lnw, w16):
    B, S, _ = x.shape
    nt = S // PROJ_ROWS
    w_specs = [pl.BlockSpec((D_MODEL, QKV_WIDTH), lambda b, t, part=part: (0, part),
                            pipeline_mode=RESIDENT) for part in range(3)]
    w_specs.append(pl.BlockSpec((D_MODEL, RET_V_WIDTH), lambda b, t: (0, COL_RV // RET_V_WIDTH),
                                pipeline_mode=RESIDENT))
    return pl.pallas_call(
        _qkv_proj_kernel,
        grid=(B, nt),
        in_specs=[
            pl.BlockSpec((None, PROJ_ROWS, D_MODEL), lambda b, t: (b, t, 0)),
            pl.BlockSpec((1, D_MODEL), lambda b, t: (0, 0)),
            *w_specs,
        ],
        out_specs=[
            *[pl.BlockSpec((None, dil, PROJ_ROWS // dil, QKV_WIDTH), lambda b, t: (b, 0, t, 0))
              for _, dil in ATTN_GROUPS],
            pl.BlockSpec((None, PROJ_ROWS, RET_V_WIDTH), lambda b, t: (b, t, 0)),
        ],
        out_shape=[
            *[jax.ShapeDtypeStruct((B, dil, S // dil, QKV_WIDTH), BF16) for _, dil in ATTN_GROUPS],
            jax.ShapeDtypeStruct((B, S, RET_V_WIDTH), BF16),
        ],
        scratch_shapes=[pltpu.VMEM((D_MODEL // LANES, PROJ_ROWS // QKV_PARTS, LANES), F32),
                        pltpu.VMEM((D_MODEL // LANES, PROJ_ROWS // QKV_PARTS, LANES), F32),
                        pltpu.VMEM((QKV_PARTS, 4, PROJ_ROWS // QKV_PARTS, D_MODEL), BF16)],
        compiler_params=pltpu.CompilerParams(
            dimension_semantics=("arbitrary", "arbitrary"), vmem_limit_bytes=VMEM_LIMIT),
        name="qkv_proj",
    )(x, lnw, w16, w16, w16, w16)


def _swap_lane_pairs(t):
    n = t.shape[-1]
    lane = lax.broadcasted_iota(jnp.int32, t.shape, 1)
    up = pltpu.roll(t, n - 1, axis=1)
    down = pltpu.roll(t, 1, axis=1)
    return jnp.where(lane % 2 == 0, up, down)


def _gate_proj_items(x_ref, lnw_ref, wqk_ref, wg_ref, wrg_ref, wmg_ref,
                     cos_ref, sin_ref, kdec_ref, bg_ref,
                     rq_ref, rkt_ref, ag_ref, rg_ref, mg_ref):
    cache = {}

    def lhs():
        if "xb" not in cache:
            cache["xb"] = _rmsnorm_rows(x_ref[...], lnw_ref[...]).astype(BF16)
        return cache["xb"]

    def rope(t, rows):
        cos = cos_ref[rows, :]
        sin = sin_ref[rows, :]
        tiles = [t[:, i * LANES:(i + 1) * LANES] for i in range(t.shape[1] // LANES)]
        return jnp.concatenate([tt * cos + _swap_lane_pairs(tt) * sin for tt in tiles], axis=1)

    def retention_q(cols, rows):
        t = _dot(lhs()[rows], wqk_ref[:, cols])
        yield
        rq_ref[rows, cols] = rope(t, rows).astype(BF16)

    def retention_k(cols, rows):
        t = _dot(lhs()[rows], wqk_ref[:, RET_QK_WIDTH + cols.start:RET_QK_WIDTH + cols.stop])
        yield
        k = rope(t, rows) * kdec_ref[rows, cols]
        for c in range((rows.stop - rows.start) // RET_CHUNK):
            rkt_ref[rows.start // RET_CHUNK + c, cols, :] = (
                k[c * RET_CHUNK:(c + 1) * RET_CHUNK, :].T.astype(BF16))

    def merge_gate(cols, rows):
        t = _dot(lhs()[rows], wmg_ref[:, cols])
        yield
        mg_ref[rows, cols] = _sigmoid(t + bg_ref[:, cols]).astype(BF16)

    def retention_gate(cols, rows):
        t = _dot(lhs()[rows], wrg_ref[:, cols])
        yield
        rg_ref[rows, cols] = _silu(t).astype(BF16)

    def attention_gate(cols, rows):
        t = _dot(lhs()[rows], wg_ref[:, cols])
        yield
        ag_ref[rows, cols] = _silu(t).astype(BF16)

    items = []
    part = GATE_ROWS // GATE_ROW_PARTS
    for fn, total, pieces in ((retention_q, RET_QK_WIDTH, GATE_PIECES[0]),
                              (retention_k, RET_QK_WIDTH, GATE_PIECES[1]),
                              (merge_gate, 2 * D_MODEL, GATE_PIECES[2]),
                              (retention_gate, RET_V_WIDTH, GATE_PIECES[3]),
                              (attention_gate, GROUP_WIDTH, GATE_PIECES[4])):
        width = total // pieces
        items += [lambda fn=fn, cols=slice(i * width, (i + 1) * width),
                  rows=slice(j * part, (j + 1) * part): fn(cols, rows)
                  for i in range(pieces) for j in range(GATE_ROW_PARTS)]
    return items


def _attention_items(in_refs, bias_ref, out_ref, acc_scr, max_scr, den_scr, nat_scr, first):
    ones = jnp.ones((2 * BAND, HEAD_DIM), BF16)
    quarter = ATTN_TILE // MIX_STRIDE

    def unit(g, r, n):
        dil = ATTN_GROUPS[g][1]
        cur_ref, prev_ref = in_refs[2 * g:2 * g + 2]
        q_cols, k_cols, v_cols = (slice(i * HEAD_DIM, (i + 1) * HEAD_DIM) for i in range(3))
        bias_base = 2 if dil == 1 else 0
        q = cur_ref[r, n * BAND:(n + 1) * BAND, q_cols]
        if n == 0:
            k = jnp.concatenate([prev_ref[r, :, k_cols], cur_ref[r, 0:BAND, k_cols]], axis=0)
            v = jnp.concatenate([prev_ref[r, :, v_cols], cur_ref[r, 0:BAND, v_cols]], axis=0)
            bias = bias_ref[bias_base + first]
        else:
            k = cur_ref[r, (n - 1) * BAND:(n + 1) * BAND, k_cols]
            v = cur_ref[r, (n - 1) * BAND:(n + 1) * BAND, v_cols]
            bias = bias_ref[bias_base]
        s = lax.dot_general(q, k, (((1,), (1,)), ((), ())), preferred_element_type=F32) + bias
        yield
        half = BAND // 2
        ms, ps = [], []
        for rows in (slice(0, half), slice(half, BAND)):
            s_part = s[rows]
            m_part = jnp.max(s_part, axis=-1, keepdims=True)
            ms.append(m_part)
            ps.append(jnp.exp2(s_part - m_part).astype(BF16))
        m = jnp.concatenate(ms, axis=0)
        p = jnp.concatenate(ps, axis=0)
        yield
        res = _dot(p, jnp.concatenate([v, ones], axis=1))
        acc, den = res[:, :HEAD_DIM], res[:, HEAD_DIM:]
        mb = jnp.broadcast_to(m, (BAND, HEAD_DIM))
        if dil == 1:
            sub = BAND // MIX_STRIDE
            for r4 in range(MIX_STRIDE):
                dst = pl.ds(r4 * quarter + n * sub, sub)
                src = slice(r4 * sub, (r4 + 1) * sub)
                acc_scr[g, dst, :] = acc[src]
                max_scr[g, dst, :] = mb[src]
                den_scr[g, dst, :] = den[src]
        else:
            if dil == MIX_STRIDE:
                dst = pl.ds(r * quarter + n * BAND, BAND)
            else:
                sub_stride = dil // MIX_STRIDE
                dst = pl.ds((r % MIX_STRIDE) * quarter + r // MIX_STRIDE
                            + n * BAND * sub_stride, BAND, stride=sub_stride)
            acc_scr[g, dst, :] = acc
            max_scr[g, dst, :] = mb
            den_scr[g, dst, :] = den

    def mix(r4, ch):
        rows = pl.ds(r4 * quarter + ch * BAND, BAND)
        m0, m1, m2 = max_scr[0, rows, :], max_scr[1, rows, :], max_scr[2, rows, :]
        mx = jnp.maximum(jnp.maximum(m0, m1), m2)
        w0 = jnp.exp2(m0 - mx)
        w1 = jnp.exp2(m1 - mx)
        w2 = jnp.exp2(m2 - mx)
        num = w0 * acc_scr[0, rows, :] + w1 * acc_scr[1, rows, :] + w2 * acc_scr[2, rows, :]
        den = w0 * den_scr[0, rows, :] + w1 * den_scr[1, rows, :] + w2 * den_scr[2, rows, :]
        nat_scr[pl.ds(r4 + ch * BAND * MIX_STRIDE, BAND, stride=MIX_STRIDE), :] = num / den

    def write_out():
        out_ref[...] = nat_scr[...].astype(BF16)

    def mix_quarter(r4):
        for ch in range(quarter // BAND):
            mix(r4, ch)

    units = [lambda n=n: unit(0, 0, n) for n in range(ATTN_TILE // BAND)]
    quarter_done = []
    for r4 in range(MIX_STRIDE):
        units += [lambda g=g, r=r, n=n: unit(g, r, n)
                  for g in range(1, N_GROUPS)
                  for r in range(r4, ATTN_GROUPS[g][1], MIX_STRIDE)
                  for n in range(ATTN_TILE // ATTN_GROUPS[g][1] // BAND)]
        quarter_done.append(len(units))
    return units, quarter_done, mix_quarter, write_out


def _attn_gates_kernel(*refs):
    n_attn = 2 * N_GROUPS
    in_refs = refs[:n_attn]
    bias_ref = refs[n_attn]
    proj_in = refs[n_attn + 1:n_attn + 11]
    attn_out = refs[n_attn + 11]
    proj_out = refs[n_attn + 12:n_attn + 17]
    acc_scr, max_scr, den_scr, nat_scr = refs[n_attn + 17:]
    first = (pl.program_id(1) == 0).astype(jnp.int32)
    unit_makers, quarter_done, mix_quarter, write_out = _attention_items(
        in_refs, bias_ref, attn_out, acc_scr, max_scr, den_scr, nat_scr, first)
    proj = _gate_proj_items(*proj_in, *proj_out)
    per_round = len(unit_makers) // len(proj)
    assert per_round * len(proj) == len(unit_makers)
    rounds = [unit_makers[i * per_round:(i + 1) * per_round] for i in range(len(proj))]

    def advance(gen):
        next(gen, None)

    prev_units = []
    mixed = 0
    for r in range(len(rounds) + 1):
        units = [make() for make in rounds[r]] if r < len(rounds) else []
        for j in range(max(len(units), len(prev_units))):
            if j < len(units):
                advance(units[j])
            if j < len(prev_units):
                advance(prev_units[j])
        while mixed < len(quarter_done) and quarter_done[mixed] <= r * per_round:
            mix_quarter(mixed)
            mixed += 1
        if r < len(proj):
            for _ in proj[r]():
                pass
        for u in units:
            advance(u)
        prev_units = units
    write_out()


def _attn_bias_tables():
    rho = jnp.arange(BAND)[:, None]
    c = jnp.arange(2 * BAND)[None, :]
    sub = BAND // MIX_STRIDE
    tables = []
    for a in (rho, MIX_STRIDE * (rho % sub) + rho // sub):
        ok = (c >= a) & (c <= a + BAND)
        tables += [ok, ok & (c >= BAND)]
    return jnp.where(jnp.stack(tables), 0.0, NEG_BIG).astype(F32)


def _attn_gates(qkv_groups, x, lnw, w16, cos_t, sin_t, kdec, bg):
    B, S, _ = x.shape
    nt = S // ATTN_TILE
    cpt = GATE_ROWS // RET_CHUNK
    in_specs = []
    args = []
    for (_, dil), qkv in zip(ATTN_GROUPS, qkv_groups):
        rows = ATTN_TILE // dil
        ratio = rows // BAND

        in_specs += [
            pl.BlockSpec((None, dil, rows, 3 * HEAD_DIM), lambda b, t, h: (b, 0, t, h)),
            pl.BlockSpec((None, dil, BAND, 3 * HEAD_DIM),
                         lambda b, t, h, ratio=ratio: (b, 0, jnp.maximum(t * ratio - 1, 0), h)),
        ]
        args += [qkv] * 2
    bias = _attn_bias_tables()
    in_specs.append(pl.BlockSpec(bias.shape, lambda b, t, h: (0, 0, 0), pipeline_mode=RESIDENT))

    def wspec(n, col0=0):
        return pl.BlockSpec((D_MODEL, n), lambda b, t, h: (0, col0 // n), pipeline_mode=RESIDENT)

    def row_spec(n):
        return pl.BlockSpec((None, GATE_ROWS, n), lambda b, t, h: (b, t * ATTN_HEADS + h, 0))

    tspec = pl.BlockSpec((GATE_ROWS, LANES), lambda b, t, h: (t * ATTN_HEADS + h, 0))
    in_specs += [
        row_spec(D_MODEL),
        pl.BlockSpec((1, D_MODEL), lambda b, t, h: (0, 0)),
        wspec(2 * RET_QK_WIDTH, COL_RQ),
        wspec(GROUP_WIDTH, COL_AG), wspec(RET_V_WIDTH, COL_RG), wspec(2 * D_MODEL, COL_MG),
        tspec, tspec,
        pl.BlockSpec((GATE_ROWS, RET_QK_WIDTH), lambda b, t, h: (0, 0), pipeline_mode=RESIDENT),
        pl.BlockSpec((1, 2 * D_MODEL), lambda b, t, h: (0, 0)),
    ]
    out_widths = (GROUP_WIDTH, RET_V_WIDTH, 2 * D_MODEL)
    per_group = pltpu.VMEM((N_GROUPS, ATTN_TILE, HEAD_DIM), F32)
    return pl.pallas_call(
        _attn_gates_kernel,
        grid=(B, nt, ATTN_HEADS),
        in_specs=in_specs,
        out_specs=[
            pl.BlockSpec((None, ATTN_TILE, HEAD_DIM), lambda b, t, h: (b, t, h)),
            row_spec(RET_QK_WIDTH),
            pl.BlockSpec((None, cpt, RET_QK_WIDTH, RET_CHUNK),
                         lambda b, t, h: (b, t * ATTN_HEADS + h, 0, 0)),
            *[row_spec(n) for n in out_widths],
        ],
        out_shape=[
            jax.ShapeDtypeStruct((B, S, GROUP_WIDTH), BF16),
            jax.ShapeDtypeStruct((B, S, RET_QK_WIDTH), BF16),
            jax.ShapeDtypeStruct((B, S // RET_CHUNK, RET_QK_WIDTH, RET_CHUNK), BF16),
            *[jax.ShapeDtypeStruct((B, S, n), BF16) for n in out_widths],
        ],
        scratch_shapes=[per_group, per_group, per_group,
                        pltpu.VMEM((ATTN_TILE, HEAD_DIM), F32)],
        compiler_params=pltpu.CompilerParams(
            dimension_semantics=("arbitrary", "arbitrary", "arbitrary"),
            vmem_limit_bytes=VMEM_LIMIT),
        name="attn_gates",
    )(*args, bias, x, lnw, w16, w16, w16, w16, cos_t, sin_t, kdec, bg)


def _retention_kernel(q_ref, kt_ref, v_ref, gnw_ref, eps_ref, cd_ref,
                      out_ref, state, p_scr):
    @pl.when(pl.program_id(1) == 0)
    def _():
        state[...] = jnp.zeros_like(state)

    dk, dv, C = RET_KEY_DIM, RET_VALUE_DIM, RET_CHUNK
    row_k = lax.broadcasted_iota(jnp.int32, (2 * dk, C), 0)
    row = lax.broadcasted_iota(jnp.int32, (2 * dk, 2 * dv), 0)
    col = lax.broadcasted_iota(jnp.int32, (C, 2 * dv), 1)
    diag_blk = (row < dk) == (col < dv)
    causal = (col % C) <= lax.broadcasted_iota(jnp.int32, (C, 2 * C), 0)
    units = [(bi, j) for bi in range(RET_BATCH) for j in range(RET_HEADS // 2)]

    def q_pair(bi, j):
        return q_ref[bi, :, j * 2 * dk:(j + 1) * 2 * dk]

    def kt_pair(bi, j):
        return kt_ref[bi, j * 2 * dk:(j + 1) * 2 * dk, :]

    def v_pair(bi, j):
        return v_ref[bi, :, j * 2 * dv:(j + 1) * 2 * dv]

    for bi, j in units:
        kt = kt_pair(bi, j)
        zero = jnp.zeros_like(kt)
        kt_blk = jnp.concatenate([jnp.where(row_k < dk, kt, zero),
                                  jnp.where(row_k >= dk, kt, zero)], axis=1)
        s = _dot(q_pair(bi, j), kt_blk).astype(BF16)
        p_scr[bi, j] = jnp.where(causal, s, jnp.zeros_like(s))

    for bi, j in units:
        v = v_pair(bi, j)
        zero = jnp.zeros_like(v)
        v_blk = jnp.concatenate([jnp.where(col < dv, v, zero),
                                 jnp.where(col >= dv, v, zero)], axis=0)
        u = _dot(jnp.concatenate([p_scr[bi, j], q_pair(bi, j)], axis=1),
                 jnp.concatenate([v_blk, state[bi, j].astype(BF16)], axis=0))
        for hh in range(2):
            h = 2 * j + hh
            hs = slice(h * dv, (h + 1) * dv)
            uh = u[:, hh * dv:(hh + 1) * dv]
            uh = uh * lax.rsqrt(jnp.sum(uh * uh, axis=-1, keepdims=True) + eps_ref[h])
            out_ref[bi, :, hs] = (uh * gnw_ref[:, hs]).astype(BF16)

    for bi, j in units:
        kv = _dot(kt_pair(bi, j), v_pair(bi, j))
        state[bi, j] = cd_ref[j] * (state[bi, j] + jnp.where(diag_blk, kv, 0.0))


def _retention(rq, rkt, rv, gnw, eps_t, cd):
    B, S, _ = rv.shape
    nc = S // RET_CHUNK
    npair = RET_HEADS // 2
    pair_shape = (RET_BATCH, npair, 2 * RET_KEY_DIM, 2 * RET_VALUE_DIM)

    def const(shape):
        return pl.BlockSpec(shape, lambda b, n: (0,) * len(shape))

    def rows(width):
        return pl.BlockSpec((RET_BATCH, RET_CHUNK, width), lambda b, n: (b, n, 0))

    return pl.pallas_call(
        _retention_kernel,
        grid=(B // RET_BATCH, nc),
        in_specs=[
            rows(RET_QK_WIDTH),
            pl.BlockSpec((RET_BATCH, None, RET_QK_WIDTH, RET_CHUNK), lambda b, n: (b, n, 0, 0)),
            rows(RET_V_WIDTH),
            const((1, RET_V_WIDTH)),
            const((RET_HEADS, RET_CHUNK, RET_VALUE_DIM)),
            const(pair_shape[1:]),
        ],
        out_specs=rows(RET_V_WIDTH),
        out_shape=jax.ShapeDtypeStruct((B, S, RET_V_WIDTH), BF16),
        scratch_shapes=[pltpu.VMEM(pair_shape, F32),
                        pltpu.VMEM((RET_BATCH, npair, RET_CHUNK, 2 * RET_CHUNK), BF16)],
        compiler_params=pltpu.CompilerParams(
            dimension_semantics=("arbitrary", "arbitrary"), vmem_limit_bytes=VMEM_LIMIT),
        name="retention",
    )(rq, rkt, rv, gnw, eps_t, cd)


def _merge_out_kernel(a_ref, ag_ref, r_ref, rg_ref, mg_ref, x_ref, wa_ref, wr_ref, wo_ref,
                      lnf_ref, out_ref):
    def part(rows):
        y_attn = _dot(a_ref[rows, :] * ag_ref[rows, :], wa_ref[...])
        y_ret = _dot(r_ref[rows, :] * rg_ref[rows, :], wr_ref[...])
        yield
        merged = (mg_ref[rows, :D_MODEL].astype(F32) * y_attn
                  + mg_ref[rows, D_MODEL:].astype(F32) * y_ret)
        h = x_ref[rows, :] + _dot(merged.astype(BF16), wo_ref[...])
        yield
        out_ref[rows, :] = _rmsnorm_rows(h, lnf_ref[...])

    n_parts = OUT_ROWS // OUT_PART
    parts = [part(slice(i * OUT_PART, (i + 1) * OUT_PART)) for i in range(n_parts)]
    for step in range(n_parts + 2):
        for lag in range(3):
            i = step - lag
            if 0 <= i < n_parts:
                next(parts[i], None)


def _merge_out(attn, ag, ret, rg, mg, x, wa, wr, wo, lnf):
    B, S, _ = x.shape
    nt = S // OUT_ROWS

    def rows(n):
        return pl.BlockSpec((None, OUT_ROWS, n), lambda b, t: (b, t, 0))

    def const(shape):
        return pl.BlockSpec(shape, lambda b, t: (0,) * len(shape), pipeline_mode=RESIDENT)

    return pl.pallas_call(
        _merge_out_kernel,
        grid=(B, nt),
        in_specs=[rows(GROUP_WIDTH), rows(GROUP_WIDTH), rows(RET_V_WIDTH), rows(RET_V_WIDTH),
                  rows(2 * D_MODEL), rows(D_MODEL),
                  const((GROUP_WIDTH, D_MODEL)), const((RET_V_WIDTH, D_MODEL)),
                  const((D_MODEL, D_MODEL)), const((1, D_MODEL))],
        out_specs=rows(D_MODEL),
        out_shape=jax.ShapeDtypeStruct((B, S, D_MODEL), F32),
        compiler_params=pltpu.CompilerParams(
            dimension_semantics=("arbitrary", "arbitrary"), vmem_limit_bytes=VMEM_LIMIT),
        name="merge_out",
    )(attn, ag, ret, rg, mg, x, wa, wr, wo, lnf)


def _rope_tables(S):
    half = RET_KEY_DIM // 2
    inv_freq = ROPE_BASE ** (-jnp.linspace(0.0, 1.0, half, dtype=F32))
    step = 64
    a = (jnp.arange(S // step, dtype=F32) * step)[:, None] * inv_freq[None, :]
    b = jnp.arange(step, dtype=F32)[:, None] * inv_freq[None, :]
    ca, sa, cb, sb = jnp.cos(a)[:, None], jnp.sin(a)[:, None], jnp.cos(b)[None], jnp.sin(b)[None]
    cos = (ca * cb - sa * sb).reshape(S, half)
    sin = (sa * cb + ca * sb).reshape(S, half)
    heads_per_tile = LANES // RET_KEY_DIM
    sign = jnp.where(jnp.arange(LANES) % 2 == 0, -1.0, 1.0).astype(F32)
    cos_t = jnp.tile(jnp.repeat(cos, 2, axis=1), (1, heads_per_tile))
    sin_t = jnp.tile(jnp.repeat(sin, 2, axis=1), (1, heads_per_tile)) * sign[None, :]
    return cos_t, sin_t


def _retention_constants():
    H, C, dk, dv = RET_HEADS, RET_CHUNK, RET_KEY_DIM, RET_VALUE_DIM
    log_gamma = jnp.log(1.0 - 2.0 ** (-5.0 - jnp.arange(H, dtype=F32)))
    idx = jnp.arange(C, dtype=F32)
    inv_decay = jnp.exp(-(idx + 1.0)[None, :] * log_gamma[:, None])
    chunk_decay = jnp.exp(C * log_gamma)
    kdec = jnp.repeat(inv_decay.T, dk, axis=1) * dk ** -0.5
    kdec = jnp.tile(kdec, (GATE_ROWS // C, 1))
    eps_t = jnp.broadcast_to((dv * NORM_EPS * inv_decay * inv_decay)[:, :, None], (H, C, dv))
    cd = jnp.broadcast_to(chunk_decay[:, None, None], (H, dk, 2 * dv)).reshape(H // 2, 2 * dk, 2 * dv)
    return kdec, eps_t, cd


def kernel(x, ln1_w, w_in, b_gate, attn_proj, ret_proj, ret_gn_w, w_out, lnf_w):
    B, S, _ = x.shape
    assert w_in.shape[0] == 1, "single layer"
    w16 = w_in[0].astype(BF16)

    lnw = ln1_w[0].reshape(1, D_MODEL)
    q1, q2, q3, rv = _qkv_proj(x, lnw, w16)
    cos_t, sin_t = _rope_tables(S)
    kdec, eps_t, cd = _retention_constants()
    attn, rq, rkt, ag, rg, mg = _attn_gates(
        (q1, q2, q3), x, lnw, w16, cos_t, sin_t, kdec, b_gate[0].reshape(1, 2 * D_MODEL))
    gnw = (ret_gn_w[0] * RET_VALUE_DIM ** 0.5).reshape(1, RET_V_WIDTH)
    ret = _retention(rq, rkt, rv, gnw, eps_t, cd)

    return _merge_out(attn, ag, ret, rg, mg, x,
                      attn_proj[0].astype(BF16), ret_proj[0].astype(BF16),
                      w_out[0].astype(BF16), lnf_w.reshape(1, D_MODEL))
```
